```python
import math
import jax
import jax.numpy as jnp
from jax import lax
import numpy as np

D_MODEL = 1024
BATCH = 16
SEQ = 256
DEPTH = 4
DEC_BATCH = 2
DEC_SEQ = 2048
PAST_LEN = 256

GRID_W = 64
ATT_WIDTH = D_MODEL // 2
ATT_QK = 64
ATT_V = 2 * ATT_QK
ATT_HEADS = ATT_WIDTH // ATT_V
QK_WIDTH = ATT_HEADS * ATT_QK
S5_WIDTH = D_MODEL // 4
S5_GROUP = 16
S5_GROUPS = S5_WIDTH // S5_GROUP
S5_STATE = 64
LRU_WIDTH = D_MODEL // 4
LRU_BLOCKS = 4
LRU_BLOCK = LRU_WIDTH // LRU_BLOCKS
LRU_CONV = 4
LRU_CONV_LEFT = 1
LRU_C = 8.0
N_GROUPS = 4
EXPERTS_PER_GROUP = 4
N_EXPERTS = N_GROUPS * EXPERTS_PER_GROUP
TOP_K = 2
D_EXPERT = D_MODEL // 2
IN_WIDTH = 4 * QK_WIDTH + ATT_WIDTH + S5_WIDTH + 2 * LRU_WIDTH
MIX_WIDTH = ATT_WIDTH + S5_WIDTH + LRU_WIDTH
N_MOD = 6
ROPE_BASE = 10000.0
Q_BLOCK = 128
DEEPNORM_ALPHA = (2 * DEPTH) ** 0.25
DEEPNORM_BETA = (8 * DEPTH) ** -0.25
NORM_EPS = 1e-5

kernel_name = 'hybrid_diffusion_prefix_trunk_step'


def layer_norm(x, g, b):
    xf = x.astype(jnp.float32)
    mu = jnp.mean(xf, axis=-1, keepdims=True)
    var = jnp.mean(jnp.square(xf - mu), axis=-1, keepdims=True)
    y = (xf - mu) * lax.rsqrt(var + NORM_EPS) * g.astype(jnp.float32) + b.astype(jnp.float32)
    return y.astype(x.dtype)


def rms_norm(x, g):
    xf = x.astype(jnp.float32)
    y = xf * lax.rsqrt(jnp.mean(jnp.square(xf), axis=-1, keepdims=True) + NORM_EPS)
    return (y * g.astype(jnp.float32)).astype(x.dtype)


def axial_rope_tables(n_tok):
    rows = n_tok // GRID_W
    row = jnp.repeat(jnp.arange(rows, dtype=jnp.float32), GRID_W)
    col = jnp.tile(jnp.arange(GRID_W, dtype=jnp.float32), rows)
    n_freq = ATT_QK // 4
    inv = ROPE_BASE ** (-jnp.arange(n_freq, dtype=jnp.float32) / n_freq)
    ang = jnp.stack([row[:, None] * inv, col[:, None] * inv], axis=1)
    return jnp.cos(ang), jnp.sin(ang)


def apply_axial_rope(x, cos, sin):
    xs = x.astype(jnp.float32).reshape(x.shape[:-1] + (2, 2, ATT_QK // 4))
    x1, x2 = xs[..., 0, :], xs[..., 1, :]
    y = jnp.stack([x1 * cos - x2 * sin, x1 * sin + x2 * cos], axis=-2)
    return y.reshape(x.shape).astype(x.dtype)


def diff_attention(q1, q2, k1, k2, v, lam):
    b, h, nq, dk = q1.shape
    nb = nq // Q_BLOCK
    scale = dk ** -0.5
    qs = jnp.stack([q1, q2], axis=2)
    ks = jnp.stack([k1, k2], axis=2)
    qb = qs.reshape(b, h, 2, nb, Q_BLOCK, dk).transpose(3, 0, 1, 2, 4, 5)

    def block(qblk):
        s = jnp.einsum('bhmqd,bhmkd->bhmqk', qblk, ks).astype(jnp.float32) * scale
        p = jax.nn.softmax(s, axis=-1)
        w = p[:, :, 0] - lam * p[:, :, 1]
        return jnp.einsum('bhqk,bhkd->bhqd', w.astype(v.dtype), v)

    out = lax.map(block, qb)
    return out.transpose(1, 2, 0, 3, 4).reshape(b, h, nq, v.shape[-1])


def linear_recurrence_combine(e1, e2):
    a1, b1 = e1
    a2, b2 = e2
    return a1 * a2, a2 * b1 + b2


def s5_scan(u, a_re, a_im, b_re, b_im, c_re, c_im, log_dt, h0, reverse):
    f32 = jnp.float32
    lam = lax.complex(a_re.astype(f32), a_im.astype(f32))
    dt = jnp.exp(log_dt.astype(f32))[:, None]
    a_bar = jnp.exp(lam * dt)
    b_bar = ((a_bar - 1.0) / lam)[..., None] * lax.complex(b_re.astype(f32), b_im.astype(f32))
    bu = jnp.einsum('gpc,bngc->bngp', b_bar, u.astype(jnp.complex64))
    if h0 is not None:
        t0 = -1 if reverse else 0
        bu = bu.at[:, t0].add(a_bar * h0)
    a_seq = jnp.broadcast_to(a_bar, bu.shape)
    _, h = lax.associative_scan(linear_recurrence_combine, (a_seq, bu), reverse=reverse, axis=1)
    y = jnp.einsum('gcp,bngp->bngc', lax.complex(c_re.astype(f32), c_im.astype(f32)), h).real
    return h, y


def s5_mixer(u, p, h0s):
    b, n, _ = u.shape
    uf = u.astype(jnp.float32)
    ug = uf.reshape(b, n, S5_GROUPS, S5_GROUP)
    h_f, y_f = s5_scan(ug, p['s5_a_re'][0], p['s5_a_im'][0], p['s5_b_re'][0], p['s5_b_im'][0],
                       p['s5_c_re'][0], p['s5_c_im'][0], p['s5_log_dt'][0], h0s[0], False)
    h_b, y_b = s5_scan(ug, p['s5_a_re'][1], p['s5_a_im'][1], p['s5_b_re'][1], p['s5_b_im'][1],
                       p['s5_c_re'][1], p['s5_c_im'][1], p['s5_log_dt'][1], h0s[1], True)
    y = (y_f + y_b).reshape(b, n, S5_WIDTH) + p['s5_d'].astype(jnp.float32) * uf
    g = jax.nn.gelu(y)
    out = g * jax.nn.sigmoid(g @ p['s5_w_glu'].astype(jnp.float32) + p['s5_b_glu'].astype(jnp.float32))
    return out.astype(u.dtype), h_f, h_b


def centred_depthwise_conv(x, w, bias):
    n = x.shape[1]
    xp = jnp.pad(x, ((0, 0), (LRU_CONV_LEFT, LRU_CONV - 1 - LRU_CONV_LEFT), (0, 0)))
    y = bias
    for k in range(LRU_CONV):
        y = y + xp[:, k:k + n] * w[k]
    return y


def rglru_scan(x, w_a, b_a, w_i, b_i, lam, h0, reverse):
    f32 = jnp.float32
    b, n, _ = x.shape
    xb = x.reshape(b, n, LRU_BLOCKS, LRU_BLOCK)
    r = jax.nn.sigmoid((jnp.einsum('bnki,kij->bnkj', xb, w_a).reshape(b, n, LRU_WIDTH) + b_a).astype(f32))
    i = jax.nn.sigmoid((jnp.einsum('bnki,kij->bnkj', xb, w_i).reshape(b, n, LRU_WIDTH) + b_i).astype(f32))
    log_a = -LRU_C * r * jax.nn.softplus(-lam.astype(f32))
    a = jnp.exp(log_a)
    bx = jnp.sqrt(-jnp.expm1(2.0 * log_a)) * i * x.astype(f32)
    if h0 is not None:
        t0 = -1 if reverse else 0
        bx = bx.at[:, t0].add(a[:, t0] * h0.astype(f32))
    _, h = lax.associative_scan(linear_recurrence_combine, (a, bx), reverse=reverse, axis=1)
    return h


def rglru_mixer(x, gate, p, h0s):
    xc = centred_depthwise_conv(x, p['lru_conv_w'], p['lru_conv_b'])
    h_f = rglru_scan(xc, p['lru_w_a'][0], p['lru_b_a'][0], p['lru_w_i'][0], p['lru_b_i'][0],
                     p['lru_lambda'][0], h0s[0], False)
    h_b = rglru_scan(xc, p['lru_w_a'][1], p['lru_b_a'][1], p['lru_w_i'][1], p['lru_b_i'][1],
                     p['lru_lambda'][1], h0s[1], True)
    out = (h_f + h_b) * jax.nn.gelu(gate.astype(jnp.float32))
    return out.astype(x.dtype), h_f, h_b


def token_mixers(h, p, lambda_init, ctx):
    b, n, _ = h.shape
    f32 = jnp.float32
    proj = h @ p['w_in']
    c0 = 4 * QK_WIDTH + ATT_WIDTH
    cuts = [QK_WIDTH, 2 * QK_WIDTH, 3 * QK_WIDTH, 4 * QK_WIDTH, c0, c0 + S5_WIDTH, c0 + S5_WIDTH + LRU_WIDTH]
    q1, q2, k1, k2, v, u_s5, x_lru, g_lru = jnp.split(proj, cuts, axis=-1)
    q1, q2, k1, k2 = [t.reshape(b, n, ATT_HEADS, ATT_QK).transpose(0, 2, 1, 3) for t in (q1, q2, k1, k2)]
    v = v.reshape(b, n, ATT_HEADS, ATT_V).transpose(0, 2, 1, 3)
    lam = (jnp.exp(jnp.sum(p['lam_q1'].astype(f32) * p['lam_k1'].astype(f32)))
           - jnp.exp(jnp.sum(p['lam_q2'].astype(f32) * p['lam_k2'].astype(f32))) + lambda_init)
    if ctx is None:
        kk1, kk2, vv = k1, k2, v
        h0_s5 = (None, None)
        h0_lru = (None, None)
    else:
        ck, cv, cs5, clru = ctx
        cos, sin = axial_rope_tables(n)
        q1, q2, k1, k2 = [apply_axial_rope(t, cos, sin) for t in (q1, q2, k1, k2)]
        kk1 = jnp.concatenate([k1, ck[..., :ATT_QK]], axis=2)
        kk2 = jnp.concatenate([k2, ck[..., ATT_QK:]], axis=2)
        vv = jnp.concatenate([v, cv], axis=2)
        cs5 = cs5.astype(f32)
        h0_s5 = (lax.complex(cs5[:, 0, 0], cs5[:, 0, 1]), lax.complex(cs5[:, 1, 0], cs5[:, 1, 1]))
        h0_lru = (clru[:, 0], clru[:, 1])
    o = diff_attention(q1, q2, kk1, kk2, vv, lam)
    o = rms_norm(o, p['subln_g']) * (1.0 - lambda_init)
    o_att = o.transpose(0, 2, 1, 3).reshape(b, n, ATT_WIDTH)
    o_s5, hs_f, hs_b = s5_mixer(u_s5, p, h0_s5)
    o_lru, hl_f, hl_b = rglru_mixer(x_lru, g_lru, p, h0_lru)
    out = jnp.concatenate([o_att, o_s5, o_lru], axis=-1) @ p['w_out']
    if ctx is None:
        k_cache = jnp.concatenate([k1, k2], axis=-1)
        s5_fin_f = jnp.stack([hs_f[:, -1].real, hs_f[:, -1].imag], axis=1)
        s5_fin_b = jnp.stack([hs_b[:, 0].real, hs_b[:, 0].imag], axis=1)
        s5_state = jnp.stack([s5_fin_f, s5_fin_b], axis=1).astype(h.dtype)
        lru_state = jnp.stack([hl_f[:, -1], hl_b[:, 0]], axis=1).astype(h.dtype)
        return out, (k_cache, v, s5_state, lru_state)
    return out, None


def hier_moe(h, p):
    f32 = jnp.float32
    b, n, _ = h.shape
    g_logits = (h @ p['router_group_w'] + p['router_group_b']).astype(f32)
    g_sel = jnp.argmax(g_logits, axis=-1)
    p_group = jnp.take_along_axis(jax.nn.softmax(g_logits, axis=-1), g_sel[..., None], axis=-1)
    e_logits = (h @ p['router_expert_w'] + p['router_expert_b']).astype(f32)
    e_logits = e_logits.reshape(b, n, N_GROUPS, EXPERTS_PER_GROUP)
    e_in = jnp.take_along_axis(e_logits, g_sel[..., None, None], axis=2)[:, :, 0]
    top_v, top_i = lax.top_k(e_in, TOP_K)
    top_w = jax.nn.softmax(top_v, axis=-1) * p_group
    eid = g_sel[..., None] * EXPERTS_PER_GROUP + top_i
    gates = jnp.einsum('bnke,bnk->bne', jax.nn.one_hot(eid, N_EXPERTS, dtype=f32), top_w)
    a = jnp.einsum('bnd,edf->bnef', h, p['moe_w_gate'])
    u = jnp.einsum('bnd,edf->bnef', h, p['moe_w_up'])
    act = jax.nn.silu(a) * u * gates[..., None].astype(h.dtype)
    return jnp.einsum('bnef,efd->bnd', act, p['moe_w_down'])


def trunk_layer(x, cond, p, layer_idx, ctx):
    mod = jax.nn.silu(cond) @ p['w_ada'] + p['b_ada']
    sh1, sc1, g1, sh2, sc2, g2 = [m[:, None, :] for m in jnp.split(mod, N_MOD, axis=-1)]
    lambda_init = 0.8 - 0.6 * math.exp(-0.3 * layer_idx)
    mix, ctx_out = token_mixers(x * (1.0 + sc1) + sh1, p, lambda_init, ctx)
    x = layer_norm(DEEPNORM_ALPHA * x + g1 * mix, p['ln1_g'], p['ln1_b'])
    ffn = hier_moe(x * (1.0 + sc2) + sh2, p)
    x = layer_norm(DEEPNORM_ALPHA * x + g2 * ffn, p['ln2_g'], p['ln2_b'])
    return x, ctx_out


def setup_inputs(seed: int = 0) -> dict:
    key = jax.random.key(seed)
    ks = jax.random.split(key, 48)
    f32 = jnp.float32

    def nrm(i, shape, s):
        return jax.random.normal(ks[i], shape, f32) * s

    d = D_MODEL
    u_lru = jax.random.uniform(ks[33], (DEPTH, 2, LRU_WIDTH), f32, 0.9, 0.999)
    s_lru = u_lru ** (1.0 / LRU_C)
    return {
        'x_prompt': nrm(0, (BATCH, SEQ, d), 1.0),
        'x_sample': nrm(1, (DEC_BATCH, DEC_SEQ, d), 1.0),
        'cache_k': nrm(2, (DEC_BATCH, DEPTH, ATT_HEADS, PAST_LEN, 2 * ATT_QK), 1.0),
        'cache_v': nrm(3, (DEC_BATCH, DEPTH, ATT_HEADS, PAST_LEN, ATT_V), 1.0),
        'state_s5': nrm(4, (DEC_BATCH, DEPTH, 2, 2, S5_GROUPS, S5_STATE), 0.5),
        'state_lru': nrm(5, (DEC_BATCH, DEPTH, 2, LRU_WIDTH), 0.5),
        'c': nrm(6, (DEC_BATCH, d), 1.0),
        'c_ctx': nrm(7, (d,), 1.0),
        'w_ada': nrm(8, (DEPTH, d, N_MOD * d), d ** -0.5),
        'b_ada': nrm(9, (DEPTH, N_MOD * d), 0.01),
        'w_in': nrm(10, (DEPTH, d, IN_WIDTH), d ** -0.5),
        'w_out': nrm(11, (DEPTH, MIX_WIDTH, d), MIX_WIDTH ** -0.5 * DEEPNORM_BETA),
        'lam_q1': nrm(12, (DEPTH, ATT_QK), 0.1),
        'lam_k1': nrm(13, (DEPTH, ATT_QK), 0.1),
        'lam_q2': nrm(14, (DEPTH, ATT_QK), 0.1),
        'lam_k2': nrm(15, (DEPTH, ATT_QK), 0.1),
        'subln_g': 1.0 + nrm(16, (DEPTH, ATT_V), 0.01),
        's5_a_re': -0.5 * jnp.exp(nrm(17, (DEPTH, 2, S5_GROUPS, S5_STATE), 0.05)),
        's5_a_im': jnp.pi * jnp.arange(S5_STATE, dtype=f32) + nrm(18, (DEPTH, 2, S5_GROUPS, S5_STATE), 0.05),
        's5_b_re': nrm(19, (DEPTH, 2, S5_GROUPS, S5_STATE, S5_GROUP), (2 * S5_GROUP) ** -0.5),
        's5_b_im': nrm(20, (DEPTH, 2, S5_GROUPS, S5_STATE, S5_GROUP), (2 * S5_GROUP) ** -0.5),
        's5_c_re': nrm(21, (DEPTH, 2, S5_GROUPS, S5_GROUP, S5_STATE), (2 * S5_STATE) ** -0.5),
        's5_c_im': nrm(22, (DEPTH, 2, S5_GROUPS, S5_GROUP, S5_STATE), (2 * S5_STATE) ** -0.5),
        's5_log_dt': jax.random.uniform(ks[23], (DEPTH, 2, S5_GROUPS), f32, math.log(1e-3), math.log(1e-1)),
        's5_d': nrm(24, (DEPTH, S5_WIDTH), 1.0),
        's5_w_glu': nrm(25, (DEPTH, S5_WIDTH, S5_WIDTH), S5_WIDTH ** -0.5),
        's5_b_glu': nrm(26, (DEPTH, S5_WIDTH), 0.01),
        'lru_conv_w': nrm(27, (DEPTH, LRU_CONV, LRU_WIDTH), LRU_CONV ** -0.5),
        'lru_conv_b': nrm(28, (DEPTH, LRU_WIDTH), 0.01),
        'lru_w_a': nrm(29, (DEPTH, 2, LRU_BLOCKS, LRU_BLOCK, LRU_BLOCK), LRU_BLOCK ** -0.5),
        'lru_b_a': nrm(30, (DEPTH, 2, LRU_WIDTH), 0.01),
        'lru_w_i': nrm(31, (DEPTH, 2, LRU_BLOCKS, LRU_BLOCK, LRU_BLOCK), LRU_BLOCK ** -0.5),
        'lru_b_i': nrm(32, (DEPTH, 2, LRU_WIDTH), 0.01),
        'lru_lambda': jnp.log(s_lru) - jnp.log1p(-s_lru),
        'ln1_g': 1.0 + nrm(34, (DEPTH, d), 0.01),
        'ln1_b': nrm(35, (DEPTH, d), 0.01),
        'ln2_g': 1.0 + nrm(36, (DEPTH, d), 0.01),
        'ln2_b': nrm(37, (DEPTH, d), 0.01),
        'router_group_w': nrm(38, (DEPTH, d, N_GROUPS), d ** -0.5),
        'router_group_b': nrm(39, (DEPTH, N_GROUPS), 0.01),
        'router_expert_w': nrm(40, (DEPTH, d, N_EXPERTS), d ** -0.5),
        'router_expert_b': nrm(41, (DEPTH, N_EXPERTS), 0.01),
        'moe_w_gate': nrm(42, (DEPTH, N_EXPERTS, d, D_EXPERT), d ** -0.5),
        'moe_w_up': nrm(43, (DEPTH, N_EXPERTS, d, D_EXPERT), d ** -0.5),
        'moe_w_down': nrm(44, (DEPTH, N_EXPERTS, D_EXPERT, d), D_EXPERT ** -0.5 * DEEPNORM_BETA),
    }


def reference(x_prompt, x_sample, cache_k, cache_v, state_s5, state_lru, c, c_ctx,
              w_ada, b_ada, w_in, w_out, lam_q1, lam_k1, lam_q2, lam_k2, subln_g,
              s5_a_re, s5_a_im, s5_b_re, s5_b_im, s5_c_re, s5_c_im, s5_log_dt, s5_d,
              s5_w_glu, s5_b_glu, lru_conv_w, lru_conv_b, lru_w_a, lru_b_a, lru_w_i, lru_b_i,
              lru_lambda, ln1_g, ln1_b, ln2_g, ln2_b, router_group_w, router_group_b,
              router_expert_w, router_expert_b, moe_w_gate, moe_w_up, moe_w_down):
    y_p = x_prompt
    y_s = x_sample
    cond_ctx = c_ctx[None, :]
    k_list, v_list, s5_list, lru_list = [], [], [], []
    for l in range(DEPTH):
        p = {
            'w_ada': w_ada[l], 'b_ada': b_ada[l], 'w_in': w_in[l], 'w_out': w_out[l],
            'lam_q1': lam_q1[l], 'lam_k1': lam_k1[l], 'lam_q2': lam_q2[l], 'lam_k2': lam_k2[l],
            'subln_g': subln_g[l],
            's5_a_re': s5_a_re[l], 's5_a_im': s5_a_im[l], 's5_b_re': s5_b_re[l], 's5_b_im': s5_b_im[l],
            's5_c_re': s5_c_re[l], 's5_c_im': s5_c_im[l], 's5_log_dt': s5_log_dt[l], 's5_d': s5_d[l],
            's5_w_glu': s5_w_glu[l], 's5_b_glu': s5_b_glu[l],
            'lru_conv_w': lru_conv_w[l], 'lru_conv_b': lru_conv_b[l], 'lru_w_a': lru_w_a[l],
            'lru_b_a': lru_b_a[l], 'lru_w_i': lru_w_i[l], 'lru_b_i': lru_b_i[l], 'lru_lambda': lru_lambda[l],
            'ln1_g': ln1_g[l], 'ln1_b': ln1_b[l], 'ln2_g': ln2_g[l], 'ln2_b': ln2_b[l],
            'router_group_w': router_group_w[l], 'router_group_b': router_group_b[l],
            'router_expert_w': router_expert_w[l], 'router_expert_b': router_expert_b[l],
            'moe_w_gate': moe_w_gate[l], 'moe_w_up': moe_w_up[l], 'moe_w_down': moe_w_down[l],
        }
        y_p, ctx_new = trunk_layer(y_p, cond_ctx, p, l, None)
        k_list.append(ctx_new[0])
        v_list.append(ctx_new[1])
        s5_list.append(ctx_new[2])
        lru_list.append(ctx_new[3])
        ctx_cached = (cache_k[:, l], cache_v[:, l], state_s5[:, l], state_lru[:, l])
        y_s, _ = trunk_layer(y_s, c, p, l, ctx_cached)
    new_cache_k = jnp.stack(k_list, axis=1)
    new_cache_v = jnp.stack(v_list, axis=1)
    new_state_s5 = jnp.stack(s5_list, axis=1)
    new_state_lru = jnp.stack(lru_list, axis=1)
    return (y_p, y_s, new_cache_k, new_cache_v, new_state_s5, new_state_lru)
```

```python
import functools
import math

import jax
import jax.numpy as jnp
from jax import lax
from jax.experimental import pallas as pl
from jax.experimental.pallas import tpu as pltpu

F32 = jnp.float32
BF16 = jnp.bfloat16

D = 1024
DEPTH = 4
CTX_B = 16
CTX_T = 256
LAT_B = 2
LAT_T = 2048
PAST = 256
GRID_W = 64
HEADS = 4
QK = 64
DV = 128
S5_W = 256
S5_G = 16
S5_C = 16
S5_P = 64
S5_N = S5_G * S5_P
LRU_W = 256
LRU_BLOCKS = 4
LRU_C = 8.0
N_GROUPS = 4
EPG = 4
N_EXPERTS = 16
D_EXPERT = 512
N_MOD = 6
IN_W = 2304
ROPE_BASE = 10000.0
ALPHA = (2 * DEPTH) ** 0.25
EPS = 1e-5

TILE = 256
N_TOK = CTX_B * CTX_T + LAT_B * LAT_T
N_SEQ = N_TOK // TILE
N_CTX = CTX_B
CHUNKS = LAT_T // TILE
N_LAT = LAT_B * CHUNKS
MOD_ROWS = 8
GATE_LANE0 = 4
VMEM_LIMIT = 56 * 1024 * 1024


def _cparams(sem):
    return pltpu.CompilerParams(dimension_semantics=sem, vmem_limit_bytes=VMEM_LIMIT)


def _mod_row(i):
    return jnp.where(i < N_CTX, 0, 1 + (i - N_CTX) // CHUNKS)


def _ada_kernel(c_ref, w_ref, b_ref, o_ref):
    c = c_ref[...]
    s = (c * jax.nn.sigmoid(c)).astype(BF16)
    o_ref[0] = jnp.dot(s, w_ref[0].astype(BF16), preferred_element_type=F32) + b_ref[0]


def _ada_call(cond, w_ada, b_ada):
    tn = 1536
    return pl.pallas_call(
        _ada_kernel,
        grid=(DEPTH, N_MOD * D // tn),
        in_specs=[
            pl.BlockSpec((MOD_ROWS, D), lambda l, j: (0, 0)),
            pl.BlockSpec((1, D, tn), lambda l, j: (l, 0, j)),
            pl.BlockSpec((1, 1, tn), lambda l, j: (l, 0, j)),
        ],
        out_specs=pl.BlockSpec((1, MOD_ROWS, tn), lambda l, j: (l, 0, j)),
        out_shape=jax.ShapeDtypeStruct((DEPTH, MOD_ROWS, N_MOD * D), F32),
        compiler_params=_cparams(("arbitrary", "arbitrary")),
        name="adaln",
    )(cond, w_ada, b_ada.reshape(DEPTH, 1, N_MOD * D))


def _inproj_kernel(x_ref, xp_ref, xn_ref, mod_ref, w_ref, cos_ref, sin_ref, cw_ref, cb_ref,
                   kc_in, vc_in, q_ref, k_ref, v_ref, kc_ref, vc_ref, u_ref, xc_ref, g_ref):
    del kc_in, vc_in
    i = pl.program_id(0)
    r = _mod_row(i)
    sh = mod_ref[0, pl.ds(r, 1), 0:D]
    sc = mod_ref[0, pl.ds(r, 1), D:2 * D]
    xm = (x_ref[...] * (1.0 + sc) + sh).astype(BF16)
    proj = jnp.dot(xm, w_ref[0], preferred_element_type=F32)

    qk = proj[:, 0:1024]
    cos = jnp.concatenate([cos_ref[...]] * 8, axis=1)
    sin = jnp.concatenate([sin_ref[...]] * 8, axis=1)
    lane = lax.broadcasted_iota(jnp.int32, qk.shape, 1)
    swapped = jnp.where((lane & 31) < 16, pltpu.roll(qk, 1024 - 16, 1), pltpu.roll(qk, 16, 1))
    qk = qk * cos + swapped * sin
    q_ref[...] = qk[:, 0:512].astype(BF16)
    k_ref[...] = qk[:, 512:1024].astype(BF16)
    v = proj[:, 1024:1536]
    v_ref[...] = v.astype(BF16)

    @pl.when(i < N_CTX)
    def _():
        for h in range(HEADS):
            kc_ref[0, 0, h] = qk[:, 512 + h * 128:512 + (h + 1) * 128]
            vc_ref[0, 0, h] = v[:, h * 128:(h + 1) * 128]

    u_ref[...] = proj[:, 1536:1792]
    g_ref[...] = proj[:, 2048:2304]

    xl = proj[:, 1792:2048]
    halo_x = jnp.concatenate([xp_ref[...], xn_ref[...]], axis=0)
    halo_m = (halo_x * (1.0 + sc) + sh).astype(BF16)
    halo = jnp.dot(halo_m, w_ref[0, :, 1792:2048], preferred_element_type=F32)
    j = (i - N_CTX) % CHUNKS
    has_prev = jnp.logical_and(i >= N_CTX, j > 0)
    has_next = jnp.logical_and(i >= N_CTX, j < CHUNKS - 1)
    h_prev = jnp.where(has_prev, halo[7:8], 0.0)
    h_n0 = jnp.where(has_next, halo[8:9], 0.0)
    h_n1 = jnp.where(has_next, halo[9:10], 0.0)
    row = lax.broadcasted_iota(jnp.int32, xl.shape, 0)
    x_m1 = jnp.where(row == 0, h_prev, pltpu.roll(xl, 1, 0))
    x_p1 = jnp.where(row == TILE - 1, h_n0, pltpu.roll(xl, TILE - 1, 0))
    x_p2 = jnp.where(row == TILE - 2, h_n0,
                     jnp.where(row == TILE - 1, h_n1, pltpu.roll(xl, TILE - 2, 0)))
    cw = cw_ref[0]
    xc_ref[...] = (cb_ref[0] + x_m1 * cw[0:1] + xl * cw[1:2] + x_p1 * cw[2:3] + x_p2 * cw[3:4])


def _inproj_call(layer, x, mod, w_in_p, rope_cos, rope_sin, conv_w, conv_b, kc, vc):
    def rope_idx(i):
        return (jnp.where(i < N_CTX, CHUNKS, (i - N_CTX) % CHUNKS), 0)

    cache_spec = pl.BlockSpec((1, 1, HEADS, TILE, 128),
                              lambda i: (jnp.minimum(i, N_CTX - 1), layer, 0, 0, 0))
    return pl.pallas_call(
        _inproj_kernel,
        grid=(N_SEQ,),
        in_specs=[
            pl.BlockSpec((TILE, D), lambda i: (i, 0)),
            pl.BlockSpec((8, D), lambda i: (jnp.maximum(i * (TILE // 8) - 1, 0), 0)),
            pl.BlockSpec((8, D), lambda i: (jnp.minimum((i + 1) * (TILE // 8), N_TOK // 8 - 1), 0)),
            pl.BlockSpec((1, MOD_ROWS, N_MOD * D), lambda i: (layer, 0, 0)),
            pl.BlockSpec((1, D, IN_W), lambda i: (layer, 0, 0)),
            pl.BlockSpec((TILE, 128), rope_idx),
            pl.BlockSpec((TILE, 128), rope_idx),
            pl.BlockSpec((1, 4, LRU_W), lambda i: (layer, 0, 0)),
            pl.BlockSpec((1, 1, LRU_W), lambda i: (layer, 0, 0)),
            pl.BlockSpec(memory_space=pl.ANY),
            pl.BlockSpec(memory_space=pl.ANY),
        ],
        out_specs=[
            pl.BlockSpec((TILE, 512), lambda i: (i, 0)),
            pl.BlockSpec((TILE, 512), lambda i: (i, 0)),
            pl.BlockSpec((TILE, 512), lambda i: (i, 0)),
            cache_spec,
            cache_spec,
            pl.BlockSpec((TILE, S5_W), lambda i: (0, i)),
            pl.BlockSpec((TILE, LRU_W), lambda i: (0, i)),
            pl.BlockSpec((TILE, LRU_W), lambda i: (i, 0)),
        ],
        out_shape=[
            jax.ShapeDtypeStruct((N_TOK, 512), BF16),
            jax.ShapeDtypeStruct((N_TOK, 512), BF16),
            jax.ShapeDtypeStruct((N_TOK, 512), BF16),
            jax.ShapeDtypeStruct(kc.shape, F32),
            jax.ShapeDtypeStruct(vc.shape, F32),
            jax.ShapeDtypeStruct((TILE, N_SEQ * S5_W), F32),
            jax.ShapeDtypeStruct((TILE, N_SEQ * LRU_W), F32),
            jax.ShapeDtypeStruct((N_TOK, LRU_W), F32),
        ],
        input_output_aliases={9: 3, 10: 4},
        compiler_params=_cparams(("arbitrary",)),
        name="inproj",
    )(x, x, x, mod, w_in_p, rope_cos, rope_sin, conv_w, conv_b.reshape(DEPTH, 1, LRU_W), kc, vc)


def _softmax_parts(parts):
    m = parts[0].max(axis=-1, keepdims=True)
    for p in parts[1:]:
        m = jnp.maximum(m, p.max(axis=-1, keepdims=True))
    es = [jnp.exp(p - m) for p in parts]
    z = es[0].sum(axis=-1, keepdims=True)
    for e in es[1:]:
        z = z + e.sum(axis=-1, keepdims=True)
    inv = 1.0 / z
    return [e * inv for e in es]


_NT = (((1,), (1,)), ((), ()))


def _split_maps(q):
    lane = lax.broadcasted_iota(jnp.int32, q.shape, 1)
    zero = jnp.zeros_like(q)
    return jnp.where(lane < QK, q, zero), jnp.where(lane >= QK, q, zero)


def _subln(o, g, post_scale):
    ms = jnp.mean(o * o, axis=-1, keepdims=True)
    return (o * lax.rsqrt(ms + EPS) * g) * post_scale


def _att_ctx_kernel(lam_ref, q_ref, k_ref, v_ref, g_ref, o_ref, *, post_scale):
    lam = lam_ref[0]
    q1, q2 = _split_maps(q_ref[...])
    k = k_ref[...]
    s1 = lax.dot_general(q1, k, _NT, preferred_element_type=F32) * (QK ** -0.5)
    s2 = lax.dot_general(q2, k, _NT, preferred_element_type=F32) * (QK ** -0.5)
    (p1,) = _softmax_parts([s1])
    (p2,) = _softmax_parts([s2])
    w = (p1 - lam * p2).astype(BF16)
    o = jnp.dot(w, v_ref[...], preferred_element_type=F32)
    o_ref[...] = _subln(o, g_ref[0], post_scale).astype(BF16)


def _att_ctx_call(layer, lam, q, k, v, subln_g, post_scale):
    blk = pl.BlockSpec((TILE, 128), lambda b, h: (b, h))
    return pl.pallas_call(
        functools.partial(_att_ctx_kernel, post_scale=post_scale),
        grid=(CTX_B, HEADS),
        in_specs=[
            pl.BlockSpec(memory_space=pltpu.SMEM),
            blk, blk, blk,
            pl.BlockSpec((1, 1, DV), lambda b, h: (layer, 0, 0)),
        ],
        out_specs=blk,
        out_shape=jax.ShapeDtypeStruct((CTX_B * CTX_T, HEADS * DV), BF16),
        compiler_params=_cparams(("arbitrary", "arbitrary")),
        name="att_ctx",
    )(lam, q, k, v, subln_g.reshape(DEPTH, 1, DV))


def _att_lat_kernel(lam_ref, q_ref, k_ref, v_ref, ck_ref, cv_ref, g_ref, o_ref, *, post_scale):
    lam = lam_ref[0]
    q1, q2 = _split_maps(q_ref[...])
    k = k_ref[...]
    ck = ck_ref[...].astype(BF16)
    scale = QK ** -0.5
    s1l = lax.dot_general(q1, k, _NT, preferred_element_type=F32) * scale
    s1c = lax.dot_general(q1, ck, _NT, preferred_element_type=F32) * scale
    p1l, p1c = _softmax_parts([s1l, s1c])
    s2l = lax.dot_general(q2, k, _NT, preferred_element_type=F32) * scale
    s2c = lax.dot_general(q2, ck, _NT, preferred_element_type=F32) * scale
    p2l, p2c = _softmax_parts([s2l, s2c])
    wl = (p1l - lam * p2l).astype(BF16)
    wc = (p1c - lam * p2c).astype(BF16)
    o = (jnp.dot(wl, v_ref[...], preferred_element_type=F32)
         + jnp.dot(wc, cv_ref[...].astype(BF16), preferred_element_type=F32))
    o_ref[...] = _subln(o, g_ref[0], post_scale).astype(BF16)


def _att_lat_call(layer, lam, q, k, v, cache_k, cache_v, subln_g, post_scale):
    lat0 = N_CTX
    cache_spec = pl.BlockSpec((None, None, None, PAST, 128), lambda b, h, t: (b, layer, h, 0, 0))
    return pl.pallas_call(
        functools.partial(_att_lat_kernel, post_scale=post_scale),
        grid=(LAT_B, HEADS, CHUNKS),
        in_specs=[
            pl.BlockSpec(memory_space=pltpu.SMEM),
            pl.BlockSpec((TILE, 128), lambda b, h, t: (lat0 + b * CHUNKS + t, h)),
            pl.BlockSpec((LAT_T, 128), lambda b, h, t: (lat0 // CHUNKS + b, h)),
            pl.BlockSpec((LAT_T, 128), lambda b, h, t: (lat0 // CHUNKS + b, h)),
            cache_spec, cache_spec,
            pl.BlockSpec((1, 1, DV), lambda b, h, t: (layer, 0, 0)),
        ],
        out_specs=pl.BlockSpec((TILE, 128), lambda b, h, t: (b * CHUNKS + t, h)),
        out_shape=jax.ShapeDtypeStruct((LAT_B * LAT_T, HEADS * DV), BF16),
        compiler_params=_cparams(("arbitrary", "arbitrary", "arbitrary")),
        name="att_lat",
    )(lam, q, k, v, cache_k, cache_v, subln_g.reshape(DEPTH, 1, DV))


S5_TB = 32
S5_LANES = 128


def _cmul(ar, ai, br, bi):
    return ar * br - ai * bi, ar * bi + ai * br


def _s5_kernel(uf_ref, ub_ref, bf_ref, bb_ref, cf_ref, cb_ref, a_ref, yf_ref, yb_ref, hend_ref,
               hf_s, hb_s, buf_f, buf_b):
    i = pl.program_id(0)

    @pl.when(i == 0)
    def _():
        hf_s[...] = jnp.zeros_like(hf_s)
        hb_s[...] = jnp.zeros_like(hb_s)

    buf_f[...] = jnp.dot(uf_ref[...].astype(BF16), bf_ref[0, 0], preferred_element_type=F32)
    buf_b[...] = jnp.dot(ub_ref[...].astype(BF16), bb_ref[0, 0], preferred_element_type=F32)

    for c in range(S5_N // S5_LANES):
        re = slice(c * S5_LANES, (c + 1) * S5_LANES)
        im = slice(S5_N + c * S5_LANES, S5_N + (c + 1) * S5_LANES)
        afr = a_ref[0, 0:1, re]
        afi = a_ref[0, 0:1, im]
        abr = a_ref[0, 1:2, re]
        abi = a_ref[0, 1:2, im]

        def body(t, carry, re=re, im=im, afr=afr, afi=afi, abr=abr, abi=abi):
            hfr, hfi, hbr, hbi = carry
            rf = pl.multiple_of(t * N_SEQ, N_SEQ)
            rb = pl.multiple_of((S5_TB - 1 - t) * N_SEQ, N_SEQ)
            pr, pi = _cmul(afr, afi, hfr, hfi)
            hfr = pr + buf_f[pl.ds(rf, N_SEQ), re]
            hfi = pi + buf_f[pl.ds(rf, N_SEQ), im]
            buf_f[pl.ds(rf, N_SEQ), re] = hfr
            buf_f[pl.ds(rf, N_SEQ), im] = hfi
            pr, pi = _cmul(abr, abi, hbr, hbi)
            hbr = pr + buf_b[pl.ds(rb, N_SEQ), re]
            hbi = pi + buf_b[pl.ds(rb, N_SEQ), im]
            buf_b[pl.ds(rb, N_SEQ), re] = hbr
            buf_b[pl.ds(rb, N_SEQ), im] = hbi
            return hfr, hfi, hbr, hbi

        out = lax.fori_loop(0, S5_TB, body, (hf_s[:, re], hf_s[:, im], hb_s[:, re], hb_s[:, im]))
        hf_s[:, re] = out[0]
        hf_s[:, im] = out[1]
        hb_s[:, re] = out[2]
        hb_s[:, im] = out[3]

    yf_ref[...] = jnp.dot(buf_f[...].astype(BF16), cf_ref[0, 0], preferred_element_type=F32)
    yb_ref[...] = jnp.dot(buf_b[...].astype(BF16), cb_ref[0, 0], preferred_element_type=F32)

    @pl.when(i == pl.num_programs(0) - 1)
    def _():
        hend_ref[0] = hf_s[...]
        hend_ref[1] = hb_s[...]


def _s5_call(layer, u2, b_blk, c_blk, a_bar):
    nb = CTX_T // S5_TB
    rows = S5_TB * N_SEQ
    return pl.pallas_call(
        _s5_kernel,
        grid=(nb,),
        in_specs=[
            pl.BlockSpec((rows, S5_W), lambda i: (i, 0)),
            pl.BlockSpec((rows, S5_W), lambda i: (nb - 1 - i, 0)),
            pl.BlockSpec((1, 1, S5_W, 2 * S5_N), lambda i: (layer, 0, 0, 0)),
            pl.BlockSpec((1, 1, S5_W, 2 * S5_N), lambda i: (layer, 1, 0, 0)),
            pl.BlockSpec((1, 1, 2 * S5_N, S5_W), lambda i: (layer, 0, 0, 0)),
            pl.BlockSpec((1, 1, 2 * S5_N, S5_W), lambda i: (layer, 1, 0, 0)),
            pl.BlockSpec((1, 2, 2 * S5_N), lambda i: (layer, 0, 0)),
        ],
        out_specs=[
            pl.BlockSpec((rows, S5_W), lambda i: (i, 0)),
            pl.BlockSpec((rows, S5_W), lambda i: (nb - 1 - i, 0)),
            pl.BlockSpec((2, N_SEQ, 2 * S5_N), lambda i: (0, 0, 0)),
        ],
        out_shape=[
            jax.ShapeDtypeStruct((N_TOK, S5_W), F32),
            jax.ShapeDtypeStruct((N_TOK, S5_W), F32),
            jax.ShapeDtypeStruct((2, N_SEQ, 2 * S5_N), F32),
        ],
        scratch_shapes=[
            pltpu.VMEM((N_SEQ, 2 * S5_N), F32),
            pltpu.VMEM((N_SEQ, 2 * S5_N), F32),
            pltpu.VMEM((rows, 2 * S5_N), F32),
            pltpu.VMEM((rows, 2 * S5_N), F32),
        ],
        compiler_params=_cparams(("arbitrary",)),
        name="s5_scan",
    )(u2, u2, b_blk, b_blk, c_blk, c_blk, a_bar)


def _s5_fix_kernel(hend_ref, h0_ref, a_ref, cf_ref, cb_ref, o_ref, pf_s, pb_s, cf_s, cb_s):
    s = pl.program_id(0)

    @pl.when(s == 0)
    def _():
        row8 = lax.broadcasted_iota(jnp.int32, (8, S5_N), 0)
        for d, tab in ((0, pf_s), (1, pb_s)):
            ar = a_ref[0, d:d + 1, 0:S5_N]
            ai = a_ref[0, d:d + 1, S5_N:2 * S5_N]
            pr, pi = ar, ai
            r8 = jnp.zeros((8, S5_N), F32)
            i8 = jnp.zeros((8, S5_N), F32)
            for r in range(8):
                if r:
                    pr, pi = _cmul(pr, pi, ar, ai)
                at = r if d == 0 else 7 - r
                r8 = jnp.where(row8 == at, pr, r8)
                i8 = jnp.where(row8 == at, pi, i8)
            base = 0 if d == 0 else TILE - 8
            tab[base:base + 8, 0:S5_N] = r8
            tab[base:base + 8, S5_N:2 * S5_N] = i8
            m = 8
            while m < TILE:
                if d == 0:
                    src, dst, top = slice(0, m), slice(m, 2 * m), slice(m - 1, m)
                else:
                    src, dst, top = slice(TILE - m, TILE), slice(TILE - 2 * m, TILE - m), slice(TILE - m, TILE - m + 1)
                mr, mi = tab[top, 0:S5_N], tab[top, S5_N:2 * S5_N]
                nr, ni = _cmul(tab[src, 0:S5_N], tab[src, S5_N:2 * S5_N], mr, mi)
                tab[dst, 0:S5_N] = nr
                tab[dst, S5_N:2 * S5_N] = ni
                m *= 2

        for b in range(LAT_B):
            ar, ai = pf_s[TILE - 1:TILE, 0:S5_N], pf_s[TILE - 1:TILE, S5_N:2 * S5_N]
            cr, ci = h0_ref[b, 0:1, 0:S5_N], h0_ref[b, 0:1, S5_N:2 * S5_N]
            for j in range(CHUNKS):
                row = b * CHUNKS + j
                cf_s[row:row + 1, 0:S5_N] = cr
                cf_s[row:row + 1, S5_N:2 * S5_N] = ci
                pr, pi = _cmul(ar, ai, cr, ci)
                cr = pr + hend_ref[0, N_CTX + row:N_CTX + row + 1, 0:S5_N]
                ci = pi + hend_ref[0, N_CTX + row:N_CTX + row + 1, S5_N:2 * S5_N]
            ar, ai = pb_s[0:1, 0:S5_N], pb_s[0:1, S5_N:2 * S5_N]
            cr, ci = h0_ref[b, 1:2, 0:S5_N], h0_ref[b, 1:2, S5_N:2 * S5_N]
            for j in reversed(range(CHUNKS)):
                row = b * CHUNKS + j
                cb_s[row:row + 1, 0:S5_N] = cr
                cb_s[row:row + 1, S5_N:2 * S5_N] = ci
                pr, pi = _cmul(ar, ai, cr, ci)
                cr = pr + hend_ref[1, N_CTX + row:N_CTX + row + 1, 0:S5_N]
                ci = pi + hend_ref[1, N_CTX + row:N_CTX + row + 1, S5_N:2 * S5_N]

    acc = None
    for tab, car, c_ref in ((pf_s, cf_s, cf_ref), (pb_s, cb_s, cb_ref)):
        cr = car[pl.ds(s, 1), 0:S5_N]
        ci = car[pl.ds(s, 1), S5_N:2 * S5_N]
        hr, hi = _cmul(tab[:, 0:S5_N], tab[:, S5_N:2 * S5_N], cr, ci)
        h = jnp.concatenate([hr, hi], axis=1).astype(BF16)
        y = jnp.dot(h, c_ref[0, 0], preferred_element_type=F32)
        acc = y if acc is None else acc + y
    o_ref[...] = acc


def _s5_fix_call(layer, hend, h0, a_bar, c_blk):
    return pl.pallas_call(
        _s5_fix_kernel,
        grid=(N_LAT,),
        in_specs=[
            pl.BlockSpec((2, N_SEQ, 2 * S5_N), lambda s: (0, 0, 0)),
            pl.BlockSpec((LAT_B, 2, 2 * S5_N), lambda s: (0, 0, 0)),
            pl.BlockSpec((1, 2, 2 * S5_N), lambda s: (layer, 0, 0)),
            pl.BlockSpec((1, 1, 2 * S5_N, S5_W), lambda s: (layer, 0, 0, 0)),
            pl.BlockSpec((1, 1, 2 * S5_N, S5_W), lambda s: (layer, 1, 0, 0)),
        ],
        out_specs=pl.BlockSpec((TILE, S5_W), lambda s: (s, 0)),
        out_shape=jax.ShapeDtypeStruct((N_LAT * TILE, S5_W), F32),
        scratch_shapes=[
            pltpu.VMEM((TILE, 2 * S5_N), F32),
            pltpu.VMEM((TILE, 2 * S5_N), F32),
            pltpu.VMEM((N_LAT, 2 * S5_N), F32),
            pltpu.VMEM((N_LAT, 2 * S5_N), F32),
        ],
        compiler_params=_cparams(("arbitrary",)),
        name="s5_fix",
    )(hend, h0, a_bar, c_blk, c_blk)


LRU_ROWS = 1024


def _lru_kernel(xc_ref, wa_ref, ba_ref, wi_ref, bi_ref, sp_ref, h0_ref, out_ref, hend_ref,
                a_s, h_s, p_s):
    row = lax.broadcasted_iota(jnp.int32, (N_SEQ, LRU_W), 0)
    for d in range(2):
        def gates(cix, _, d=d):
            r0 = pl.multiple_of(cix * LRU_ROWS, LRU_ROWS)
            xc = xc_ref[pl.ds(r0, LRU_ROWS), :]
            xb = xc.astype(BF16)
            r = jax.nn.sigmoid(jnp.dot(xb, wa_ref[0, d], preferred_element_type=F32) + ba_ref[0, d])
            g = jax.nn.sigmoid(jnp.dot(xb, wi_ref[0, d], preferred_element_type=F32) + bi_ref[0, d])
            log_a = (-LRU_C) * r * sp_ref[0, d]
            a_s[pl.ds(r0, LRU_ROWS), :] = jnp.exp(log_a)
            th = jnp.tanh(log_a)
            h_s[pl.ds(r0, LRU_ROWS), :] = jnp.sqrt(-2.0 * th / (1.0 - th)) * g * xc
            return 0

        lax.fori_loop(0, N_TOK // LRU_ROWS, gates, 0)

        def scan(t, carry, d=d):
            h, p = carry
            tt = t if d == 0 else TILE - 1 - t
            r0 = pl.multiple_of(tt * N_SEQ, N_SEQ)
            a = a_s[pl.ds(r0, N_SEQ), :]
            h = a * h + h_s[pl.ds(r0, N_SEQ), :]
            p = a * p
            h_s[pl.ds(r0, N_SEQ), :] = h
            p_s[pl.ds(r0, N_SEQ), :] = p
            return h, p

        h_end, p_end = lax.fori_loop(0, TILE, scan,
                                     (jnp.zeros((N_SEQ, LRU_W), F32), jnp.ones((N_SEQ, LRU_W), F32)))
        hend_ref[d] = h_end

        carry_slab = jnp.zeros((N_SEQ, LRU_W), F32)
        for b in range(LAT_B):
            c = h0_ref[0, b, d:d + 1, :]
            order = range(CHUNKS) if d == 0 else reversed(range(CHUNKS))
            for j in order:
                s = N_CTX + b * CHUNKS + j
                carry_slab = jnp.where(row == s, c, carry_slab)
                c = h_end[s:s + 1, :] + p_end[s:s + 1, :] * c

        def fix(t, _, d=d, carry_slab=carry_slab):
            r0 = pl.multiple_of(t * N_SEQ, N_SEQ)
            v = h_s[pl.ds(r0, N_SEQ), :] + p_s[pl.ds(r0, N_SEQ), :] * carry_slab
            if d == 0:
                out_ref[pl.ds(r0, N_SEQ), :] = v
            else:
                out_ref[pl.ds(r0, N_SEQ), :] += v
            return 0

        lax.fori_loop(0, TILE, fix, 0)


def _lru_call(layer, xc2, wa, ba, wi, bi, sp, h0):
    full = lambda shape: pl.BlockSpec(shape, lambda i: (0,) * len(shape))
    per_layer = lambda shape: pl.BlockSpec((1,) + shape, lambda i: (layer,) + (0,) * len(shape))
    return pl.pallas_call(
        _lru_kernel,
        grid=(1,),
        in_specs=[
            full((N_TOK, LRU_W)),
            per_layer((2, LRU_W, LRU_W)),
            per_layer((2, 1, LRU_W)),
            per_layer((2, LRU_W, LRU_W)),
            per_layer((2, 1, LRU_W)),
            per_layer((2, 1, LRU_W)),
            per_layer((LAT_B, 2, LRU_W)),
        ],
        out_specs=[full((N_TOK, LRU_W)), full((2, N_SEQ, LRU_W))],
        out_shape=[
            jax.ShapeDtypeStruct((N_TOK, LRU_W), F32),
            jax.ShapeDtypeStruct((2, N_SEQ, LRU_W), F32),
        ],
        scratch_shapes=[pltpu.VMEM((N_TOK, LRU_W), F32)] * 3,
        compiler_params=_cparams(("arbitrary",)),
        name="rglru",
    )(xc2, wa, ba, wi, bi, sp, h0)


def _layer_norm(z, g, b):
    mu = jnp.mean(z, axis=-1, keepdims=True)
    zc = z - mu
    var = jnp.mean(zc * zc, axis=-1, keepdims=True)
    return zc * lax.rsqrt(var + EPS) * g + b


def _route(logits):
    lane_i = lax.broadcasted_iota(jnp.int32, logits.shape, 1)
    lane = lane_i.astype(F32)
    big = jnp.float32(1024.0)
    neg = jnp.float32(-jnp.inf)
    is_g = lane_i < N_GROUPS
    gmax = jnp.max(jnp.where(is_g, logits, neg), axis=-1, keepdims=True)
    g_sel = jnp.min(jnp.where(jnp.logical_and(is_g, logits == gmax), lane, big), axis=-1, keepdims=True)
    p_group = 1.0 / jnp.sum(jnp.where(is_g, jnp.exp(logits - gmax), 0.0), axis=-1, keepdims=True)
    e_idx = lane_i - GATE_LANE0
    e_group = (e_idx >> 2).astype(F32)
    in_g = jnp.logical_and(jnp.logical_and(e_idx >= 0, e_idx < N_EXPERTS), e_group == g_sel)
    v1 = jnp.max(jnp.where(in_g, logits, neg), axis=-1, keepdims=True)
    i1 = jnp.min(jnp.where(jnp.logical_and(in_g, logits == v1), lane, big), axis=-1, keepdims=True)
    rest = jnp.logical_and(in_g, lane != i1)
    v2 = jnp.max(jnp.where(rest, logits, neg), axis=-1, keepdims=True)
    i2 = jnp.min(jnp.where(jnp.logical_and(rest, logits == v2), lane, big), axis=-1, keepdims=True)
    e2 = jnp.exp(v2 - v1)
    inv = 1.0 / (1.0 + e2)
    w1 = inv * p_group
    w2 = e2 * inv * p_group
    return jnp.where(lane == i1, w1, jnp.where(lane == i2, w2, 0.0))


def _outproj_kernel(x_ref, oc_ref, ol_ref, yf_ref, yb_ref, yfix_ref, u_ref, hl_ref, g_ref, mod_ref,
                    d_ref, wglu_ref, bglu_ref, wout_ref, lng_ref, lnb_ref, rw_ref, rb_ref,
                    x1_ref, hm_ref, gates_ref):
    i = pl.program_id(0)
    lat = i >= N_CTX
    o_att = jnp.where(lat, ol_ref[...], oc_ref[...])
    u = u_ref[...]
    y = yf_ref[...] + yb_ref[...] + jnp.where(lat, yfix_ref[...], 0.0) + d_ref[0] * u
    g = jax.nn.gelu(y, approximate=True)
    glu = jnp.dot(g.astype(BF16), wglu_ref[0], preferred_element_type=F32) + bglu_ref[0]
    o_s5 = g * jax.nn.sigmoid(glu)
    o_lru = hl_ref[...] * jax.nn.gelu(g_ref[...], approximate=True)
    mix = jnp.concatenate([o_att, o_s5.astype(BF16), o_lru.astype(BF16)], axis=-1)
    out = jnp.dot(mix, wout_ref[0], preferred_element_type=F32)
    r = _mod_row(i)
    g1 = mod_ref[0, pl.ds(r, 1), 2 * D:3 * D]
    sh2 = mod_ref[0, pl.ds(r, 1), 3 * D:4 * D]
    sc2 = mod_ref[0, pl.ds(r, 1), 4 * D:5 * D]
    x1 = _layer_norm(ALPHA * x_ref[...] + g1 * out, lng_ref[0], lnb_ref[0])
    x1_ref[...] = x1
    hm = x1 * (1.0 + sc2) + sh2
    hm_ref[...] = hm.astype(BF16)
    logits = jnp.dot(hm, rw_ref[0], preferred_element_type=F32,
                     precision=lax.Precision.HIGHEST) + rb_ref[0]
    gates_ref[...] = _route(logits)


def _outproj_call(layer, x, o_ctx, o_lat, yf, yb, yfix, u_tm, hl_tm, g_lru, mod, s5_d, w_glu, b_glu,
                  w_out, ln_g, ln_b, rw, rb):
    tm_spec = pl.BlockSpec((TILE, 256), lambda i: (0, i))
    vec = lambda n: pl.BlockSpec((1, 1, n), lambda i: (layer, 0, 0))
    mat = lambda a, b: pl.BlockSpec((1, a, b), lambda i: (layer, 0, 0))
    return pl.pallas_call(
        _outproj_kernel,
        grid=(N_SEQ,),
        in_specs=[
            pl.BlockSpec((TILE, D), lambda i: (i, 0)),
            pl.BlockSpec((TILE, 512), lambda i: (jnp.minimum(i, N_CTX - 1), 0)),
            pl.BlockSpec((TILE, 512), lambda i: (jnp.maximum(i - N_CTX, 0), 0)),
            tm_spec, tm_spec,
            pl.BlockSpec((TILE, S5_W), lambda i: (jnp.maximum(i - N_CTX, 0), 0)),
            tm_spec, tm_spec,
            pl.BlockSpec((TILE, LRU_W), lambda i: (i, 0)),
            mat(MOD_ROWS, N_MOD * D),
            vec(S5_W), mat(S5_W, S5_W), vec(S5_W),
            mat(D, D), vec(D), vec(D),
            mat(D, 128), vec(128),
        ],
        out_specs=[
            pl.BlockSpec((TILE, D), lambda i: (i, 0)),
            pl.BlockSpec((TILE, D), lambda i: (i, 0)),
            pl.BlockSpec((TILE, 128), lambda i: (i, 0)),
        ],
        out_shape=[
            jax.ShapeDtypeStruct((N_TOK, D), F32),
            jax.ShapeDtypeStruct((N_TOK, D), BF16),
            jax.ShapeDtypeStruct((N_TOK, 128), F32),
        ],
        compiler_params=_cparams(("arbitrary",)),
        name="outproj",
    )(x, o_ctx, o_lat, yf.reshape(TILE, N_SEQ * S5_W), yb.reshape(TILE, N_SEQ * S5_W), yfix,
      u_tm, hl_tm.reshape(TILE, N_SEQ * LRU_W), g_lru, mod, s5_d.reshape(DEPTH, 1, S5_W), w_glu,
      b_glu.reshape(DEPTH, 1, S5_W), w_out, ln_g.reshape(DEPTH, 1, D), ln_b.reshape(DEPTH, 1, D), rw, rb)


MOE_ROWS = 1024


def _moe_kernel(hm_ref, gates_ref, wg_ref, wu_ref, wd_ref, x1_ref, mod_ref, lng_ref, lnb_ref,
                o_ref, acc_s):
    t = pl.program_id(0)
    e = pl.program_id(1)

    @pl.when(e == 0)
    def _():
        acc_s[...] = jnp.zeros_like(acc_s)

    hm = hm_ref[...]
    a = jnp.dot(hm, wg_ref[0, 0].astype(BF16), preferred_element_type=F32)
    u = jnp.dot(hm, wu_ref[0, 0].astype(BF16), preferred_element_type=F32)
    gates = gates_ref[...]
    lane = lax.broadcasted_iota(jnp.int32, gates.shape, 1)
    ge = jnp.sum(jnp.where(lane == e + GATE_LANE0, gates, 0.0), axis=-1, keepdims=True)
    act = ((a * jax.nn.sigmoid(a)) * u * ge).astype(BF16)
    acc_s[...] += jnp.dot(act, wd_ref[0, 0].astype(BF16), preferred_element_type=F32)

    @pl.when(e == N_EXPERTS - 1)
    def _():
        r = _mod_row(t * (MOE_ROWS // TILE))
        g2 = mod_ref[0, pl.ds(r, 1), 5 * D:6 * D]
        o_ref[...] = _layer_norm(ALPHA * x1_ref[...] + g2 * acc_s[...], lng_ref[0], lnb_ref[0])


def _moe_call(layer, hm, gates, w_gate, w_up, w_down, x1, mod, ln_g, ln_b):
    vec = lambda n: pl.BlockSpec((1, 1, n), lambda t, e: (layer, 0, 0))
    return pl.pallas_call(
        _moe_kernel,
        grid=(N_TOK // MOE_ROWS, N_EXPERTS),
        in_specs=[
            pl.BlockSpec((MOE_ROWS, D), lambda t, e: (t, 0)),
            pl.BlockSpec((MOE_ROWS, 128), lambda t, e: (t, 0)),
            pl.BlockSpec((1, 1, D, D_EXPERT), lambda t, e: (layer, e, 0, 0)),
            pl.BlockSpec((1, 1, D, D_EXPERT), lambda t, e: (layer, e, 0, 0)),
            pl.BlockSpec((1, 1, D_EXPERT, D), lambda t, e: (layer, e, 0, 0)),
            pl.BlockSpec((MOE_ROWS, D), lambda t, e: (t, 0)),
            pl.BlockSpec((1, MOD_ROWS, N_MOD * D), lambda t, e: (layer, 0, 0)),
            vec(D), vec(D),
        ],
        out_specs=pl.BlockSpec((MOE_ROWS, D), lambda t, e: (t, 0)),
        out_shape=jax.ShapeDtypeStruct((N_TOK, D), F32),
        scratch_shapes=[pltpu.VMEM((MOE_ROWS, D), F32)],
        compiler_params=_cparams(("arbitrary", "arbitrary")),
        name="moe",
    )(hm, gates, w_gate, w_up, w_down, x1, mod, ln_g.reshape(DEPTH, 1, D), ln_b.reshape(DEPTH, 1, D))


def _rope_tables():
    rows = LAT_T // GRID_W
    row = jnp.repeat(jnp.arange(rows, dtype=F32), GRID_W)
    col = jnp.tile(jnp.arange(GRID_W, dtype=F32), rows)
    n_freq = QK // 4
    inv = ROPE_BASE ** (-jnp.arange(n_freq, dtype=F32) / n_freq)
    ang_r = row[:, None] * inv
    ang_c = col[:, None] * inv
    cos64 = jnp.concatenate([jnp.cos(ang_r), jnp.cos(ang_r), jnp.cos(ang_c), jnp.cos(ang_c)], axis=1)
    sin64 = jnp.concatenate([-jnp.sin(ang_r), jnp.sin(ang_r), -jnp.sin(ang_c), jnp.sin(ang_c)], axis=1)
    cos = jnp.concatenate([jnp.tile(cos64, (1, 2)), jnp.ones((TILE, 128), F32)], axis=0)
    sin = jnp.concatenate([jnp.tile(sin64, (1, 2)), jnp.zeros((TILE, 128), F32)], axis=0)
    return cos, sin


def _permute_w_in(w_in):
    qk = w_in[:, :, 0:1024].reshape(DEPTH, D, 2, 2, HEADS, QK)
    qk = qk.transpose(0, 1, 2, 4, 3, 5).reshape(DEPTH, D, 1024)
    return jnp.concatenate([qk, w_in[:, :, 1024:]], axis=-1).astype(BF16)


def _s5_params(a_re, a_im, b_re, b_im, c_re, c_im, log_dt):
    dt = jnp.exp(log_dt)[..., None]
    mag = jnp.exp(a_re * dt)
    abar_r = mag * jnp.cos(a_im * dt)
    abar_i = mag * jnp.sin(a_im * dt)
    den = a_re * a_re + a_im * a_im
    nr = abar_r - 1.0
    coef_r = (nr * a_re + abar_i * a_im) / den
    coef_i = (abar_i * a_re - nr * a_im) / den
    bbar_r = coef_r[..., None] * b_re - coef_i[..., None] * b_im
    bbar_i = coef_r[..., None] * b_im + coef_i[..., None] * b_re
    eye = jnp.eye(S5_G, dtype=F32)
    def b_block(bb):
        m = jnp.einsum('ldgpc,gh->ldgchp', bb, eye)
        return m.reshape(DEPTH, 2, S5_W, S5_N)
    b_blk = jnp.concatenate([b_block(bbar_r), b_block(bbar_i)], axis=-1).astype(BF16)
    def c_block(cc):
        m = jnp.einsum('ldgcp,gh->ldgphc', cc, eye)
        return m.reshape(DEPTH, 2, S5_N, S5_W)
    c_blk = jnp.concatenate([c_block(c_re), c_block(-c_im)], axis=-2).astype(BF16)
    a_bar = jnp.concatenate([abar_r.reshape(DEPTH, 2, S5_N), abar_i.reshape(DEPTH, 2, S5_N)], axis=-1)
    return b_blk, c_blk, a_bar


def _block_diag(w):
    eye = jnp.eye(LRU_BLOCKS, dtype=F32)
    m = jnp.einsum('ldkij,kh->ldkihj', w, eye)
    return m.reshape(DEPTH, 2, LRU_W, LRU_W).astype(BF16)


def kernel(x_prompt, x_sample, cache_k, cache_v, state_s5, state_lru, c, c_ctx, w_ada, b_ada, w_in, w_out, lam_q1, lam_k1, lam_q2, lam_k2, subln_g, s5_a_re, s5_a_im, s5_b_re, s5_b_im, s5_c_re, s5_c_im, s5_log_dt, s5_d, s5_w_glu, s5_b_glu, lru_conv_w, lru_conv_b, lru_w_a, lru_b_a, lru_w_i, lru_b_i, lru_lambda, ln1_g, ln1_b, ln2_g, ln2_b, router_group_w, router_group_b, router_expert_w, router_expert_b, moe_w_gate, moe_w_up, moe_w_down):
    x = jnp.concatenate([x_prompt.reshape(CTX_B * CTX_T, D), x_sample.reshape(LAT_B * LAT_T, D)], axis=0)
    cond = jnp.concatenate([c_ctx[None, :], c, jnp.zeros((MOD_ROWS - 1 - LAT_B, D), F32)], axis=0)
    mod = _ada_call(cond, w_ada, b_ada)

    w_in_p = _permute_w_in(w_in)
    w_out_b = w_out.astype(BF16)
    w_glu_b = s5_w_glu.astype(BF16)
    rope_cos, rope_sin = _rope_tables()
    b_blk, c_blk, a_bar = _s5_params(s5_a_re, s5_a_im, s5_b_re, s5_b_im, s5_c_re, s5_c_im, s5_log_dt)
    wa_blk = _block_diag(lru_w_a)
    wi_blk = _block_diag(lru_w_i)
    ba = lru_b_a.reshape(DEPTH, 2, 1, LRU_W)
    bi = lru_b_i.reshape(DEPTH, 2, 1, LRU_W)
    sp = jax.nn.softplus(-lru_lambda).reshape(DEPTH, 2, 1, LRU_W)
    lam = (jnp.exp(jnp.sum(lam_q1 * lam_k1, axis=-1)) - jnp.exp(jnp.sum(lam_q2 * lam_k2, axis=-1)))
    rw = jnp.concatenate([router_group_w, router_expert_w,
                          jnp.zeros((DEPTH, D, 128 - N_GROUPS - N_EXPERTS), F32)], axis=-1)
    rb = jnp.concatenate([router_group_b, router_expert_b,
                          jnp.zeros((DEPTH, 128 - N_GROUPS - N_EXPERTS), F32)], axis=-1).reshape(DEPTH, 1, 128)
    s5_h0 = state_s5.reshape(LAT_B, DEPTH, 2, 2 * S5_N)
    lru_h0 = state_lru.transpose(1, 0, 2, 3)

    kc = jnp.zeros((CTX_B, DEPTH, HEADS, CTX_T, 2 * QK), F32)
    vc = jnp.zeros((CTX_B, DEPTH, HEADS, CTX_T, DV), F32)
    s5_states = []
    lru_states = []
    for l in range(DEPTH):
        lambda_init = 0.8 - 0.6 * math.exp(-0.3 * l)
        lam_l = (lam[l] + lambda_init).reshape(1)
        q, k, v, kc, vc, u_tm, xc_tm, g_lru = _inproj_call(
            l, x, mod, w_in_p, rope_cos, rope_sin, lru_conv_w, lru_conv_b, kc, vc)
        o_ctx = _att_ctx_call(l, lam_l, q, k, v, subln_g, 1.0 - lambda_init)
        o_lat = _att_lat_call(l, lam_l, q, k, v, cache_k, cache_v, subln_g, 1.0 - lambda_init)
        yf, yb, hend = _s5_call(l, u_tm.reshape(N_TOK, S5_W), b_blk, c_blk, a_bar)
        yfix = _s5_fix_call(l, hend, s5_h0[:, l], a_bar, c_blk)
        hl, lru_end = _lru_call(l, xc_tm.reshape(N_TOK, LRU_W), wa_blk, ba, wi_blk, bi, sp, lru_h0)
        x1, hm, gates = _outproj_call(l, x, o_ctx, o_lat, yf, yb, yfix, u_tm, hl, g_lru, mod, s5_d,
                                      w_glu_b, s5_b_glu, w_out_b, ln1_g, ln1_b, rw, rb)
        x = _moe_call(l, hm, gates, moe_w_gate, moe_w_up, moe_w_down, x1, mod, ln2_g, ln2_b)
        s5_states.append(hend[:, :N_CTX].reshape(2, CTX_B, 2, S5_G, S5_P).transpose(1, 0, 2, 3, 4))
        lru_states.append(lru_end[:, :N_CTX].transpose(1, 0, 2))

    y_p = x[:CTX_B * CTX_T].reshape(CTX_B, CTX_T, D)
    y_s = x[CTX_B * CTX_T:].reshape(LAT_B, LAT_T, D)
    return (y_p, y_s, kc, vc, jnp.stack(s5_states, axis=1), jnp.stack(lru_states, axis=1))
```

```python
import functools
import math

import jax
import jax.numpy as jnp
from jax import lax
from jax.experimental import pallas as pl
from jax.experimental.pallas import tpu as pltpu

F32 = jnp.float32
BF16 = jnp.bfloat16
I32 = jnp.int32

D = 1024
DEPTH = 4
CTX_B = 16
CTX_T = 256
LAT_B = 2
LAT_T = 2048
PAST = 256
GRID_W = 64
HEADS = 4
QK = 64
DV = 128
S5_W = 256
S5_G = 16
S5_C = 16
S5_P = 64
S5_N = S5_G * S5_P
LRU_W = 256
LRU_BLOCKS = 4
LRU_C = 8.0
N_GROUPS = 4
EPG = 4
N_EXPERTS = 16
D_EXPERT = 512
N_MOD = 6
IN_W = 2304
ROPE_BASE = 10000.0
ALPHA = (2 * DEPTH) ** 0.25
EPS = 1e-5

TILE = 256
N_TOK = CTX_B * CTX_T + LAT_B * LAT_T
N_SEQ = N_TOK // TILE
N_CTX = CTX_B
CHUNKS = LAT_T // TILE
N_LAT = LAT_B * CHUNKS
MOD_ROWS = 8
GATE_LANE0 = 4
VMEM_LIMIT = 56 * 1024 * 1024
MOE_VMEM_LIMIT = 60 * 1024 * 1024

GROUP_SEQS = 8
GROUP_ROWS = GROUP_SEQS * TILE
N_GROUP_TILES = N_TOK // GROUP_ROWS
CTX_GROUP_TILES = N_CTX // GROUP_SEQS
SUB_ROWS = 512
SUBS = GROUP_ROWS // SUB_ROWS
SUB_SEQS = SUB_ROWS // TILE


def _cparams(sem):
    return pltpu.CompilerParams(dimension_semantics=sem, vmem_limit_bytes=VMEM_LIMIT)


def _group_mod_row(i):
    return jnp.where(i < CTX_GROUP_TILES, 0, i - CTX_GROUP_TILES + 1)


def _ada_kernel(c_ref, w_ref, b_ref, o_ref):
    c = c_ref[...]
    s = (c * jax.nn.sigmoid(c)).astype(BF16)
    o_ref[0] = jnp.dot(s, w_ref[0].astype(BF16), preferred_element_type=F32) + b_ref[0]


def _ada_call(cond, w_ada, b_ada):
    tn = 1536
    return pl.pallas_call(
        _ada_kernel,
        grid=(DEPTH, N_MOD * D // tn),
        in_specs=[
            pl.BlockSpec((MOD_ROWS, D), lambda l, j: (0, 0)),
            pl.BlockSpec((1, D, tn), lambda l, j: (l, 0, j)),
            pl.BlockSpec((1, 1, tn), lambda l, j: (l, 0, j)),
        ],
        out_specs=pl.BlockSpec((1, MOD_ROWS, tn), lambda l, j: (l, 0, j)),
        out_shape=jax.ShapeDtypeStruct((DEPTH, MOD_ROWS, N_MOD * D), F32),
        compiler_params=_cparams(("arbitrary", "arbitrary")),
        name="adaln",
    )(cond, w_ada, b_ada.reshape(DEPTH, 1, N_MOD * D))


def _inproj_kernel(x_ref, mod_ref, w_ref, cos_ref, sin_ref, cw_ref, cb_ref, kc_in, vc_in,
                   q_ref, k_ref, v_ref, kc_ref, vc_ref, u_ref, xc_ref, g_ref, u_s, xl_s):
    del kc_in, vc_in
    i = pl.program_id(0)
    s = pl.program_id(1)
    r = _group_mod_row(i)
    sh = mod_ref[0, pl.ds(r, 1), 0:D]
    sc = mod_ref[0, pl.ds(r, 1), D:2 * D]
    xm = (x_ref[...] * (1.0 + sc) + sh).astype(BF16)
    proj = jnp.dot(xm, w_ref[0], preferred_element_type=F32)

    qk = proj[:, 0:1024]
    cos = jnp.concatenate([cos_ref[...]] * 8, axis=1)
    sin = jnp.concatenate([sin_ref[...]] * 8, axis=1)
    lane = lax.broadcasted_iota(I32, qk.shape, 1)
    swapped = jnp.where((lane & 31) < 16, pltpu.roll(qk, 1024 - 16, 1), pltpu.roll(qk, 16, 1))
    qk = qk * cos + swapped * sin
    q_ref[...] = qk[:, 0:512].astype(BF16)
    k_ref[...] = qk[:, 512:1024].astype(BF16)
    v = proj[:, 1024:1536]
    v_ref[...] = v.astype(BF16)

    @pl.when(i < CTX_GROUP_TILES)
    def _():
        for jj in range(SUB_SEQS):
            rows = slice(jj * TILE, (jj + 1) * TILE)
            for h in range(HEADS):
                kc_ref[jj, 0, h] = qk[rows, 512 + h * 128:512 + (h + 1) * 128]
                vc_ref[jj, 0, h] = v[rows, h * 128:(h + 1) * 128]

    g_ref[...] = proj[:, 2048:2304]
    sub_rows = pl.ds(pl.multiple_of(s * SUB_ROWS, SUB_ROWS), SUB_ROWS)
    u_s[sub_rows, :] = proj[:, 1536:1792]
    xl_s[sub_rows, :] = proj[:, 1792:2048]

    @pl.when(s == SUBS - 1)
    def _():
        xl = xl_s[...]
        row = lax.broadcasted_iota(I32, xl.shape, 0)
        is_ctx = i < CTX_GROUP_TILES
        pos = jnp.where(is_ctx, row & (TILE - 1), row)
        last = jnp.where(is_ctx, TILE - 1, GROUP_ROWS - 1)
        x_m1 = jnp.where(pos == 0, 0.0, pltpu.roll(xl, 1, 0))
        x_p1 = jnp.where(pos == last, 0.0, pltpu.roll(xl, GROUP_ROWS - 1, 0))
        x_p2 = jnp.where(pos >= last - 1, 0.0, pltpu.roll(xl, GROUP_ROWS - 2, 0))
        cw = cw_ref[0]
        xc = cb_ref[0] + x_m1 * cw[0:1] + xl * cw[1:2] + x_p1 * cw[2:3] + x_p2 * cw[3:4]
        for j in range(GROUP_SEQS):
            xc_ref[:, j, :] = xc[j * TILE:(j + 1) * TILE]
            u_ref[:, j, :] = u_s[j * TILE:(j + 1) * TILE, :]


def _inproj_call(layer, x, mod, w_in_p, rope_cos, rope_sin, conv_w, conv_b, kc, vc):
    n_ctx_blocks = N_CTX // SUB_SEQS

    def rope_idx(i, s):
        return (jnp.where(i < CTX_GROUP_TILES, SUBS, s), 0)

    def cache_idx(i, s):
        return (jnp.minimum(i * SUBS + s, n_ctx_blocks - 1), layer, 0, 0, 0)

    row_spec = lambda w: pl.BlockSpec((SUB_ROWS, w), lambda i, s: (i * SUBS + s, 0))
    tm_spec = pl.BlockSpec((TILE, GROUP_SEQS, 256), lambda i, s: (0, i, 0))
    cache_spec = pl.BlockSpec((SUB_SEQS, 1, HEADS, TILE, 128), cache_idx)
    return pl.pallas_call(
        _inproj_kernel,
        grid=(N_GROUP_TILES, SUBS),
        in_specs=[
            row_spec(D),
            pl.BlockSpec((1, MOD_ROWS, N_MOD * D), lambda i, s: (layer, 0, 0)),
            pl.BlockSpec((1, D, IN_W), lambda i, s: (layer, 0, 0)),
            pl.BlockSpec((SUB_ROWS, 128), rope_idx),
            pl.BlockSpec((SUB_ROWS, 128), rope_idx),
            pl.BlockSpec((1, 4, LRU_W), lambda i, s: (layer, 0, 0)),
            pl.BlockSpec((1, 1, LRU_W), lambda i, s: (layer, 0, 0)),
            pl.BlockSpec(memory_space=pl.ANY),
            pl.BlockSpec(memory_space=pl.ANY),
        ],
        out_specs=[
            row_spec(512), row_spec(512), row_spec(512),
            cache_spec, cache_spec,
            tm_spec, tm_spec,
            row_spec(LRU_W),
        ],
        out_shape=[
            jax.ShapeDtypeStruct((N_TOK, 512), BF16),
            jax.ShapeDtypeStruct((N_TOK, 512), BF16),
            jax.ShapeDtypeStruct((N_TOK, 512), BF16),
            jax.ShapeDtypeStruct(kc.shape, F32),
            jax.ShapeDtypeStruct(vc.shape, F32),
            jax.ShapeDtypeStruct((TILE, N_SEQ, S5_W), F32),
            jax.ShapeDtypeStruct((TILE, N_SEQ, LRU_W), F32),
            jax.ShapeDtypeStruct((N_TOK, LRU_W), F32),
        ],
        scratch_shapes=[pltpu.VMEM((GROUP_ROWS, S5_W), F32), pltpu.VMEM((GROUP_ROWS, LRU_W), F32)],
        input_output_aliases={7: 3, 8: 4},
        compiler_params=_cparams(("arbitrary", "arbitrary")),
        name="inproj",
    )(x, mod, w_in_p, rope_cos, rope_sin, conv_w, conv_b.reshape(DEPTH, 1, LRU_W), kc, vc)


def _softmax_parts(parts):
    m = parts[0].max(axis=-1, keepdims=True)
    for p in parts[1:]:
        m = jnp.maximum(m, p.max(axis=-1, keepdims=True))
    es = [jnp.exp(p - m) for p in parts]
    z = es[0].sum(axis=-1, keepdims=True)
    for e in es[1:]:
        z = z + e.sum(axis=-1, keepdims=True)
    inv = 1.0 / z
    return [e * inv for e in es]


_NT = (((1,), (1,)), ((), ()))


def _split_maps(q):
    lane = lax.broadcasted_iota(I32, q.shape, 1)
    zero = jnp.zeros_like(q)
    return jnp.where(lane < QK, q, zero), jnp.where(lane >= QK, q, zero)


def _subln(o, g, post_scale):
    ms = jnp.mean(o * o, axis=-1, keepdims=True)
    return (o * lax.rsqrt(ms + EPS) * g) * post_scale


def _att_ctx_kernel(lam_ref, q_ref, k_ref, v_ref, g_ref, o_ref, *, post_scale):
    lam = lam_ref[0]
    q1, q2 = _split_maps(q_ref[...])
    k = k_ref[...]
    s1 = lax.dot_general(q1, k, _NT, preferred_element_type=F32) * (QK ** -0.5)
    s2 = lax.dot_general(q2, k, _NT, preferred_element_type=F32) * (QK ** -0.5)
    (p1,) = _softmax_parts([s1])
    (p2,) = _softmax_parts([s2])
    w = (p1 - lam * p2).astype(BF16)
    o = jnp.dot(w, v_ref[...], preferred_element_type=F32)
    o_ref[...] = _subln(o, g_ref[0], post_scale).astype(BF16)


def _att_ctx_call(layer, lam, q, k, v, subln_g, post_scale):
    blk = pl.BlockSpec((TILE, 128), lambda b, h: (b, h))
    return pl.pallas_call(
        functools.partial(_att_ctx_kernel, post_scale=post_scale),
        grid=(CTX_B, HEADS),
        in_specs=[
            pl.BlockSpec(memory_space=pltpu.SMEM),
            blk, blk, blk,
            pl.BlockSpec((1, 1, DV), lambda b, h: (layer, 0, 0)),
        ],
        out_specs=blk,
        out_shape=jax.ShapeDtypeStruct((CTX_B * CTX_T, HEADS * DV), BF16),
        compiler_params=_cparams(("arbitrary", "arbitrary")),
        name="att_ctx",
    )(lam, q, k, v, subln_g.reshape(DEPTH, 1, DV))


def _att_lat_kernel(lam_ref, q_ref, k_ref, v_ref, ck_ref, cv_ref, g_ref, o_ref, *, post_scale):
    lam = lam_ref[0]
    q1, q2 = _split_maps(q_ref[...])
    k = k_ref[...]
    ck = ck_ref[...].astype(BF16)
    scale = QK ** -0.5
    s1l = lax.dot_general(q1, k, _NT, preferred_element_type=F32) * scale
    s1c = lax.dot_general(q1, ck, _NT, preferred_element_type=F32) * scale
    p1l, p1c = _softmax_parts([s1l, s1c])
    s2l = lax.dot_general(q2, k, _NT, preferred_element_type=F32) * scale
    s2c = lax.dot_general(q2, ck, _NT, preferred_element_type=F32) * scale
    p2l, p2c = _softmax_parts([s2l, s2c])
    wl = (p1l - lam * p2l).astype(BF16)
    wc = (p1c - lam * p2c).astype(BF16)
    o = (jnp.dot(wl, v_ref[...], preferred_element_type=F32)
         + jnp.dot(wc, cv_ref[...].astype(BF16), preferred_element_type=F32))
    o_ref[...] = _subln(o, g_ref[0], post_scale).astype(BF16)


def _att_lat_call(layer, lam, q, k, v, cache_k, cache_v, subln_g, post_scale):
    lat0 = N_CTX
    cache_spec = pl.BlockSpec((None, None, None, PAST, 128), lambda b, h, t: (b, layer, h, 0, 0))
    return pl.pallas_call(
        functools.partial(_att_lat_kernel, post_scale=post_scale),
        grid=(LAT_B, HEADS, CHUNKS),
        in_specs=[
            pl.BlockSpec(memory_space=pltpu.SMEM),
            pl.BlockSpec((TILE, 128), lambda b, h, t: (lat0 + b * CHUNKS + t, h)),
            pl.BlockSpec((LAT_T, 128), lambda b, h, t: (lat0 // CHUNKS + b, h)),
            pl.BlockSpec((LAT_T, 128), lambda b, h, t: (lat0 // CHUNKS + b, h)),
            cache_spec, cache_spec,
            pl.BlockSpec((1, 1, DV), lambda b, h, t: (layer, 0, 0)),
        ],
        out_specs=pl.BlockSpec((TILE, 128), lambda b, h, t: (b * CHUNKS + t, h)),
        out_shape=jax.ShapeDtypeStruct((LAT_B * LAT_T, HEADS * DV), BF16),
        compiler_params=_cparams(("arbitrary", "arbitrary", "arbitrary")),
        name="att_lat",
    )(lam, q, k, v, cache_k, cache_v, subln_g.reshape(DEPTH, 1, DV))


S5_TB = 32
S5_LANES = 128


def _cmul(ar, ai, br, bi):
    return ar * br - ai * bi, ar * bi + ai * br


def _s5_kernel(uf_ref, ub_ref, bf_ref, bb_ref, cf_ref, cb_ref, a_ref, yf_ref, yb_ref, hend_ref,
               hf_s, hb_s, buf_f, buf_b):
    i = pl.program_id(0)

    @pl.when(i == 0)
    def _():
        hf_s[...] = jnp.zeros_like(hf_s)
        hb_s[...] = jnp.zeros_like(hb_s)

    buf_f[...] = jnp.dot(uf_ref[...].astype(BF16), bf_ref[0, 0], preferred_element_type=F32)
    buf_b[...] = jnp.dot(ub_ref[...].astype(BF16), bb_ref[0, 0], preferred_element_type=F32)

    for c in range(S5_N // S5_LANES):
        re = slice(c * S5_LANES, (c + 1) * S5_LANES)
        im = slice(S5_N + c * S5_LANES, S5_N + (c + 1) * S5_LANES)
        afr = a_ref[0, 0:1, re]
        afi = a_ref[0, 0:1, im]
        abr = a_ref[0, 1:2, re]
        abi = a_ref[0, 1:2, im]

        def body(t, carry, re=re, im=im, afr=afr, afi=afi, abr=abr, abi=abi):
            hfr, hfi, hbr, hbi = carry
            rf = pl.multiple_of(t * N_SEQ, N_SEQ)
            rb = pl.multiple_of((S5_TB - 1 - t) * N_SEQ, N_SEQ)
            pr, pi = _cmul(afr, afi, hfr, hfi)
            hfr = pr + buf_f[pl.ds(rf, N_SEQ), re]
            hfi = pi + buf_f[pl.ds(rf, N_SEQ), im]
            buf_f[pl.ds(rf, N_SEQ), re] = hfr
            buf_f[pl.ds(rf, N_SEQ), im] = hfi
            pr, pi = _cmul(abr, abi, hbr, hbi)
            hbr = pr + buf_b[pl.ds(rb, N_SEQ), re]
            hbi = pi + buf_b[pl.ds(rb, N_SEQ), im]
            buf_b[pl.ds(rb, N_SEQ), re] = hbr
            buf_b[pl.ds(rb, N_SEQ), im] = hbi
            return hfr, hfi, hbr, hbi

        out = lax.fori_loop(0, S5_TB, body, (hf_s[:, re], hf_s[:, im], hb_s[:, re], hb_s[:, im]))
        hf_s[:, re] = out[0]
        hf_s[:, im] = out[1]
        hb_s[:, re] = out[2]
        hb_s[:, im] = out[3]

    yf_ref[...] = jnp.dot(buf_f[...].astype(BF16), cf_ref[0, 0], preferred_element_type=F32)
    yb_ref[...] = jnp.dot(buf_b[...].astype(BF16), cb_ref[0, 0], preferred_element_type=F32)

    @pl.when(i == pl.num_programs(0) - 1)
    def _():
        hend_ref[0] = hf_s[...]
        hend_ref[1] = hb_s[...]


def _s5_call(layer, u2, b_blk, c_blk, a_bar):
    nb = CTX_T // S5_TB
    rows = S5_TB * N_SEQ
    return pl.pallas_call(
        _s5_kernel,
        grid=(nb,),
        in_specs=[
            pl.BlockSpec((rows, S5_W), lambda i: (i, 0)),
            pl.BlockSpec((rows, S5_W), lambda i: (nb - 1 - i, 0)),
            pl.BlockSpec((1, 1, S5_W, 2 * S5_N), lambda i: (layer, 0, 0, 0)),
            pl.BlockSpec((1, 1, S5_W, 2 * S5_N), lambda i: (layer, 1, 0, 0)),
            pl.BlockSpec((1, 1, 2 * S5_N, S5_W), lambda i: (layer, 0, 0, 0)),
            pl.BlockSpec((1, 1, 2 * S5_N, S5_W), lambda i: (layer, 1, 0, 0)),
            pl.BlockSpec((1, 2, 2 * S5_N), lambda i: (layer, 0, 0)),
        ],
        out_specs=[
            pl.BlockSpec((rows, S5_W), lambda i: (i, 0)),
            pl.BlockSpec((rows, S5_W), lambda i: (nb - 1 - i, 0)),
            pl.BlockSpec((2, N_SEQ, 2 * S5_N), lambda i: (0, 0, 0)),
        ],
        out_shape=[
            jax.ShapeDtypeStruct((N_TOK, S5_W), F32),
            jax.ShapeDtypeStruct((N_TOK, S5_W), F32),
            jax.ShapeDtypeStruct((2, N_SEQ, 2 * S5_N), F32),
        ],
        scratch_shapes=[
            pltpu.VMEM((N_SEQ, 2 * S5_N), F32),
            pltpu.VMEM((N_SEQ, 2 * S5_N), F32),
            pltpu.VMEM((rows, 2 * S5_N), F32),
            pltpu.VMEM((rows, 2 * S5_N), F32),
        ],
        compiler_params=_cparams(("arbitrary",)),
        name="s5_scan",
    )(u2, u2, b_blk, b_blk, c_blk, c_blk, a_bar)


def _s5_fix_kernel(hend_ref, h0_ref, a_ref, cf_ref, cb_ref, o_ref, pf_s, pb_s, cf_s, cb_s):
    s = pl.program_id(0)

    @pl.when(s == 0)
    def _():
        row8 = lax.broadcasted_iota(I32, (8, S5_N), 0)
        for d, tab in ((0, pf_s), (1, pb_s)):
            ar = a_ref[0, d:d + 1, 0:S5_N]
            ai = a_ref[0, d:d + 1, S5_N:2 * S5_N]
            pr, pi = ar, ai
            r8 = jnp.zeros((8, S5_N), F32)
            i8 = jnp.zeros((8, S5_N), F32)
            for r in range(8):
                if r:
                    pr, pi = _cmul(pr, pi, ar, ai)
                at = r if d == 0 else 7 - r
                r8 = jnp.where(row8 == at, pr, r8)
                i8 = jnp.where(row8 == at, pi, i8)
            base = 0 if d == 0 else TILE - 8
            tab[base:base + 8, 0:S5_N] = r8
            tab[base:base + 8, S5_N:2 * S5_N] = i8
            m = 8
            while m < TILE:
                if d == 0:
                    src, dst, top = slice(0, m), slice(m, 2 * m), slice(m - 1, m)
                else:
                    src, dst, top = slice(TILE - m, TILE), slice(TILE - 2 * m, TILE - m), slice(TILE - m, TILE - m + 1)
                mr, mi = tab[top, 0:S5_N], tab[top, S5_N:2 * S5_N]
                nr, ni = _cmul(tab[src, 0:S5_N], tab[src, S5_N:2 * S5_N], mr, mi)
                tab[dst, 0:S5_N] = nr
                tab[dst, S5_N:2 * S5_N] = ni
                m *= 2

        for b in range(LAT_B):
            ar, ai = pf_s[TILE - 1:TILE, 0:S5_N], pf_s[TILE - 1:TILE, S5_N:2 * S5_N]
            cr, ci = h0_ref[b, 0:1, 0:S5_N], h0_ref[b, 0:1, S5_N:2 * S5_N]
            for j in range(CHUNKS):
                row = b * CHUNKS + j
                cf_s[row:row + 1, 0:S5_N] = cr
                cf_s[row:row + 1, S5_N:2 * S5_N] = ci
                pr, pi = _cmul(ar, ai, cr, ci)
                cr = pr + hend_ref[0, N_CTX + row:N_CTX + row + 1, 0:S5_N]
                ci = pi + hend_ref[0, N_CTX + row:N_CTX + row + 1, S5_N:2 * S5_N]
            ar, ai = pb_s[0:1, 0:S5_N], pb_s[0:1, S5_N:2 * S5_N]
            cr, ci = h0_ref[b, 1:2, 0:S5_N], h0_ref[b, 1:2, S5_N:2 * S5_N]
            for j in reversed(range(CHUNKS)):
                row = b * CHUNKS + j
                cb_s[row:row + 1, 0:S5_N] = cr
                cb_s[row:row + 1, S5_N:2 * S5_N] = ci
                pr, pi = _cmul(ar, ai, cr, ci)
                cr = pr + hend_ref[1, N_CTX + row:N_CTX + row + 1, 0:S5_N]
                ci = pi + hend_ref[1, N_CTX + row:N_CTX + row + 1, S5_N:2 * S5_N]

    acc = None
    for tab, car, c_ref in ((pf_s, cf_s, cf_ref), (pb_s, cb_s, cb_ref)):
        cr = car[pl.ds(s, 1), 0:S5_N]
        ci = car[pl.ds(s, 1), S5_N:2 * S5_N]
        hr, hi = _cmul(tab[:, 0:S5_N], tab[:, S5_N:2 * S5_N], cr, ci)
        h = jnp.concatenate([hr, hi], axis=1).astype(BF16)
        y = jnp.dot(h, c_ref[0, 0], preferred_element_type=F32)
        acc = y if acc is None else acc + y
    o_ref[...] = acc


def _s5_fix_call(layer, hend, h0, a_bar, c_blk):
    return pl.pallas_call(
        _s5_fix_kernel,
        grid=(N_LAT,),
        in_specs=[
            pl.BlockSpec((2, N_SEQ, 2 * S5_N), lambda s: (0, 0, 0)),
            pl.BlockSpec((LAT_B, 2, 2 * S5_N), lambda s: (0, 0, 0)),
            pl.BlockSpec((1, 2, 2 * S5_N), lambda s: (layer, 0, 0)),
            pl.BlockSpec((1, 1, 2 * S5_N, S5_W), lambda s: (layer, 0, 0, 0)),
            pl.BlockSpec((1, 1, 2 * S5_N, S5_W), lambda s: (layer, 1, 0, 0)),
        ],
        out_specs=pl.BlockSpec((TILE, S5_W), lambda s: (s, 0)),
        out_shape=jax.ShapeDtypeStruct((N_LAT * TILE, S5_W), F32),
        scratch_shapes=[
            pltpu.VMEM((TILE, 2 * S5_N), F32),
            pltpu.VMEM((TILE, 2 * S5_N), F32),
            pltpu.VMEM((N_LAT, 2 * S5_N), F32),
            pltpu.VMEM((N_LAT, 2 * S5_N), F32),
        ],
        compiler_params=_cparams(("arbitrary",)),
        name="s5_fix",
    )(hend, h0, a_bar, c_blk, c_blk)


LRU_ROWS = 1024


def _lru_kernel(xc_ref, wa_ref, ba_ref, wi_ref, bi_ref, sp_ref, h0_ref, out_ref, hend_ref,
                a_s, h_s, p_s):
    row = lax.broadcasted_iota(I32, (N_SEQ, LRU_W), 0)
    for d in range(2):
        def gates(cix, _, d=d):
            r0 = pl.multiple_of(cix * LRU_ROWS, LRU_ROWS)
            xc = xc_ref[pl.ds(r0, LRU_ROWS), :]
            xb = xc.astype(BF16)
            r = jax.nn.sigmoid(jnp.dot(xb, wa_ref[0, d], preferred_element_type=F32) + ba_ref[0, d])
            g = jax.nn.sigmoid(jnp.dot(xb, wi_ref[0, d], preferred_element_type=F32) + bi_ref[0, d])
            log_a = (-LRU_C) * r * sp_ref[0, d]
            a_s[pl.ds(r0, LRU_ROWS), :] = jnp.exp(log_a)
            th = jnp.tanh(log_a)
            h_s[pl.ds(r0, LRU_ROWS), :] = jnp.sqrt(-2.0 * th / (1.0 - th)) * g * xc
            return 0

        lax.fori_loop(0, N_TOK // LRU_ROWS, gates, 0)

        def scan(t, carry, d=d):
            h, p = carry
            tt = t if d == 0 else TILE - 1 - t
            r0 = pl.multiple_of(tt * N_SEQ, N_SEQ)
            a = a_s[pl.ds(r0, N_SEQ), :]
            h = a * h + h_s[pl.ds(r0, N_SEQ), :]
            p = a * p
            h_s[pl.ds(r0, N_SEQ), :] = h
            p_s[pl.ds(r0, N_SEQ), :] = p
            return h, p

        h_end, p_end = lax.fori_loop(0, TILE, scan,
                                     (jnp.zeros((N_SEQ, LRU_W), F32), jnp.ones((N_SEQ, LRU_W), F32)))
        hend_ref[d] = h_end

        carry_slab = jnp.zeros((N_SEQ, LRU_W), F32)
        for b in range(LAT_B):
            c = h0_ref[0, b, d:d + 1, :]
            order = range(CHUNKS) if d == 0 else reversed(range(CHUNKS))
            for j in order:
                s = N_CTX + b * CHUNKS + j
                carry_slab = jnp.where(row == s, c, carry_slab)
                c = h_end[s:s + 1, :] + p_end[s:s + 1, :] * c

        def fix(t, _, d=d, carry_slab=carry_slab):
            r0 = pl.multiple_of(t * N_SEQ, N_SEQ)
            v = h_s[pl.ds(r0, N_SEQ), :] + p_s[pl.ds(r0, N_SEQ), :] * carry_slab
            if d == 0:
                out_ref[pl.ds(r0, N_SEQ), :] = v
            else:
                out_ref[pl.ds(r0, N_SEQ), :] += v
            return 0

        lax.fori_loop(0, TILE, fix, 0)


def _lru_call(layer, xc2, wa, ba, wi, bi, sp, h0):
    full = lambda shape: pl.BlockSpec(shape, lambda i: (0,) * len(shape))
    per_layer = lambda shape: pl.BlockSpec((1,) + shape, lambda i: (layer,) + (0,) * len(shape))
    return pl.pallas_call(
        _lru_kernel,
        grid=(1,),
        in_specs=[
            full((N_TOK, LRU_W)),
            per_layer((2, LRU_W, LRU_W)),
            per_layer((2, 1, LRU_W)),
            per_layer((2, LRU_W, LRU_W)),
            per_layer((2, 1, LRU_W)),
            per_layer((2, 1, LRU_W)),
            per_layer((LAT_B, 2, LRU_W)),
        ],
        out_specs=[full((N_TOK, LRU_W)), full((2, N_SEQ, LRU_W))],
        out_shape=[
            jax.ShapeDtypeStruct((N_TOK, LRU_W), F32),
            jax.ShapeDtypeStruct((2, N_SEQ, LRU_W), F32),
        ],
        scratch_shapes=[pltpu.VMEM((N_TOK, LRU_W), F32)] * 3,
        compiler_params=_cparams(("arbitrary",)),
        name="rglru",
    )(xc2, wa, ba, wi, bi, sp, h0)


def _layer_norm(z, g, b):
    mu = jnp.mean(z, axis=-1, keepdims=True)
    zc = z - mu
    var = jnp.mean(zc * zc, axis=-1, keepdims=True)
    return zc * lax.rsqrt(var + EPS) * g + b


def _route(logits):
    lane_i = lax.broadcasted_iota(I32, logits.shape, 1)
    lane = lane_i.astype(F32)
    big = jnp.float32(1024.0)
    neg = jnp.float32(-jnp.inf)
    is_g = lane_i < N_GROUPS
    gmax = jnp.max(jnp.where(is_g, logits, neg), axis=-1, keepdims=True)
    g_sel = jnp.min(jnp.where(jnp.logical_and(is_g, logits == gmax), lane, big), axis=-1, keepdims=True)
    p_group = 1.0 / jnp.sum(jnp.where(is_g, jnp.exp(logits - gmax), 0.0), axis=-1, keepdims=True)
    e_idx = lane_i - GATE_LANE0
    e_group = (e_idx >> 2).astype(F32)
    in_g = jnp.logical_and(jnp.logical_and(e_idx >= 0, e_idx < N_EXPERTS), e_group == g_sel)
    v1 = jnp.max(jnp.where(in_g, logits, neg), axis=-1, keepdims=True)
    i1 = jnp.min(jnp.where(jnp.logical_and(in_g, logits == v1), lane, big), axis=-1, keepdims=True)
    rest = jnp.logical_and(in_g, lane != i1)
    v2 = jnp.max(jnp.where(rest, logits, neg), axis=-1, keepdims=True)
    i2 = jnp.min(jnp.where(jnp.logical_and(rest, logits == v2), lane, big), axis=-1, keepdims=True)
    e2 = jnp.exp(v2 - v1)
    inv = 1.0 / (1.0 + e2)
    w1 = inv * p_group
    w2 = e2 * inv * p_group
    return jnp.where(lane == i1, w1, jnp.where(lane == i2, w2, jnp.where(lane_i == 0, g_sel, 0.0)))


def _outproj_kernel(x_ref, oc_ref, ol_ref, yf_ref, yb_ref, yfix_ref, u_ref, hl_ref, g_ref, mod_ref,
                    d_ref, wglu_ref, bglu_ref, wout_ref, lng_ref, lnb_ref, rw_ref, rwhi_ref, rb_ref,
                    x1_ref, hm_ref, gates_ref, y_s, u_s, hl_s):
    i = pl.program_id(0)
    s = pl.program_id(1)

    @pl.when(s == 0)
    def _():
        for j in range(GROUP_SEQS):
            rows = slice(j * TILE, (j + 1) * TILE)
            y_s[rows, :] = yf_ref[:, j, :] + yb_ref[:, j, :]
            u_s[rows, :] = u_ref[:, j, :]
            hl_s[rows, :] = hl_ref[:, j, :]

    lat = i >= CTX_GROUP_TILES
    sub_rows = pl.ds(pl.multiple_of(s * SUB_ROWS, SUB_ROWS), SUB_ROWS)
    o_att = jnp.where(lat, ol_ref[...], oc_ref[...])
    y = y_s[sub_rows, :] + jnp.where(lat, yfix_ref[...], 0.0) + d_ref[0] * u_s[sub_rows, :]
    g = jax.nn.gelu(y, approximate=True)
    glu = jnp.dot(g.astype(BF16), wglu_ref[0], preferred_element_type=F32) + bglu_ref[0]
    o_s5 = g * jax.nn.sigmoid(glu)
    o_lru = hl_s[sub_rows, :] * jax.nn.gelu(g_ref[...], approximate=True)
    mix = jnp.concatenate([o_att, o_s5.astype(BF16), o_lru.astype(BF16)], axis=-1)
    out = jnp.dot(mix, wout_ref[0], preferred_element_type=F32)
    r = _group_mod_row(i)
    g1 = mod_ref[0, pl.ds(r, 1), 2 * D:3 * D]
    sh2 = mod_ref[0, pl.ds(r, 1), 3 * D:4 * D]
    sc2 = mod_ref[0, pl.ds(r, 1), 4 * D:5 * D]
    x1 = _layer_norm(ALPHA * x_ref[...] + g1 * out, lng_ref[0], lnb_ref[0])
    x1_ref[...] = x1
    hm = x1 * (1.0 + sc2) + sh2
    hm_ref[...] = hm
    hm_hi = hm.astype(BF16)
    hm_lo = (hm - hm_hi.astype(F32)).astype(BF16)
    p_hi = jnp.dot(hm_hi, rw_ref[0], preferred_element_type=F32)
    p_lo = jnp.dot(hm_lo, rwhi_ref[0], preferred_element_type=F32)
    logits = p_hi[:, 0:128] + p_hi[:, 128:256] + p_lo + rb_ref[0]
    gates_ref[...] = _route(logits)


def _outproj_call(layer, x, o_ctx, o_lat, yf, yb, yfix, u_tm, hl, g_lru, mod, s5_d, w_glu, b_glu,
                  w_out, ln_g, ln_b, rw_split, rw_hi, rb):
    n_ctx_blocks = N_CTX * TILE // SUB_ROWS
    row_spec = lambda w: pl.BlockSpec((SUB_ROWS, w), lambda i, s: (i * SUBS + s, 0))
    lat_spec = lambda w: pl.BlockSpec((SUB_ROWS, w), lambda i, s: (jnp.maximum(i * SUBS + s - n_ctx_blocks, 0), 0))
    tm_spec = pl.BlockSpec((TILE, GROUP_SEQS, 256), lambda i, s: (0, i, 0))
    vec = lambda n: pl.BlockSpec((1, 1, n), lambda i, s: (layer, 0, 0))
    mat = lambda a, b: pl.BlockSpec((1, a, b), lambda i, s: (layer, 0, 0))
    return pl.pallas_call(
        _outproj_kernel,
        grid=(N_GROUP_TILES, SUBS),
        in_specs=[
            row_spec(D),
            pl.BlockSpec((SUB_ROWS, 512), lambda i, s: (jnp.minimum(i * SUBS + s, n_ctx_blocks - 1), 0)),
            lat_spec(512),
            tm_spec, tm_spec,
            lat_spec(S5_W),
            tm_spec, tm_spec,
            row_spec(LRU_W),
            mat(MOD_ROWS, N_MOD * D),
            vec(S5_W), mat(S5_W, S5_W), vec(S5_W),
            mat(D, D), vec(D), vec(D),
            mat(D, 256), mat(D, 128), vec(128),
        ],
        out_specs=[row_spec(D), row_spec(D), row_spec(128)],
        out_shape=[
            jax.ShapeDtypeStruct((N_TOK, D), F32),
            jax.ShapeDtypeStruct((N_TOK, D), F32),
            jax.ShapeDtypeStruct((N_TOK, 128), F32),
        ],
        scratch_shapes=[pltpu.VMEM((GROUP_ROWS, 256), F32)] * 3,
        compiler_params=_cparams(("arbitrary", "arbitrary")),
        name="outproj",
    )(x, o_ctx, o_lat, yf.reshape(TILE, N_SEQ, S5_W), yb.reshape(TILE, N_SEQ, S5_W), yfix,
      u_tm, hl.reshape(TILE, N_SEQ, LRU_W), g_lru, mod, s5_d.reshape(DEPTH, 1, S5_W), w_glu,
      b_glu.reshape(DEPTH, 1, S5_W), w_out, ln_g.reshape(DEPTH, 1, D), ln_b.reshape(DEPTH, 1, D),
      rw_split, rw_hi, rb)


HALF = N_TOK // 2
HALF_TILES = HALF // TILE


def _sort_kernel(gates_ref, pos_ref, cnt_ref, rank_s):
    lane = lax.broadcasted_iota(I32, (TILE, 128), 1)
    ri = lax.broadcasted_iota(I32, (TILE, TILE), 0)
    ci = lax.broadcasted_iota(I32, (TILE, TILE), 1)
    tri = jnp.where(ri >= ci, 1.0, 0.0).astype(BF16)

    def count(b, carry):
        rows = pl.ds(pl.multiple_of(b * TILE, TILE), TILE)
        g = gates_ref[rows, :]
        g_sel = jnp.sum(jnp.where(lane == 0, g, 0.0), axis=-1, keepdims=True)
        onehot = jnp.where(jnp.logical_and(lane < N_GROUPS, lane.astype(F32) == g_sel), 1.0, 0.0)
        cum = jnp.dot(tri, onehot.astype(BF16), preferred_element_type=F32) + carry
        rank_s[rows, :] = jnp.where(onehot > 0.0, cum, 0.0)
        return carry + jnp.sum(onehot, axis=0, keepdims=True)

    counts = lax.fori_loop(0, HALF_TILES, count, jnp.zeros((1, 128), F32))
    lane1 = lax.broadcasted_iota(I32, (1, 128), 1)
    c = [jnp.sum(jnp.where(lane1 == g, counts, 0.0), axis=-1, keepdims=True) for g in range(N_GROUPS - 1)]
    start = jnp.where(lane1 == 1, c[0], jnp.where(lane1 == 2, c[0] + c[1],
                      jnp.where(lane1 == 3, c[0] + c[1] + c[2], 0.0)))

    def place(b, _):
        rows = pl.ds(pl.multiple_of(b * TILE, TILE), TILE)
        rk = rank_s[rows, :]
        p = jnp.sum(jnp.where(rk > 0.0, rk + start - 1.0, 0.0), axis=-1, keepdims=True)
        pos_ref[rows, :] = p.astype(I32)
        return 0

    lax.fori_loop(0, HALF_TILES, place, 0)
    cnt_ref[0] = jnp.broadcast_to(counts, (8, 128))


def _sort_call(gates):
    return pl.pallas_call(
        _sort_kernel,
        grid=(2,),
        in_specs=[pl.BlockSpec((HALF, 128), lambda h: (h, 0))],
        out_specs=[pl.BlockSpec((HALF, 1), lambda h: (h, 0)), pl.BlockSpec((1, 8, 128), lambda h: (h, 0, 0))],
        out_shape=[jax.ShapeDtypeStruct((N_TOK, 1), I32), jax.ShapeDtypeStruct((2, 8, 128), F32)],
        scratch_shapes=[pltpu.VMEM((HALF, 128), F32)],
        compiler_params=_cparams(("arbitrary",)),
        name="group_sort",
    )(gates)


N_ITEMS = 4 * (HALF_TILES + N_GROUPS)
MOE_STEPS = HALF_TILES + N_ITEMS + HALF_TILES


def _moe_schedule(counts):
    start = jnp.cumsum(counts, axis=1) - counts
    lo = start // TILE
    hi = (start + counts + TILE - 1) // TILE
    n_g = jnp.where(counts > 0, hi - lo, 0)
    n_e = jnp.repeat(n_g, EPG, axis=1)
    lo_e = jnp.repeat(lo, EPG, axis=1)
    off_end = jnp.cumsum(n_e, axis=1)
    off = off_end - n_e
    total = off_end[:, -1:]
    w = jnp.arange(N_ITEMS, dtype=I32)[None, :]
    w_eff = jnp.minimum(w, total - 1)
    e_w = jnp.sum((w_eff[:, :, None] >= off_end[:, None, :]).astype(I32), axis=-1)
    tile = jnp.take_along_axis(lo_e, e_w, axis=1) + (w_eff - jnp.take_along_axis(off, e_w, axis=1))
    valid = (w < total).astype(I32)
    return e_w.reshape(-1), tile.reshape(-1), valid.reshape(-1)


def _moe_kernel(pos_ref, ite_ref, itt_ref, itv_ref,
                hm_ref, gates_ref, wg_ref, wu_ref, wd_ref, x1_ref, mod_ref, lng_ref, lnb_ref,
                o_ref, xs_s, gs_s, acc_s):
    h = pl.program_id(0)
    w = pl.program_id(1)
    base = h * HALF

    @pl.when(w < HALF_TILES)
    def _():
        @pl.when(w == 0)
        def _():
            acc_s[...] = jnp.zeros_like(acc_s)

        def body(r, c):
            p = pos_ref[base + w * TILE + r]
            xs_s[pl.ds(p, 1), :] = hm_ref[pl.ds(r, 1), :]
            gs_s[pl.ds(p, 1), :] = gates_ref[pl.ds(r, 1), :]
            return c

        lax.fori_loop(0, TILE, body, 0, unroll=8)

    @pl.when(jnp.logical_and(w >= HALF_TILES, w < HALF_TILES + N_ITEMS))
    def _():
        idx = h * N_ITEMS + (w - HALF_TILES)

        @pl.when(itv_ref[idx] > 0)
        def _():
            e = ite_ref[idx]
            rows = pl.ds(pl.multiple_of(itt_ref[idx] * TILE, TILE), TILE)
            x = xs_s[rows, :].astype(BF16)
            a = jnp.dot(x, wg_ref[0, 0].astype(BF16), preferred_element_type=F32)
            u = jnp.dot(x, wu_ref[0, 0].astype(BF16), preferred_element_type=F32)
            g = gs_s[rows, :]
            lane = lax.broadcasted_iota(I32, g.shape, 1)
            ge = jnp.sum(jnp.where(lane == e + GATE_LANE0, g, 0.0), axis=-1, keepdims=True)
            act = ((a * jax.nn.sigmoid(a)) * u * ge).astype(BF16)
            acc_s[rows, :] += jnp.dot(act, wd_ref[0, 0].astype(BF16), preferred_element_type=F32)

    @pl.when(w >= HALF_TILES + N_ITEMS)
    def _():
        j = w - HALF_TILES - N_ITEMS

        def body(r, c):
            p = pos_ref[base + j * TILE + r]
            o_ref[pl.ds(r, 1), :] = acc_s[pl.ds(p, 1), :]
            return c

        lax.fori_loop(0, TILE, body, 0, unroll=8)
        r = jnp.where(h == 0, 0, 1 + j // CHUNKS)
        g2 = mod_ref[0, pl.ds(r, 1), 5 * D:6 * D]
        o_ref[...] = _layer_norm(ALPHA * x1_ref[...] + g2 * o_ref[...], lng_ref[0], lnb_ref[0])


def _moe_call(layer, pos, it_e, it_t, it_v, hm, gates, w_gate, w_up, w_down, x1, mod, ln_g, ln_b):
    def in_tile(h, w, *_):
        return (h * HALF_TILES + jnp.minimum(w, HALF_TILES - 1), 0)

    def out_tile(h, w, *_):
        return (h * HALF_TILES + jnp.clip(w - HALF_TILES - N_ITEMS, 0, HALF_TILES - 1), 0)

    def expert(h, w, pos_r, ite_r, itt_r, itv_r):
        return (layer, ite_r[h * N_ITEMS + jnp.clip(w - HALF_TILES, 0, N_ITEMS - 1)], 0, 0)

    vec = lambda n: pl.BlockSpec((1, 1, n), lambda h, w, *_: (layer, 0, 0))
    grid_spec = pltpu.PrefetchScalarGridSpec(
        num_scalar_prefetch=4,
        grid=(2, MOE_STEPS),
        in_specs=[
            pl.BlockSpec((TILE, D), in_tile),
            pl.BlockSpec((TILE, 128), in_tile),
            pl.BlockSpec((1, 1, D, D_EXPERT), expert),
            pl.BlockSpec((1, 1, D, D_EXPERT), expert),
            pl.BlockSpec((1, 1, D_EXPERT, D), expert),
            pl.BlockSpec((TILE, D), out_tile),
            pl.BlockSpec((1, MOD_ROWS, N_MOD * D), lambda h, w, *_: (layer, 0, 0)),
            vec(D), vec(D),
        ],
        out_specs=pl.BlockSpec((TILE, D), out_tile),
        scratch_shapes=[
            pltpu.VMEM((HALF, D), F32),
            pltpu.VMEM((HALF, 128), F32),
            pltpu.VMEM((HALF, D), F32),
        ],
    )
    return pl.pallas_call(
        _moe_kernel,
        grid_spec=grid_spec,
        out_shape=jax.ShapeDtypeStruct((N_TOK, D), F32),
        compiler_params=pltpu.CompilerParams(dimension_semantics=("arbitrary", "arbitrary"),
                                             vmem_limit_bytes=MOE_VMEM_LIMIT),
        name="moe",
    )(pos, it_e, it_t, it_v, hm, gates, w_gate, w_up, w_down, x1, mod,
      ln_g.reshape(DEPTH, 1, D), ln_b.reshape(DEPTH, 1, D))


def _rope_tables():
    rows = LAT_T // GRID_W
    row = jnp.repeat(jnp.arange(rows, dtype=F32), GRID_W)
    col = jnp.tile(jnp.arange(GRID_W, dtype=F32), rows)
    n_freq = QK // 4
    inv = ROPE_BASE ** (-jnp.arange(n_freq, dtype=F32) / n_freq)
    ang_r = row[:, None] * inv
    ang_c = col[:, None] * inv
    cos64 = jnp.concatenate([jnp.cos(ang_r), jnp.cos(ang_r), jnp.cos(ang_c), jnp.cos(ang_c)], axis=1)
    sin64 = jnp.concatenate([-jnp.sin(ang_r), jnp.sin(ang_r), -jnp.sin(ang_c), jnp.sin(ang_c)], axis=1)
    cos = jnp.concatenate([jnp.tile(cos64, (1, 2)), jnp.ones((SUB_ROWS, 128), F32)], axis=0)
    sin = jnp.concatenate([jnp.tile(sin64, (1, 2)), jnp.zeros((SUB_ROWS, 128), F32)], axis=0)
    return cos, sin


def _permute_w_in(w_in):
    qk = w_in[:, :, 0:1024].reshape(DEPTH, D, 2, 2, HEADS, QK)
    qk = qk.transpose(0, 1, 2, 4, 3, 5).reshape(DEPTH, D, 1024)
    return jnp.concatenate([qk, w_in[:, :, 1024:]], axis=-1).astype(BF16)


def _s5_params(a_re, a_im, b_re, b_im, c_re, c_im, log_dt):
    dt = jnp.exp(log_dt)[..., None]
    mag = jnp.exp(a_re * dt)
    abar_r = mag * jnp.cos(a_im * dt)
    abar_i = mag * jnp.sin(a_im * dt)
    den = a_re * a_re + a_im * a_im
    nr = abar_r - 1.0
    coef_r = (nr * a_re + abar_i * a_im) / den
    coef_i = (abar_i * a_re - nr * a_im) / den
    bbar_r = coef_r[..., None] * b_re - coef_i[..., None] * b_im
    bbar_i = coef_r[..., None] * b_im + coef_i[..., None] * b_re
    eye = jnp.eye(S5_G, dtype=F32)
    def b_block(bb):
        m = jnp.einsum('ldgpc,gh->ldgchp', bb, eye)
        return m.reshape(DEPTH, 2, S5_W, S5_N)
    b_blk = jnp.concatenate([b_block(bbar_r), b_block(bbar_i)], axis=-1).astype(BF16)
    def c_block(cc):
        m = jnp.einsum('ldgcp,gh->ldgphc', cc, eye)
        return m.reshape(DEPTH, 2, S5_N, S5_W)
    c_blk = jnp.concatenate([c_block(c_re), c_block(-c_im)], axis=-2).astype(BF16)
    a_bar = jnp.concatenate([abar_r.reshape(DEPTH, 2, S5_N), abar_i.reshape(DEPTH, 2, S5_N)], axis=-1)
    return b_blk, c_blk, a_bar


def _block_diag(w):
    eye = jnp.eye(LRU_BLOCKS, dtype=F32)
    m = jnp.einsum('ldkij,kh->ldkihj', w, eye)
    return m.reshape(DEPTH, 2, LRU_W, LRU_W).astype(BF16)


def kernel(x_prompt, x_sample, cache_k, cache_v, state_s5, state_lru, c, c_ctx, w_ada, b_ada, w_in, w_out, lam_q1, lam_k1, lam_q2, lam_k2, subln_g, s5_a_re, s5_a_im, s5_b_re, s5_b_im, s5_c_re, s5_c_im, s5_log_dt, s5_d, s5_w_glu, s5_b_glu, lru_conv_w, lru_conv_b, lru_w_a, lru_b_a, lru_w_i, lru_b_i, lru_lambda, ln1_g, ln1_b, ln2_g, ln2_b, router_group_w, router_group_b, router_expert_w, router_expert_b, moe_w_gate, moe_w_up, moe_w_down):
    x = jnp.concatenate([x_prompt.reshape(CTX_B * CTX_T, D), x_sample.reshape(LAT_B * LAT_T, D)], axis=0)
    cond = jnp.concatenate([c_ctx[None, :], c, jnp.zeros((MOD_ROWS - 1 - LAT_B, D), F32)], axis=0)
    mod = _ada_call(cond, w_ada, b_ada)

    w_in_p = _permute_w_in(w_in)
    w_out_b = w_out.astype(BF16)
    w_glu_b = s5_w_glu.astype(BF16)
    rope_cos, rope_sin = _rope_tables()
    b_blk, c_blk, a_bar = _s5_params(s5_a_re, s5_a_im, s5_b_re, s5_b_im, s5_c_re, s5_c_im, s5_log_dt)
    wa_blk = _block_diag(lru_w_a)
    wi_blk = _block_diag(lru_w_i)
    ba = lru_b_a.reshape(DEPTH, 2, 1, LRU_W)
    bi = lru_b_i.reshape(DEPTH, 2, 1, LRU_W)
    sp = jax.nn.softplus(-lru_lambda).reshape(DEPTH, 2, 1, LRU_W)
    lam = (jnp.exp(jnp.sum(lam_q1 * lam_k1, axis=-1)) - jnp.exp(jnp.sum(lam_q2 * lam_k2, axis=-1)))
    rw = jnp.concatenate([router_group_w, router_expert_w,
                          jnp.zeros((DEPTH, D, 128 - N_GROUPS - N_EXPERTS), F32)], axis=-1)
    rw_hi = rw.astype(BF16)
    rw_split = jnp.concatenate([rw_hi, (rw - rw_hi.astype(F32)).astype(BF16)], axis=-1)
    rb = jnp.concatenate([router_group_b, router_expert_b,
                          jnp.zeros((DEPTH, 128 - N_GROUPS - N_EXPERTS), F32)], axis=-1).reshape(DEPTH, 1, 128)
    s5_h0 = state_s5.reshape(LAT_B, DEPTH, 2, 2 * S5_N)
    lru_h0 = state_lru.transpose(1, 0, 2, 3)

    kc = jnp.zeros((CTX_B, DEPTH, HEADS, CTX_T, 2 * QK), F32)
    vc = jnp.zeros((CTX_B, DEPTH, HEADS, CTX_T, DV), F32)
    s5_states = []
    lru_states = []
    for l in range(DEPTH):
        lambda_init = 0.8 - 0.6 * math.exp(-0.3 * l)
        lam_l = (lam[l] + lambda_init).reshape(1)
        q, k, v, kc, vc, u_tm, xc_tm, g_lru = _inproj_call(
            l, x, mod, w_in_p, rope_cos, rope_sin, lru_conv_w, lru_conv_b, kc, vc)
        o_ctx = _att_ctx_call(l, lam_l, q, k, v, subln_g, 1.0 - lambda_init)
        o_lat = _att_lat_call(l, lam_l, q, k, v, cache_k, cache_v, subln_g, 1.0 - lambda_init)
        yf, yb, hend = _s5_call(l, u_tm.reshape(N_TOK, S5_W), b_blk, c_blk, a_bar)
        yfix = _s5_fix_call(l, hend, s5_h0[:, l], a_bar, c_blk)
        hl, lru_end = _lru_call(l, xc_tm.reshape(N_TOK, LRU_W), wa_blk, ba, wi_blk, bi, sp, lru_h0)
        x1, hm, gates = _outproj_call(l, x, o_ctx, o_lat, yf, yb, yfix, u_tm, hl, g_lru, mod, s5_d,
                                      w_glu_b, s5_b_glu, w_out_b, ln1_g, ln1_b, rw_split, rw_hi, rb)
        pos, counts = _sort_call(gates)
        it_e, it_t, it_v = _moe_schedule(counts[:, 0, :N_GROUPS].astype(I32))
        x = _moe_call(l, pos.reshape(N_TOK), it_e, it_t, it_v, hm, gates, moe_w_gate, moe_w_up,
                      moe_w_down, x1, mod, ln2_g, ln2_b)
        s5_states.append(hend[:, :N_CTX].reshape(2, CTX_B, 2, S5_G, S5_P).transpose(1, 0, 2, 3, 4))
        lru_states.append(lru_end[:, :N_CTX].transpose(1, 0, 2))

    y_p = x[:CTX_B * CTX_T].reshape(CTX_B, CTX_T, D)
    y_s = x[CTX_B * CTX_T:].reshape(LAT_B, LAT_T, D)
    return (y_p, y_s, kc, vc, jnp.stack(s5_states, axis=1), jnp.stack(lru_states, axis=1))
```

```python
import functools
import math

import jax
import jax.numpy as jnp
from jax import lax
from jax.experimental import pallas as pl
from jax.experimental.pallas import tpu as pltpu

F32 = jnp.float32
BF16 = jnp.bfloat16
I32 = jnp.int32

D = 1024
DEPTH = 4
CTX_B = 16
CTX_T = 256
LAT_B = 2
LAT_T = 2048
PAST = 256
GRID_W = 64
HEADS = 4
QK = 64
DV = 128
S5_W = 256
S5_G = 16
S5_C = 16
S5_P = 64
S5_N = S5_G * S5_P
LRU_W = 256
LRU_BLOCKS = 4
LRU_C = 8.0
N_GROUPS = 4
EPG = 4
N_EXPERTS = 16
D_EXPERT = 512
N_MOD = 6
IN_W = 2304
ROPE_BASE = 10000.0
ALPHA = (2 * DEPTH) ** 0.25
EPS = 1e-5

TILE = 256
N_TOK = CTX_B * CTX_T + LAT_B * LAT_T
N_SEQ = N_TOK // TILE
N_CTX = CTX_B
CHUNKS = LAT_T // TILE
N_LAT = LAT_B * CHUNKS
MOD_ROWS = 8
GATE_LANE0 = 4
VMEM_LIMIT = 56 * 1024 * 1024
MOE_VMEM_LIMIT = 60 * 1024 * 1024

GROUP_SEQS = 8
GROUP_ROWS = GROUP_SEQS * TILE
N_GROUP_TILES = N_TOK // GROUP_ROWS
CTX_GROUP_TILES = N_CTX // GROUP_SEQS
SUB_ROWS = 512
SUBS = GROUP_ROWS // SUB_ROWS
SUB_SEQS = SUB_ROWS // TILE


def _cparams(sem):
    return pltpu.CompilerParams(dimension_semantics=sem, vmem_limit_bytes=VMEM_LIMIT)


def _group_mod_row(i):
    return jnp.where(i < CTX_GROUP_TILES, 0, i - CTX_GROUP_TILES + 1)


def _ada_kernel(c_ref, w_ref, b_ref, o_ref):
    c = c_ref[...]
    s = (c * jax.nn.sigmoid(c)).astype(BF16)
    o_ref[0] = jnp.dot(s, w_ref[0].astype(BF16), preferred_element_type=F32) + b_ref[0]


def _ada_call(cond, w_ada, b_ada):
    tn = 1536
    return pl.pallas_call(
        _ada_kernel,
        grid=(DEPTH, N_MOD * D // tn),
        in_specs=[
            pl.BlockSpec((MOD_ROWS, D), lambda l, j: (0, 0)),
            pl.BlockSpec((1, D, tn), lambda l, j: (l, 0, j)),
            pl.BlockSpec((1, 1, tn), lambda l, j: (l, 0, j)),
        ],
        out_specs=pl.BlockSpec((1, MOD_ROWS, tn), lambda l, j: (l, 0, j)),
        out_shape=jax.ShapeDtypeStruct((DEPTH, MOD_ROWS, N_MOD * D), F32),
        compiler_params=_cparams(("arbitrary", "arbitrary")),
        name="adaln",
    )(cond, w_ada, b_ada.reshape(DEPTH, 1, N_MOD * D))


def _inproj_kernel(x_ref, mod_ref, w_ref, cos_ref, sin_ref, cw_ref, cb_ref, kc_in, vc_in,
                   q_ref, k_ref, v_ref, kc_ref, vc_ref, u_ref, xc_ref, g_ref, u_s, xl_s):
    del kc_in, vc_in
    i = pl.program_id(0)
    s = pl.program_id(1)
    r = _group_mod_row(i)
    sh = mod_ref[0, pl.ds(r, 1), 0:D]
    sc = mod_ref[0, pl.ds(r, 1), D:2 * D]
    xm = (x_ref[...] * (1.0 + sc) + sh).astype(BF16)
    proj = jnp.dot(xm, w_ref[0], preferred_element_type=F32)

    qk = proj[:, 0:1024]
    cos = jnp.concatenate([cos_ref[...]] * 8, axis=1)
    sin = jnp.concatenate([sin_ref[...]] * 8, axis=1)
    lane = lax.broadcasted_iota(I32, qk.shape, 1)
    swapped = jnp.where((lane & 31) < 16, pltpu.roll(qk, 1024 - 16, 1), pltpu.roll(qk, 16, 1))
    qk = qk * cos + swapped * sin
    q_ref[...] = qk[:, 0:512].astype(BF16)
    k_ref[...] = qk[:, 512:1024].astype(BF16)
    v = proj[:, 1024:1536]
    v_ref[...] = v.astype(BF16)

    @pl.when(i < CTX_GROUP_TILES)
    def _():
        for jj in range(SUB_SEQS):
            rows = slice(jj * TILE, (jj + 1) * TILE)
            for h in range(HEADS):
                kc_ref[jj, 0, h] = qk[rows, 512 + h * 128:512 + (h + 1) * 128]
                vc_ref[jj, 0, h] = v[rows, h * 128:(h + 1) * 128]

    g_ref[...] = proj[:, 2048:2304]
    sub_rows = pl.ds(pl.multiple_of(s * SUB_ROWS, SUB_ROWS), SUB_ROWS)
    u_s[sub_rows, :] = proj[:, 1536:1792]
    xl_s[sub_rows, :] = proj[:, 1792:2048]

    @pl.when(s == SUBS - 1)
    def _():
        xl = xl_s[...]
        row = lax.broadcasted_iota(I32, xl.shape, 0)
        is_ctx = i < CTX_GROUP_TILES
        pos = jnp.where(is_ctx, row & (TILE - 1), row)
        last = jnp.where(is_ctx, TILE - 1, GROUP_ROWS - 1)
        x_m1 = jnp.where(pos == 0, 0.0, pltpu.roll(xl, 1, 0))
        x_p1 = jnp.where(pos == last, 0.0, pltpu.roll(xl, GROUP_ROWS - 1, 0))
        x_p2 = jnp.where(pos >= last - 1, 0.0, pltpu.roll(xl, GROUP_ROWS - 2, 0))
        cw = cw_ref[0]
        xc = cb_ref[0] + x_m1 * cw[0:1] + xl * cw[1:2] + x_p1 * cw[2:3] + x_p2 * cw[3:4]
        for j in range(GROUP_SEQS):
            xc_ref[:, j, :] = xc[j * TILE:(j + 1) * TILE]
            u_ref[:, j, :] = u_s[j * TILE:(j + 1) * TILE, :]


def _inproj_call(layer, x, mod, w_in_p, rope_cos, rope_sin, conv_w, conv_b, kc, vc):
    n_ctx_blocks = N_CTX // SUB_SEQS

    def rope_idx(i, s):
        return (jnp.where(i < CTX_GROUP_TILES, SUBS, s), 0)

    def cache_idx(i, s):
        return (jnp.minimum(i * SUBS + s, n_ctx_blocks - 1), layer, 0, 0, 0)

    row_spec = lambda w: pl.BlockSpec((SUB_ROWS, w), lambda i, s: (i * SUBS + s, 0))
    tm_spec = pl.BlockSpec((TILE, GROUP_SEQS, 256), lambda i, s: (0, i, 0))
    cache_spec = pl.BlockSpec((SUB_SEQS, 1, HEADS, TILE, 128), cache_idx)
    return pl.pallas_call(
        _inproj_kernel,
        grid=(N_GROUP_TILES, SUBS),
        in_specs=[
            row_spec(D),
            pl.BlockSpec((1, MOD_ROWS, N_MOD * D), lambda i, s: (layer, 0, 0)),
            pl.BlockSpec((1, D, IN_W), lambda i, s: (layer, 0, 0)),
            pl.BlockSpec((SUB_ROWS, 128), rope_idx),
            pl.BlockSpec((SUB_ROWS, 128), rope_idx),
            pl.BlockSpec((1, 4, LRU_W), lambda i, s: (layer, 0, 0)),
            pl.BlockSpec((1, 1, LRU_W), lambda i, s: (layer, 0, 0)),
            pl.BlockSpec(memory_space=pl.ANY),
            pl.BlockSpec(memory_space=pl.ANY),
        ],
        out_specs=[
            row_spec(512), row_spec(512), row_spec(512),
            cache_spec, cache_spec,
            tm_spec, tm_spec,
            row_spec(LRU_W),
        ],
        out_shape=[
            jax.ShapeDtypeStruct((N_TOK, 512), BF16),
            jax.ShapeDtypeStruct((N_TOK, 512), BF16),
            jax.ShapeDtypeStruct((N_TOK, 512), BF16),
            jax.ShapeDtypeStruct(kc.shape, F32),
            jax.ShapeDtypeStruct(vc.shape, F32),
            jax.ShapeDtypeStruct((TILE, N_SEQ, S5_W), F32),
            jax.ShapeDtypeStruct((TILE, N_SEQ, LRU_W), F32),
            jax.ShapeDtypeStruct((N_TOK, LRU_W), F32),
        ],
        scratch_shapes=[pltpu.VMEM((GROUP_ROWS, S5_W), F32), pltpu.VMEM((GROUP_ROWS, LRU_W), F32)],
        input_output_aliases={7: 3, 8: 4},
        compiler_params=_cparams(("arbitrary", "arbitrary")),
        name="inproj",
    )(x, mod, w_in_p, rope_cos, rope_sin, conv_w, conv_b.reshape(DEPTH, 1, LRU_W), kc, vc)


_NT = (((1,), (1,)), ((), ()))


def _stack_maps(q):
    lane = lax.broadcasted_iota(I32, q.shape, 1)
    zero = jnp.zeros_like(q)
    qs = q * (QK ** -0.5)
    return jnp.concatenate([jnp.where(lane < QK, qs, zero), jnp.where(lane >= QK, qs, zero)], axis=0)


def _diff_attention(q, keys, values, lam, g, post_scale):
    rows = q.shape[0]
    qq = _stack_maps(q)
    scores = [lax.dot_general(qq, k, _NT, preferred_element_type=F32) for k in keys]
    m = scores[0].max(axis=-1, keepdims=True)
    for sc in scores[1:]:
        m = jnp.maximum(m, sc.max(axis=-1, keepdims=True))
    z = None
    o = None
    for sc, v in zip(scores, values):
        e = jnp.exp(sc - m)
        ez = e.sum(axis=-1, keepdims=True)
        eo = jnp.dot(e.astype(BF16), v, preferred_element_type=F32)
        z = ez if z is None else z + ez
        o = eo if o is None else o + eo
    o = o * (1.0 / z)
    o = o[:rows] - lam * o[rows:]
    ms = jnp.mean(o * o, axis=-1, keepdims=True)
    return (o * lax.rsqrt(ms + EPS) * g) * post_scale


def _att_ctx_kernel(lam_ref, q_ref, k_ref, v_ref, g_ref, o_ref, *, post_scale):
    lam = lam_ref[0]
    for h in range(HEADS):
        cols = slice(h * 128, (h + 1) * 128)
        o = _diff_attention(q_ref[:, cols], [k_ref[:, cols]], [v_ref[:, cols]], lam, g_ref[0], post_scale)
        o_ref[:, cols] = o.astype(BF16)


def _att_ctx_call(layer, lam, q, k, v, subln_g, post_scale):
    blk = pl.BlockSpec((TILE, HEADS * 128), lambda b: (b, 0))
    return pl.pallas_call(
        functools.partial(_att_ctx_kernel, post_scale=post_scale),
        grid=(CTX_B,),
        in_specs=[
            pl.BlockSpec(memory_space=pltpu.SMEM),
            blk, blk, blk,
            pl.BlockSpec((1, 1, DV), lambda b: (layer, 0, 0)),
        ],
        out_specs=blk,
        out_shape=jax.ShapeDtypeStruct((CTX_B * CTX_T, HEADS * DV), BF16),
        compiler_params=_cparams(("arbitrary",)),
        name="att_ctx",
    )(lam, q, k, v, subln_g.reshape(DEPTH, 1, DV))


def _att_lat_kernel(lam_ref, q_ref, k_ref, v_ref, ck_ref, cv_ref, g_ref, o_ref, *, post_scale):
    o = _diff_attention(q_ref[...], [k_ref[...], ck_ref[...].astype(BF16)],
                        [v_ref[...], cv_ref[...].astype(BF16)], lam_ref[0], g_ref[0], post_scale)
    o_ref[...] = o.astype(BF16)


def _att_lat_call(layer, lam, q, k, v, cache_k, cache_v, subln_g, post_scale):
    lat0 = N_CTX
    cache_spec = pl.BlockSpec((None, None, None, PAST, 128), lambda b, h, t: (b, layer, h, 0, 0))
    return pl.pallas_call(
        functools.partial(_att_lat_kernel, post_scale=post_scale),
        grid=(LAT_B, HEADS, CHUNKS),
        in_specs=[
            pl.BlockSpec(memory_space=pltpu.SMEM),
            pl.BlockSpec((TILE, 128), lambda b, h, t: (lat0 + b * CHUNKS + t, h)),
            pl.BlockSpec((LAT_T, 128), lambda b, h, t: (lat0 // CHUNKS + b, h)),
            pl.BlockSpec((LAT_T, 128), lambda b, h, t: (lat0 // CHUNKS + b, h)),
            cache_spec, cache_spec,
            pl.BlockSpec((1, 1, DV), lambda b, h, t: (layer, 0, 0)),
        ],
        out_specs=pl.BlockSpec((TILE, 128), lambda b, h, t: (b * CHUNKS + t, h)),
        out_shape=jax.ShapeDtypeStruct((LAT_B * LAT_T, HEADS * DV), BF16),
        compiler_params=_cparams(("arbitrary", "arbitrary", "arbitrary")),
        name="att_lat",
    )(lam, q, k, v, cache_k, cache_v, subln_g.reshape(DEPTH, 1, DV))


S5_TB = 32
S5_LANES = 128


def _cmul(ar, ai, br, bi):
    return ar * br - ai * bi, ar * bi + ai * br


def _s5_kernel(uf_ref, ub_ref, bf_ref, bb_ref, cf_ref, cb_ref, a_ref, yf_ref, yb_ref, hend_ref,
               hf_s, hb_s, buf_f, buf_b):
    i = pl.program_id(0)

    @pl.when(i == 0)
    def _():
        hf_s[...] = jnp.zeros_like(hf_s)
        hb_s[...] = jnp.zeros_like(hb_s)

    buf_f[...] = jnp.dot(uf_ref[...].astype(BF16), bf_ref[0, 0], preferred_element_type=F32)
    buf_b[...] = jnp.dot(ub_ref[...].astype(BF16), bb_ref[0, 0], preferred_element_type=F32)

    for c in range(S5_N // S5_LANES):
        re = slice(c * S5_LANES, (c + 1) * S5_LANES)
        im = slice(S5_N + c * S5_LANES, S5_N + (c + 1) * S5_LANES)
        afr = a_ref[0, 0:1, re]
        afi = a_ref[0, 0:1, im]
        abr = a_ref[0, 1:2, re]
        abi = a_ref[0, 1:2, im]

        def body(t, carry, re=re, im=im, afr=afr, afi=afi, abr=abr, abi=abi):
            hfr, hfi, hbr, hbi = carry
            rf = pl.multiple_of(t * N_SEQ, N_SEQ)
            rb = pl.multiple_of((S5_TB - 1 - t) * N_SEQ, N_SEQ)
            pr, pi = _cmul(afr, afi, hfr, hfi)
            hfr = pr + buf_f[pl.ds(rf, N_SEQ), re]
            hfi = pi + buf_f[pl.ds(rf, N_SEQ), im]
            buf_f[pl.ds(rf, N_SEQ), re] = hfr
            buf_f[pl.ds(rf, N_SEQ), im] = hfi
            pr, pi = _cmul(abr, abi, hbr, hbi)
            hbr = pr + buf_b[pl.ds(rb, N_SEQ), re]
            hbi = pi + buf_b[pl.ds(rb, N_SEQ), im]
            buf_b[pl.ds(rb, N_SEQ), re] = hbr
            buf_b[pl.ds(rb, N_SEQ), im] = hbi
            return hfr, hfi, hbr, hbi

        out = lax.fori_loop(0, S5_TB, body, (hf_s[:, re], hf_s[:, im], hb_s[:, re], hb_s[:, im]))
        hf_s[:, re] = out[0]
        hf_s[:, im] = out[1]
        hb_s[:, re] = out[2]
        hb_s[:, im] = out[3]

    yf_ref[...] = jnp.dot(buf_f[...].astype(BF16), cf_ref[0, 0], preferred_element_type=F32)
    yb_ref[...] = jnp.dot(buf_b[...].astype(BF16), cb_ref[0, 0], preferred_element_type=F32)

    @pl.when(i == pl.num_programs(0) - 1)
    def _():
        hend_ref[0] = hf_s[...]
        hend_ref[1] = hb_s[...]


def _s5_call(layer, u2, b_blk, c_blk, a_bar):
    nb = CTX_T // S5_TB
    rows = S5_TB * N_SEQ
    return pl.pallas_call(
        _s5_kernel,
        grid=(nb,),
        in_specs=[
            pl.BlockSpec((rows, S5_W), lambda i: (i, 0)),
            pl.BlockSpec((rows, S5_W), lambda i: (nb - 1 - i, 0)),
            pl.BlockSpec((1, 1, S5_W, 2 * S5_N), lambda i: (layer, 0, 0, 0)),
            pl.BlockSpec((1, 1, S5_W, 2 * S5_N), lambda i: (layer, 1, 0, 0)),
            pl.BlockSpec((1, 1, 2 * S5_N, S5_W), lambda i: (layer, 0, 0, 0)),
            pl.BlockSpec((1, 1, 2 * S5_N, S5_W), lambda i: (layer, 1, 0, 0)),
            pl.BlockSpec((1, 2, 2 * S5_N), lambda i: (layer, 0, 0)),
        ],
        out_specs=[
            pl.BlockSpec((rows, S5_W), lambda i: (i, 0)),
            pl.BlockSpec((rows, S5_W), lambda i: (nb - 1 - i, 0)),
            pl.BlockSpec((2, N_SEQ, 2 * S5_N), lambda i: (0, 0, 0)),
        ],
        out_shape=[
            jax.ShapeDtypeStruct((N_TOK, S5_W), F32),
            jax.ShapeDtypeStruct((N_TOK, S5_W), F32),
            jax.ShapeDtypeStruct((2, N_SEQ, 2 * S5_N), F32),
        ],
        scratch_shapes=[
            pltpu.VMEM((N_SEQ, 2 * S5_N), F32),
            pltpu.VMEM((N_SEQ, 2 * S5_N), F32),
            pltpu.VMEM((rows, 2 * S5_N), F32),
            pltpu.VMEM((rows, 2 * S5_N), F32),
        ],
        compiler_params=_cparams(("arbitrary",)),
        name="s5_scan",
    )(u2, u2, b_blk, b_blk, c_blk, c_blk, a_bar)


def _s5_fix_kernel(hend_ref, h0_ref, a_ref, cf_ref, cb_ref, o_ref, pf_s, pb_s, cf_s, cb_s):
    s = pl.program_id(0)

    @pl.when(s == 0)
    def _():
        row8 = lax.broadcasted_iota(I32, (8, S5_N), 0)
        for d, tab in ((0, pf_s), (1, pb_s)):
            ar = a_ref[0, d:d + 1, 0:S5_N]
            ai = a_ref[0, d:d + 1, S5_N:2 * S5_N]
            pr, pi = ar, ai
            r8 = jnp.zeros((8, S5_N), F32)
            i8 = jnp.zeros((8, S5_N), F32)
            for r in range(8):
                if r:
                    pr, pi = _cmul(pr, pi, ar, ai)
                at = r if d == 0 else 7 - r
                r8 = jnp.where(row8 == at, pr, r8)
                i8 = jnp.where(row8 == at, pi, i8)
            base = 0 if d == 0 else TILE - 8
            tab[base:base + 8, 0:S5_N] = r8
            tab[base:base + 8, S5_N:2 * S5_N] = i8
            m = 8
            while m < TILE:
                if d == 0:
                    src, dst, top = slice(0, m), slice(m, 2 * m), slice(m - 1, m)
                else:
                    src, dst, top = slice(TILE - m, TILE), slice(TILE - 2 * m, TILE - m), slice(TILE - m, TILE - m + 1)
                mr, mi = tab[top, 0:S5_N], tab[top, S5_N:2 * S5_N]
                nr, ni = _cmul(tab[src, 0:S5_N], tab[src, S5_N:2 * S5_N], mr, mi)
                tab[dst, 0:S5_N] = nr
                tab[dst, S5_N:2 * S5_N] = ni
                m *= 2

        for b in range(LAT_B):
            ar, ai = pf_s[TILE - 1:TILE, 0:S5_N], pf_s[TILE - 1:TILE, S5_N:2 * S5_N]
            cr, ci = h0_ref[b, 0:1, 0:S5_N], h0_ref[b, 0:1, S5_N:2 * S5_N]
            for j in range(CHUNKS):
                row = b * CHUNKS + j
                cf_s[row:row + 1, 0:S5_N] = cr
                cf_s[row:row + 1, S5_N:2 * S5_N] = ci
                pr, pi = _cmul(ar, ai, cr, ci)
                cr = pr + hend_ref[0, N_CTX + row:N_CTX + row + 1, 0:S5_N]
                ci = pi + hend_ref[0, N_CTX + row:N_CTX + row + 1, S5_N:2 * S5_N]
            ar, ai = pb_s[0:1, 0:S5_N], pb_s[0:1, S5_N:2 * S5_N]
            cr, ci = h0_ref[b, 1:2, 0:S5_N], h0_ref[b, 1:2, S5_N:2 * S5_N]
            for j in reversed(range(CHUNKS)):
                row = b * CHUNKS + j
                cb_s[row:row + 1, 0:S5_N] = cr
                cb_s[row:row + 1, S5_N:2 * S5_N] = ci
                pr, pi = _cmul(ar, ai, cr, ci)
                cr = pr + hend_ref[1, N_CTX + row:N_CTX + row + 1, 0:S5_N]
                ci = pi + hend_ref[1, N_CTX + row:N_CTX + row + 1, S5_N:2 * S5_N]

    acc = None
    for tab, car, c_ref in ((pf_s, cf_s, cf_ref), (pb_s, cb_s, cb_ref)):
        cr = car[pl.ds(s, 1), 0:S5_N]
        ci = car[pl.ds(s, 1), S5_N:2 * S5_N]
        hr, hi = _cmul(tab[:, 0:S5_N], tab[:, S5_N:2 * S5_N], cr, ci)
        h = jnp.concatenate([hr, hi], axis=1).astype(BF16)
        y = jnp.dot(h, c_ref[0, 0], preferred_element_type=F32)
        acc = y if acc is None else acc + y
    o_ref[...] = acc


def _s5_fix_call(layer, hend, h0, a_bar, c_blk):
    return pl.pallas_call(
        _s5_fix_kernel,
        grid=(N_LAT,),
        in_specs=[
            pl.BlockSpec((2, N_SEQ, 2 * S5_N), lambda s: (0, 0, 0)),
            pl.BlockSpec((LAT_B, 2, 2 * S5_N), lambda s: (0, 0, 0)),
            pl.BlockSpec((1, 2, 2 * S5_N), lambda s: (layer, 0, 0)),
            pl.BlockSpec((1, 1, 2 * S5_N, S5_W), lambda s: (layer, 0, 0, 0)),
            pl.BlockSpec((1, 1, 2 * S5_N, S5_W), lambda s: (layer, 1, 0, 0)),
        ],
        out_specs=pl.BlockSpec((TILE, S5_W), lambda s: (s, 0)),
        out_shape=jax.ShapeDtypeStruct((N_LAT * TILE, S5_W), F32),
        scratch_shapes=[
            pltpu.VMEM((TILE, 2 * S5_N), F32),
            pltpu.VMEM((TILE, 2 * S5_N), F32),
            pltpu.VMEM((N_LAT, 2 * S5_N), F32),
            pltpu.VMEM((N_LAT, 2 * S5_N), F32),
        ],
        compiler_params=_cparams(("arbitrary",)),
        name="s5_fix",
    )(hend, h0, a_bar, c_blk, c_blk)


LRU_ROWS = 1024


def _lru_kernel(xc_ref, wa_ref, ba_ref, wi_ref, bi_ref, sp_ref, h0_ref, out_ref, hend_ref,
                a_s, h_s, p_s):
    row = lax.broadcasted_iota(I32, (N_SEQ, LRU_W), 0)
    for d in range(2):
        def gates(cix, _, d=d):
            r0 = pl.multiple_of(cix * LRU_ROWS, LRU_ROWS)
            xc = xc_ref[pl.ds(r0, LRU_ROWS), :]
            xb = xc.astype(BF16)
            r = jax.nn.sigmoid(jnp.dot(xb, wa_ref[0, d], preferred_element_type=F32) + ba_ref[0, d])
            g = jax.nn.sigmoid(jnp.dot(xb, wi_ref[0, d], preferred_element_type=F32) + bi_ref[0, d])
            log_a = (-LRU_C) * r * sp_ref[0, d]
            a_s[pl.ds(r0, LRU_ROWS), :] = jnp.exp(log_a)
            th = jnp.tanh(log_a)
            h_s[pl.ds(r0, LRU_ROWS), :] = jnp.sqrt(-2.0 * th / (1.0 - th)) * g * xc
            return 0

        lax.fori_loop(0, N_TOK // LRU_ROWS, gates, 0)

        def scan(t, carry, d=d):
            h, p = carry
            tt = t if d == 0 else TILE - 1 - t
            r0 = pl.multiple_of(tt * N_SEQ, N_SEQ)
            a = a_s[pl.ds(r0, N_SEQ), :]
            h = a * h + h_s[pl.ds(r0, N_SEQ), :]
            p = a * p
            h_s[pl.ds(r0, N_SEQ), :] = h
            p_s[pl.ds(r0, N_SEQ), :] = p
            return h, p

        h_end, p_end = lax.fori_loop(0, TILE, scan,
                                     (jnp.zeros((N_SEQ, LRU_W), F32), jnp.ones((N_SEQ, LRU_W), F32)))
        hend_ref[d] = h_end

        carry_slab = jnp.zeros((N_SEQ, LRU_W), F32)
        for b in range(LAT_B):
            c = h0_ref[0, b, d:d + 1, :]
            order = range(CHUNKS) if d == 0 else reversed(range(CHUNKS))
            for j in order:
                s = N_CTX + b * CHUNKS + j
                carry_slab = jnp.where(row == s, c, carry_slab)
                c = h_end[s:s + 1, :] + p_end[s:s + 1, :] * c

        def fix(t, _, d=d, carry_slab=carry_slab):
            r0 = pl.multiple_of(t * N_SEQ, N_SEQ)
            v = h_s[pl.ds(r0, N_SEQ), :] + p_s[pl.ds(r0, N_SEQ), :] * carry_slab
            if d == 0:
                out_ref[pl.ds(r0, N_SEQ), :] = v
            else:
                out_ref[pl.ds(r0, N_SEQ), :] += v
            return 0

        lax.fori_loop(0, TILE, fix, 0)


def _lru_call(layer, xc2, wa, ba, wi, bi, sp, h0):
    full = lambda shape: pl.BlockSpec(shape, lambda i: (0,) * len(shape))
    per_layer = lambda shape: pl.BlockSpec((1,) + shape, lambda i: (layer,) + (0,) * len(shape))
    return pl.pallas_call(
        _lru_kernel,
        grid=(1,),
        in_specs=[
            full((N_TOK, LRU_W)),
            per_layer((2, LRU_W, LRU_W)),
            per_layer((2, 1, LRU_W)),
            per_layer((2, LRU_W, LRU_W)),
            per_layer((2, 1, LRU_W)),
            per_layer((2, 1, LRU_W)),
            per_layer((LAT_B, 2, LRU_W)),
        ],
        out_specs=[full((N_TOK, LRU_W)), full((2, N_SEQ, LRU_W))],
        out_shape=[
            jax.ShapeDtypeStruct((N_TOK, LRU_W), F32),
            jax.ShapeDtypeStruct((2, N_SEQ, LRU_W), F32),
        ],
        scratch_shapes=[pltpu.VMEM((N_TOK, LRU_W), F32)] * 3,
        compiler_params=_cparams(("arbitrary",)),
        name="rglru",
    )(xc2, wa, ba, wi, bi, sp, h0)


def _layer_norm(z, g, b):
    mu = jnp.mean(z, axis=-1, keepdims=True)
    zc = z - mu
    var = jnp.mean(zc * zc, axis=-1, keepdims=True)
    return zc * lax.rsqrt(var + EPS) * g + b


def _route(logits):
    lane_i = lax.broadcasted_iota(I32, logits.shape, 1)
    lane = lane_i.astype(F32)
    big = jnp.float32(1024.0)
    neg = jnp.float32(-jnp.inf)
    is_g = lane_i < N_GROUPS
    gmax = jnp.max(jnp.where(is_g, logits, neg), axis=-1, keepdims=True)
    g_sel = jnp.min(jnp.where(jnp.logical_and(is_g, logits == gmax), lane, big), axis=-1, keepdims=True)
    p_group = 1.0 / jnp.sum(jnp.where(is_g, jnp.exp(logits - gmax), 0.0), axis=-1, keepdims=True)
    e_idx = lane_i - GATE_LANE0
    e_group = (e_idx >> 2).astype(F32)
    in_g = jnp.logical_and(jnp.logical_and(e_idx >= 0, e_idx < N_EXPERTS), e_group == g_sel)
    v1 = jnp.max(jnp.where(in_g, logits, neg), axis=-1, keepdims=True)
    i1 = jnp.min(jnp.where(jnp.logical_and(in_g, logits == v1), lane, big), axis=-1, keepdims=True)
    rest = jnp.logical_and(in_g, lane != i1)
    v2 = jnp.max(jnp.where(rest, logits, neg), axis=-1, keepdims=True)
    i2 = jnp.min(jnp.where(jnp.logical_and(rest, logits == v2), lane, big), axis=-1, keepdims=True)
    e2 = jnp.exp(v2 - v1)
    inv = 1.0 / (1.0 + e2)
    w1 = inv * p_group
    w2 = e2 * inv * p_group
    return jnp.where(lane == i1, w1, jnp.where(lane == i2, w2, jnp.where(lane_i == 0, g_sel, 0.0)))


def _outproj_kernel(x_ref, oc_ref, ol_ref, yf_ref, yb_ref, yfix_ref, u_ref, hl_ref, g_ref, mod_ref,
                    d_ref, wglu_ref, bglu_ref, wout_ref, lng_ref, lnb_ref, rw_ref, rwhi_ref, rb_ref,
                    x1_ref, hm_ref, gates_ref, y_s, u_s, hl_s):
    i = pl.program_id(0)
    s = pl.program_id(1)

    @pl.when(s == 0)
    def _():
        for j in range(GROUP_SEQS):
            rows = slice(j * TILE, (j + 1) * TILE)
            y_s[rows, :] = yf_ref[:, j, :] + yb_ref[:, j, :]
            u_s[rows, :] = u_ref[:, j, :]
            hl_s[rows, :] = hl_ref[:, j, :]

    lat = i >= CTX_GROUP_TILES
    sub_rows = pl.ds(pl.multiple_of(s * SUB_ROWS, SUB_ROWS), SUB_ROWS)
    o_att = jnp.where(lat, ol_ref[...], oc_ref[...])
    y = y_s[sub_rows, :] + jnp.where(lat, yfix_ref[...], 0.0) + d_ref[0] * u_s[sub_rows, :]
    g = jax.nn.gelu(y, approximate=True)
    glu = jnp.dot(g.astype(BF16), wglu_ref[0], preferred_element_type=F32) + bglu_ref[0]
    o_s5 = g * jax.nn.sigmoid(glu)
    o_lru = hl_s[sub_rows, :] * jax.nn.gelu(g_ref[...], approximate=True)
    mix = jnp.concatenate([o_att, o_s5.astype(BF16), o_lru.astype(BF16)], axis=-1)
    out = jnp.dot(mix, wout_ref[0], preferred_element_type=F32)
    r = _group_mod_row(i)
    g1 = mod_ref[0, pl.ds(r, 1), 2 * D:3 * D]
    sh2 = mod_ref[0, pl.ds(r, 1), 3 * D:4 * D]
    sc2 = mod_ref[0, pl.ds(r, 1), 4 * D:5 * D]
    x1 = _layer_norm(ALPHA * x_ref[...] + g1 * out, lng_ref[0], lnb_ref[0])
    x1_ref[...] = x1
    hm = x1 * (1.0 + sc2) + sh2
    hm_ref[...] = hm
    hm_hi = hm.astype(BF16)
    hm_lo = (hm - hm_hi.astype(F32)).astype(BF16)
    p_hi = jnp.dot(hm_hi, rw_ref[0], preferred_element_type=F32)
    p_lo = jnp.dot(hm_lo, rwhi_ref[0], preferred_element_type=F32)
    logits = p_hi[:, 0:128] + p_hi[:, 128:256] + p_lo + rb_ref[0]
    gates_ref[...] = _route(logits)


def _outproj_call(layer, x, o_ctx, o_lat, yf, yb, yfix, u_tm, hl, g_lru, mod, s5_d, w_glu, b_glu,
                  w_out, ln_g, ln_b, rw_split, rw_hi, rb):
    n_ctx_blocks = N_CTX * TILE // SUB_ROWS
    row_spec = lambda w: pl.BlockSpec((SUB_ROWS, w), lambda i, s: (i * SUBS + s, 0))
    lat_spec = lambda w: pl.BlockSpec((SUB_ROWS, w), lambda i, s: (jnp.maximum(i * SUBS + s - n_ctx_blocks, 0), 0))
    tm_spec = pl.BlockSpec((TILE, GROUP_SEQS, 256), lambda i, s: (0, i, 0))
    vec = lambda n: pl.BlockSpec((1, 1, n), lambda i, s: (layer, 0, 0))
    mat = lambda a, b: pl.BlockSpec((1, a, b), lambda i, s: (layer, 0, 0))
    return pl.pallas_call(
        _outproj_kernel,
        grid=(N_GROUP_TILES, SUBS),
        in_specs=[
            row_spec(D),
            pl.BlockSpec((SUB_ROWS, 512), lambda i, s: (jnp.minimum(i * SUBS + s, n_ctx_blocks - 1), 0)),
            lat_spec(512),
            tm_spec, tm_spec,
            lat_spec(S5_W),
            tm_spec, tm_spec,
            row_spec(LRU_W),
            mat(MOD_ROWS, N_MOD * D),
            vec(S5_W), mat(S5_W, S5_W), vec(S5_W),
            mat(D, D), vec(D), vec(D),
            mat(D, 256), mat(D, 128), vec(128),
        ],
        out_specs=[row_spec(D), row_spec(D), row_spec(128)],
        out_shape=[
            jax.ShapeDtypeStruct((N_TOK, D), F32),
            jax.ShapeDtypeStruct((N_TOK, D), F32),
            jax.ShapeDtypeStruct((N_TOK, 128), F32),
        ],
        scratch_shapes=[pltpu.VMEM((GROUP_ROWS, 256), F32)] * 3,
        compiler_params=_cparams(("arbitrary", "arbitrary")),
        name="outproj",
    )(x, o_ctx, o_lat, yf.reshape(TILE, N_SEQ, S5_W), yb.reshape(TILE, N_SEQ, S5_W), yfix,
      u_tm, hl.reshape(TILE, N_SEQ, LRU_W), g_lru, mod, s5_d.reshape(DEPTH, 1, S5_W), w_glu,
      b_glu.reshape(DEPTH, 1, S5_W), w_out, ln_g.reshape(DEPTH, 1, D), ln_b.reshape(DEPTH, 1, D),
      rw_split, rw_hi, rb)


HALF = N_TOK // 2
HALF_TILES = HALF // TILE


def _sort_kernel(gates_ref, pos_ref, cnt_ref, rank_s):
    sub = lax.broadcasted_iota(I32, (8, TILE), 0)
    sel_r = lax.broadcasted_iota(I32, (8, 128), 0)
    sel_c = lax.broadcasted_iota(I32, (8, 128), 1)
    pick_lane0 = jnp.where(jnp.logical_and(sel_r == 0, sel_c == 0), 1.0, 0.0).astype(BF16)
    ri = lax.broadcasted_iota(I32, (TILE, TILE), 0)
    ci = lax.broadcasted_iota(I32, (TILE, TILE), 1)
    upper = jnp.where(ri <= ci, 1.0, 0.0).astype(BF16)

    def count(b, carry):
        rows = pl.ds(pl.multiple_of(b * TILE, TILE), TILE)
        g = gates_ref[rows, :].astype(BF16)
        g_t = lax.dot_general(pick_lane0, g, _NT, preferred_element_type=F32)
        g_sel = jnp.sum(g_t, axis=0, keepdims=True)
        onehot = jnp.where(jnp.logical_and(sub < N_GROUPS, sub.astype(F32) == g_sel), 1.0, 0.0)
        cum = jnp.dot(onehot.astype(BF16), upper, preferred_element_type=F32) + carry
        rank_s[b] = jnp.where(onehot > 0.0, cum, 0.0)
        return carry + jnp.sum(onehot, axis=1, keepdims=True)

    counts = lax.fori_loop(0, HALF_TILES, count, jnp.zeros((8, 1), F32))
    sub1 = lax.broadcasted_iota(I32, (8, 1), 0)
    c = [jnp.sum(jnp.where(sub1 == g, counts, 0.0), axis=0, keepdims=True) for g in range(N_GROUPS - 1)]
    start = jnp.where(sub1 == 1, c[0], jnp.where(sub1 == 2, c[0] + c[1],
                      jnp.where(sub1 == 3, c[0] + c[1] + c[2], 0.0)))

    def place(b, _):
        rk = rank_s[b]
        p = jnp.sum(jnp.where(rk > 0.0, rk + start - 1.0, 0.0), axis=0, keepdims=True)
        pos_ref[pl.ds(b, 1), :] = p.astype(I32)
        return 0

    lax.fori_loop(0, HALF_TILES, place, 0)
    cnt_ref[0] = jnp.broadcast_to(counts, (8, 128))


def _sort_call(gates):
    return pl.pallas_call(
        _sort_kernel,
        grid=(2,),
        in_specs=[pl.BlockSpec((HALF, 128), lambda h: (h, 0))],
        out_specs=[pl.BlockSpec((HALF_TILES, TILE), lambda h: (h, 0)),
                   pl.BlockSpec((1, 8, 128), lambda h: (h, 0, 0))],
        out_shape=[jax.ShapeDtypeStruct((N_SEQ, TILE), I32), jax.ShapeDtypeStruct((2, 8, 128), F32)],
        scratch_shapes=[pltpu.VMEM((HALF_TILES, 8, TILE), F32)],
        compiler_params=_cparams(("arbitrary",)),
        name="group_sort",
    )(gates)


ITEM_TILES = 2
ITEM_ROWS = ITEM_TILES * TILE
SORT_ROWS = HALF + TILE
N_ITEMS = N_EXPERTS * (HALF_TILES + 2 * N_GROUPS) // (EPG * ITEM_TILES)
MOE_STEPS = HALF_TILES + N_ITEMS + HALF_TILES


def _moe_schedule(counts):
    start = jnp.cumsum(counts, axis=1) - counts
    lo = start // TILE
    hi = (start + counts + TILE - 1) // TILE
    n_g = jnp.where(counts > 0, (hi - lo + ITEM_TILES - 1) // ITEM_TILES, 0)
    n_e = jnp.repeat(n_g, EPG, axis=1)
    lo_e = jnp.repeat(lo, EPG, axis=1)
    off_end = jnp.cumsum(n_e, axis=1)
    off = off_end - n_e
    total = off_end[:, -1:]
    w = jnp.arange(N_ITEMS, dtype=I32)[None, :]
    w_eff = jnp.minimum(w, total - 1)
    e_w = jnp.sum((w_eff[:, :, None] >= off_end[:, None, :]).astype(I32), axis=-1)
    tile = jnp.take_along_axis(lo_e, e_w, axis=1) + ITEM_TILES * (w_eff - jnp.take_along_axis(off, e_w, axis=1))
    valid = (w < total).astype(I32)
    return e_w.reshape(-1), tile.reshape(-1), valid.reshape(-1)


def _moe_kernel(pos_ref, ite_ref, itt_ref, itv_ref,
                hm_ref, gates_ref, wg_ref, wu_ref, wd_ref, x1_ref, mod_ref, lng_ref, lnb_ref,
                o_ref, xs_s, gs_s, acc_s):
    h = pl.program_id(0)
    w = pl.program_id(1)

    @pl.when(w < HALF_TILES)
    def _():
        @pl.when(w == 0)
        def _():
            acc_s[...] = jnp.zeros_like(acc_s)
            xs_s[HALF:SORT_ROWS, :] = jnp.zeros((TILE, D), F32)
            gs_s[HALF:SORT_ROWS, :] = jnp.zeros((TILE, 128), F32)

        def body(r, c):
            p = pos_ref[h * HALF_TILES + w, r]
            xs_s[pl.ds(p, 1), :] = hm_ref[pl.ds(r, 1), :]
            gs_s[pl.ds(p, 1), :] = gates_ref[pl.ds(r, 1), :]
            return c

        lax.fori_loop(0, TILE, body, 0, unroll=8)

    @pl.when(jnp.logical_and(w >= HALF_TILES, w < HALF_TILES + N_ITEMS))
    def _():
        idx = h * N_ITEMS + (w - HALF_TILES)

        @pl.when(itv_ref[idx] > 0)
        def _():
            e = ite_ref[idx]
            rows = pl.ds(pl.multiple_of(itt_ref[idx] * TILE, TILE), ITEM_ROWS)
            x = xs_s[rows, :].astype(BF16)
            a = jnp.dot(x, wg_ref[0, 0].astype(BF16), preferred_element_type=F32)
            u = jnp.dot(x, wu_ref[0, 0].astype(BF16), preferred_element_type=F32)
            g = gs_s[rows, :]
            lane = lax.broadcasted_iota(I32, g.shape, 1)
            ge = jnp.sum(jnp.where(lane == e + GATE_LANE0, g, 0.0), axis=-1, keepdims=True)
            act = ((a * jax.nn.sigmoid(a)) * u * ge).astype(BF16)
            acc_s[rows, :] += jnp.dot(act, wd_ref[0, 0].astype(BF16), preferred_element_type=F32)

    @pl.when(w >= HALF_TILES + N_ITEMS)
    def _():
        j = w - HALF_TILES - N_ITEMS

        def body(r, c):
            p = pos_ref[h * HALF_TILES + j, r]
            o_ref[pl.ds(r, 1), :] = acc_s[pl.ds(p, 1), :]
            return c

        lax.fori_loop(0, TILE, body, 0, unroll=8)
        r = jnp.where(h == 0, 0, 1 + j // CHUNKS)
        g2 = mod_ref[0, pl.ds(r, 1), 5 * D:6 * D]
        o_ref[...] = _layer_norm(ALPHA * x1_ref[...] + g2 * o_ref[...], lng_ref[0], lnb_ref[0])


def _moe_call(layer, pos, it_e, it_t, it_v, hm, gates, w_gate, w_up, w_down, x1, mod, ln_g, ln_b):
    def in_tile(h, w, *_):
        return (h * HALF_TILES + jnp.minimum(w, HALF_TILES - 1), 0)

    def out_tile(h, w, *_):
        return (h * HALF_TILES + jnp.clip(w - HALF_TILES - N_ITEMS, 0, HALF_TILES - 1), 0)

    def expert(h, w, pos_r, ite_r, itt_r, itv_r):
        return (layer, ite_r[h * N_ITEMS + jnp.clip(w - HALF_TILES, 0, N_ITEMS - 1)], 0, 0)

    vec = lambda n: pl.BlockSpec((1, 1, n), lambda h, w, *_: (layer, 0, 0))
    grid_spec = pltpu.PrefetchScalarGridSpec(
        num_scalar_prefetch=4,
        grid=(2, MOE_STEPS),
        in_specs=[
            pl.BlockSpec((TILE, D), in_tile),
            pl.BlockSpec((TILE, 128), in_tile),
            pl.BlockSpec((1, 1, D, D_EXPERT), expert),
            pl.BlockSpec((1, 1, D, D_EXPERT), expert),
            pl.BlockSpec((1, 1, D_EXPERT, D), expert),
            pl.BlockSpec((TILE, D), out_tile),
            pl.BlockSpec((1, MOD_ROWS, N_MOD * D), lambda h, w, *_: (layer, 0, 0)),
            vec(D), vec(D),
        ],
        out_specs=pl.BlockSpec((TILE, D), out_tile),
        scratch_shapes=[
            pltpu.VMEM((SORT_ROWS, D), F32),
            pltpu.VMEM((SORT_ROWS, 128), F32),
            pltpu.VMEM((SORT_ROWS, D), F32),
        ],
    )
    return pl.pallas_call(
        _moe_kernel,
        grid_spec=grid_spec,
        out_shape=jax.ShapeDtypeStruct((N_TOK, D), F32),
        compiler_params=pltpu.CompilerParams(dimension_semantics=("arbitrary", "arbitrary"),
                                             vmem_limit_bytes=MOE_VMEM_LIMIT),
        name="moe",
    )(pos, it_e, it_t, it_v, hm, gates, w_gate, w_up, w_down, x1, mod,
      ln_g.reshape(DEPTH, 1, D), ln_b.reshape(DEPTH, 1, D))


def _rope_tables():
    rows = LAT_T // GRID_W
    row = jnp.repeat(jnp.arange(rows, dtype=F32), GRID_W)
    col = jnp.tile(jnp.arange(GRID_W, dtype=F32), rows)
    n_freq = QK // 4
    inv = ROPE_BASE ** (-jnp.arange(n_freq, dtype=F32) / n_freq)
    ang_r = row[:, None] * inv
    ang_c = col[:, None] * inv
    cos64 = jnp.concatenate([jnp.cos(ang_r), jnp.cos(ang_r), jnp.cos(ang_c), jnp.cos(ang_c)], axis=1)
    sin64 = jnp.concatenate([-jnp.sin(ang_r), jnp.sin(ang_r), -jnp.sin(ang_c), jnp.sin(ang_c)], axis=1)
    cos = jnp.concatenate([jnp.tile(cos64, (1, 2)), jnp.ones((SUB_ROWS, 128), F32)], axis=0)
    sin = jnp.concatenate([jnp.tile(sin64, (1, 2)), jnp.zeros((SUB_ROWS, 128), F32)], axis=0)
    return cos, sin


def _permute_w_in(w_in):
    qk = w_in[:, :, 0:1024].reshape(DEPTH, D, 2, 2, HEADS, QK)
    qk = qk.transpose(0, 1, 2, 4, 3, 5).reshape(DEPTH, D, 1024)
    return jnp.concatenate([qk, w_in[:, :, 1024:]], axis=-1).astype(BF16)


def _s5_params(a_re, a_im, b_re, b_im, c_re, c_im, log_dt):
    dt = jnp.exp(log_dt)[..., None]
    mag = jnp.exp(a_re * dt)
    abar_r = mag * jnp.cos(a_im * dt)
    abar_i = mag * jnp.sin(a_im * dt)
    den = a_re * a_re + a_im * a_im
    nr = abar_r - 1.0
    coef_r = (nr * a_re + abar_i * a_im) / den
    coef_i = (abar_i * a_re - nr * a_im) / den
    bbar_r = coef_r[..., None] * b_re - coef_i[..., None] * b_im
    bbar_i = coef_r[..., None] * b_im + coef_i[..., None] * b_re
    eye = jnp.eye(S5_G, dtype=F32)
    def b_block(bb):
        m = jnp.einsum('ldgpc,gh->ldgchp', bb, eye)
        return m.reshape(DEPTH, 2, S5_W, S5_N)
    b_blk = jnp.concatenate([b_block(bbar_r), b_block(bbar_i)], axis=-1).astype(BF16)
    def c_block(cc):
        m = jnp.einsum('ldgcp,gh->ldgphc', cc, eye)
        return m.reshape(DEPTH, 2, S5_N, S5_W)
    c_blk = jnp.concatenate([c_block(c_re), c_block(-c_im)], axis=-2).astype(BF16)
    a_bar = jnp.concatenate([abar_r.reshape(DEPTH, 2, S5_N), abar_i.reshape(DEPTH, 2, S5_N)], axis=-1)
    return b_blk, c_blk, a_bar


def _block_diag(w):
    eye = jnp.eye(LRU_BLOCKS, dtype=F32)
    m = jnp.einsum('ldkij,kh->ldkihj', w, eye)
    return m.reshape(DEPTH, 2, LRU_W, LRU_W).astype(BF16)


def kernel(x_prompt, x_sample, cache_k, cache_v, state_s5, state_lru, c, c_ctx, w_ada, b_ada, w_in, w_out, lam_q1, lam_k1, lam_q2, lam_k2, subln_g, s5_a_re, s5_a_im, s5_b_re, s5_b_im, s5_c_re, s5_c_im, s5_log_dt, s5_d, s5_w_glu, s5_b_glu, lru_conv_w, lru_conv_b, lru_w_a, lru_b_a, lru_w_i, lru_b_i, lru_lambda, ln1_g, ln1_b, ln2_g, ln2_b, router_group_w, router_group_b, router_expert_w, router_expert_b, moe_w_gate, moe_w_up, moe_w_down):
    x = jnp.concatenate([x_prompt.reshape(CTX_B * CTX_T, D), x_sample.reshape(LAT_B * LAT_T, D)], axis=0)
    cond = jnp.concatenate([c_ctx[None, :], c, jnp.zeros((MOD_ROWS - 1 - LAT_B, D), F32)], axis=0)
    mod = _ada_call(cond, w_ada, b_ada)

    w_in_p = _permute_w_in(w_in)
    w_out_b = w_out.astype(BF16)
    w_glu_b = s5_w_glu.astype(BF16)
    rope_cos, rope_sin = _rope_tables()
    b_blk, c_blk, a_bar = _s5_params(s5_a_re, s5_a_im, s5_b_re, s5_b_im, s5_c_re, s5_c_im, s5_log_dt)
    wa_blk = _block_diag(lru_w_a)
    wi_blk = _block_diag(lru_w_i)
    ba = lru_b_a.reshape(DEPTH, 2, 1, LRU_W)
    bi = lru_b_i.reshape(DEPTH, 2, 1, LRU_W)
    sp = jax.nn.softplus(-lru_lambda).reshape(DEPTH, 2, 1, LRU_W)
    lam = (jnp.exp(jnp.sum(lam_q1 * lam_k1, axis=-1)) - jnp.exp(jnp.sum(lam_q2 * lam_k2, axis=-1)))
    rw = jnp.concatenate([router_group_w, router_expert_w,
                          jnp.zeros((DEPTH, D, 128 - N_GROUPS - N_EXPERTS), F32)], axis=-1)
    rw_hi = rw.astype(BF16)
    rw_split = jnp.concatenate([rw_hi, (rw - rw_hi.astype(F32)).astype(BF16)], axis=-1)
    rb = jnp.concatenate([router_group_b, router_expert_b,
                          jnp.zeros((DEPTH, 128 - N_GROUPS - N_EXPERTS), F32)], axis=-1).reshape(DEPTH, 1, 128)
    s5_h0 = state_s5.reshape(LAT_B, DEPTH, 2, 2 * S5_N)
    lru_h0 = state_lru.transpose(1, 0, 2, 3)

    kc = jnp.zeros((CTX_B, DEPTH, HEADS, CTX_T, 2 * QK), F32)
    vc = jnp.zeros((CTX_B, DEPTH, HEADS, CTX_T, DV), F32)
    s5_states = []
    lru_states = []
    for l in range(DEPTH):
        lambda_init = 0.8 - 0.6 * math.exp(-0.3 * l)
        lam_l = (lam[l] + lambda_init).reshape(1)
        q, k, v, kc, vc, u_tm, xc_tm, g_lru = _inproj_call(
            l, x, mod, w_in_p, rope_cos, rope_sin, lru_conv_w, lru_conv_b, kc, vc)
        o_ctx = _att_ctx_call(l, lam_l, q, k, v, subln_g, 1.0 - lambda_init)
        o_lat = _att_lat_call(l, lam_l, q, k, v, cache_k, cache_v, subln_g, 1.0 - lambda_init)
        yf, yb, hend = _s5_call(l, u_tm.reshape(N_TOK, S5_W), b_blk, c_blk, a_bar)
        yfix = _s5_fix_call(l, hend, s5_h0[:, l], a_bar, c_blk)
        hl, lru_end = _lru_call(l, xc_tm.reshape(N_TOK, LRU_W), wa_blk, ba, wi_blk, bi, sp, lru_h0)
        x1, hm, gates = _outproj_call(l, x, o_ctx, o_lat, yf, yb, yfix, u_tm, hl, g_lru, mod, s5_d,
                                      w_glu_b, s5_b_glu, w_out_b, ln1_g, ln1_b, rw_split, rw_hi, rb)
        pos, counts = _sort_call(gates)
        it_e, it_t, it_v = _moe_schedule(counts[:, :N_GROUPS, 0].astype(I32))
        x = _moe_call(l, pos, it_e, it_t, it_v, hm, gates, moe_w_gate, moe_w_up,
                      moe_w_down, x1, mod, ln2_g, ln2_b)
        s5_states.append(hend[:, :N_CTX].reshape(2, CTX_B, 2, S5_G, S5_P).transpose(1, 0, 2, 3, 4))
        lru_states.append(lru_end[:, :N_CTX].transpose(1, 0, 2))

    y_p = x[:CTX_B * CTX_T].reshape(CTX_B, CTX_T, D)
    y_s = x[CTX_B * CTX_T:].reshape(LAT_B, LAT_T, D)
    return (y_p, y_s, kc, vc, jnp.stack(s5_states, axis=1), jnp.stack(lru_states, axis=1))
```

```python
import functools
import math

import jax
import jax.numpy as jnp
from jax import lax
from jax.experimental import pallas as pl
from jax.experimental.pallas import tpu as pltpu

F32 = jnp.float32
BF16 = jnp.bfloat16
I32 = jnp.int32

D = 1024
DEPTH = 4
CTX_B = 16
CTX_T = 256
LAT_B = 2
LAT_T = 2048
PAST = 256
GRID_W = 64
HEADS = 4
QK = 64
DV = 128
S5_W = 256
S5_G = 16
S5_C = 16
S5_P = 64
S5_N = S5_G * S5_P
S5_C_SHIFT = S5_C.bit_length() - 1
S5_P_SHIFT = S5_P.bit_length() - 1
LRU_W = 256
LRU_BLOCKS = 4
LRU_C = 8.0
N_GROUPS = 4
EPG = 4
N_EXPERTS = 16
D_EXPERT = 512
N_MOD = 6
IN_W = 2304
ROPE_BASE = 10000.0
ALPHA = (2 * DEPTH) ** 0.25
EPS = 1e-5

TILE = 256
N_TOK = CTX_B * CTX_T + LAT_B * LAT_T
N_SEQ = N_TOK // TILE
N_CTX = CTX_B
CHUNKS = LAT_T // TILE
N_LAT = LAT_B * CHUNKS
MOD_ROWS = 8
GATE_LANE0 = 4
VMEM_LIMIT = 56 * 1024 * 1024
MOE_VMEM_LIMIT = 60 * 1024 * 1024

GROUP_SEQS = 8
GROUP_ROWS = GROUP_SEQS * TILE
N_GROUP_TILES = N_TOK // GROUP_ROWS
CTX_GROUP_TILES = N_CTX // GROUP_SEQS
SUB_ROWS = 512
SUBS = GROUP_ROWS // SUB_ROWS
SUB_SEQS = SUB_ROWS // TILE


def _cparams(sem):
    return pltpu.CompilerParams(dimension_semantics=sem, vmem_limit_bytes=VMEM_LIMIT)


def _group_mod_row(i):
    return jnp.where(i < CTX_GROUP_TILES, 0, i - CTX_GROUP_TILES + 1)


def _ada_kernel(c_ref, w_ref, b_ref, o_ref):
    c = c_ref[...]
    s = (c * jax.nn.sigmoid(c)).astype(BF16)
    o_ref[0] = jnp.dot(s, w_ref[0].astype(BF16), preferred_element_type=F32) + b_ref[0]


def _ada_call(cond, w_ada, b_ada):
    tn = 1536
    return pl.pallas_call(
        _ada_kernel,
        grid=(DEPTH, N_MOD * D // tn),
        in_specs=[
            pl.BlockSpec((MOD_ROWS, D), lambda l, j: (0, 0)),
            pl.BlockSpec((1, D, tn), lambda l, j: (l, 0, j)),
            pl.BlockSpec((1, 1, tn), lambda l, j: (l, 0, j)),
        ],
        out_specs=pl.BlockSpec((1, MOD_ROWS, tn), lambda l, j: (l, 0, j)),
        out_shape=jax.ShapeDtypeStruct((DEPTH, MOD_ROWS, N_MOD * D), F32),
        compiler_params=_cparams(("arbitrary", "arbitrary")),
        name="adaln",
    )(cond, w_ada, b_ada.reshape(DEPTH, 1, N_MOD * D))


def _inproj_kernel(x_ref, mod_ref, w_ref, cos_ref, sin_ref, cw_ref, cb_ref, kc_in, vc_in,
                   q_ref, k_ref, v_ref, kc_ref, vc_ref, u_ref, xc_ref, g_ref, u_s, xl_s, w_s):
    del kc_in, vc_in
    i = pl.program_id(0)
    s = pl.program_id(1)

    @pl.when(jnp.logical_and(i == 0, s == 0))
    def _():
        for t in range(2 * HEADS):
            kind, head = divmod(t, HEADS)
            halves = []
            for m in range(2):
                col = (2 * kind + m) * HEADS * QK + head * QK
                blk = w_ref[0, :, (col // 128) * 128:(col // 128 + 1) * 128]
                halves.append(blk[:, col % 128:col % 128 + QK])
            w_s[:, t * 128:(t + 1) * 128] = jnp.concatenate(halves, axis=1).astype(BF16)
        w_s[:, 1024:IN_W] = w_ref[0, :, 1024:IN_W].astype(BF16)

    r = _group_mod_row(i)
    sh = mod_ref[0, pl.ds(r, 1), 0:D]
    sc = mod_ref[0, pl.ds(r, 1), D:2 * D]
    xm = (x_ref[...] * (1.0 + sc) + sh).astype(BF16)
    proj = jnp.dot(xm, w_s[...], preferred_element_type=F32)

    qk = proj[:, 0:1024]
    cos = jnp.concatenate([cos_ref[...]] * 8, axis=1)
    sin = jnp.concatenate([sin_ref[...]] * 8, axis=1)
    lane = lax.broadcasted_iota(I32, qk.shape, 1)
    swapped = jnp.where((lane & 31) < 16, pltpu.roll(qk, 1024 - 16, 1), pltpu.roll(qk, 16, 1))
    qk = qk * cos + swapped * sin
    q_ref[...] = qk[:, 0:512].astype(BF16)
    k_ref[...] = qk[:, 512:1024].astype(BF16)
    v = proj[:, 1024:1536]
    v_ref[...] = v.astype(BF16)

    @pl.when(i < CTX_GROUP_TILES)
    def _():
        for jj in range(SUB_SEQS):
            rows = slice(jj * TILE, (jj + 1) * TILE)
            for h in range(HEADS):
                kc_ref[jj, 0, h] = qk[rows, 512 + h * 128:512 + (h + 1) * 128]
                vc_ref[jj, 0, h] = v[rows, h * 128:(h + 1) * 128]

    g_ref[...] = proj[:, 2048:2304]
    sub_rows = pl.ds(pl.multiple_of(s * SUB_ROWS, SUB_ROWS), SUB_ROWS)
    u_s[sub_rows, :] = proj[:, 1536:1792]
    xl_s[sub_rows, :] = proj[:, 1792:2048]

    @pl.when(s == SUBS - 1)
    def _():
        xl = xl_s[...]
        row = lax.broadcasted_iota(I32, xl.shape, 0)
        is_ctx = i < CTX_GROUP_TILES
        pos = jnp.where(is_ctx, row & (TILE - 1), row)
        last = jnp.where(is_ctx, TILE - 1, GROUP_ROWS - 1)
        x_m1 = jnp.where(pos == 0, 0.0, pltpu.roll(xl, 1, 0))
        x_p1 = jnp.where(pos == last, 0.0, pltpu.roll(xl, GROUP_ROWS - 1, 0))
        x_p2 = jnp.where(pos >= last - 1, 0.0, pltpu.roll(xl, GROUP_ROWS - 2, 0))
        cw = cw_ref[0]
        xc = cb_ref[0] + x_m1 * cw[0:1] + xl * cw[1:2] + x_p1 * cw[2:3] + x_p2 * cw[3:4]
        for j in range(GROUP_SEQS):
            xc_ref[:, j, :] = xc[j * TILE:(j + 1) * TILE]
            u_ref[:, j, :] = u_s[j * TILE:(j + 1) * TILE, :]


def _inproj_call(layer, x, mod, w_in, rope_cos, rope_sin, conv_w, conv_b, kc, vc):
    n_ctx_blocks = N_CTX // SUB_SEQS

    def rope_idx(i, s):
        return (jnp.where(i < CTX_GROUP_TILES, SUBS, s), 0)

    def cache_idx(i, s):
        return (jnp.minimum(i * SUBS + s, n_ctx_blocks - 1), layer, 0, 0, 0)

    row_spec = lambda w: pl.BlockSpec((SUB_ROWS, w), lambda i, s: (i * SUBS + s, 0))
    tm_spec = pl.BlockSpec((TILE, GROUP_SEQS, 256), lambda i, s: (0, i, 0))
    cache_spec = pl.BlockSpec((SUB_SEQS, 1, HEADS, TILE, 128), cache_idx)
    return pl.pallas_call(
        _inproj_kernel,
        grid=(N_GROUP_TILES, SUBS),
        in_specs=[
            row_spec(D),
            pl.BlockSpec((1, MOD_ROWS, N_MOD * D), lambda i, s: (layer, 0, 0)),
            pl.BlockSpec((1, D, IN_W), lambda i, s: (layer, 0, 0)),
            pl.BlockSpec((SUB_ROWS, 128), rope_idx),
            pl.BlockSpec((SUB_ROWS, 128), rope_idx),
            pl.BlockSpec((1, 4, LRU_W), lambda i, s: (layer, 0, 0)),
            pl.BlockSpec((1, 1, LRU_W), lambda i, s: (layer, 0, 0)),
            pl.BlockSpec(memory_space=pl.ANY),
            pl.BlockSpec(memory_space=pl.ANY),
        ],
        out_specs=[
            row_spec(512), row_spec(512), row_spec(512),
            cache_spec, cache_spec,
            tm_spec, tm_spec,
            row_spec(LRU_W),
        ],
        out_shape=[
            jax.ShapeDtypeStruct((N_TOK, 512), BF16),
            jax.ShapeDtypeStruct((N_TOK, 512), BF16),
            jax.ShapeDtypeStruct((N_TOK, 512), BF16),
            jax.ShapeDtypeStruct(kc.shape, F32),
            jax.ShapeDtypeStruct(vc.shape, F32),
            jax.ShapeDtypeStruct((TILE, N_SEQ, S5_W), F32),
            jax.ShapeDtypeStruct((TILE, N_SEQ, LRU_W), F32),
            jax.ShapeDtypeStruct((N_TOK, LRU_W), F32),
        ],
        scratch_shapes=[pltpu.VMEM((GROUP_ROWS, S5_W), F32), pltpu.VMEM((GROUP_ROWS, LRU_W), F32),
                        pltpu.VMEM((D, IN_W), BF16)],
        input_output_aliases={7: 3, 8: 4},
        compiler_params=_cparams(("arbitrary", "arbitrary")),
        name="inproj",
    )(x, mod, w_in, rope_cos, rope_sin, conv_w, conv_b.reshape(DEPTH, 1, LRU_W), kc, vc)


_NT = (((1,), (1,)), ((), ()))


def _split_maps(q):
    lane = lax.broadcasted_iota(I32, q.shape, 1)
    zero = jnp.zeros_like(q)
    qs = q * (QK ** -0.5)
    return jnp.where(lane < QK, qs, zero), jnp.where(lane >= QK, qs, zero)


def _softmax_values(qm, keys, values):
    scores = [lax.dot_general(qm, k, _NT, preferred_element_type=F32) for k in keys]
    m = scores[0].max(axis=-1, keepdims=True)
    for sc in scores[1:]:
        m = jnp.maximum(m, sc.max(axis=-1, keepdims=True))
    z = None
    o = None
    for sc, v in zip(scores, values):
        e = jnp.exp(sc - m)
        ez = e.sum(axis=-1, keepdims=True)
        eo = jnp.dot(e.astype(BF16), v, preferred_element_type=F32)
        z = ez if z is None else z + ez
        o = eo if o is None else o + eo
    return o * (1.0 / z)


def _diff_attention(q, keys, values, lam, g, post_scale):
    q1, q2 = _split_maps(q)
    o = _softmax_values(q1, keys, values) - lam * _softmax_values(q2, keys, values)
    ms = jnp.mean(o * o, axis=-1, keepdims=True)
    return (o * lax.rsqrt(ms + EPS) * g) * post_scale


def _att_ctx_kernel(lam_ref, q_ref, k_ref, v_ref, g_ref, o_ref, *, post_scale):
    lam = lam_ref[0]
    for h in range(HEADS):
        cols = slice(h * 128, (h + 1) * 128)
        o = _diff_attention(q_ref[:, cols], [k_ref[:, cols]], [v_ref[:, cols]], lam, g_ref[0], post_scale)
        o_ref[:, cols] = o.astype(BF16)


def _att_ctx_call(layer, lam, q, k, v, subln_g, post_scale):
    blk = pl.BlockSpec((TILE, HEADS * 128), lambda b: (b, 0))
    return pl.pallas_call(
        functools.partial(_att_ctx_kernel, post_scale=post_scale),
        grid=(CTX_B,),
        in_specs=[
            pl.BlockSpec(memory_space=pltpu.SMEM),
            blk, blk, blk,
            pl.BlockSpec((1, 1, DV), lambda b: (layer, 0, 0)),
        ],
        out_specs=blk,
        out_shape=jax.ShapeDtypeStruct((CTX_B * CTX_T, HEADS * DV), BF16),
        compiler_params=_cparams(("arbitrary",)),
        name="att_ctx",
    )(lam, q, k, v, subln_g.reshape(DEPTH, 1, DV))


def _att_lat_kernel(lam_ref, q_ref, k_ref, v_ref, ck_ref, cv_ref, g_ref, o_ref, *, post_scale):
    o = _diff_attention(q_ref[...], [k_ref[...], ck_ref[...].astype(BF16)],
                        [v_ref[...], cv_ref[...].astype(BF16)], lam_ref[0], g_ref[0], post_scale)
    o_ref[...] = o.astype(BF16)


def _att_lat_call(layer, lam, q, k, v, cache_k, cache_v, subln_g, post_scale):
    lat0 = N_CTX
    cache_spec = pl.BlockSpec((None, None, None, PAST, 128), lambda b, h, t: (b, layer, h, 0, 0))
    return pl.pallas_call(
        functools.partial(_att_lat_kernel, post_scale=post_scale),
        grid=(LAT_B, HEADS, CHUNKS),
        in_specs=[
            pl.BlockSpec(memory_space=pltpu.SMEM),
            pl.BlockSpec((TILE, 128), lambda b, h, t: (lat0 + b * CHUNKS + t, h)),
            pl.BlockSpec((LAT_T, 128), lambda b, h, t: (lat0 // CHUNKS + b, h)),
            pl.BlockSpec((LAT_T, 128), lambda b, h, t: (lat0 // CHUNKS + b, h)),
            cache_spec, cache_spec,
            pl.BlockSpec((1, 1, DV), lambda b, h, t: (layer, 0, 0)),
        ],
        out_specs=pl.BlockSpec((TILE, 128), lambda b, h, t: (b * CHUNKS + t, h)),
        out_shape=jax.ShapeDtypeStruct((LAT_B * LAT_T, HEADS * DV), BF16),
        compiler_params=_cparams(("arbitrary", "arbitrary", "arbitrary")),
        name="att_lat",
    )(lam, q, k, v, cache_k, cache_v, subln_g.reshape(DEPTH, 1, DV))


S5_TB = 32
S5_LANES = 128


def _cmul(ar, ai, br, bi):
    return ar * br - ai * bi, ar * bi + ai * br


def _expand_b(b):
    full = jnp.concatenate([b[:, 0:128]] * (S5_N // 128) + [b[:, 128:256]] * (S5_N // 128), axis=1)
    row = lax.broadcasted_iota(I32, full.shape, 0)
    col = lax.broadcasted_iota(I32, full.shape, 1)
    same_group = (row >> S5_C_SHIFT) == ((col & (S5_N - 1)) >> S5_P_SHIFT)
    return jnp.where(same_group, full, 0.0).astype(BF16)


def _expand_c(c):
    full = jnp.concatenate([c[0:S5_P]] * S5_G + [c[S5_P:2 * S5_P]] * S5_G, axis=0)
    row = lax.broadcasted_iota(I32, full.shape, 0)
    col = lax.broadcasted_iota(I32, full.shape, 1)
    same_group = ((row & (S5_N - 1)) >> S5_P_SHIFT) == (col >> S5_C_SHIFT)
    return jnp.where(same_group, full, 0.0).astype(BF16)


def _s5_kernel(uf_ref, ub_ref, b_ref, c_ref, a_ref, d_ref, yf_ref, yb_ref, hend_ref,
               hf_s, hb_s, buf_f, buf_b, bblk_s, cblk_s):
    i = pl.program_id(0)

    @pl.when(i == 0)
    def _():
        hf_s[...] = jnp.zeros_like(hf_s)
        hb_s[...] = jnp.zeros_like(hb_s)
        for d in range(2):
            bblk_s[d] = _expand_b(b_ref[0, d])
            cblk_s[d] = _expand_c(c_ref[0, d])

    buf_f[...] = jnp.dot(uf_ref[...].astype(BF16), bblk_s[0], preferred_element_type=F32)
    buf_b[...] = jnp.dot(ub_ref[...].astype(BF16), bblk_s[1], preferred_element_type=F32)

    for c in range(S5_N // S5_LANES):
        re = slice(c * S5_LANES, (c + 1) * S5_LANES)
        im = slice(S5_N + c * S5_LANES, S5_N + (c + 1) * S5_LANES)
        afr = a_ref[0, 0:1, re]
        afi = a_ref[0, 0:1, im]
        abr = a_ref[0, 1:2, re]
        abi = a_ref[0, 1:2, im]

        def body(t, carry, re=re, im=im, afr=afr, afi=afi, abr=abr, abi=abi):
            hfr, hfi, hbr, hbi = carry
            rf = pl.multiple_of(t * N_SEQ, N_SEQ)
            rb = pl.multiple_of((S5_TB - 1 - t) * N_SEQ, N_SEQ)
            pr, pi = _cmul(afr, afi, hfr, hfi)
            hfr = pr + buf_f[pl.ds(rf, N_SEQ), re]
            hfi = pi + buf_f[pl.ds(rf, N_SEQ), im]
            buf_f[pl.ds(rf, N_SEQ), re] = hfr
            buf_f[pl.ds(rf, N_SEQ), im] = hfi
            pr, pi = _cmul(abr, abi, hbr, hbi)
            hbr = pr + buf_b[pl.ds(rb, N_SEQ), re]
            hbi = pi + buf_b[pl.ds(rb, N_SEQ), im]
            buf_b[pl.ds(rb, N_SEQ), re] = hbr
            buf_b[pl.ds(rb, N_SEQ), im] = hbi
            return hfr, hfi, hbr, hbi

        out = lax.fori_loop(0, S5_TB, body, (hf_s[:, re], hf_s[:, im], hb_s[:, re], hb_s[:, im]))
        hf_s[:, re] = out[0]
        hf_s[:, im] = out[1]
        hb_s[:, re] = out[2]
        hb_s[:, im] = out[3]

    yf_ref[...] = (jnp.dot(buf_f[...].astype(BF16), cblk_s[0], preferred_element_type=F32)
                   + d_ref[0] * uf_ref[...])
    yb_ref[...] = jnp.dot(buf_b[...].astype(BF16), cblk_s[1], preferred_element_type=F32)

    @pl.when(i == pl.num_programs(0) - 1)
    def _():
        hend_ref[0] = hf_s[...]
        hend_ref[1] = hb_s[...]


def _s5_call(layer, u2, b_cmp, c_cmp, a_bar, s5_d):
    nb = CTX_T // S5_TB
    rows = S5_TB * N_SEQ
    return pl.pallas_call(
        _s5_kernel,
        grid=(nb,),
        in_specs=[
            pl.BlockSpec((rows, S5_W), lambda i: (i, 0)),
            pl.BlockSpec((rows, S5_W), lambda i: (nb - 1 - i, 0)),
            pl.BlockSpec((1, 2, S5_W, 256), lambda i: (layer, 0, 0, 0)),
            pl.BlockSpec((1, 2, 2 * S5_P, S5_W), lambda i: (layer, 0, 0, 0)),
            pl.BlockSpec((1, 2, 2 * S5_N), lambda i: (layer, 0, 0)),
            pl.BlockSpec((1, 1, S5_W), lambda i: (layer, 0, 0)),
        ],
        out_specs=[
            pl.BlockSpec((rows, S5_W), lambda i: (i, 0)),
            pl.BlockSpec((rows, S5_W), lambda i: (nb - 1 - i, 0)),
            pl.BlockSpec((2, N_SEQ, 2 * S5_N), lambda i: (0, 0, 0)),
        ],
        out_shape=[
            jax.ShapeDtypeStruct((N_TOK, S5_W), F32),
            jax.ShapeDtypeStruct((N_TOK, S5_W), F32),
            jax.ShapeDtypeStruct((2, N_SEQ, 2 * S5_N), F32),
        ],
        scratch_shapes=[
            pltpu.VMEM((N_SEQ, 2 * S5_N), F32),
            pltpu.VMEM((N_SEQ, 2 * S5_N), F32),
            pltpu.VMEM((rows, 2 * S5_N), F32),
            pltpu.VMEM((rows, 2 * S5_N), F32),
            pltpu.VMEM((2, S5_W, 2 * S5_N), BF16),
            pltpu.VMEM((2, 2 * S5_N, S5_W), BF16),
        ],
        compiler_params=_cparams(("arbitrary",)),
        name="s5_scan",
    )(u2, u2, b_cmp, c_cmp, a_bar, s5_d.reshape(DEPTH, 1, S5_W))


def _s5_fix_kernel(hend_ref, h0_ref, a_ref, c_ref, o_ref, pf_s, pb_s, cf_s, cb_s, cblk_s):
    s = pl.program_id(0)

    @pl.when(s == 0)
    def _():
        for d in range(2):
            cblk_s[d] = _expand_c(c_ref[0, d])
        row8 = lax.broadcasted_iota(I32, (8, S5_N), 0)
        for d, tab in ((0, pf_s), (1, pb_s)):
            ar = a_ref[0, d:d + 1, 0:S5_N]
            ai = a_ref[0, d:d + 1, S5_N:2 * S5_N]
            pr, pi = ar, ai
            r8 = jnp.zeros((8, S5_N), F32)
            i8 = jnp.zeros((8, S5_N), F32)
            for r in range(8):
                if r:
                    pr, pi = _cmul(pr, pi, ar, ai)
                at = r if d == 0 else 7 - r
                r8 = jnp.where(row8 == at, pr, r8)
                i8 = jnp.where(row8 == at, pi, i8)
            base = 0 if d == 0 else TILE - 8
            tab[base:base + 8, 0:S5_N] = r8
            tab[base:base + 8, S5_N:2 * S5_N] = i8
            m = 8
            while m < TILE:
                if d == 0:
                    src, dst, top = slice(0, m), slice(m, 2 * m), slice(m - 1, m)
                else:
                    src, dst, top = slice(TILE - m, TILE), slice(TILE - 2 * m, TILE - m), slice(TILE - m, TILE - m + 1)
                mr, mi = tab[top, 0:S5_N], tab[top, S5_N:2 * S5_N]
                nr, ni = _cmul(tab[src, 0:S5_N], tab[src, S5_N:2 * S5_N], mr, mi)
                tab[dst, 0:S5_N] = nr
                tab[dst, S5_N:2 * S5_N] = ni
                m *= 2

        for b in range(LAT_B):
            ar, ai = pf_s[TILE - 1:TILE, 0:S5_N], pf_s[TILE - 1:TILE, S5_N:2 * S5_N]
            cr, ci = h0_ref[b, 0:1, 0:S5_N], h0_ref[b, 0:1, S5_N:2 * S5_N]
            for j in range(CHUNKS):
                row = b * CHUNKS + j
                cf_s[row:row + 1, 0:S5_N] = cr
                cf_s[row:row + 1, S5_N:2 * S5_N] = ci
                pr, pi = _cmul(ar, ai, cr, ci)
                cr = pr + hend_ref[0, N_CTX + row:N_CTX + row + 1, 0:S5_N]
                ci = pi + hend_ref[0, N_CTX + row:N_CTX + row + 1, S5_N:2 * S5_N]
            ar, ai = pb_s[0:1, 0:S5_N], pb_s[0:1, S5_N:2 * S5_N]
            cr, ci = h0_ref[b, 1:2, 0:S5_N], h0_ref[b, 1:2, S5_N:2 * S5_N]
            for j in reversed(range(CHUNKS)):
                row = b * CHUNKS + j
                cb_s[row:row + 1, 0:S5_N] = cr
                cb_s[row:row + 1, S5_N:2 * S5_N] = ci
                pr, pi = _cmul(ar, ai, cr, ci)
                cr = pr + hend_ref[1, N_CTX + row:N_CTX + row + 1, 0:S5_N]
                ci = pi + hend_ref[1, N_CTX + row:N_CTX + row + 1, S5_N:2 * S5_N]

    acc = None
    for d, tab, car in ((0, pf_s, cf_s), (1, pb_s, cb_s)):
        cr = car[pl.ds(s, 1), 0:S5_N]
        ci = car[pl.ds(s, 1), S5_N:2 * S5_N]
        hr, hi = _cmul(tab[:, 0:S5_N], tab[:, S5_N:2 * S5_N], cr, ci)
        h = jnp.concatenate([hr, hi], axis=1).astype(BF16)
        y = jnp.dot(h, cblk_s[d], preferred_element_type=F32)
        acc = y if acc is None else acc + y
    o_ref[...] = acc


def _s5_fix_call(layer, hend, h0, a_bar, c_cmp):
    return pl.pallas_call(
        _s5_fix_kernel,
        grid=(N_LAT,),
        in_specs=[
            pl.BlockSpec((2, N_SEQ, 2 * S5_N), lambda s: (0, 0, 0)),
            pl.BlockSpec((LAT_B, 2, 2 * S5_N), lambda s: (0, 0, 0)),
            pl.BlockSpec((1, 2, 2 * S5_N), lambda s: (layer, 0, 0)),
            pl.BlockSpec((1, 2, 2 * S5_P, S5_W), lambda s: (layer, 0, 0, 0)),
        ],
        out_specs=pl.BlockSpec((TILE, S5_W), lambda s: (s, 0)),
        out_shape=jax.ShapeDtypeStruct((N_LAT * TILE, S5_W), F32),
        scratch_shapes=[
            pltpu.VMEM((TILE, 2 * S5_N), F32),
            pltpu.VMEM((TILE, 2 * S5_N), F32),
            pltpu.VMEM((N_LAT, 2 * S5_N), F32),
            pltpu.VMEM((N_LAT, 2 * S5_N), F32),
            pltpu.VMEM((2, 2 * S5_N, S5_W), BF16),
        ],
        compiler_params=_cparams(("arbitrary",)),
        name="s5_fix",
    )(hend, h0, a_bar, c_cmp)


LRU_ROWS = 1024


def _lru_kernel(xc_ref, wa_ref, ba_ref, wi_ref, bi_ref, sp_ref, h0_ref, out_ref, hend_ref,
                a_s, h_s, p_s):
    row = lax.broadcasted_iota(I32, (N_SEQ, LRU_W), 0)
    for d in range(2):
        def gates(cix, _, d=d):
            r0 = pl.multiple_of(cix * LRU_ROWS, LRU_ROWS)
            xc = xc_ref[pl.ds(r0, LRU_ROWS), :]
            xb = xc.astype(BF16)
            r = jax.nn.sigmoid(jnp.dot(xb, wa_ref[0, d], preferred_element_type=F32) + ba_ref[0, d])
            g = jax.nn.sigmoid(jnp.dot(xb, wi_ref[0, d], preferred_element_type=F32) + bi_ref[0, d])
            log_a = (-LRU_C) * r * sp_ref[0, d]
            a_s[pl.ds(r0, LRU_ROWS), :] = jnp.exp(log_a)
            th = jnp.tanh(log_a)
            h_s[pl.ds(r0, LRU_ROWS), :] = jnp.sqrt(-2.0 * th / (1.0 - th)) * g * xc
            return 0

        lax.fori_loop(0, N_TOK // LRU_ROWS, gates, 0)

        def scan(t, carry, d=d):
            h, p = carry
            tt = t if d == 0 else TILE - 1 - t
            r0 = pl.multiple_of(tt * N_SEQ, N_SEQ)
            a = a_s[pl.ds(r0, N_SEQ), :]
            h = a * h + h_s[pl.ds(r0, N_SEQ), :]
            p = a * p
            h_s[pl.ds(r0, N_SEQ), :] = h
            p_s[pl.ds(r0, N_SEQ), :] = p
            return h, p

        h_end, p_end = lax.fori_loop(0, TILE, scan,
                                     (jnp.zeros((N_SEQ, LRU_W), F32), jnp.ones((N_SEQ, LRU_W), F32)))
        hend_ref[d] = h_end

        carry_slab = jnp.zeros((N_SEQ, LRU_W), F32)
        for b in range(LAT_B):
            c = h0_ref[0, b, d:d + 1, :]
            order = range(CHUNKS) if d == 0 else reversed(range(CHUNKS))
            for j in order:
                s = N_CTX + b * CHUNKS + j
                carry_slab = jnp.where(row == s, c, carry_slab)
                c = h_end[s:s + 1, :] + p_end[s:s + 1, :] * c

        def fix(t, _, d=d, carry_slab=carry_slab):
            r0 = pl.multiple_of(t * N_SEQ, N_SEQ)
            v = h_s[pl.ds(r0, N_SEQ), :] + p_s[pl.ds(r0, N_SEQ), :] * carry_slab
            if d == 0:
                out_ref[pl.ds(r0, N_SEQ), :] = v
            else:
                out_ref[pl.ds(r0, N_SEQ), :] += v
            return 0

        lax.fori_loop(0, TILE, fix, 0)


def _lru_call(layer, xc2, wa, ba, wi, bi, sp, h0):
    full = lambda shape: pl.BlockSpec(shape, lambda i: (0,) * len(shape))
    per_layer = lambda shape: pl.BlockSpec((1,) + shape, lambda i: (layer,) + (0,) * len(shape))
    return pl.pallas_call(
        _lru_kernel,
        grid=(1,),
        in_specs=[
            full((N_TOK, LRU_W)),
            per_layer((2, LRU_W, LRU_W)),
            per_layer((2, 1, LRU_W)),
            per_layer((2, LRU_W, LRU_W)),
            per_layer((2, 1, LRU_W)),
            per_layer((2, 1, LRU_W)),
            per_layer((LAT_B, 2, LRU_W)),
        ],
        out_specs=[full((N_TOK, LRU_W)), full((2, N_SEQ, LRU_W))],
        out_shape=[
            jax.ShapeDtypeStruct((N_TOK, LRU_W), F32),
            jax.ShapeDtypeStruct((2, N_SEQ, LRU_W), F32),
        ],
        scratch_shapes=[pltpu.VMEM((N_TOK, LRU_W), F32)] * 3,
        compiler_params=_cparams(("arbitrary",)),
        name="rglru",
    )(xc2, wa, ba, wi, bi, sp, h0)


def _layer_norm(z, g, b):
    mu = jnp.mean(z, axis=-1, keepdims=True)
    zc = z - mu
    var = jnp.mean(zc * zc, axis=-1, keepdims=True)
    return zc * lax.rsqrt(var + EPS) * g + b


def _route(logits):
    lane_i = lax.broadcasted_iota(I32, logits.shape, 1)
    lane = lane_i.astype(F32)
    big = jnp.float32(1024.0)
    neg = jnp.float32(-jnp.inf)
    is_g = lane_i < N_GROUPS
    gmax = jnp.max(jnp.where(is_g, logits, neg), axis=-1, keepdims=True)
    g_sel = jnp.min(jnp.where(jnp.logical_and(is_g, logits == gmax), lane, big), axis=-1, keepdims=True)
    p_group = 1.0 / jnp.sum(jnp.where(is_g, jnp.exp(logits - gmax), 0.0), axis=-1, keepdims=True)
    e_idx = lane_i - GATE_LANE0
    e_group = (e_idx >> 2).astype(F32)
    in_g = jnp.logical_and(jnp.logical_and(e_idx >= 0, e_idx < N_EXPERTS), e_group == g_sel)
    v1 = jnp.max(jnp.where(in_g, logits, neg), axis=-1, keepdims=True)
    i1 = jnp.min(jnp.where(jnp.logical_and(in_g, logits == v1), lane, big), axis=-1, keepdims=True)
    rest = jnp.logical_and(in_g, lane != i1)
    v2 = jnp.max(jnp.where(rest, logits, neg), axis=-1, keepdims=True)
    i2 = jnp.min(jnp.where(jnp.logical_and(rest, logits == v2), lane, big), axis=-1, keepdims=True)
    e2 = jnp.exp(v2 - v1)
    inv = 1.0 / (1.0 + e2)
    w1 = inv * p_group
    w2 = e2 * inv * p_group
    return jnp.where(lane == i1, w1, jnp.where(lane == i2, w2, jnp.where(lane_i == 0, g_sel, 0.0)))


def _outproj_kernel(x_ref, oc_ref, ol_ref, yf_ref, yb_ref, yfix_ref, hl_ref, g_ref, mod_ref,
                    wglu_ref, bglu_ref, wout_ref, lng_ref, lnb_ref, rw_ref, rwhi_ref, rb_ref,
                    x1_ref, hm_ref, gates_ref, y_s, hl_s):
    i = pl.program_id(0)
    s = pl.program_id(1)

    @pl.when(s == 0)
    def _():
        for j in range(GROUP_SEQS):
            rows = slice(j * TILE, (j + 1) * TILE)
            y_s[rows, :] = yf_ref[:, j, :] + yb_ref[:, j, :]
            hl_s[rows, :] = hl_ref[:, j, :]

    lat = i >= CTX_GROUP_TILES
    sub_rows = pl.ds(pl.multiple_of(s * SUB_ROWS, SUB_ROWS), SUB_ROWS)
    o_att = jnp.where(lat, ol_ref[...], oc_ref[...])
    y = y_s[sub_rows, :] + jnp.where(lat, yfix_ref[...], 0.0)
    g = jax.nn.gelu(y, approximate=True)
    glu = jnp.dot(g.astype(BF16), wglu_ref[0], preferred_element_type=F32) + bglu_ref[0]
    o_s5 = g * jax.nn.sigmoid(glu)
    o_lru = hl_s[sub_rows, :] * jax.nn.gelu(g_ref[...], approximate=True)
    mix = jnp.concatenate([o_att, o_s5.astype(BF16), o_lru.astype(BF16)], axis=-1)
    out = jnp.dot(mix, wout_ref[0], preferred_element_type=F32)
    r = _group_mod_row(i)
    g1 = mod_ref[0, pl.ds(r, 1), 2 * D:3 * D]
    sh2 = mod_ref[0, pl.ds(r, 1), 3 * D:4 * D]
    sc2 = mod_ref[0, pl.ds(r, 1), 4 * D:5 * D]
    x1 = _layer_norm(ALPHA * x_ref[...] + g1 * out, lng_ref[0], lnb_ref[0])
    x1_ref[...] = x1
    hm = x1 * (1.0 + sc2) + sh2
    hm_ref[...] = hm
    hm_hi = hm.astype(BF16)
    hm_lo = (hm - hm_hi.astype(F32)).astype(BF16)
    p_hi = jnp.dot(hm_hi, rw_ref[0], preferred_element_type=F32)
    p_lo = jnp.dot(hm_lo, rwhi_ref[0], preferred_element_type=F32)
    logits = p_hi[:, 0:128] + p_hi[:, 128:256] + p_lo + rb_ref[0]
    gates_ref[...] = _route(logits)


def _outproj_call(layer, x, o_ctx, o_lat, yf, yb, yfix, hl, g_lru, mod, w_glu, b_glu,
                  w_out, ln_g, ln_b, rw_split, rw_hi, rb):
    n_ctx_blocks = N_CTX * TILE // SUB_ROWS
    row_spec = lambda w: pl.BlockSpec((SUB_ROWS, w), lambda i, s: (i * SUBS + s, 0))
    lat_spec = lambda w: pl.BlockSpec((SUB_ROWS, w), lambda i, s: (jnp.maximum(i * SUBS + s - n_ctx_blocks, 0), 0))
    tm_spec = pl.BlockSpec((TILE, GROUP_SEQS, 256), lambda i, s: (0, i, 0))
    vec = lambda n: pl.BlockSpec((1, 1, n), lambda i, s: (layer, 0, 0))
    mat = lambda a, b: pl.BlockSpec((1, a, b), lambda i, s: (layer, 0, 0))
    return pl.pallas_call(
        _outproj_kernel,
        grid=(N_GROUP_TILES, SUBS),
        in_specs=[
            row_spec(D),
            pl.BlockSpec((SUB_ROWS, 512), lambda i, s: (jnp.minimum(i * SUBS + s, n_ctx_blocks - 1), 0)),
            lat_spec(512),
            tm_spec, tm_spec,
            lat_spec(S5_W),
            tm_spec,
            row_spec(LRU_W),
            mat(MOD_ROWS, N_MOD * D),
            mat(S5_W, S5_W), vec(S5_W),
            mat(D, D), vec(D), vec(D),
            mat(D, 256), mat(D, 128), vec(128),
        ],
        out_specs=[row_spec(D), row_spec(D), row_spec(128)],
        out_shape=[
            jax.ShapeDtypeStruct((N_TOK, D), F32),
            jax.ShapeDtypeStruct((N_TOK, D), F32),
            jax.ShapeDtypeStruct((N_TOK, 128), F32),
        ],
        scratch_shapes=[pltpu.VMEM((GROUP_ROWS, 256), F32)] * 2,
        compiler_params=_cparams(("arbitrary", "arbitrary")),
        name="outproj",
    )(x, o_ctx, o_lat, yf.reshape(TILE, N_SEQ, S5_W), yb.reshape(TILE, N_SEQ, S5_W), yfix,
      hl.reshape(TILE, N_SEQ, LRU_W), g_lru, mod, w_glu,
      b_glu.reshape(DEPTH, 1, S5_W), w_out, ln_g.reshape(DEPTH, 1, D), ln_b.reshape(DEPTH, 1, D),
      rw_split, rw_hi, rb)


HALF = N_TOK // 2
HALF_TILES = HALF // TILE


def _sort_kernel(gates_ref, pos_ref, cnt_ref, rank_s):
    sub = lax.broadcasted_iota(I32, (8, TILE), 0)
    sel_r = lax.broadcasted_iota(I32, (8, 128), 0)
    sel_c = lax.broadcasted_iota(I32, (8, 128), 1)
    pick_lane0 = jnp.where(jnp.logical_and(sel_r == 0, sel_c == 0), 1.0, 0.0).astype(BF16)
    ri = lax.broadcasted_iota(I32, (TILE, TILE), 0)
    ci = lax.broadcasted_iota(I32, (TILE, TILE), 1)
    upper = jnp.where(ri <= ci, 1.0, 0.0).astype(BF16)

    def count(b, carry):
        rows = pl.ds(pl.multiple_of(b * TILE, TILE), TILE)
        g = gates_ref[rows, :].astype(BF16)
        g_t = lax.dot_general(pick_lane0, g, _NT, preferred_element_type=F32)
        g_sel = jnp.sum(g_t, axis=0, keepdims=True)
        onehot = jnp.where(jnp.logical_and(sub < N_GROUPS, sub.astype(F32) == g_sel), 1.0, 0.0)
        cum = jnp.dot(onehot.astype(BF16), upper, preferred_element_type=F32) + carry
        rank_s[b] = jnp.where(onehot > 0.0, cum, 0.0)
        return carry + jnp.sum(onehot, axis=1, keepdims=True)

    counts = lax.fori_loop(0, HALF_TILES, count, jnp.zeros((8, 1), F32))
    sub1 = lax.broadcasted_iota(I32, (8, 1), 0)
    c = [jnp.sum(jnp.where(sub1 == g, counts, 0.0), axis=0, keepdims=True) for g in range(N_GROUPS - 1)]
    start = jnp.where(sub1 == 1, c[0], jnp.where(sub1 == 2, c[0] + c[1],
                      jnp.where(sub1 == 3, c[0] + c[1] + c[2], 0.0)))

    def place(b, _):
        rk = rank_s[b]
        p = jnp.sum(jnp.where(rk > 0.0, rk + start - 1.0, 0.0), axis=0, keepdims=True)
        pos_ref[pl.ds(b, 1), :] = p.astype(I32)
        return 0

    lax.fori_loop(0, HALF_TILES, place, 0)
    cnt_ref[0] = jnp.broadcast_to(counts, (8, 128))


def _sort_call(gates):
    return pl.pallas_call(
        _sort_kernel,
        grid=(2,),
        in_specs=[pl.BlockSpec((HALF, 128), lambda h: (h, 0))],
        out_specs=[pl.BlockSpec((HALF_TILES, TILE), lambda h: (h, 0)),
                   pl.BlockSpec((1, 8, 128), lambda h: (h, 0, 0))],
        out_shape=[jax.ShapeDtypeStruct((N_SEQ, TILE), I32), jax.ShapeDtypeStruct((2, 8, 128), F32)],
        scratch_shapes=[pltpu.VMEM((HALF_TILES, 8, TILE), F32)],
        compiler_params=_cparams(("arbitrary",)),
        name="group_sort",
    )(gates)


ITEM_TILES = 2
N_ITEMS = N_EXPERTS * (HALF_TILES + 2 * N_GROUPS) // (EPG * ITEM_TILES)
MOE_STEPS = HALF_TILES + N_ITEMS + HALF_TILES


def _moe_schedule(counts):
    start = jnp.cumsum(counts, axis=1) - counts
    lo = start // TILE
    hi = (start + counts + TILE - 1) // TILE
    tiles_g = jnp.where(counts > 0, hi - lo, 0)
    n_g = (tiles_g + ITEM_TILES - 1) // ITEM_TILES
    n_e = jnp.repeat(n_g, EPG, axis=1)
    lo_e = jnp.repeat(lo, EPG, axis=1)
    tiles_e = jnp.repeat(tiles_g, EPG, axis=1)
    off_end = jnp.cumsum(n_e, axis=1)
    off = off_end - n_e
    total = off_end[:, -1:]
    w = jnp.arange(N_ITEMS, dtype=I32)[None, :]
    w_eff = jnp.minimum(w, total - 1)
    e_w = jnp.sum((w_eff[:, :, None] >= off_end[:, None, :]).astype(I32), axis=-1)
    first = ITEM_TILES * (w_eff - jnp.take_along_axis(off, e_w, axis=1))
    tile = jnp.take_along_axis(lo_e, e_w, axis=1) + first
    size = jnp.minimum(jnp.take_along_axis(tiles_e, e_w, axis=1) - first, ITEM_TILES)
    size = jnp.where(w < total, size, 0)
    e_f, n_f = e_w.reshape(-1), size.reshape(-1)
    n_all = 2 * N_ITEMS
    idx = jnp.arange(n_all, dtype=I32)
    key = (idx // N_ITEMS) * N_EXPERTS + e_f
    run_start = jnp.logical_and(n_f > 0, key != jnp.concatenate([jnp.full((1,), -1, I32), key[:-1]]))
    slot = (jnp.cumsum(run_start.astype(I32)) - 1) % 2
    at_or_after = lax.cummin(jnp.where(run_start, idx, n_all), reverse=True)
    after = jnp.concatenate([at_or_after[1:], jnp.full((1,), n_all, I32)])
    next_e = jnp.where(after < n_all, e_f[jnp.minimum(after, n_all - 1)], -1)
    return e_f, tile.reshape(-1), n_f, run_start.astype(I32), slot, next_e


def _moe_kernel(pos_ref, ite_ref, itt_ref, itv_ref, itf_ref, its_ref, itx_ref,
                hm_ref, gates_ref, wg_hbm, wu_hbm, wd_hbm, x1_ref, mod_ref, lng_ref, lnb_ref,
                *rest, split, layer):
    out_refs, (xs_s, gs_s, acc_s, wg_b, wu_b, wd_b, w_sem) = rest[:-7], rest[-7:]
    h = pl.program_id(0)
    w = pl.program_id(1)
    base = h * HALF

    def weight_copies(e, slot):
        pairs = ((wg_hbm, wg_b), (wu_hbm, wu_b), (wd_hbm, wd_b))
        return [pltpu.make_async_copy(src.at[layer, e], dst.at[slot], w_sem.at[slot, k])
                for k, (src, dst) in enumerate(pairs)]

    @pl.when(w < HALF_TILES)
    def _():
        @pl.when(w == 0)
        def _():
            acc_s[...] = jnp.zeros_like(acc_s)

        @pl.when(jnp.logical_and(h == 0, w == 0))
        def _():
            for c in weight_copies(ite_ref[0], 0):
                c.start()

        def body(r, c):
            p = pos_ref[base + w * TILE + r]
            xs_s[pl.ds(p, 1), :] = hm_ref[pl.ds(r, 1), :]
            gs_s[pl.ds(p, 1), :] = gates_ref[pl.ds(r, 1), :]
            return c

        lax.fori_loop(0, TILE, body, 0, unroll=8)

    @pl.when(jnp.logical_and(w >= HALF_TILES, w < HALF_TILES + N_ITEMS))
    def _():
        idx = h * N_ITEMS + (w - HALF_TILES)

        slot = its_ref[idx]

        @pl.when(itf_ref[idx] > 0)
        def _():
            @pl.when(itx_ref[idx] >= 0)
            def _():
                for c in weight_copies(itx_ref[idx], 1 - slot):
                    c.start()

            for c in weight_copies(ite_ref[idx], slot):
                c.wait()

        def item(n_tiles):
            e = ite_ref[idx]
            rows = pl.ds(pl.multiple_of(itt_ref[idx] * TILE, TILE), n_tiles * TILE)
            x = xs_s[rows, :].astype(BF16)
            a = jnp.dot(x, wg_b[slot].astype(BF16), preferred_element_type=F32)
            u = jnp.dot(x, wu_b[slot].astype(BF16), preferred_element_type=F32)
            g = gs_s[rows, :]
            lane = lax.broadcasted_iota(I32, g.shape, 1)
            ge = jnp.sum(jnp.where(lane == e + GATE_LANE0, g, 0.0), axis=-1, keepdims=True)
            act = ((a * jax.nn.sigmoid(a)) * u * ge).astype(BF16)
            acc_s[rows, :] += jnp.dot(act, wd_b[slot].astype(BF16), preferred_element_type=F32)

        for n_tiles in range(1, ITEM_TILES + 1):
            pl.when(itv_ref[idx] == n_tiles)(functools.partial(item, n_tiles))

    @pl.when(w >= HALF_TILES + N_ITEMS)
    def _():
        j = w - HALF_TILES - N_ITEMS

        def finish(o_ref):
            def body(r, c):
                p = pos_ref[base + j * TILE + r]
                o_ref[pl.ds(r, 1), :] = acc_s[pl.ds(p, 1), :]
                return c

            lax.fori_loop(0, TILE, body, 0, unroll=8)
            r = jnp.where(h == 0, 0, 1 + j // CHUNKS)
            g2 = mod_ref[0, pl.ds(r, 1), 5 * D:6 * D]
            o_ref[...] = _layer_norm(ALPHA * x1_ref[...] + g2 * o_ref[...], lng_ref[0], lnb_ref[0])

        if split:
            for half, o_ref in enumerate(out_refs):
                pl.when(h == half)(functools.partial(finish, o_ref))
        else:
            finish(out_refs[0])


def _moe_call(layer, pos, items, hm, gates, w_gate, w_up, w_down, x1, mod, ln_g, ln_b, split):
    def in_tile(h, w, *_):
        return (h * HALF_TILES + jnp.minimum(w, HALF_TILES - 1), 0)

    def out_step(w):
        return jnp.clip(w - HALF_TILES - N_ITEMS, 0, HALF_TILES - 1)

    def out_tile(h, w, *_):
        return (h * HALF_TILES + out_step(w), 0)

    if split:
        out_specs = [pl.BlockSpec((TILE, D), lambda h, w, *_: (jnp.where(h == 0, out_step(w), HALF_TILES - 1), 0)),
                     pl.BlockSpec((TILE, D), lambda h, w, *_: (jnp.where(h == 1, out_step(w), 0), 0))]
        out_shape = [jax.ShapeDtypeStruct((HALF, D), F32)] * 2
    else:
        out_specs = pl.BlockSpec((TILE, D), out_tile)
        out_shape = jax.ShapeDtypeStruct((N_TOK, D), F32)

    vec = lambda n: pl.BlockSpec((1, 1, n), lambda h, w, *_: (layer, 0, 0))
    grid_spec = pltpu.PrefetchScalarGridSpec(
        num_scalar_prefetch=1 + len(items),
        grid=(2, MOE_STEPS),
        in_specs=[
            pl.BlockSpec((TILE, D), in_tile),
            pl.BlockSpec((TILE, 128), in_tile),
            pl.BlockSpec(memory_space=pl.ANY),
            pl.BlockSpec(memory_space=pl.ANY),
            pl.BlockSpec(memory_space=pl.ANY),
            pl.BlockSpec((TILE, D), out_tile),
            pl.BlockSpec((1, MOD_ROWS, N_MOD * D), lambda h, w, *_: (layer, 0, 0)),
            vec(D), vec(D),
        ],
        out_specs=out_specs,
        scratch_shapes=[
            pltpu.VMEM((HALF, D), F32),
            pltpu.VMEM((HALF, 128), F32),
            pltpu.VMEM((HALF, D), F32),
            pltpu.VMEM((2, D, D_EXPERT), F32),
            pltpu.VMEM((2, D, D_EXPERT), F32),
            pltpu.VMEM((2, D_EXPERT, D), F32),
            pltpu.SemaphoreType.DMA((2, 3)),
        ],
    )
    return pl.pallas_call(
        functools.partial(_moe_kernel, split=split, layer=layer),
        grid_spec=grid_spec,
        out_shape=out_shape,
        compiler_params=pltpu.CompilerParams(dimension_semantics=("arbitrary", "arbitrary"),
                                             vmem_limit_bytes=MOE_VMEM_LIMIT),
        name="moe",
    )(pos, *items, hm, gates, w_gate, w_up, w_down, x1, mod,
      ln_g.reshape(DEPTH, 1, D), ln_b.reshape(DEPTH, 1, D))


def _rope_tables():
    rows = LAT_T // GRID_W
    row = jnp.repeat(jnp.arange(rows, dtype=F32), GRID_W)
    col = jnp.tile(jnp.arange(GRID_W, dtype=F32), rows)
    n_freq = QK // 4
    inv = ROPE_BASE ** (-jnp.arange(n_freq, dtype=F32) / n_freq)
    ang_r = row[:, None] * inv
    ang_c = col[:, None] * inv
    cos64 = jnp.concatenate([jnp.cos(ang_r), jnp.cos(ang_r), jnp.cos(ang_c), jnp.cos(ang_c)], axis=1)
    sin64 = jnp.concatenate([-jnp.sin(ang_r), jnp.sin(ang_r), -jnp.sin(ang_c), jnp.sin(ang_c)], axis=1)
    cos = jnp.concatenate([jnp.tile(cos64, (1, 2)), jnp.ones((SUB_ROWS, 128), F32)], axis=0)
    sin = jnp.concatenate([jnp.tile(sin64, (1, 2)), jnp.zeros((SUB_ROWS, 128), F32)], axis=0)
    return cos, sin


def _s5_params(a_re, a_im, b_re, b_im, c_re, c_im, log_dt):
    dt = jnp.exp(log_dt)[..., None]
    mag = jnp.exp(a_re * dt)
    abar_r = mag * jnp.cos(a_im * dt)
    abar_i = mag * jnp.sin(a_im * dt)
    den = a_re * a_re + a_im * a_im
    nr = abar_r - 1.0
    coef_r = (nr * a_re + abar_i * a_im) / den
    coef_i = (abar_i * a_re - nr * a_im) / den
    bbar_r = coef_r[..., None] * b_re - coef_i[..., None] * b_im
    bbar_i = coef_r[..., None] * b_im + coef_i[..., None] * b_re
    def b_rows(bb):
        return bb.transpose(0, 1, 2, 4, 3).reshape(DEPTH, 2, S5_W, S5_P)
    br, bi = b_rows(bbar_r), b_rows(bbar_i)
    b_cmp = jnp.concatenate([br, br, bi, bi], axis=-1)
    def c_rows(cc):
        return cc.transpose(0, 1, 4, 2, 3).reshape(DEPTH, 2, S5_P, S5_W)
    c_cmp = jnp.concatenate([c_rows(c_re), c_rows(-c_im)], axis=-2)
    a_bar = jnp.concatenate([abar_r.reshape(DEPTH, 2, S5_N), abar_i.reshape(DEPTH, 2, S5_N)], axis=-1)
    return b_cmp, c_cmp, a_bar


def _block_diag(w):
    eye = jnp.eye(LRU_BLOCKS, dtype=F32)
    m = jnp.einsum('ldkij,kh->ldkihj', w, eye)
    return m.reshape(DEPTH, 2, LRU_W, LRU_W).astype(BF16)


def kernel(x_prompt, x_sample, cache_k, cache_v, state_s5, state_lru, c, c_ctx, w_ada, b_ada, w_in, w_out, lam_q1, lam_k1, lam_q2, lam_k2, subln_g, s5_a_re, s5_a_im, s5_b_re, s5_b_im, s5_c_re, s5_c_im, s5_log_dt, s5_d, s5_w_glu, s5_b_glu, lru_conv_w, lru_conv_b, lru_w_a, lru_b_a, lru_w_i, lru_b_i, lru_lambda, ln1_g, ln1_b, ln2_g, ln2_b, router_group_w, router_group_b, router_expert_w, router_expert_b, moe_w_gate, moe_w_up, moe_w_down):
    x = jnp.concatenate([x_prompt.reshape(CTX_B * CTX_T, D), x_sample.reshape(LAT_B * LAT_T, D)], axis=0)
    cond = jnp.concatenate([c_ctx[None, :], c, jnp.zeros((MOD_ROWS - 1 - LAT_B, D), F32)], axis=0)
    mod = _ada_call(cond, w_ada, b_ada)

    w_out_b = w_out.astype(BF16)
    w_glu_b = s5_w_glu.astype(BF16)
    rope_cos, rope_sin = _rope_tables()
    b_cmp, c_cmp, a_bar = _s5_params(s5_a_re, s5_a_im, s5_b_re, s5_b_im, s5_c_re, s5_c_im, s5_log_dt)
    wa_blk = _block_diag(lru_w_a)
    wi_blk = _block_diag(lru_w_i)
    ba = lru_b_a.reshape(DEPTH, 2, 1, LRU_W)
    bi = lru_b_i.reshape(DEPTH, 2, 1, LRU_W)
    sp = jax.nn.softplus(-lru_lambda).reshape(DEPTH, 2, 1, LRU_W)
    lam = (jnp.exp(jnp.sum(lam_q1 * lam_k1, axis=-1)) - jnp.exp(jnp.sum(lam_q2 * lam_k2, axis=-1)))
    rw = jnp.concatenate([router_group_w, router_expert_w,
                          jnp.zeros((DEPTH, D, 128 - N_GROUPS - N_EXPERTS), F32)], axis=-1)
    rw_hi = rw.astype(BF16)
    rw_split = jnp.concatenate([rw_hi, (rw - rw_hi.astype(F32)).astype(BF16)], axis=-1)
    rb = jnp.concatenate([router_group_b, router_expert_b,
                          jnp.zeros((DEPTH, 128 - N_GROUPS - N_EXPERTS), F32)], axis=-1).reshape(DEPTH, 1, 128)
    s5_h0 = state_s5.reshape(LAT_B, DEPTH, 2, 2 * S5_N)
    lru_h0 = state_lru.transpose(1, 0, 2, 3)

    kc = jnp.zeros((CTX_B, DEPTH, HEADS, CTX_T, 2 * QK), F32)
    vc = jnp.zeros((CTX_B, DEPTH, HEADS, CTX_T, DV), F32)
    s5_states = []
    lru_states = []
    for l in range(DEPTH):
        lambda_init = 0.8 - 0.6 * math.exp(-0.3 * l)
        lam_l = (lam[l] + lambda_init).reshape(1)
        q, k, v, kc, vc, u_tm, xc_tm, g_lru = _inproj_call(
            l, x, mod, w_in, rope_cos, rope_sin, lru_conv_w, lru_conv_b, kc, vc)
        o_ctx = _att_ctx_call(l, lam_l, q, k, v, subln_g, 1.0 - lambda_init)
        o_lat = _att_lat_call(l, lam_l, q, k, v, cache_k, cache_v, subln_g, 1.0 - lambda_init)
        yf, yb, hend = _s5_call(l, u_tm.reshape(N_TOK, S5_W), b_cmp, c_cmp, a_bar, s5_d)
        yfix = _s5_fix_call(l, hend, s5_h0[:, l], a_bar, c_cmp)
        hl, lru_end = _lru_call(l, xc_tm.reshape(N_TOK, LRU_W), wa_blk, ba, wi_blk, bi, sp, lru_h0)
        x1, hm, gates = _outproj_call(l, x, o_ctx, o_lat, yf, yb, yfix, hl, g_lru, mod,
                                      w_glu_b, s5_b_glu, w_out_b, ln1_g, ln1_b, rw_split, rw_hi, rb)
        pos, counts = _sort_call(gates)
        items = _moe_schedule(counts[:, :N_GROUPS, 0].astype(I32))
        x = _moe_call(l, pos.reshape(N_TOK), items, hm, gates, moe_w_gate, moe_w_up,
                      moe_w_down, x1, mod, ln2_g, ln2_b, split=(l == DEPTH - 1))
        s5_states.append(hend[:, :N_CTX].reshape(2, CTX_B, 2, S5_G, S5_P).transpose(1, 0, 2, 3, 4))
        lru_states.append(lru_end[:, :N_CTX].transpose(1, 0, 2))

    y_p = x[0].reshape(CTX_B, CTX_T, D)
    y_s = x[1].reshape(LAT_B, LAT_T, D)
    return (y_p, y_s, kc, vc, jnp.stack(s5_states, axis=1), jnp.stack(lru_states, axis=1))
```

```python
import functools
import math

import jax
import jax.numpy as jnp
from jax import lax
from jax.experimental import pallas as pl
from jax.experimental.pallas import tpu as pltpu

F32 = jnp.float32
BF16 = jnp.bfloat16
I32 = jnp.int32

D = 1024
DEPTH = 4
CTX_B = 16
CTX_T = 256
LAT_B = 2
LAT_T = 2048
PAST = 256
GRID_W = 64
HEADS = 4
QK = 64
DV = 128
S5_W = 256
S5_G = 16
S5_C = 16
S5_P = 64
S5_N = S5_G * S5_P
S5_C_SHIFT = S5_C.bit_length() - 1
S5_P_SHIFT = S5_P.bit_length() - 1
LRU_W = 256
LRU_BLOCKS = 4
LRU_C = 8.0
N_GROUPS = 4
EPG = 4
N_EXPERTS = 16
D_EXPERT = 512
N_MOD = 6
IN_W = 2304
ROPE_BASE = 10000.0
ALPHA = (2 * DEPTH) ** 0.25
EPS = 1e-5

TILE = 256
N_TOK = CTX_B * CTX_T + LAT_B * LAT_T
N_SEQ = N_TOK // TILE
N_CTX = CTX_B
CHUNKS = LAT_T // TILE
N_LAT = LAT_B * CHUNKS
MOD_ROWS = 8
GATE_LANE0 = 4
VMEM_LIMIT = 56 * 1024 * 1024
MOE_VMEM_LIMIT = 60 * 1024 * 1024

GROUP_SEQS = 8
GROUP_ROWS = GROUP_SEQS * TILE
N_GROUP_TILES = N_TOK // GROUP_ROWS
CTX_GROUP_TILES = N_CTX // GROUP_SEQS
SUB_ROWS = 512
SUBS = GROUP_ROWS // SUB_ROWS
SUB_SEQS = SUB_ROWS // TILE


def _cparams(sem):
    return pltpu.CompilerParams(dimension_semantics=sem, vmem_limit_bytes=VMEM_LIMIT)


def _group_mod_row(i):
    return jnp.where(i < CTX_GROUP_TILES, 0, i - CTX_GROUP_TILES + 1)


def _ada_kernel(c_ref, w_ref, b_ref, o_ref):
    c = c_ref[...]
    s = (c * jax.nn.sigmoid(c)).astype(BF16)
    o_ref[0] = jnp.dot(s, w_ref[0].astype(BF16), preferred_element_type=F32) + b_ref[0]


def _ada_call(cond, w_ada, b_ada):
    tn = 1536
    return pl.pallas_call(
        _ada_kernel,
        grid=(DEPTH, N_MOD * D // tn),
        in_specs=[
            pl.BlockSpec((MOD_ROWS, D), lambda l, j: (0, 0)),
            pl.BlockSpec((1, D, tn), lambda l, j: (l, 0, j)),
            pl.BlockSpec((1, 1, tn), lambda l, j: (l, 0, j)),
        ],
        out_specs=pl.BlockSpec((1, MOD_ROWS, tn), lambda l, j: (l, 0, j)),
        out_shape=jax.ShapeDtypeStruct((DEPTH, MOD_ROWS, N_MOD * D), F32),
        compiler_params=_cparams(("arbitrary", "arbitrary")),
        name="adaln",
    )(cond, w_ada, b_ada.reshape(DEPTH, 1, N_MOD * D))


def _inproj_kernel(x_ref, mod_ref, w_ref, cos_ref, sin_ref, cw_ref, cb_ref, kc_in, vc_in,
                   q_ref, k_ref, v_ref, kc_ref, vc_ref, u_ref, xc_ref, g_ref, u_s, xl_s, w_s):
    del kc_in, vc_in
    i = pl.program_id(0)
    s = pl.program_id(1)

    @pl.when(jnp.logical_and(i == 0, s == 0))
    def _():
        for t in range(2 * HEADS):
            kind, head = divmod(t, HEADS)
            halves = []
            for m in range(2):
                col = (2 * kind + m) * HEADS * QK + head * QK
                blk = w_ref[0, :, (col // 128) * 128:(col // 128 + 1) * 128]
                halves.append(blk[:, col % 128:col % 128 + QK])
            w_s[:, t * 128:(t + 1) * 128] = jnp.concatenate(halves, axis=1).astype(BF16)
        w_s[:, 1024:IN_W] = w_ref[0, :, 1024:IN_W].astype(BF16)

    r = _group_mod_row(i)
    sh = mod_ref[0, pl.ds(r, 1), 0:D]
    sc = mod_ref[0, pl.ds(r, 1), D:2 * D]
    xm = (x_ref[...] * (1.0 + sc) + sh).astype(BF16)
    proj = jnp.dot(xm, w_s[...], preferred_element_type=F32)

    qk = proj[:, 0:1024]
    cos = jnp.concatenate([cos_ref[...]] * 8, axis=1)
    sin = jnp.concatenate([sin_ref[...]] * 8, axis=1)
    lane = lax.broadcasted_iota(I32, qk.shape, 1)
    swapped = jnp.where((lane & 31) < 16, pltpu.roll(qk, 1024 - 16, 1), pltpu.roll(qk, 16, 1))
    qk = qk * cos + swapped * sin
    q_ref[...] = qk[:, 0:512].astype(BF16)
    k_ref[...] = qk[:, 512:1024].astype(BF16)
    v = proj[:, 1024:1536]
    v_ref[...] = v.astype(BF16)

    @pl.when(i < CTX_GROUP_TILES)
    def _():
        for jj in range(SUB_SEQS):
            rows = slice(jj * TILE, (jj + 1) * TILE)
            for h in range(HEADS):
                kc_ref[jj, 0, h] = qk[rows, 512 + h * 128:512 + (h + 1) * 128]
                vc_ref[jj, 0, h] = v[rows, h * 128:(h + 1) * 128]

    g_ref[...] = proj[:, 2048:2304]
    sub_rows = pl.ds(pl.multiple_of(s * SUB_ROWS, SUB_ROWS), SUB_ROWS)
    u_s[sub_rows, :] = proj[:, 1536:1792]
    xl_s[sub_rows, :] = proj[:, 1792:2048]

    @pl.when(s == SUBS - 1)
    def _():
        xl = xl_s[...]
        row = lax.broadcasted_iota(I32, xl.shape, 0)
        is_ctx = i < CTX_GROUP_TILES
        pos = jnp.where(is_ctx, row & (TILE - 1), row)
        last = jnp.where(is_ctx, TILE - 1, GROUP_ROWS - 1)
        x_m1 = jnp.where(pos == 0, 0.0, pltpu.roll(xl, 1, 0))
        x_p1 = jnp.where(pos == last, 0.0, pltpu.roll(xl, GROUP_ROWS - 1, 0))
        x_p2 = jnp.where(pos >= last - 1, 0.0, pltpu.roll(xl, GROUP_ROWS - 2, 0))
        cw = cw_ref[0]
        xc = cb_ref[0] + x_m1 * cw[0:1] + xl * cw[1:2] + x_p1 * cw[2:3] + x_p2 * cw[3:4]
        for j in range(GROUP_SEQS):
            xc_ref[:, j, :] = xc[j * TILE:(j + 1) * TILE]
            u_ref[:, j, :] = u_s[j * TILE:(j + 1) * TILE, :]


def _inproj_call(layer, x, mod, w_in, rope_cos, rope_sin, conv_w, conv_b, kc, vc):
    n_ctx_blocks = N_CTX // SUB_SEQS

    def rope_idx(i, s):
        return (jnp.where(i < CTX_GROUP_TILES, SUBS, s), 0)

    def cache_idx(i, s):
        return (jnp.minimum(i * SUBS + s, n_ctx_blocks - 1), layer, 0, 0, 0)

    row_spec = lambda w: pl.BlockSpec((SUB_ROWS, w), lambda i, s: (i * SUBS + s, 0))
    tm_spec = pl.BlockSpec((TILE, GROUP_SEQS, 256), lambda i, s: (0, i, 0))
    cache_spec = pl.BlockSpec((SUB_SEQS, 1, HEADS, TILE, 128), cache_idx)
    return pl.pallas_call(
        _inproj_kernel,
        grid=(N_GROUP_TILES, SUBS),
        in_specs=[
            row_spec(D),
            pl.BlockSpec((1, MOD_ROWS, N_MOD * D), lambda i, s: (layer, 0, 0)),
            pl.BlockSpec((1, D, IN_W), lambda i, s: (layer, 0, 0)),
            pl.BlockSpec((SUB_ROWS, 128), rope_idx),
            pl.BlockSpec((SUB_ROWS, 128), rope_idx),
            pl.BlockSpec((1, 4, LRU_W), lambda i, s: (layer, 0, 0)),
            pl.BlockSpec((1, 1, LRU_W), lambda i, s: (layer, 0, 0)),
            pl.BlockSpec(memory_space=pl.ANY),
            pl.BlockSpec(memory_space=pl.ANY),
        ],
        out_specs=[
            row_spec(512), row_spec(512), row_spec(512),
            cache_spec, cache_spec,
            tm_spec, tm_spec,
            row_spec(LRU_W),
        ],
        out_shape=[
            jax.ShapeDtypeStruct((N_TOK, 512), BF16),
            jax.ShapeDtypeStruct((N_TOK, 512), BF16),
            jax.ShapeDtypeStruct((N_TOK, 512), BF16),
            jax.ShapeDtypeStruct(kc.shape, F32),
            jax.ShapeDtypeStruct(vc.shape, F32),
            jax.ShapeDtypeStruct((TILE, N_SEQ, S5_W), F32),
            jax.ShapeDtypeStruct((TILE, N_SEQ, LRU_W), F32),
            jax.ShapeDtypeStruct((N_TOK, LRU_W), F32),
        ],
        scratch_shapes=[pltpu.VMEM((GROUP_ROWS, S5_W), F32), pltpu.VMEM((GROUP_ROWS, LRU_W), F32),
                        pltpu.VMEM((D, IN_W), BF16)],
        input_output_aliases={7: 3, 8: 4},
        compiler_params=_cparams(("arbitrary", "arbitrary")),
        name="inproj",
    )(x, mod, w_in, rope_cos, rope_sin, conv_w, conv_b.reshape(DEPTH, 1, LRU_W), kc, vc)


_NT = (((1,), (1,)), ((), ()))


def _split_maps(q):
    lane = lax.broadcasted_iota(I32, q.shape, 1)
    zero = jnp.zeros_like(q)
    qs = q * (QK ** -0.5)
    return jnp.where(lane < QK, qs, zero), jnp.where(lane >= QK, qs, zero)


def _softmax_values(qm, keys, values):
    scores = [lax.dot_general(qm, k, _NT, preferred_element_type=F32) for k in keys]
    m = scores[0].max(axis=-1, keepdims=True)
    for sc in scores[1:]:
        m = jnp.maximum(m, sc.max(axis=-1, keepdims=True))
    z = None
    o = None
    for sc, v in zip(scores, values):
        e = jnp.exp(sc - m)
        ez = e.sum(axis=-1, keepdims=True)
        eo = jnp.dot(e.astype(BF16), v, preferred_element_type=F32)
        z = ez if z is None else z + ez
        o = eo if o is None else o + eo
    return o * (1.0 / z)


def _diff_attention(q, keys, values, lam, g, post_scale):
    q1, q2 = _split_maps(q)
    o = _softmax_values(q1, keys, values) - lam * _softmax_values(q2, keys, values)
    ms = jnp.mean(o * o, axis=-1, keepdims=True)
    return (o * lax.rsqrt(ms + EPS) * g) * post_scale


def _att_ctx_kernel(lam_ref, q_ref, k_ref, v_ref, g_ref, o_ref, *, post_scale):
    lam = lam_ref[0]
    for h in range(HEADS):
        cols = slice(h * 128, (h + 1) * 128)
        o = _diff_attention(q_ref[:, cols], [k_ref[:, cols]], [v_ref[:, cols]], lam, g_ref[0], post_scale)
        o_ref[:, cols] = o.astype(BF16)


def _att_ctx_call(layer, lam, q, k, v, subln_g, post_scale):
    blk = pl.BlockSpec((TILE, HEADS * 128), lambda b: (b, 0))
    return pl.pallas_call(
        functools.partial(_att_ctx_kernel, post_scale=post_scale),
        grid=(CTX_B,),
        in_specs=[
            pl.BlockSpec(memory_space=pltpu.SMEM),
            blk, blk, blk,
            pl.BlockSpec((1, 1, DV), lambda b: (layer, 0, 0)),
        ],
        out_specs=blk,
        out_shape=jax.ShapeDtypeStruct((CTX_B * CTX_T, HEADS * DV), BF16),
        compiler_params=_cparams(("arbitrary",)),
        name="att_ctx",
    )(lam, q, k, v, subln_g.reshape(DEPTH, 1, DV))


def _att_lat_kernel(lam_ref, q_ref, k_ref, v_ref, ck_ref, cv_ref, g_ref, o_ref, *, post_scale):
    o = _diff_attention(q_ref[...], [k_ref[...], ck_ref[...].astype(BF16)],
                        [v_ref[...], cv_ref[...].astype(BF16)], lam_ref[0], g_ref[0], post_scale)
    o_ref[...] = o.astype(BF16)


def _att_lat_call(layer, lam, q, k, v, cache_k, cache_v, subln_g, post_scale):
    lat0 = N_CTX
    cache_spec = pl.BlockSpec((None, None, None, PAST, 128), lambda b, h, t: (b, layer, h, 0, 0))
    return pl.pallas_call(
        functools.partial(_att_lat_kernel, post_scale=post_scale),
        grid=(LAT_B, HEADS, CHUNKS),
        in_specs=[
            pl.BlockSpec(memory_space=pltpu.SMEM),
            pl.BlockSpec((TILE, 128), lambda b, h, t: (lat0 + b * CHUNKS + t, h)),
            pl.BlockSpec((LAT_T, 128), lambda b, h, t: (lat0 // CHUNKS + b, h)),
            pl.BlockSpec((LAT_T, 128), lambda b, h, t: (lat0 // CHUNKS + b, h)),
            cache_spec, cache_spec,
            pl.BlockSpec((1, 1, DV), lambda b, h, t: (layer, 0, 0)),
        ],
        out_specs=pl.BlockSpec((TILE, 128), lambda b, h, t: (b * CHUNKS + t, h)),
        out_shape=jax.ShapeDtypeStruct((LAT_B * LAT_T, HEADS * DV), BF16),
        compiler_params=_cparams(("arbitrary", "arbitrary", "arbitrary")),
        name="att_lat",
    )(lam, q, k, v, cache_k, cache_v, subln_g.reshape(DEPTH, 1, DV))


S5_TB = 32
S5_LANES = 128


def _cmul(ar, ai, br, bi):
    return ar * br - ai * bi, ar * bi + ai * br


def _expand_b(b):
    full = jnp.concatenate([b[:, 0:128]] * (S5_N // 128) + [b[:, 128:256]] * (S5_N // 128), axis=1)
    row = lax.broadcasted_iota(I32, full.shape, 0)
    col = lax.broadcasted_iota(I32, full.shape, 1)
    same_group = (row >> S5_C_SHIFT) == ((col & (S5_N - 1)) >> S5_P_SHIFT)
    return jnp.where(same_group, full, 0.0).astype(BF16)


def _expand_c(c):
    full = jnp.concatenate([c[0:S5_P]] * S5_G + [c[S5_P:2 * S5_P]] * S5_G, axis=0)
    row = lax.broadcasted_iota(I32, full.shape, 0)
    col = lax.broadcasted_iota(I32, full.shape, 1)
    same_group = ((row & (S5_N - 1)) >> S5_P_SHIFT) == (col >> S5_C_SHIFT)
    return jnp.where(same_group, full, 0.0).astype(BF16)


def _s5_kernel(uf_ref, ub_ref, b_ref, c_ref, a_ref, d_ref, yf_ref, yb_ref, hend_ref,
               hf_s, hb_s, buf_f, buf_b, bblk_s, cblk_s):
    i = pl.program_id(0)

    @pl.when(i == 0)
    def _():
        hf_s[...] = jnp.zeros_like(hf_s)
        hb_s[...] = jnp.zeros_like(hb_s)
        for d in range(2):
            bblk_s[d] = _expand_b(b_ref[0, d])
            cblk_s[d] = _expand_c(c_ref[0, d])

    buf_f[...] = jnp.dot(uf_ref[...].astype(BF16), bblk_s[0], preferred_element_type=F32)
    buf_b[...] = jnp.dot(ub_ref[...].astype(BF16), bblk_s[1], preferred_element_type=F32)

    def scan(buf, h_s, d):
        for c in range(S5_N // S5_LANES):
            re = slice(c * S5_LANES, (c + 1) * S5_LANES)
            im = slice(S5_N + c * S5_LANES, S5_N + (c + 1) * S5_LANES)
            ar = a_ref[0, d:d + 1, re]
            ai = a_ref[0, d:d + 1, im]
            hr, hi = h_s[:, re], h_s[:, im]
            for t in range(S5_TB):
                tt = t if d == 0 else S5_TB - 1 - t
                rows = slice(tt * N_SEQ, (tt + 1) * N_SEQ)
                pr, pi = _cmul(ar, ai, hr, hi)
                hr = pr + buf[rows, re]
                hi = pi + buf[rows, im]
                buf[rows, re] = hr
                buf[rows, im] = hi
            h_s[:, re] = hr
            h_s[:, im] = hi

    scan(buf_f, hf_s, 0)
    yf_ref[...] = (jnp.dot(buf_f[...].astype(BF16), cblk_s[0], preferred_element_type=F32)
                   + d_ref[0] * uf_ref[...])
    scan(buf_b, hb_s, 1)
    yb_ref[...] = jnp.dot(buf_b[...].astype(BF16), cblk_s[1], preferred_element_type=F32)

    @pl.when(i == pl.num_programs(0) - 1)
    def _():
        hend_ref[0] = hf_s[...]
        hend_ref[1] = hb_s[...]


def _s5_call(layer, u2, b_cmp, c_cmp, a_bar, s5_d):
    nb = CTX_T // S5_TB
    rows = S5_TB * N_SEQ
    return pl.pallas_call(
        _s5_kernel,
        grid=(nb,),
        in_specs=[
            pl.BlockSpec((rows, S5_W), lambda i: (i, 0)),
            pl.BlockSpec((rows, S5_W), lambda i: (nb - 1 - i, 0)),
            pl.BlockSpec((1, 2, S5_W, 256), lambda i: (layer, 0, 0, 0)),
            pl.BlockSpec((1, 2, 2 * S5_P, S5_W), lambda i: (layer, 0, 0, 0)),
            pl.BlockSpec((1, 2, 2 * S5_N), lambda i: (layer, 0, 0)),
            pl.BlockSpec((1, 1, S5_W), lambda i: (layer, 0, 0)),
        ],
        out_specs=[
            pl.BlockSpec((rows, S5_W), lambda i: (i, 0)),
            pl.BlockSpec((rows, S5_W), lambda i: (nb - 1 - i, 0)),
            pl.BlockSpec((2, N_SEQ, 2 * S5_N), lambda i: (0, 0, 0)),
        ],
        out_shape=[
            jax.ShapeDtypeStruct((N_TOK, S5_W), F32),
            jax.ShapeDtypeStruct((N_TOK, S5_W), F32),
            jax.ShapeDtypeStruct((2, N_SEQ, 2 * S5_N), F32),
        ],
        scratch_shapes=[
            pltpu.VMEM((N_SEQ, 2 * S5_N), F32),
            pltpu.VMEM((N_SEQ, 2 * S5_N), F32),
            pltpu.VMEM((rows, 2 * S5_N), F32),
            pltpu.VMEM((rows, 2 * S5_N), F32),
            pltpu.VMEM((2, S5_W, 2 * S5_N), BF16),
            pltpu.VMEM((2, 2 * S5_N, S5_W), BF16),
        ],
        compiler_params=_cparams(("arbitrary",)),
        name="s5_scan",
    )(u2, u2, b_cmp, c_cmp, a_bar, s5_d.reshape(DEPTH, 1, S5_W))


def _s5_fix_kernel(hend_ref, h0_ref, a_ref, c_ref, o_ref, pf_s, pb_s, cf_s, cb_s, cblk_s):
    s = pl.program_id(0)

    @pl.when(s == 0)
    def _():
        for d in range(2):
            cblk_s[d] = _expand_c(c_ref[0, d])
        row8 = lax.broadcasted_iota(I32, (8, S5_N), 0)
        for d, tab in ((0, pf_s), (1, pb_s)):
            ar = a_ref[0, d:d + 1, 0:S5_N]
            ai = a_ref[0, d:d + 1, S5_N:2 * S5_N]
            pr, pi = ar, ai
            r8 = jnp.zeros((8, S5_N), F32)
            i8 = jnp.zeros((8, S5_N), F32)
            for r in range(8):
                if r:
                    pr, pi = _cmul(pr, pi, ar, ai)
                at = r if d == 0 else 7 - r
                r8 = jnp.where(row8 == at, pr, r8)
                i8 = jnp.where(row8 == at, pi, i8)
            base = 0 if d == 0 else TILE - 8
            tab[base:base + 8, 0:S5_N] = r8
            tab[base:base + 8, S5_N:2 * S5_N] = i8
            m = 8
            while m < TILE:
                if d == 0:
                    src, dst, top = slice(0, m), slice(m, 2 * m), slice(m - 1, m)
                else:
                    src, dst, top = slice(TILE - m, TILE), slice(TILE - 2 * m, TILE - m), slice(TILE - m, TILE - m + 1)
                mr, mi = tab[top, 0:S5_N], tab[top, S5_N:2 * S5_N]
                nr, ni = _cmul(tab[src, 0:S5_N], tab[src, S5_N:2 * S5_N], mr, mi)
                tab[dst, 0:S5_N] = nr
                tab[dst, S5_N:2 * S5_N] = ni
                m *= 2

        for b in range(LAT_B):
            ar, ai = pf_s[TILE - 1:TILE, 0:S5_N], pf_s[TILE - 1:TILE, S5_N:2 * S5_N]
            cr, ci = h0_ref[b, 0:1, 0:S5_N], h0_ref[b, 0:1, S5_N:2 * S5_N]
            for j in range(CHUNKS):
                row = b * CHUNKS + j
                cf_s[row:row + 1, 0:S5_N] = cr
                cf_s[row:row + 1, S5_N:2 * S5_N] = ci
                pr, pi = _cmul(ar, ai, cr, ci)
                cr = pr + hend_ref[0, N_CTX + row:N_CTX + row + 1, 0:S5_N]
                ci = pi + hend_ref[0, N_CTX + row:N_CTX + row + 1, S5_N:2 * S5_N]
            ar, ai = pb_s[0:1, 0:S5_N], pb_s[0:1, S5_N:2 * S5_N]
            cr, ci = h0_ref[b, 1:2, 0:S5_N], h0_ref[b, 1:2, S5_N:2 * S5_N]
            for j in reversed(range(CHUNKS)):
                row = b * CHUNKS + j
                cb_s[row:row + 1, 0:S5_N] = cr
                cb_s[row:row + 1, S5_N:2 * S5_N] = ci
                pr, pi = _cmul(ar, ai, cr, ci)
                cr = pr + hend_ref[1, N_CTX + row:N_CTX + row + 1, 0:S5_N]
                ci = pi + hend_ref[1, N_CTX + row:N_CTX + row + 1, S5_N:2 * S5_N]

    acc = None
    for d, tab, car in ((0, pf_s, cf_s), (1, pb_s, cb_s)):
        cr = car[pl.ds(s, 1), 0:S5_N]
        ci = car[pl.ds(s, 1), S5_N:2 * S5_N]
        hr, hi = _cmul(tab[:, 0:S5_N], tab[:, S5_N:2 * S5_N], cr, ci)
        h = jnp.concatenate([hr, hi], axis=1).astype(BF16)
        y = jnp.dot(h, cblk_s[d], preferred_element_type=F32)
        acc = y if acc is None else acc + y
    o_ref[...] = acc


def _s5_fix_call(layer, hend, h0, a_bar, c_cmp):
    return pl.pallas_call(
        _s5_fix_kernel,
        grid=(N_LAT,),
        in_specs=[
            pl.BlockSpec((2, N_SEQ, 2 * S5_N), lambda s: (0, 0, 0)),
            pl.BlockSpec((LAT_B, 2, 2 * S5_N), lambda s: (0, 0, 0)),
            pl.BlockSpec((1, 2, 2 * S5_N), lambda s: (layer, 0, 0)),
            pl.BlockSpec((1, 2, 2 * S5_P, S5_W), lambda s: (layer, 0, 0, 0)),
        ],
        out_specs=pl.BlockSpec((TILE, S5_W), lambda s: (s, 0)),
        out_shape=jax.ShapeDtypeStruct((N_LAT * TILE, S5_W), F32),
        scratch_shapes=[
            pltpu.VMEM((TILE, 2 * S5_N), F32),
            pltpu.VMEM((TILE, 2 * S5_N), F32),
            pltpu.VMEM((N_LAT, 2 * S5_N), F32),
            pltpu.VMEM((N_LAT, 2 * S5_N), F32),
            pltpu.VMEM((2, 2 * S5_N, S5_W), BF16),
        ],
        compiler_params=_cparams(("arbitrary",)),
        name="s5_fix",
    )(hend, h0, a_bar, c_cmp)


LRU_ROWS = 1024


def _sigmoid(x):
    return 0.5 * jnp.tanh(0.5 * x) + 0.5


def _lru_kernel(xc_ref, wa_ref, ba_ref, wi_ref, bi_ref, sp_ref, h0_ref, out_ref, hend_ref,
                a_s, h_s, p_s):
    row = lax.broadcasted_iota(I32, (N_SEQ, LRU_W), 0)
    for d in range(2):
        def gates(cix, _, d=d):
            r0 = pl.multiple_of(cix * LRU_ROWS, LRU_ROWS)
            xc = xc_ref[pl.ds(r0, LRU_ROWS), :]
            xb = xc.astype(BF16)
            r = _sigmoid(jnp.dot(xb, wa_ref[0, d], preferred_element_type=F32) + ba_ref[0, d])
            g = _sigmoid(jnp.dot(xb, wi_ref[0, d], preferred_element_type=F32) + bi_ref[0, d])
            log_a = (-LRU_C) * r * sp_ref[0, d]
            a_s[pl.ds(r0, LRU_ROWS), :] = jnp.exp(log_a)
            th = jnp.tanh(log_a)
            h_s[pl.ds(r0, LRU_ROWS), :] = jnp.sqrt(-2.0 * th / (1.0 - th)) * g * xc
            return 0

        lax.fori_loop(0, N_TOK // LRU_ROWS, gates, 0)

        def scan(t, carry, d=d):
            h, p = carry
            tt = t if d == 0 else TILE - 1 - t
            r0 = pl.multiple_of(tt * N_SEQ, N_SEQ)
            a = a_s[pl.ds(r0, N_SEQ), :]
            h = a * h + h_s[pl.ds(r0, N_SEQ), :]
            p = a * p
            h_s[pl.ds(r0, N_SEQ), :] = h
            p_s[pl.ds(r0, N_SEQ), :] = p
            return h, p

        h_end, p_end = lax.fori_loop(0, TILE, scan,
                                     (jnp.zeros((N_SEQ, LRU_W), F32), jnp.ones((N_SEQ, LRU_W), F32)))
        hend_ref[d] = h_end

        carry_slab = jnp.zeros((N_SEQ, LRU_W), F32)
        for b in range(LAT_B):
            c = h0_ref[0, b, d:d + 1, :]
            order = range(CHUNKS) if d == 0 else reversed(range(CHUNKS))
            for j in order:
                s = N_CTX + b * CHUNKS + j
                carry_slab = jnp.where(row == s, c, carry_slab)
                c = h_end[s:s + 1, :] + p_end[s:s + 1, :] * c

        def fix(t, _, d=d, carry_slab=carry_slab):
            r0 = pl.multiple_of(t * N_SEQ, N_SEQ)
            v = h_s[pl.ds(r0, N_SEQ), :] + p_s[pl.ds(r0, N_SEQ), :] * carry_slab
            if d == 0:
                out_ref[pl.ds(r0, N_SEQ), :] = v
            else:
                out_ref[pl.ds(r0, N_SEQ), :] += v
            return 0

        lax.fori_loop(0, TILE, fix, 0)


def _lru_call(layer, xc2, wa, ba, wi, bi, sp, h0):
    full = lambda shape: pl.BlockSpec(shape, lambda i: (0,) * len(shape))
    per_layer = lambda shape: pl.BlockSpec((1,) + shape, lambda i: (layer,) + (0,) * len(shape))
    return pl.pallas_call(
        _lru_kernel,
        grid=(1,),
        in_specs=[
            full((N_TOK, LRU_W)),
            per_layer((2, LRU_W, LRU_W)),
            per_layer((2, 1, LRU_W)),
            per_layer((2, LRU_W, LRU_W)),
            per_layer((2, 1, LRU_W)),
            per_layer((2, 1, LRU_W)),
            per_layer((LAT_B, 2, LRU_W)),
        ],
        out_specs=[full((N_TOK, LRU_W)), full((2, N_SEQ, LRU_W))],
        out_shape=[
            jax.ShapeDtypeStruct((N_TOK, LRU_W), F32),
            jax.ShapeDtypeStruct((2, N_SEQ, LRU_W), F32),
        ],
        scratch_shapes=[pltpu.VMEM((N_TOK, LRU_W), F32)] * 3,
        compiler_params=_cparams(("arbitrary",)),
        name="rglru",
    )(xc2, wa, ba, wi, bi, sp, h0)


def _layer_norm(z, g, b):
    mu = jnp.mean(z, axis=-1, keepdims=True)
    zc = z - mu
    var = jnp.mean(zc * zc, axis=-1, keepdims=True)
    return zc * lax.rsqrt(var + EPS) * g + b


def _route(logits):
    lane_i = lax.broadcasted_iota(I32, logits.shape, 1)
    lane = lane_i.astype(F32)
    big = jnp.float32(1024.0)
    neg = jnp.float32(-jnp.inf)
    is_g = lane_i < N_GROUPS
    gmax = jnp.max(jnp.where(is_g, logits, neg), axis=-1, keepdims=True)
    g_sel = jnp.min(jnp.where(jnp.logical_and(is_g, logits == gmax), lane, big), axis=-1, keepdims=True)
    p_group = 1.0 / jnp.sum(jnp.where(is_g, jnp.exp(logits - gmax), 0.0), axis=-1, keepdims=True)
    e_idx = lane_i - GATE_LANE0
    e_group = (e_idx >> 2).astype(F32)
    in_g = jnp.logical_and(jnp.logical_and(e_idx >= 0, e_idx < N_EXPERTS), e_group == g_sel)
    v1 = jnp.max(jnp.where(in_g, logits, neg), axis=-1, keepdims=True)
    i1 = jnp.min(jnp.where(jnp.logical_and(in_g, logits == v1), lane, big), axis=-1, keepdims=True)
    rest = jnp.logical_and(in_g, lane != i1)
    v2 = jnp.max(jnp.where(rest, logits, neg), axis=-1, keepdims=True)
    i2 = jnp.min(jnp.where(jnp.logical_and(rest, logits == v2), lane, big), axis=-1, keepdims=True)
    e2 = jnp.exp(v2 - v1)
    inv = 1.0 / (1.0 + e2)
    w1 = inv * p_group
    w2 = e2 * inv * p_group
    return jnp.where(lane == i1, w1, jnp.where(lane == i2, w2, jnp.where(lane_i == 0, g_sel, 0.0)))


def _outproj_kernel(x_ref, oc_ref, ol_ref, yf_ref, yb_ref, yfix_ref, hl_ref, g_ref, mod_ref,
                    wglu_ref, bglu_ref, wout_ref, lng_ref, lnb_ref, rw_ref, rwhi_ref, rb_ref,
                    x1_ref, hm_ref, gates_ref, y_s, hl_s):
    i = pl.program_id(0)
    s = pl.program_id(1)

    @pl.when(s == 0)
    def _():
        for j in range(GROUP_SEQS):
            rows = slice(j * TILE, (j + 1) * TILE)
            y_s[rows, :] = yf_ref[:, j, :] + yb_ref[:, j, :]
            hl_s[rows, :] = hl_ref[:, j, :]

    lat = i >= CTX_GROUP_TILES
    sub_rows = pl.ds(pl.multiple_of(s * SUB_ROWS, SUB_ROWS), SUB_ROWS)
    o_att = jnp.where(lat, ol_ref[...], oc_ref[...])
    y = y_s[sub_rows, :] + jnp.where(lat, yfix_ref[...], 0.0)
    g = jax.nn.gelu(y, approximate=True)
    glu = jnp.dot(g.astype(BF16), wglu_ref[0], preferred_element_type=F32) + bglu_ref[0]
    o_s5 = g * jax.nn.sigmoid(glu)
    o_lru = hl_s[sub_rows, :] * jax.nn.gelu(g_ref[...], approximate=True)
    mix = jnp.concatenate([o_att, o_s5.astype(BF16), o_lru.astype(BF16)], axis=-1)
    out = jnp.dot(mix, wout_ref[0], preferred_element_type=F32)
    r = _group_mod_row(i)
    g1 = mod_ref[0, pl.ds(r, 1), 2 * D:3 * D]
    sh2 = mod_ref[0, pl.ds(r, 1), 3 * D:4 * D]
    sc2 = mod_ref[0, pl.ds(r, 1), 4 * D:5 * D]
    x1 = _layer_norm(ALPHA * x_ref[...] + g1 * out, lng_ref[0], lnb_ref[0])
    x1_ref[...] = x1
    hm = x1 * (1.0 + sc2) + sh2
    hm_ref[...] = hm.astype(BF16)
    hm_hi = hm.astype(BF16)
    hm_lo = (hm - hm_hi.astype(F32)).astype(BF16)
    p_hi = jnp.dot(hm_hi, rw_ref[0], preferred_element_type=F32)
    p_lo = jnp.dot(hm_lo, rwhi_ref[0], preferred_element_type=F32)
    logits = p_hi[:, 0:128] + p_hi[:, 128:256] + p_lo + rb_ref[0]
    gates_ref[...] = _route(logits)


def _outproj_call(layer, x, o_ctx, o_lat, yf, yb, yfix, hl, g_lru, mod, w_glu, b_glu,
                  w_out, ln_g, ln_b, rw_split, rw_hi, rb):
    n_ctx_blocks = N_CTX * TILE // SUB_ROWS
    row_spec = lambda w: pl.BlockSpec((SUB_ROWS, w), lambda i, s: (i * SUBS + s, 0))
    lat_spec = lambda w: pl.BlockSpec((SUB_ROWS, w), lambda i, s: (jnp.maximum(i * SUBS + s - n_ctx_blocks, 0), 0))
    tm_spec = pl.BlockSpec((TILE, GROUP_SEQS, 256), lambda i, s: (0, i, 0))
    vec = lambda n: pl.BlockSpec((1, 1, n), lambda i, s: (layer, 0, 0))
    mat = lambda a, b: pl.BlockSpec((1, a, b), lambda i, s: (layer, 0, 0))
    return pl.pallas_call(
        _outproj_kernel,
        grid=(N_GROUP_TILES, SUBS),
        in_specs=[
            row_spec(D),
            pl.BlockSpec((SUB_ROWS, 512), lambda i, s: (jnp.minimum(i * SUBS + s, n_ctx_blocks - 1), 0)),
            lat_spec(512),
            tm_spec, tm_spec,
            lat_spec(S5_W),
            tm_spec,
            row_spec(LRU_W),
            mat(MOD_ROWS, N_MOD * D),
            mat(S5_W, S5_W), vec(S5_W),
            mat(D, D), vec(D), vec(D),
            mat(D, 256), mat(D, 128), vec(128),
        ],
        out_specs=[row_spec(D), row_spec(D), row_spec(128)],
        out_shape=[
            jax.ShapeDtypeStruct((N_TOK, D), F32),
            jax.ShapeDtypeStruct((N_TOK, D), BF16),
            jax.ShapeDtypeStruct((N_TOK, 128), F32),
        ],
        scratch_shapes=[pltpu.VMEM((GROUP_ROWS, 256), F32)] * 2,
        compiler_params=_cparams(("arbitrary", "arbitrary")),
        name="outproj",
    )(x, o_ctx, o_lat, yf.reshape(TILE, N_SEQ, S5_W), yb.reshape(TILE, N_SEQ, S5_W), yfix,
      hl.reshape(TILE, N_SEQ, LRU_W), g_lru, mod, w_glu,
      b_glu.reshape(DEPTH, 1, S5_W), w_out, ln_g.reshape(DEPTH, 1, D), ln_b.reshape(DEPTH, 1, D),
      rw_split, rw_hi, rb)


HALF = N_TOK // 2
HALF_TILES = HALF // TILE


def _sort_kernel(gates_ref, pos_ref, cnt_ref, rank_s):
    sub = lax.broadcasted_iota(I32, (8, TILE), 0)
    sel_r = lax.broadcasted_iota(I32, (8, 128), 0)
    sel_c = lax.broadcasted_iota(I32, (8, 128), 1)
    pick_lane0 = jnp.where(jnp.logical_and(sel_r == 0, sel_c == 0), 1.0, 0.0).astype(BF16)
    ri = lax.broadcasted_iota(I32, (TILE, TILE), 0)
    ci = lax.broadcasted_iota(I32, (TILE, TILE), 1)
    upper = jnp.where(ri <= ci, 1.0, 0.0).astype(BF16)

    def count(b, carry):
        rows = pl.ds(pl.multiple_of(b * TILE, TILE), TILE)
        g = gates_ref[rows, :].astype(BF16)
        g_t = lax.dot_general(pick_lane0, g, _NT, preferred_element_type=F32)
        g_sel = jnp.sum(g_t, axis=0, keepdims=True)
        onehot = jnp.where(jnp.logical_and(sub < N_GROUPS, sub.astype(F32) == g_sel), 1.0, 0.0)
        cum = jnp.dot(onehot.astype(BF16), upper, preferred_element_type=F32) + carry
        rank_s[b] = jnp.where(onehot > 0.0, cum, 0.0)
        return carry + jnp.sum(onehot, axis=1, keepdims=True)

    counts = lax.fori_loop(0, HALF_TILES, count, jnp.zeros((8, 1), F32))
    sub1 = lax.broadcasted_iota(I32, (8, 1), 0)
    c = [jnp.sum(jnp.where(sub1 == g, counts, 0.0), axis=0, keepdims=True) for g in range(N_GROUPS - 1)]
    start = jnp.where(sub1 == 1, c[0], jnp.where(sub1 == 2, c[0] + c[1],
                      jnp.where(sub1 == 3, c[0] + c[1] + c[2], 0.0)))

    def place(b, _):
        rk = rank_s[b]
        p = jnp.sum(jnp.where(rk > 0.0, rk + start - 1.0, 0.0), axis=0, keepdims=True)
        pos_ref[pl.ds(b, 1), :] = p.astype(I32)
        return 0

    lax.fori_loop(0, HALF_TILES, place, 0)
    cnt_ref[0] = jnp.broadcast_to(counts, (8, 128))


def _sort_call(gates):
    return pl.pallas_call(
        _sort_kernel,
        grid=(2,),
        in_specs=[pl.BlockSpec((HALF, 128), lambda h: (h, 0))],
        out_specs=[pl.BlockSpec((HALF_TILES, TILE), lambda h: (h, 0)),
                   pl.BlockSpec((1, 8, 128), lambda h: (h, 0, 0))],
        out_shape=[jax.ShapeDtypeStruct((N_SEQ, TILE), I32), jax.ShapeDtypeStruct((2, 8, 128), F32)],
        scratch_shapes=[pltpu.VMEM((HALF_TILES, 8, TILE), F32)],
        compiler_params=_cparams(("arbitrary",)),
        name="group_sort",
    )(gates)


ITEM_TILES = 2
N_ITEMS = N_EXPERTS * (HALF_TILES + 2 * N_GROUPS) // (EPG * ITEM_TILES)
MOE_STEPS = HALF_TILES + N_ITEMS + HALF_TILES


def _moe_schedule(counts):
    start = jnp.cumsum(counts, axis=1) - counts
    lo = start // TILE
    hi = (start + counts + TILE - 1) // TILE
    tiles_g = jnp.where(counts > 0, hi - lo, 0)
    n_g = (tiles_g + ITEM_TILES - 1) // ITEM_TILES
    n_e = jnp.repeat(n_g, EPG, axis=1)
    lo_e = jnp.repeat(lo, EPG, axis=1)
    tiles_e = jnp.repeat(tiles_g, EPG, axis=1)
    off_end = jnp.cumsum(n_e, axis=1)
    off = off_end - n_e
    total = off_end[:, -1:]
    w = jnp.arange(N_ITEMS, dtype=I32)[None, :]
    w_eff = jnp.minimum(w, total - 1)
    e_w = jnp.sum((w_eff[:, :, None] >= off_end[:, None, :]).astype(I32), axis=-1)
    first = ITEM_TILES * (w_eff - jnp.take_along_axis(off, e_w, axis=1))
    tile = jnp.take_along_axis(lo_e, e_w, axis=1) + first
    size = jnp.minimum(jnp.take_along_axis(tiles_e, e_w, axis=1) - first, ITEM_TILES)
    size = jnp.where(w < total, size, 0)
    e_f, n_f = e_w.reshape(-1), size.reshape(-1)
    n_all = 2 * N_ITEMS
    idx = jnp.arange(n_all, dtype=I32)
    key = (idx // N_ITEMS) * N_EXPERTS + e_f
    run_start = jnp.logical_and(n_f > 0, key != jnp.concatenate([jnp.full((1,), -1, I32), key[:-1]]))
    slot = (jnp.cumsum(run_start.astype(I32)) - 1) % 2
    at_or_after = lax.cummin(jnp.where(run_start, idx, n_all), reverse=True)
    after = jnp.concatenate([at_or_after[1:], jnp.full((1,), n_all, I32)])
    next_e = jnp.where(after < n_all, e_f[jnp.minimum(after, n_all - 1)], -1)
    return e_f, tile.reshape(-1), n_f, run_start.astype(I32), slot, next_e


def _moe_kernel(pos_ref, ite_ref, itt_ref, itv_ref, itf_ref, its_ref, itx_ref,
                hm_ref, gates_ref, wg_hbm, wu_hbm, wd_hbm, x1_ref, mod_ref, lng_ref, lnb_ref,
                *rest, split, layer):
    out_refs, (xs_s, gs_s, acc_s, hm_s, wg_b, wu_b, wd_b, w_sem) = rest[:-8], rest[-8:]
    h = pl.program_id(0)
    w = pl.program_id(1)
    base = h * HALF

    def weight_copies(e, slot):
        pairs = ((wg_hbm, wg_b), (wu_hbm, wu_b), (wd_hbm, wd_b))
        return [pltpu.make_async_copy(src.at[layer, e], dst.at[slot], w_sem.at[slot, k])
                for k, (src, dst) in enumerate(pairs)]

    @pl.when(w < HALF_TILES)
    def _():
        @pl.when(w == 0)
        def _():
            acc_s[...] = jnp.zeros_like(acc_s)

        @pl.when(jnp.logical_and(h == 0, w == 0))
        def _():
            for c in weight_copies(ite_ref[0], 0):
                c.start()

        hm_s[...] = hm_ref[...].astype(F32)

        def body(r, c):
            p = pos_ref[base + w * TILE + r]
            xs_s[pl.ds(p, 1), :] = hm_s[pl.ds(r, 1), :]
            gs_s[pl.ds(p, 1), :] = gates_ref[pl.ds(r, 1), :]
            return c

        lax.fori_loop(0, TILE, body, 0, unroll=8)

    @pl.when(jnp.logical_and(w >= HALF_TILES, w < HALF_TILES + N_ITEMS))
    def _():
        idx = h * N_ITEMS + (w - HALF_TILES)

        slot = its_ref[idx]

        @pl.when(itf_ref[idx] > 0)
        def _():
            @pl.when(itx_ref[idx] >= 0)
            def _():
                for c in weight_copies(itx_ref[idx], 1 - slot):
                    c.start()

            for c in weight_copies(ite_ref[idx], slot):
                c.wait()

        def item(n_tiles):
            e = ite_ref[idx]
            rows = pl.ds(pl.multiple_of(itt_ref[idx] * TILE, TILE), n_tiles * TILE)
            x = xs_s[rows, :].astype(BF16)
            a = jnp.dot(x, wg_b[slot].astype(BF16), preferred_element_type=F32)
            u = jnp.dot(x, wu_b[slot].astype(BF16), preferred_element_type=F32)
            g = gs_s[rows, :]
            lane = lax.broadcasted_iota(I32, g.shape, 1)
            ge = jnp.sum(jnp.where(lane == e + GATE_LANE0, g, 0.0), axis=-1, keepdims=True)
            act = ((a * jax.nn.sigmoid(a)) * u * ge).astype(BF16)
            acc_s[rows, :] += jnp.dot(act, wd_b[slot].astype(BF16), preferred_element_type=F32)

        for n_tiles in range(1, ITEM_TILES + 1):
            pl.when(itv_ref[idx] == n_tiles)(functools.partial(item, n_tiles))

    @pl.when(w >= HALF_TILES + N_ITEMS)
    def _():
        j = w - HALF_TILES - N_ITEMS

        def finish(o_ref):
            def body(r, c):
                p = pos_ref[base + j * TILE + r]
                o_ref[pl.ds(r, 1), :] = acc_s[pl.ds(p, 1), :]
                return c

            lax.fori_loop(0, TILE, body, 0, unroll=8)
            r = jnp.where(h == 0, 0, 1 + j // CHUNKS)
            g2 = mod_ref[0, pl.ds(r, 1), 5 * D:6 * D]
            o_ref[...] = _layer_norm(ALPHA * x1_ref[...] + g2 * o_ref[...], lng_ref[0], lnb_ref[0])

        if split:
            for half, o_ref in enumerate(out_refs):
                pl.when(h == half)(functools.partial(finish, o_ref))
        else:
            finish(out_refs[0])


def _moe_call(layer, pos, items, hm, gates, w_gate, w_up, w_down, x1, mod, ln_g, ln_b, split):
    def in_tile(h, w, *_):
        return (h * HALF_TILES + jnp.minimum(w, HALF_TILES - 1), 0)

    def out_step(w):
        return jnp.clip(w - HALF_TILES - N_ITEMS, 0, HALF_TILES - 1)

    def out_tile(h, w, *_):
        return (h * HALF_TILES + out_step(w), 0)

    if split:
        out_specs = [pl.BlockSpec((TILE, D), lambda h, w, *_: (jnp.where(h == 0, out_step(w), HALF_TILES - 1), 0)),
                     pl.BlockSpec((TILE, D), lambda h, w, *_: (jnp.where(h == 1, out_step(w), 0), 0))]
        out_shape = [jax.ShapeDtypeStruct((HALF, D), F32)] * 2
    else:
        out_specs = pl.BlockSpec((TILE, D), out_tile)
        out_shape = jax.ShapeDtypeStruct((N_TOK, D), F32)

    vec = lambda n: pl.BlockSpec((1, 1, n), lambda h, w, *_: (layer, 0, 0))
    grid_spec = pltpu.PrefetchScalarGridSpec(
        num_scalar_prefetch=1 + len(items),
        grid=(2, MOE_STEPS),
        in_specs=[
            pl.BlockSpec((TILE, D), in_tile),
            pl.BlockSpec((TILE, 128), in_tile),
            pl.BlockSpec(memory_space=pl.ANY),
            pl.BlockSpec(memory_space=pl.ANY),
            pl.BlockSpec(memory_space=pl.ANY),
            pl.BlockSpec((TILE, D), out_tile),
            pl.BlockSpec((1, MOD_ROWS, N_MOD * D), lambda h, w, *_: (layer, 0, 0)),
            vec(D), vec(D),
        ],
        out_specs=out_specs,
        scratch_shapes=[
            pltpu.VMEM((HALF, D), F32),
            pltpu.VMEM((HALF, 128), F32),
            pltpu.VMEM((HALF, D), F32),
            pltpu.VMEM((TILE, D), F32),
            pltpu.VMEM((2, D, D_EXPERT), F32),
            pltpu.VMEM((2, D, D_EXPERT), F32),
            pltpu.VMEM((2, D_EXPERT, D), F32),
            pltpu.SemaphoreType.DMA((2, 3)),
        ],
    )
    return pl.pallas_call(
        functools.partial(_moe_kernel, split=split, layer=layer),
        grid_spec=grid_spec,
        out_shape=out_shape,
        compiler_params=pltpu.CompilerParams(dimension_semantics=("arbitrary", "arbitrary"),
                                             vmem_limit_bytes=MOE_VMEM_LIMIT),
        name="moe",
    )(pos, *items, hm, gates, w_gate, w_up, w_down, x1, mod,
      ln_g.reshape(DEPTH, 1, D), ln_b.reshape(DEPTH, 1, D))


def _rope_tables():
    rows = LAT_T // GRID_W
    row = jnp.repeat(jnp.arange(rows, dtype=F32), GRID_W)
    col = jnp.tile(jnp.arange(GRID_W, dtype=F32), rows)
    n_freq = QK // 4
    inv = ROPE_BASE ** (-jnp.arange(n_freq, dtype=F32) / n_freq)
    ang_r = row[:, None] * inv
    ang_c = col[:, None] * inv
    cos64 = jnp.concatenate([jnp.cos(ang_r), jnp.cos(ang_r), jnp.cos(ang_c), jnp.cos(ang_c)], axis=1)
    sin64 = jnp.concatenate([-jnp.sin(ang_r), jnp.sin(ang_r), -jnp.sin(ang_c), jnp.sin(ang_c)], axis=1)
    cos = jnp.concatenate([jnp.tile(cos64, (1, 2)), jnp.ones((SUB_ROWS, 128), F32)], axis=0)
    sin = jnp.concatenate([jnp.tile(sin64, (1, 2)), jnp.zeros((SUB_ROWS, 128), F32)], axis=0)
    return cos, sin


def _s5_params(a_re, a_im, b_re, b_im, c_re, c_im, log_dt):
    dt = jnp.exp(log_dt)[..., None]
    mag = jnp.exp(a_re * dt)
    abar_r = mag * jnp.cos(a_im * dt)
    abar_i = mag * jnp.sin(a_im * dt)
    den = a_re * a_re + a_im * a_im
    nr = abar_r - 1.0
    coef_r = (nr * a_re + abar_i * a_im) / den
    coef_i = (abar_i * a_re - nr * a_im) / den
    bbar_r = coef_r[..., None] * b_re - coef_i[..., None] * b_im
    bbar_i = coef_r[..., None] * b_im + coef_i[..., None] * b_re
    def b_rows(bb):
        return bb.transpose(0, 1, 2, 4, 3).reshape(DEPTH, 2, S5_W, S5_P)
    br, bi = b_rows(bbar_r), b_rows(bbar_i)
    b_cmp = jnp.concatenate([br, br, bi, bi], axis=-1)
    def c_rows(cc):
        return cc.transpose(0, 1, 4, 2, 3).reshape(DEPTH, 2, S5_P, S5_W)
    c_cmp = jnp.concatenate([c_rows(c_re), c_rows(-c_im)], axis=-2)
    a_bar = jnp.concatenate([abar_r.reshape(DEPTH, 2, S5_N), abar_i.reshape(DEPTH, 2, S5_N)], axis=-1)
    return b_cmp, c_cmp, a_bar


def _block_diag(w):
    eye = jnp.eye(LRU_BLOCKS, dtype=F32)
    m = jnp.einsum('ldkij,kh->ldkihj', w, eye)
    return m.reshape(DEPTH, 2, LRU_W, LRU_W).astype(BF16)


def kernel(x_prompt, x_sample, cache_k, cache_v, state_s5, state_lru, c, c_ctx, w_ada, b_ada, w_in, w_out, lam_q1, lam_k1, lam_q2, lam_k2, subln_g, s5_a_re, s5_a_im, s5_b_re, s5_b_im, s5_c_re, s5_c_im, s5_log_dt, s5_d, s5_w_glu, s5_b_glu, lru_conv_w, lru_conv_b, lru_w_a, lru_b_a, lru_w_i, lru_b_i, lru_lambda, ln1_g, ln1_b, ln2_g, ln2_b, router_group_w, router_group_b, router_expert_w, router_expert_b, moe_w_gate, moe_w_up, moe_w_down):
    x = jnp.concatenate([x_prompt.reshape(CTX_B * CTX_T, D), x_sample.reshape(LAT_B * LAT_T, D)], axis=0)
    cond = jnp.concatenate([c_ctx[None, :], c, jnp.zeros((MOD_ROWS - 1 - LAT_B, D), F32)], axis=0)
    mod = _ada_call(cond, w_ada, b_ada)

    w_out_b = w_out.astype(BF16)
    w_glu_b = s5_w_glu.astype(BF16)
    rope_cos, rope_sin = _rope_tables()
    b_cmp, c_cmp, a_bar = _s5_params(s5_a_re, s5_a_im, s5_b_re, s5_b_im, s5_c_re, s5_c_im, s5_log_dt)
    wa_blk = _block_diag(lru_w_a)
    wi_blk = _block_diag(lru_w_i)
    ba = lru_b_a.reshape(DEPTH, 2, 1, LRU_W)
    bi = lru_b_i.reshape(DEPTH, 2, 1, LRU_W)
    sp = jax.nn.softplus(-lru_lambda).reshape(DEPTH, 2, 1, LRU_W)
    lam = (jnp.exp(jnp.sum(lam_q1 * lam_k1, axis=-1)) - jnp.exp(jnp.sum(lam_q2 * lam_k2, axis=-1)))
    rw = jnp.concatenate([router_group_w, router_expert_w,
                          jnp.zeros((DEPTH, D, 128 - N_GROUPS - N_EXPERTS), F32)], axis=-1)
    rw_hi = rw.astype(BF16)
    rw_split = jnp.concatenate([rw_hi, (rw - rw_hi.astype(F32)).astype(BF16)], axis=-1)
    rb = jnp.concatenate([router_group_b, router_expert_b,
                          jnp.zeros((DEPTH, 128 - N_GROUPS - N_EXPERTS), F32)], axis=-1).reshape(DEPTH, 1, 128)
    s5_h0 = state_s5.reshape(LAT_B, DEPTH, 2, 2 * S5_N)
    lru_h0 = state_lru.transpose(1, 0, 2, 3)

    kc = jnp.zeros((CTX_B, DEPTH, HEADS, CTX_T, 2 * QK), F32)
    vc = jnp.zeros((CTX_B, DEPTH, HEADS, CTX_T, DV), F32)
    s5_states = []
    lru_states = []
    for l in range(DEPTH):
        lambda_init = 0.8 - 0.6 * math.exp(-0.3 * l)
        lam_l = (lam[l] + lambda_init).reshape(1)
        q, k, v, kc, vc, u_tm, xc_tm, g_lru = _inproj_call(
            l, x, mod, w_in, rope_cos, rope_sin, lru_conv_w, lru_conv_b, kc, vc)
        o_ctx = _att_ctx_call(l, lam_l, q, k, v, subln_g, 1.0 - lambda_init)
        o_lat = _att_lat_call(l, lam_l, q, k, v, cache_k, cache_v, subln_g, 1.0 - lambda_init)
        yf, yb, hend = _s5_call(l, u_tm.reshape(N_TOK, S5_W), b_cmp, c_cmp, a_bar, s5_d)
        yfix = _s5_fix_call(l, hend, s5_h0[:, l], a_bar, c_cmp)
        hl, lru_end = _lru_call(l, xc_tm.reshape(N_TOK, LRU_W), wa_blk, ba, wi_blk, bi, sp, lru_h0)
        x1, hm, gates = _outproj_call(l, x, o_ctx, o_lat, yf, yb, yfix, hl, g_lru, mod,
                                      w_glu_b, s5_b_glu, w_out_b, ln1_g, ln1_b, rw_split, rw_hi, rb)
        pos, counts = _sort_call(gates)
        items = _moe_schedule(counts[:, :N_GROUPS, 0].astype(I32))
        x = _moe_call(l, pos.reshape(N_TOK), items, hm, gates, moe_w_gate, moe_w_up,
                      moe_w_down, x1, mod, ln2_g, ln2_b, split=(l == DEPTH - 1))
        s5_states.append(hend[:, :N_CTX].reshape(2, CTX_B, 2, S5_G, S5_P).transpose(1, 0, 2, 3, 4))
        lru_states.append(lru_end[:, :N_CTX].transpose(1, 0, 2))

    y_p = x[0].reshape(CTX_B, CTX_T, D)
    y_s = x[1].reshape(LAT_B, LAT_T, D)
    return (y_p, y_s, kc, vc, jnp.stack(s5_states, axis=1), jnp.stack(lru_states, axis=1))
```

```python
import functools
import math

import jax
import jax.numpy as jnp
from jax import lax
from jax.experimental import pallas as pl
from jax.experimental.pallas import tpu as pltpu

F32 = jnp.float32
BF16 = jnp.bfloat16
I32 = jnp.int32

D = 1024
DEPTH = 4
CTX_B = 16
CTX_T = 256
LAT_B = 2
LAT_T = 2048
PAST = 256
GRID_W = 64
HEADS = 4
QK = 64
DV = 128
S5_W = 256
S5_G = 16
S5_C = 16
S5_P = 64
S5_N = S5_G * S5_P
S5_C_SHIFT = S5_C.bit_length() - 1
S5_P_SHIFT = S5_P.bit_length() - 1
LRU_W = 256
LRU_BLOCKS = 4
LRU_C = 8.0
N_GROUPS = 4
EPG = 4
N_EXPERTS = 16
D_EXPERT = 512
N_MOD = 6
IN_W = 2304
ROPE_BASE = 10000.0
ALPHA = (2 * DEPTH) ** 0.25
EPS = 1e-5

TILE = 256
N_TOK = CTX_B * CTX_T + LAT_B * LAT_T
N_SEQ = N_TOK // TILE
N_CTX = CTX_B
CHUNKS = LAT_T // TILE
N_LAT = LAT_B * CHUNKS
MOD_ROWS = 8
GATE_LANE0 = 4
VMEM_LIMIT = 56 * 1024 * 1024
MOE_VMEM_LIMIT = 60 * 1024 * 1024

GROUP_SEQS = 8
GROUP_ROWS = GROUP_SEQS * TILE
N_GROUP_TILES = N_TOK // GROUP_ROWS
CTX_GROUP_TILES = N_CTX // GROUP_SEQS
SUB_ROWS = 512
SUBS = GROUP_ROWS // SUB_ROWS
SUB_SEQS = SUB_ROWS // TILE


def _cparams(sem):
    return pltpu.CompilerParams(dimension_semantics=sem, vmem_limit_bytes=VMEM_LIMIT)


def _group_mod_row(i):
    return jnp.where(i < CTX_GROUP_TILES, 0, i - CTX_GROUP_TILES + 1)


def _ada_kernel(c_ref, w_ref, b_ref, o_ref):
    c = c_ref[...]
    s = (c * jax.nn.sigmoid(c)).astype(BF16)
    o_ref[0] = jnp.dot(s, w_ref[0].astype(BF16), preferred_element_type=F32) + b_ref[0]


def _ada_call(cond, w_ada, b_ada):
    tn = 1536
    return pl.pallas_call(
        _ada_kernel,
        grid=(DEPTH, N_MOD * D // tn),
        in_specs=[
            pl.BlockSpec((MOD_ROWS, D), lambda l, j: (0, 0)),
            pl.BlockSpec((1, D, tn), lambda l, j: (l, 0, j)),
            pl.BlockSpec((1, 1, tn), lambda l, j: (l, 0, j)),
        ],
        out_specs=pl.BlockSpec((1, MOD_ROWS, tn), lambda l, j: (l, 0, j)),
        out_shape=jax.ShapeDtypeStruct((DEPTH, MOD_ROWS, N_MOD * D), F32),
        compiler_params=_cparams(("arbitrary", "arbitrary")),
        name="adaln",
    )(cond, w_ada, b_ada.reshape(DEPTH, 1, N_MOD * D))


def _inproj_kernel(x_ref, mod_ref, w_ref, cos_ref, sin_ref, cw_ref, cb_ref, kc_in, vc_in,
                   q_ref, k_ref, v_ref, kc_ref, vc_ref, u_ref, xc_ref, g_ref, u_s, xl_s, w_s):
    del kc_in, vc_in
    i = pl.program_id(0)
    s = pl.program_id(1)

    @pl.when(jnp.logical_and(i == 0, s == 0))
    def _():
        for t in range(2 * HEADS):
            kind, head = divmod(t, HEADS)
            halves = []
            for m in range(2):
                col = (2 * kind + m) * HEADS * QK + head * QK
                blk = w_ref[0, :, (col // 128) * 128:(col // 128 + 1) * 128]
                halves.append(blk[:, col % 128:col % 128 + QK])
            w_s[:, t * 128:(t + 1) * 128] = jnp.concatenate(halves, axis=1).astype(BF16)
        w_s[:, 1024:IN_W] = w_ref[0, :, 1024:IN_W].astype(BF16)

    r = _group_mod_row(i)
    sh = mod_ref[0, pl.ds(r, 1), 0:D]
    sc = mod_ref[0, pl.ds(r, 1), D:2 * D]
    xm = (x_ref[...] * (1.0 + sc) + sh).astype(BF16)
    proj = jnp.dot(xm, w_s[...], preferred_element_type=F32)

    qk = proj[:, 0:1024]
    cos = jnp.concatenate([cos_ref[...]] * 8, axis=1)
    sin = jnp.concatenate([sin_ref[...]] * 8, axis=1)
    lane = lax.broadcasted_iota(I32, qk.shape, 1)
    swapped = jnp.where((lane & 31) < 16, pltpu.roll(qk, 1024 - 16, 1), pltpu.roll(qk, 16, 1))
    qk = qk * cos + swapped * sin
    q_ref[...] = qk[:, 0:512].astype(BF16)
    k_ref[...] = qk[:, 512:1024].astype(BF16)
    v = proj[:, 1024:1536]
    v_ref[...] = v.astype(BF16)

    @pl.when(i < CTX_GROUP_TILES)
    def _():
        for jj in range(SUB_SEQS):
            rows = slice(jj * TILE, (jj + 1) * TILE)
            for h in range(HEADS):
                kc_ref[jj, 0, h] = qk[rows, 512 + h * 128:512 + (h + 1) * 128]
                vc_ref[jj, 0, h] = v[rows, h * 128:(h + 1) * 128]

    g_ref[...] = proj[:, 2048:2304]
    sub_rows = pl.ds(pl.multiple_of(s * SUB_ROWS, SUB_ROWS), SUB_ROWS)
    u_s[sub_rows, :] = proj[:, 1536:1792]
    xl_s[sub_rows, :] = proj[:, 1792:2048]

    @pl.when(s == SUBS - 1)
    def _():
        xl = xl_s[...]
        row = lax.broadcasted_iota(I32, xl.shape, 0)
        is_ctx = i < CTX_GROUP_TILES
        pos = jnp.where(is_ctx, row & (TILE - 1), row)
        last = jnp.where(is_ctx, TILE - 1, GROUP_ROWS - 1)
        x_m1 = jnp.where(pos == 0, 0.0, pltpu.roll(xl, 1, 0))
        x_p1 = jnp.where(pos == last, 0.0, pltpu.roll(xl, GROUP_ROWS - 1, 0))
        x_p2 = jnp.where(pos >= last - 1, 0.0, pltpu.roll(xl, GROUP_ROWS - 2, 0))
        cw = cw_ref[0]
        xc = cb_ref[0] + x_m1 * cw[0:1] + xl * cw[1:2] + x_p1 * cw[2:3] + x_p2 * cw[3:4]
        for j in range(GROUP_SEQS):
            xc_ref[:, j, :] = xc[j * TILE:(j + 1) * TILE]
            u_ref[:, j, :] = u_s[j * TILE:(j + 1) * TILE, :]


def _inproj_call(layer, x, mod, w_in, rope_cos, rope_sin, conv_w, conv_b, kc, vc):
    n_ctx_blocks = N_CTX // SUB_SEQS

    def rope_idx(i, s):
        return (jnp.where(i < CTX_GROUP_TILES, SUBS, s), 0)

    def cache_idx(i, s):
        return (jnp.minimum(i * SUBS + s, n_ctx_blocks - 1), layer, 0, 0, 0)

    row_spec = lambda w: pl.BlockSpec((SUB_ROWS, w), lambda i, s: (i * SUBS + s, 0))
    tm_spec = pl.BlockSpec((TILE, GROUP_SEQS, 256), lambda i, s: (0, i, 0))
    cache_spec = pl.BlockSpec((SUB_SEQS, 1, HEADS, TILE, 128), cache_idx)
    return pl.pallas_call(
        _inproj_kernel,
        grid=(N_GROUP_TILES, SUBS),
        in_specs=[
            row_spec(D),
            pl.BlockSpec((1, MOD_ROWS, N_MOD * D), lambda i, s: (layer, 0, 0)),
            pl.BlockSpec((1, D, IN_W), lambda i, s: (layer, 0, 0)),
            pl.BlockSpec((SUB_ROWS, 128), rope_idx),
            pl.BlockSpec((SUB_ROWS, 128), rope_idx),
            pl.BlockSpec((1, 4, LRU_W), lambda i, s: (layer, 0, 0)),
            pl.BlockSpec((1, 1, LRU_W), lambda i, s: (layer, 0, 0)),
            pl.BlockSpec(memory_space=pl.ANY),
            pl.BlockSpec(memory_space=pl.ANY),
        ],
        out_specs=[
            row_spec(512), row_spec(512), row_spec(512),
            cache_spec, cache_spec,
            tm_spec, tm_spec,
            row_spec(LRU_W),
        ],
        out_shape=[
            jax.ShapeDtypeStruct((N_TOK, 512), BF16),
            jax.ShapeDtypeStruct((N_TOK, 512), BF16),
            jax.ShapeDtypeStruct((N_TOK, 512), BF16),
            jax.ShapeDtypeStruct(kc.shape, F32),
            jax.ShapeDtypeStruct(vc.shape, F32),
            jax.ShapeDtypeStruct((TILE, N_SEQ, S5_W), F32),
            jax.ShapeDtypeStruct((TILE, N_SEQ, LRU_W), F32),
            jax.ShapeDtypeStruct((N_TOK, LRU_W), F32),
        ],
        scratch_shapes=[pltpu.VMEM((GROUP_ROWS, S5_W), F32), pltpu.VMEM((GROUP_ROWS, LRU_W), F32),
                        pltpu.VMEM((D, IN_W), BF16)],
        input_output_aliases={7: 3, 8: 4},
        compiler_params=_cparams(("arbitrary", "arbitrary")),
        name="inproj",
    )(x, mod, w_in, rope_cos, rope_sin, conv_w, conv_b.reshape(DEPTH, 1, LRU_W), kc, vc)


_NT = (((1,), (1,)), ((), ()))


def _split_maps(q):
    lane = lax.broadcasted_iota(I32, q.shape, 1)
    zero = jnp.zeros_like(q)
    qs = q * (QK ** -0.5)
    return jnp.where(lane < QK, qs, zero), jnp.where(lane >= QK, qs, zero)


def _softmax_values(qm, keys, values):
    scores = [lax.dot_general(qm, k, _NT, preferred_element_type=F32) for k in keys]
    m = scores[0].max(axis=-1, keepdims=True)
    for sc in scores[1:]:
        m = jnp.maximum(m, sc.max(axis=-1, keepdims=True))
    z = None
    o = None
    for sc, v in zip(scores, values):
        e = jnp.exp(sc - m)
        ez = e.sum(axis=-1, keepdims=True)
        eo = jnp.dot(e.astype(BF16), v, preferred_element_type=F32)
        z = ez if z is None else z + ez
        o = eo if o is None else o + eo
    return o * (1.0 / z)


def _diff_attention(q, keys, values, lam, g, post_scale):
    q1, q2 = _split_maps(q)
    o = _softmax_values(q1, keys, values) - lam * _softmax_values(q2, keys, values)
    ms = jnp.mean(o * o, axis=-1, keepdims=True)
    return (o * lax.rsqrt(ms + EPS) * g) * post_scale


def _att_ctx_kernel(lam_ref, q_ref, k_ref, v_ref, g_ref, o_ref, *, post_scale):
    lam = lam_ref[0]
    for h in range(HEADS):
        cols = slice(h * 128, (h + 1) * 128)
        o = _diff_attention(q_ref[:, cols], [k_ref[:, cols]], [v_ref[:, cols]], lam, g_ref[0], post_scale)
        o_ref[:, cols] = o.astype(BF16)


def _att_ctx_call(layer, lam, q, k, v, subln_g, post_scale):
    blk = pl.BlockSpec((TILE, HEADS * 128), lambda b: (b, 0))
    return pl.pallas_call(
        functools.partial(_att_ctx_kernel, post_scale=post_scale),
        grid=(CTX_B,),
        in_specs=[
            pl.BlockSpec(memory_space=pltpu.SMEM),
            blk, blk, blk,
            pl.BlockSpec((1, 1, DV), lambda b: (layer, 0, 0)),
        ],
        out_specs=blk,
        out_shape=jax.ShapeDtypeStruct((CTX_B * CTX_T, HEADS * DV), BF16),
        compiler_params=_cparams(("arbitrary",)),
        name="att_ctx",
    )(lam, q, k, v, subln_g.reshape(DEPTH, 1, DV))


def _att_lat_kernel(lam_ref, q_ref, k_ref, v_ref, ck_ref, cv_ref, g_ref, o_ref, *, post_scale):
    o = _diff_attention(q_ref[...], [k_ref[...], ck_ref[...].astype(BF16)],
                        [v_ref[...], cv_ref[...].astype(BF16)], lam_ref[0], g_ref[0], post_scale)
    o_ref[...] = o.astype(BF16)


def _att_lat_call(layer, lam, q, k, v, cache_k, cache_v, subln_g, post_scale):
    lat0 = N_CTX
    cache_spec = pl.BlockSpec((None, None, None, PAST, 128), lambda b, h, t: (b, layer, h, 0, 0))
    return pl.pallas_call(
        functools.partial(_att_lat_kernel, post_scale=post_scale),
        grid=(LAT_B, HEADS, CHUNKS),
        in_specs=[
            pl.BlockSpec(memory_space=pltpu.SMEM),
            pl.BlockSpec((TILE, 128), lambda b, h, t: (lat0 + b * CHUNKS + t, h)),
            pl.BlockSpec((LAT_T, 128), lambda b, h, t: (lat0 // CHUNKS + b, h)),
            pl.BlockSpec((LAT_T, 128), lambda b, h, t: (lat0 // CHUNKS + b, h)),
            cache_spec, cache_spec,
            pl.BlockSpec((1, 1, DV), lambda b, h, t: (layer, 0, 0)),
        ],
        out_specs=pl.BlockSpec((TILE, 128), lambda b, h, t: (b * CHUNKS + t, h)),
        out_shape=jax.ShapeDtypeStruct((LAT_B * LAT_T, HEADS * DV), BF16),
        compiler_params=_cparams(("arbitrary", "arbitrary", "arbitrary")),
        name="att_lat",
    )(lam, q, k, v, cache_k, cache_v, subln_g.reshape(DEPTH, 1, DV))


S5_TB = 32
S5_LANES = 128


def _cmul(ar, ai, br, bi):
    return ar * br - ai * bi, ar * bi + ai * br


def _expand_b(b):
    full = jnp.concatenate([b[:, 0:128]] * (S5_N // 128) + [b[:, 128:256]] * (S5_N // 128), axis=1)
    row = lax.broadcasted_iota(I32, full.shape, 0)
    col = lax.broadcasted_iota(I32, full.shape, 1)
    same_group = (row >> S5_C_SHIFT) == ((col & (S5_N - 1)) >> S5_P_SHIFT)
    return jnp.where(same_group, full, 0.0).astype(BF16)


def _expand_c(c):
    full = jnp.concatenate([c[0:S5_P]] * S5_G + [c[S5_P:2 * S5_P]] * S5_G, axis=0)
    row = lax.broadcasted_iota(I32, full.shape, 0)
    col = lax.broadcasted_iota(I32, full.shape, 1)
    same_group = ((row & (S5_N - 1)) >> S5_P_SHIFT) == (col >> S5_C_SHIFT)
    return jnp.where(same_group, full, 0.0).astype(BF16)


def _s5_kernel(uf_ref, ub_ref, b_ref, c_ref, a_ref, d_ref, yf_ref, yb_ref, hend_ref,
               hf_s, hb_s, buf_f, buf_b, bblk_s, cblk_s):
    i = pl.program_id(0)

    @pl.when(i == 0)
    def _():
        hf_s[...] = jnp.zeros_like(hf_s)
        hb_s[...] = jnp.zeros_like(hb_s)
        for d in range(2):
            bblk_s[d] = _expand_b(b_ref[0, d])
            cblk_s[d] = _expand_c(c_ref[0, d])

    buf_f[...] = jnp.dot(uf_ref[...].astype(BF16), bblk_s[0], preferred_element_type=F32)
    buf_b[...] = jnp.dot(ub_ref[...].astype(BF16), bblk_s[1], preferred_element_type=F32)

    def scan(buf, h_s, d):
        for c in range(S5_N // S5_LANES):
            re = slice(c * S5_LANES, (c + 1) * S5_LANES)
            im = slice(S5_N + c * S5_LANES, S5_N + (c + 1) * S5_LANES)
            ar = a_ref[0, d:d + 1, re]
            ai = a_ref[0, d:d + 1, im]
            hr, hi = h_s[:, re], h_s[:, im]
            for t in range(S5_TB):
                tt = t if d == 0 else S5_TB - 1 - t
                rows = slice(tt * N_SEQ, (tt + 1) * N_SEQ)
                pr, pi = _cmul(ar, ai, hr, hi)
                hr = pr + buf[rows, re]
                hi = pi + buf[rows, im]
                buf[rows, re] = hr
                buf[rows, im] = hi
            h_s[:, re] = hr
            h_s[:, im] = hi

    scan(buf_f, hf_s, 0)
    yf_ref[...] = (jnp.dot(buf_f[...].astype(BF16), cblk_s[0], preferred_element_type=F32)
                   + d_ref[0] * uf_ref[...])
    scan(buf_b, hb_s, 1)
    yb_ref[...] = jnp.dot(buf_b[...].astype(BF16), cblk_s[1], preferred_element_type=F32)

    @pl.when(i == pl.num_programs(0) - 1)
    def _():
        hend_ref[0] = hf_s[...]
        hend_ref[1] = hb_s[...]


def _s5_call(layer, u2, b_cmp, c_cmp, a_bar, s5_d):
    nb = CTX_T // S5_TB
    rows = S5_TB * N_SEQ
    return pl.pallas_call(
        _s5_kernel,
        grid=(nb,),
        in_specs=[
            pl.BlockSpec((rows, S5_W), lambda i: (i, 0)),
            pl.BlockSpec((rows, S5_W), lambda i: (nb - 1 - i, 0)),
            pl.BlockSpec((1, 2, S5_W, 256), lambda i: (layer, 0, 0, 0)),
            pl.BlockSpec((1, 2, 2 * S5_P, S5_W), lambda i: (layer, 0, 0, 0)),
            pl.BlockSpec((1, 2, 2 * S5_N), lambda i: (layer, 0, 0)),
            pl.BlockSpec((1, 1, S5_W), lambda i: (layer, 0, 0)),
        ],
        out_specs=[
            pl.BlockSpec((rows, S5_W), lambda i: (i, 0)),
            pl.BlockSpec((rows, S5_W), lambda i: (nb - 1 - i, 0)),
            pl.BlockSpec((2, N_SEQ, 2 * S5_N), lambda i: (0, 0, 0)),
        ],
        out_shape=[
            jax.ShapeDtypeStruct((N_TOK, S5_W), F32),
            jax.ShapeDtypeStruct((N_TOK, S5_W), F32),
            jax.ShapeDtypeStruct((2, N_SEQ, 2 * S5_N), F32),
        ],
        scratch_shapes=[
            pltpu.VMEM((N_SEQ, 2 * S5_N), F32),
            pltpu.VMEM((N_SEQ, 2 * S5_N), F32),
            pltpu.VMEM((rows, 2 * S5_N), F32),
            pltpu.VMEM((rows, 2 * S5_N), F32),
            pltpu.VMEM((2, S5_W, 2 * S5_N), BF16),
            pltpu.VMEM((2, 2 * S5_N, S5_W), BF16),
        ],
        compiler_params=_cparams(("arbitrary",)),
        name="s5_scan",
    )(u2, u2, b_cmp, c_cmp, a_bar, s5_d.reshape(DEPTH, 1, S5_W))


FIX_SEQS = 4


def _s5_fix_kernel(hend_ref, h0_ref, a_ref, c_ref, o_ref, pf_s, pb_s, cf_s, cb_s, cblk_s):
    s = pl.program_id(0)

    @pl.when(s == 0)
    def _():
        for d in range(2):
            cblk_s[d] = _expand_c(c_ref[0, d])
        row8 = lax.broadcasted_iota(I32, (8, S5_N), 0)
        for d, tab in ((0, pf_s), (1, pb_s)):
            ar = a_ref[0, d:d + 1, 0:S5_N]
            ai = a_ref[0, d:d + 1, S5_N:2 * S5_N]
            pr, pi = ar, ai
            r8 = jnp.zeros((8, S5_N), F32)
            i8 = jnp.zeros((8, S5_N), F32)
            for r in range(8):
                if r:
                    pr, pi = _cmul(pr, pi, ar, ai)
                at = r if d == 0 else 7 - r
                r8 = jnp.where(row8 == at, pr, r8)
                i8 = jnp.where(row8 == at, pi, i8)
            base = 0 if d == 0 else TILE - 8
            tab[base:base + 8, 0:S5_N] = r8
            tab[base:base + 8, S5_N:2 * S5_N] = i8
            m = 8
            while m < TILE:
                if d == 0:
                    src, dst, top = slice(0, m), slice(m, 2 * m), slice(m - 1, m)
                else:
                    src, dst, top = slice(TILE - m, TILE), slice(TILE - 2 * m, TILE - m), slice(TILE - m, TILE - m + 1)
                mr, mi = tab[top, 0:S5_N], tab[top, S5_N:2 * S5_N]
                nr, ni = _cmul(tab[src, 0:S5_N], tab[src, S5_N:2 * S5_N], mr, mi)
                tab[dst, 0:S5_N] = nr
                tab[dst, S5_N:2 * S5_N] = ni
                m *= 2

        for b in range(LAT_B):
            ar, ai = pf_s[TILE - 1:TILE, 0:S5_N], pf_s[TILE - 1:TILE, S5_N:2 * S5_N]
            cr, ci = h0_ref[b, 0:1, 0:S5_N], h0_ref[b, 0:1, S5_N:2 * S5_N]
            for j in range(CHUNKS):
                row = b * CHUNKS + j
                cf_s[row:row + 1, 0:S5_N] = cr
                cf_s[row:row + 1, S5_N:2 * S5_N] = ci
                pr, pi = _cmul(ar, ai, cr, ci)
                cr = pr + hend_ref[0, N_CTX + row:N_CTX + row + 1, 0:S5_N]
                ci = pi + hend_ref[0, N_CTX + row:N_CTX + row + 1, S5_N:2 * S5_N]
            ar, ai = pb_s[0:1, 0:S5_N], pb_s[0:1, S5_N:2 * S5_N]
            cr, ci = h0_ref[b, 1:2, 0:S5_N], h0_ref[b, 1:2, S5_N:2 * S5_N]
            for j in reversed(range(CHUNKS)):
                row = b * CHUNKS + j
                cb_s[row:row + 1, 0:S5_N] = cr
                cb_s[row:row + 1, S5_N:2 * S5_N] = ci
                pr, pi = _cmul(ar, ai, cr, ci)
                cr = pr + hend_ref[1, N_CTX + row:N_CTX + row + 1, 0:S5_N]
                ci = pi + hend_ref[1, N_CTX + row:N_CTX + row + 1, S5_N:2 * S5_N]

    acc = None
    for d, tab, car in ((0, pf_s, cf_s), (1, pb_s, cb_s)):
        hs = []
        for k in range(FIX_SEQS):
            cr = car[pl.ds(s * FIX_SEQS + k, 1), 0:S5_N]
            ci = car[pl.ds(s * FIX_SEQS + k, 1), S5_N:2 * S5_N]
            hr, hi = _cmul(tab[:, 0:S5_N], tab[:, S5_N:2 * S5_N], cr, ci)
            hs.append(jnp.concatenate([hr, hi], axis=1).astype(BF16))
        y = jnp.dot(jnp.concatenate(hs, axis=0), cblk_s[d], preferred_element_type=F32)
        acc = y if acc is None else acc + y
    o_ref[...] = acc


def _s5_fix_call(layer, hend, h0, a_bar, c_cmp):
    return pl.pallas_call(
        _s5_fix_kernel,
        grid=(N_LAT // FIX_SEQS,),
        in_specs=[
            pl.BlockSpec((2, N_SEQ, 2 * S5_N), lambda s: (0, 0, 0)),
            pl.BlockSpec((LAT_B, 2, 2 * S5_N), lambda s: (0, 0, 0)),
            pl.BlockSpec((1, 2, 2 * S5_N), lambda s: (layer, 0, 0)),
            pl.BlockSpec((1, 2, 2 * S5_P, S5_W), lambda s: (layer, 0, 0, 0)),
        ],
        out_specs=pl.BlockSpec((FIX_SEQS * TILE, S5_W), lambda s: (s, 0)),
        out_shape=jax.ShapeDtypeStruct((N_LAT * TILE, S5_W), F32),
        scratch_shapes=[
            pltpu.VMEM((TILE, 2 * S5_N), F32),
            pltpu.VMEM((TILE, 2 * S5_N), F32),
            pltpu.VMEM((N_LAT, 2 * S5_N), F32),
            pltpu.VMEM((N_LAT, 2 * S5_N), F32),
            pltpu.VMEM((2, 2 * S5_N, S5_W), BF16),
        ],
        compiler_params=_cparams(("arbitrary",)),
        name="s5_fix",
    )(hend, h0, a_bar, c_cmp)


LRU_ROWS = 1024


def _sigmoid(x):
    return 0.5 * jnp.tanh(0.5 * x) + 0.5


def _lru_kernel(xc_ref, wa_ref, ba_ref, wi_ref, bi_ref, sp_ref, h0_ref, out_ref, hend_ref,
                a_s, h_s, p_s):
    row = lax.broadcasted_iota(I32, (N_SEQ, LRU_W), 0)
    for d in range(2):
        def gates(cix, _, d=d):
            r0 = pl.multiple_of(cix * LRU_ROWS, LRU_ROWS)
            xc = xc_ref[pl.ds(r0, LRU_ROWS), :]
            xb = xc.astype(BF16)
            r = _sigmoid(jnp.dot(xb, wa_ref[0, d], preferred_element_type=F32) + ba_ref[0, d])
            g = _sigmoid(jnp.dot(xb, wi_ref[0, d], preferred_element_type=F32) + bi_ref[0, d])
            log_a = (-LRU_C) * r * sp_ref[0, d]
            a_s[pl.ds(r0, LRU_ROWS), :] = jnp.exp(log_a)
            th = jnp.tanh(log_a)
            h_s[pl.ds(r0, LRU_ROWS), :] = jnp.sqrt(-2.0 * th / (1.0 - th)) * g * xc
            return 0

        lax.fori_loop(0, N_TOK // LRU_ROWS, gates, 0)

        def scan(t, carry, d=d):
            h, p = carry
            tt = t if d == 0 else TILE - 1 - t
            r0 = pl.multiple_of(tt * N_SEQ, N_SEQ)
            a = a_s[pl.ds(r0, N_SEQ), :]
            h = a * h + h_s[pl.ds(r0, N_SEQ), :]
            p = a * p
            h_s[pl.ds(r0, N_SEQ), :] = h
            p_s[pl.ds(r0, N_SEQ), :] = p
            return h, p

        h_end, p_end = lax.fori_loop(0, TILE, scan,
                                     (jnp.zeros((N_SEQ, LRU_W), F32), jnp.ones((N_SEQ, LRU_W), F32)))
        hend_ref[d] = h_end

        carry_slab = jnp.zeros((N_SEQ, LRU_W), F32)
        for b in range(LAT_B):
            c = h0_ref[0, b, d:d + 1, :]
            order = range(CHUNKS) if d == 0 else reversed(range(CHUNKS))
            for j in order:
                s = N_CTX + b * CHUNKS + j
                carry_slab = jnp.where(row == s, c, carry_slab)
                c = h_end[s:s + 1, :] + p_end[s:s + 1, :] * c

        def fix(t, _, d=d, carry_slab=carry_slab):
            r0 = pl.multiple_of(t * N_SEQ, N_SEQ)
            v = h_s[pl.ds(r0, N_SEQ), :] + p_s[pl.ds(r0, N_SEQ), :] * carry_slab
            if d == 0:
                out_ref[pl.ds(r0, N_SEQ), :] = v
            else:
                out_ref[pl.ds(r0, N_SEQ), :] += v
            return 0

        lax.fori_loop(0, TILE, fix, 0)


def _lru_call(layer, xc2, wa, ba, wi, bi, sp, h0):
    full = lambda shape: pl.BlockSpec(shape, lambda i: (0,) * len(shape))
    per_layer = lambda shape: pl.BlockSpec((1,) + shape, lambda i: (layer,) + (0,) * len(shape))
    return pl.pallas_call(
        _lru_kernel,
        grid=(1,),
        in_specs=[
            full((N_TOK, LRU_W)),
            per_layer((2, LRU_W, LRU_W)),
            per_layer((2, 1, LRU_W)),
            per_layer((2, LRU_W, LRU_W)),
            per_layer((2, 1, LRU_W)),
            per_layer((2, 1, LRU_W)),
            per_layer((LAT_B, 2, LRU_W)),
        ],
        out_specs=[full((N_TOK, LRU_W)), full((2, N_SEQ, LRU_W))],
        out_shape=[
            jax.ShapeDtypeStruct((N_TOK, LRU_W), F32),
            jax.ShapeDtypeStruct((2, N_SEQ, LRU_W), F32),
        ],
        scratch_shapes=[pltpu.VMEM((N_TOK, LRU_W), F32)] * 3,
        compiler_params=_cparams(("arbitrary",)),
        name="rglru",
    )(xc2, wa, ba, wi, bi, sp, h0)


def _layer_norm(z, g, b):
    mu = jnp.mean(z, axis=-1, keepdims=True)
    zc = z - mu
    var = jnp.mean(zc * zc, axis=-1, keepdims=True)
    return zc * lax.rsqrt(var + EPS) * g + b


def _route(logits):
    lane_i = lax.broadcasted_iota(I32, logits.shape, 1)
    lane = lane_i.astype(F32)
    big = jnp.float32(1024.0)
    neg = jnp.float32(-jnp.inf)
    is_g = lane_i < N_GROUPS
    gmax = jnp.max(jnp.where(is_g, logits, neg), axis=-1, keepdims=True)
    g_sel = jnp.min(jnp.where(jnp.logical_and(is_g, logits == gmax), lane, big), axis=-1, keepdims=True)
    p_group = 1.0 / jnp.sum(jnp.where(is_g, jnp.exp(logits - gmax), 0.0), axis=-1, keepdims=True)
    e_idx = lane_i - GATE_LANE0
    e_group = (e_idx >> 2).astype(F32)
    in_g = jnp.logical_and(jnp.logical_and(e_idx >= 0, e_idx < N_EXPERTS), e_group == g_sel)
    v1 = jnp.max(jnp.where(in_g, logits, neg), axis=-1, keepdims=True)
    i1 = jnp.min(jnp.where(jnp.logical_and(in_g, logits == v1), lane, big), axis=-1, keepdims=True)
    rest = jnp.logical_and(in_g, lane != i1)
    v2 = jnp.max(jnp.where(rest, logits, neg), axis=-1, keepdims=True)
    i2 = jnp.min(jnp.where(jnp.logical_and(rest, logits == v2), lane, big), axis=-1, keepdims=True)
    e2 = jnp.exp(v2 - v1)
    inv = 1.0 / (1.0 + e2)
    w1 = inv * p_group
    w2 = e2 * inv * p_group
    return jnp.where(lane == i1, w1, jnp.where(lane == i2, w2, jnp.where(lane_i == 0, g_sel, 0.0)))


def _outproj_kernel(x_ref, oc_ref, ol_ref, yf_ref, yb_ref, yfix_ref, hl_ref, g_ref, mod_ref,
                    wglu_ref, bglu_ref, wout_ref, lng_ref, lnb_ref, rw_ref, rwhi_ref, rb_ref,
                    x1_ref, hm_ref, gates_ref, y_s, hl_s):
    i = pl.program_id(0)
    s = pl.program_id(1)

    @pl.when(s == 0)
    def _():
        for j in range(GROUP_SEQS):
            rows = slice(j * TILE, (j + 1) * TILE)
            y_s[rows, :] = yf_ref[:, j, :] + yb_ref[:, j, :]
            hl_s[rows, :] = hl_ref[:, j, :]

    lat = i >= CTX_GROUP_TILES
    sub_rows = pl.ds(pl.multiple_of(s * SUB_ROWS, SUB_ROWS), SUB_ROWS)
    o_att = jnp.where(lat, ol_ref[...], oc_ref[...])
    y = y_s[sub_rows, :] + jnp.where(lat, yfix_ref[...], 0.0)
    g = jax.nn.gelu(y, approximate=True)
    glu = jnp.dot(g.astype(BF16), wglu_ref[0], preferred_element_type=F32) + bglu_ref[0]
    o_s5 = g * jax.nn.sigmoid(glu)
    o_lru = hl_s[sub_rows, :] * jax.nn.gelu(g_ref[...], approximate=True)
    mix = jnp.concatenate([o_att, o_s5.astype(BF16), o_lru.astype(BF16)], axis=-1)
    out = jnp.dot(mix, wout_ref[0], preferred_element_type=F32)
    r = _group_mod_row(i)
    g1 = mod_ref[0, pl.ds(r, 1), 2 * D:3 * D]
    sh2 = mod_ref[0, pl.ds(r, 1), 3 * D:4 * D]
    sc2 = mod_ref[0, pl.ds(r, 1), 4 * D:5 * D]
    x1 = _layer_norm(ALPHA * x_ref[...] + g1 * out, lng_ref[0], lnb_ref[0])
    x1_ref[...] = x1
    hm = x1 * (1.0 + sc2) + sh2
    hm_ref[...] = hm.astype(BF16)
    hm_hi = hm.astype(BF16)
    hm_lo = (hm - hm_hi.astype(F32)).astype(BF16)
    p_hi = jnp.dot(hm_hi, rw_ref[0], preferred_element_type=F32)
    p_lo = jnp.dot(hm_lo, rwhi_ref[0], preferred_element_type=F32)
    logits = p_hi[:, 0:128] + p_hi[:, 128:256] + p_lo + rb_ref[0]
    gates_ref[...] = _route(logits)


def _outproj_call(layer, x, o_ctx, o_lat, yf, yb, yfix, hl, g_lru, mod, w_glu, b_glu,
                  w_out, ln_g, ln_b, rw_split, rw_hi, rb):
    n_ctx_blocks = N_CTX * TILE // SUB_ROWS
    row_spec = lambda w: pl.BlockSpec((SUB_ROWS, w), lambda i, s: (i * SUBS + s, 0))
    lat_spec = lambda w: pl.BlockSpec((SUB_ROWS, w), lambda i, s: (jnp.maximum(i * SUBS + s - n_ctx_blocks, 0), 0))
    tm_spec = pl.BlockSpec((TILE, GROUP_SEQS, 256), lambda i, s: (0, i, 0))
    vec = lambda n: pl.BlockSpec((1, 1, n), lambda i, s: (layer, 0, 0))
    mat = lambda a, b: pl.BlockSpec((1, a, b), lambda i, s: (layer, 0, 0))
    return pl.pallas_call(
        _outproj_kernel,
        grid=(N_GROUP_TILES, SUBS),
        in_specs=[
            row_spec(D),
            pl.BlockSpec((SUB_ROWS, 512), lambda i, s: (jnp.minimum(i * SUBS + s, n_ctx_blocks - 1), 0)),
            lat_spec(512),
            tm_spec, tm_spec,
            lat_spec(S5_W),
            tm_spec,
            row_spec(LRU_W),
            mat(MOD_ROWS, N_MOD * D),
            mat(S5_W, S5_W), vec(S5_W),
            mat(D, D), vec(D), vec(D),
            mat(D, 256), mat(D, 128), vec(128),
        ],
        out_specs=[row_spec(D), row_spec(D), row_spec(128)],
        out_shape=[
            jax.ShapeDtypeStruct((N_TOK, D), F32),
            jax.ShapeDtypeStruct((N_TOK, D), BF16),
            jax.ShapeDtypeStruct((N_TOK, 128), F32),
        ],
        scratch_shapes=[pltpu.VMEM((GROUP_ROWS, 256), F32)] * 2,
        compiler_params=_cparams(("arbitrary", "arbitrary")),
        name="outproj",
    )(x, o_ctx, o_lat, yf.reshape(TILE, N_SEQ, S5_W), yb.reshape(TILE, N_SEQ, S5_W), yfix,
      hl.reshape(TILE, N_SEQ, LRU_W), g_lru, mod, w_glu,
      b_glu.reshape(DEPTH, 1, S5_W), w_out, ln_g.reshape(DEPTH, 1, D), ln_b.reshape(DEPTH, 1, D),
      rw_split, rw_hi, rb)


HALF = N_TOK // 2
HALF_TILES = HALF // TILE


def _sort_kernel(gates_ref, pos_ref, cnt_ref, rank_s):
    sub = lax.broadcasted_iota(I32, (8, TILE), 0)
    sel_r = lax.broadcasted_iota(I32, (8, 128), 0)
    sel_c = lax.broadcasted_iota(I32, (8, 128), 1)
    pick_lane0 = jnp.where(jnp.logical_and(sel_r == 0, sel_c == 0), 1.0, 0.0).astype(BF16)
    ri = lax.broadcasted_iota(I32, (TILE, TILE), 0)
    ci = lax.broadcasted_iota(I32, (TILE, TILE), 1)
    upper = jnp.where(ri <= ci, 1.0, 0.0).astype(BF16)

    def count(b, carry):
        rows = slice(b * TILE, (b + 1) * TILE)
        g = gates_ref[rows, :].astype(BF16)
        g_t = lax.dot_general(pick_lane0, g, _NT, preferred_element_type=F32)
        g_sel = jnp.sum(g_t, axis=0, keepdims=True)
        onehot = jnp.where(jnp.logical_and(sub < N_GROUPS, sub.astype(F32) == g_sel), 1.0, 0.0)
        cum = jnp.dot(onehot.astype(BF16), upper, preferred_element_type=F32) + carry
        rank_s[b] = jnp.where(onehot > 0.0, cum, 0.0)
        return carry + jnp.sum(onehot, axis=1, keepdims=True)

    counts = jnp.zeros((8, 1), F32)
    for b in range(HALF_TILES):
        counts = count(b, counts)
    sub1 = lax.broadcasted_iota(I32, (8, 1), 0)
    c = [jnp.sum(jnp.where(sub1 == g, counts, 0.0), axis=0, keepdims=True) for g in range(N_GROUPS - 1)]
    start = jnp.where(sub1 == 1, c[0], jnp.where(sub1 == 2, c[0] + c[1],
                      jnp.where(sub1 == 3, c[0] + c[1] + c[2], 0.0)))

    for b in range(HALF_TILES):
        rk = rank_s[b]
        p = jnp.sum(jnp.where(rk > 0.0, rk + start - 1.0, 0.0), axis=0, keepdims=True)
        pos_ref[b:b + 1, :] = p.astype(I32)
    cnt_ref[0] = jnp.broadcast_to(counts, (8, 128))


def _sort_call(gates):
    return pl.pallas_call(
        _sort_kernel,
        grid=(2,),
        in_specs=[pl.BlockSpec((HALF, 128), lambda h: (h, 0))],
        out_specs=[pl.BlockSpec((HALF_TILES, TILE), lambda h: (h, 0)),
                   pl.BlockSpec((1, 8, 128), lambda h: (h, 0, 0))],
        out_shape=[jax.ShapeDtypeStruct((N_SEQ, TILE), I32), jax.ShapeDtypeStruct((2, 8, 128), F32)],
        scratch_shapes=[pltpu.VMEM((HALF_TILES, 8, TILE), F32)],
        compiler_params=_cparams(("arbitrary",)),
        name="group_sort",
    )(gates)


ITEM_TILES = 2
N_ITEMS = N_EXPERTS * (HALF_TILES + 2 * N_GROUPS) // (EPG * ITEM_TILES)
SCATTER_ROWS = 2 * TILE
SCATTER_STEPS = HALF // SCATTER_ROWS
MOE_STEPS = SCATTER_STEPS + N_ITEMS + HALF_TILES


def _moe_schedule(counts):
    start = jnp.cumsum(counts, axis=1) - counts
    lo = start // TILE
    hi = (start + counts + TILE - 1) // TILE
    tiles_g = jnp.where(counts > 0, hi - lo, 0)
    n_g = (tiles_g + ITEM_TILES - 1) // ITEM_TILES
    n_e = jnp.repeat(n_g, EPG, axis=1)
    lo_e = jnp.repeat(lo, EPG, axis=1)
    tiles_e = jnp.repeat(tiles_g, EPG, axis=1)
    off_end = jnp.cumsum(n_e, axis=1)
    off = off_end - n_e
    total = off_end[:, -1:]
    w = jnp.arange(N_ITEMS, dtype=I32)[None, :]
    w_eff = jnp.minimum(w, total - 1)
    e_w = jnp.sum((w_eff[:, :, None] >= off_end[:, None, :]).astype(I32), axis=-1)
    first = ITEM_TILES * (w_eff - jnp.take_along_axis(off, e_w, axis=1))
    tile = jnp.take_along_axis(lo_e, e_w, axis=1) + first
    size = jnp.minimum(jnp.take_along_axis(tiles_e, e_w, axis=1) - first, ITEM_TILES)
    size = jnp.where(w < total, size, 0)
    e_f, n_f = e_w.reshape(-1), size.reshape(-1)
    n_all = 2 * N_ITEMS
    idx = jnp.arange(n_all, dtype=I32)
    key = (idx // N_ITEMS) * N_EXPERTS + e_f
    run_start = jnp.logical_and(n_f > 0, key != jnp.concatenate([jnp.full((1,), -1, I32), key[:-1]]))
    slot = (jnp.cumsum(run_start.astype(I32)) - 1) % 2
    at_or_after = lax.cummin(jnp.where(run_start, idx, n_all), reverse=True)
    after = jnp.concatenate([at_or_after[1:], jnp.full((1,), n_all, I32)])
    next_e = jnp.where(after < n_all, e_f[jnp.minimum(after, n_all - 1)], -1)
    return e_f, tile.reshape(-1), n_f, run_start.astype(I32), slot, next_e


def _moe_kernel(pos_ref, ite_ref, itt_ref, itv_ref, itf_ref, its_ref, itx_ref,
                hm_ref, gates_ref, wg_hbm, wu_hbm, wd_hbm, x1_ref, mod_ref, lng_ref, lnb_ref,
                *rest, split, layer):
    out_refs, (xs_s, gs_s, acc_s, hm_s, wg_b, wu_b, wd_b, w_sem) = rest[:-8], rest[-8:]
    h = pl.program_id(0)
    w = pl.program_id(1)
    base = h * HALF

    def weight_copies(e, slot):
        pairs = ((wg_hbm, wg_b), (wu_hbm, wu_b), (wd_hbm, wd_b))
        return [pltpu.make_async_copy(src.at[layer, e], dst.at[slot], w_sem.at[slot, k])
                for k, (src, dst) in enumerate(pairs)]

    @pl.when(w < SCATTER_STEPS)
    def _():
        @pl.when(w == 0)
        def _():
            acc_s[...] = jnp.zeros_like(acc_s)

        @pl.when(jnp.logical_and(h == 0, w == 0))
        def _():
            for c in weight_copies(ite_ref[0], 0):
                c.start()

        hm_s[...] = hm_ref[...].astype(F32)

        def body(r, c):
            p = pos_ref[base + w * SCATTER_ROWS + r]
            xs_s[pl.ds(p, 1), :] = hm_s[pl.ds(r, 1), :]
            gs_s[pl.ds(p, 1), :] = gates_ref[pl.ds(r, 1), :]
            return c

        lax.fori_loop(0, SCATTER_ROWS, body, 0, unroll=8)

    @pl.when(jnp.logical_and(w >= SCATTER_STEPS, w < SCATTER_STEPS + N_ITEMS))
    def _():
        idx = h * N_ITEMS + (w - SCATTER_STEPS)

        slot = its_ref[idx]

        @pl.when(itf_ref[idx] > 0)
        def _():
            @pl.when(itx_ref[idx] >= 0)
            def _():
                for c in weight_copies(itx_ref[idx], 1 - slot):
                    c.start()

            for c in weight_copies(ite_ref[idx], slot):
                c.wait()

        def item(n_tiles):
            e = ite_ref[idx]
            rows = pl.ds(pl.multiple_of(itt_ref[idx] * TILE, TILE), n_tiles * TILE)
            x = xs_s[rows, :].astype(BF16)
            a = jnp.dot(x, wg_b[slot].astype(BF16), preferred_element_type=F32)
            u = jnp.dot(x, wu_b[slot].astype(BF16), preferred_element_type=F32)
            g = gs_s[rows, :]
            lane = lax.broadcasted_iota(I32, g.shape, 1)
            ge = jnp.sum(jnp.where(lane == e + GATE_LANE0, g, 0.0), axis=-1, keepdims=True)
            act = ((a * jax.nn.sigmoid(a)) * u * ge).astype(BF16)
            acc_s[rows, :] += jnp.dot(act, wd_b[slot].astype(BF16), preferred_element_type=F32)

        for n_tiles in range(1, ITEM_TILES + 1):
            pl.when(itv_ref[idx] == n_tiles)(functools.partial(item, n_tiles))

    @pl.when(w >= SCATTER_STEPS + N_ITEMS)
    def _():
        j = w - SCATTER_STEPS - N_ITEMS

        def finish(o_ref):
            def body(r, c):
                p = pos_ref[base + j * TILE + r]
                o_ref[pl.ds(r, 1), :] = acc_s[pl.ds(p, 1), :]
                return c

            lax.fori_loop(0, TILE, body, 0, unroll=8)
            r = jnp.where(h == 0, 0, 1 + j // CHUNKS)
            g2 = mod_ref[0, pl.ds(r, 1), 5 * D:6 * D]
            o_ref[...] = _layer_norm(ALPHA * x1_ref[...] + g2 * o_ref[...], lng_ref[0], lnb_ref[0])

        if split:
            for half, o_ref in enumerate(out_refs):
                pl.when(h == half)(functools.partial(finish, o_ref))
        else:
            finish(out_refs[0])


def _moe_call(layer, pos, items, hm, gates, w_gate, w_up, w_down, x1, mod, ln_g, ln_b, split):
    def in_tile(h, w, *_):
        return (h * SCATTER_STEPS + jnp.minimum(w, SCATTER_STEPS - 1), 0)

    def out_step(w):
        return jnp.clip(w - SCATTER_STEPS - N_ITEMS, 0, HALF_TILES - 1)

    def out_tile(h, w, *_):
        return (h * HALF_TILES + out_step(w), 0)

    if split:
        out_specs = [pl.BlockSpec((TILE, D), lambda h, w, *_: (jnp.where(h == 0, out_step(w), HALF_TILES - 1), 0)),
                     pl.BlockSpec((TILE, D), lambda h, w, *_: (jnp.where(h == 1, out_step(w), 0), 0))]
        out_shape = [jax.ShapeDtypeStruct((HALF, D), F32)] * 2
    else:
        out_specs = pl.BlockSpec((TILE, D), out_tile)
        out_shape = jax.ShapeDtypeStruct((N_TOK, D), F32)

    vec = lambda n: pl.BlockSpec((1, 1, n), lambda h, w, *_: (layer, 0, 0))
    grid_spec = pltpu.PrefetchScalarGridSpec(
        num_scalar_prefetch=1 + len(items),
        grid=(2, MOE_STEPS),
        in_specs=[
            pl.BlockSpec((SCATTER_ROWS, D), in_tile),
            pl.BlockSpec((SCATTER_ROWS, 128), in_tile),
            pl.BlockSpec(memory_space=pl.ANY),
            pl.BlockSpec(memory_space=pl.ANY),
            pl.BlockSpec(memory_space=pl.ANY),
            pl.BlockSpec((TILE, D), out_tile),
            pl.BlockSpec((1, MOD_ROWS, N_MOD * D), lambda h, w, *_: (layer, 0, 0)),
            vec(D), vec(D),
        ],
        out_specs=out_specs,
        scratch_shapes=[
            pltpu.VMEM((HALF, D), F32),
            pltpu.VMEM((HALF, 128), F32),
            pltpu.VMEM((HALF, D), F32),
            pltpu.VMEM((SCATTER_ROWS, D), F32),
            pltpu.VMEM((2, D, D_EXPERT), F32),
            pltpu.VMEM((2, D, D_EXPERT), F32),
            pltpu.VMEM((2, D_EXPERT, D), F32),
            pltpu.SemaphoreType.DMA((2, 3)),
        ],
    )
    return pl.pallas_call(
        functools.partial(_moe_kernel, split=split, layer=layer),
        grid_spec=grid_spec,
        out_shape=out_shape,
        compiler_params=pltpu.CompilerParams(dimension_semantics=("arbitrary", "arbitrary"),
                                             vmem_limit_bytes=MOE_VMEM_LIMIT),
        name="moe",
    )(pos, *items, hm, gates, w_gate, w_up, w_down, x1, mod,
      ln_g.reshape(DEPTH, 1, D), ln_b.reshape(DEPTH, 1, D))


def _rope_tables():
    rows = LAT_T // GRID_W
    row = jnp.repeat(jnp.arange(rows, dtype=F32), GRID_W)
    col = jnp.tile(jnp.arange(GRID_W, dtype=F32), rows)
    n_freq = QK // 4
    inv = ROPE_BASE ** (-jnp.arange(n_freq, dtype=F32) / n_freq)
    ang_r = row[:, None] * inv
    ang_c = col[:, None] * inv
    cos64 = jnp.concatenate([jnp.cos(ang_r), jnp.cos(ang_r), jnp.cos(ang_c), jnp.cos(ang_c)], axis=1)
    sin64 = jnp.concatenate([-jnp.sin(ang_r), jnp.sin(ang_r), -jnp.sin(ang_c), jnp.sin(ang_c)], axis=1)
    cos = jnp.concatenate([jnp.tile(cos64, (1, 2)), jnp.ones((SUB_ROWS, 128), F32)], axis=0)
    sin = jnp.concatenate([jnp.tile(sin64, (1, 2)), jnp.zeros((SUB_ROWS, 128), F32)], axis=0)
    return cos, sin


def _s5_params(a_re, a_im, b_re, b_im, c_re, c_im, log_dt):
    dt = jnp.exp(log_dt)[..., None]
    mag = jnp.exp(a_re * dt)
    abar_r = mag * jnp.cos(a_im * dt)
    abar_i = mag * jnp.sin(a_im * dt)
    den = a_re * a_re + a_im * a_im
    nr = abar_r - 1.0
    coef_r = (nr * a_re + abar_i * a_im) / den
    coef_i = (abar_i * a_re - nr * a_im) / den
    bbar_r = coef_r[..., None] * b_re - coef_i[..., None] * b_im
    bbar_i = coef_r[..., None] * b_im + coef_i[..., None] * b_re
    def b_rows(bb):
        return bb.transpose(0, 1, 2, 4, 3).reshape(DEPTH, 2, S5_W, S5_P)
    br, bi = b_rows(bbar_r), b_rows(bbar_i)
    b_cmp = jnp.concatenate([br, br, bi, bi], axis=-1)
    def c_rows(cc):
        return cc.transpose(0, 1, 4, 2, 3).reshape(DEPTH, 2, S5_P, S5_W)
    c_cmp = jnp.concatenate([c_rows(c_re), c_rows(-c_im)], axis=-2)
    a_bar = jnp.concatenate([abar_r.reshape(DEPTH, 2, S5_N), abar_i.reshape(DEPTH, 2, S5_N)], axis=-1)
    return b_cmp, c_cmp, a_bar


def _block_diag(w):
    eye = jnp.eye(LRU_BLOCKS, dtype=F32)
    m = jnp.einsum('ldkij,kh->ldkihj', w, eye)
    return m.reshape(DEPTH, 2, LRU_W, LRU_W).astype(BF16)


def kernel(x_prompt, x_sample, cache_k, cache_v, state_s5, state_lru, c, c_ctx, w_ada, b_ada, w_in, w_out, lam_q1, lam_k1, lam_q2, lam_k2, subln_g, s5_a_re, s5_a_im, s5_b_re, s5_b_im, s5_c_re, s5_c_im, s5_log_dt, s5_d, s5_w_glu, s5_b_glu, lru_conv_w, lru_conv_b, lru_w_a, lru_b_a, lru_w_i, lru_b_i, lru_lambda, ln1_g, ln1_b, ln2_g, ln2_b, router_group_w, router_group_b, router_expert_w, router_expert_b, moe_w_gate, moe_w_up, moe_w_down):
    x = jnp.concatenate([x_prompt.reshape(CTX_B * CTX_T, D), x_sample.reshape(LAT_B * LAT_T, D)], axis=0)
    cond = jnp.concatenate([c_ctx[None, :], c, jnp.zeros((MOD_ROWS - 1 - LAT_B, D), F32)], axis=0)
    mod = _ada_call(cond, w_ada, b_ada)

    w_out_b = w_out.astype(BF16)
    w_glu_b = s5_w_glu.astype(BF16)
    rope_cos, rope_sin = _rope_tables()
    b_cmp, c_cmp, a_bar = _s5_params(s5_a_re, s5_a_im, s5_b_re, s5_b_im, s5_c_re, s5_c_im, s5_log_dt)
    wa_blk = _block_diag(lru_w_a)
    wi_blk = _block_diag(lru_w_i)
    ba = lru_b_a.reshape(DEPTH, 2, 1, LRU_W)
    bi = lru_b_i.reshape(DEPTH, 2, 1, LRU_W)
    sp = jax.nn.softplus(-lru_lambda).reshape(DEPTH, 2, 1, LRU_W)
    lam = (jnp.exp(jnp.sum(lam_q1 * lam_k1, axis=-1)) - jnp.exp(jnp.sum(lam_q2 * lam_k2, axis=-1)))
    rw = jnp.concatenate([router_group_w, router_expert_w,
                          jnp.zeros((DEPTH, D, 128 - N_GROUPS - N_EXPERTS), F32)], axis=-1)
    rw_hi = rw.astype(BF16)
    rw_split = jnp.concatenate([rw_hi, (rw - rw_hi.astype(F32)).astype(BF16)], axis=-1)
    rb = jnp.concatenate([router_group_b, router_expert_b,
                          jnp.zeros((DEPTH, 128 - N_GROUPS - N_EXPERTS), F32)], axis=-1).reshape(DEPTH, 1, 128)
    s5_h0 = state_s5.reshape(LAT_B, DEPTH, 2, 2 * S5_N)
    lru_h0 = state_lru.transpose(1, 0, 2, 3)

    kc = jnp.zeros((CTX_B, DEPTH, HEADS, CTX_T, 2 * QK), F32)
    vc = jnp.zeros((CTX_B, DEPTH, HEADS, CTX_T, DV), F32)
    s5_states = []
    lru_states = []
    for l in range(DEPTH):
        lambda_init = 0.8 - 0.6 * math.exp(-0.3 * l)
        lam_l = (lam[l] + lambda_init).reshape(1)
        q, k, v, kc, vc, u_tm, xc_tm, g_lru = _inproj_call(
            l, x, mod, w_in, rope_cos, rope_sin, lru_conv_w, lru_conv_b, kc, vc)
        o_ctx = _att_ctx_call(l, lam_l, q, k, v, subln_g, 1.0 - lambda_init)
        o_lat = _att_lat_call(l, lam_l, q, k, v, cache_k, cache_v, subln_g, 1.0 - lambda_init)
        yf, yb, hend = _s5_call(l, u_tm.reshape(N_TOK, S5_W), b_cmp, c_cmp, a_bar, s5_d)
        yfix = _s5_fix_call(l, hend, s5_h0[:, l], a_bar, c_cmp)
        hl, lru_end = _lru_call(l, xc_tm.reshape(N_TOK, LRU_W), wa_blk, ba, wi_blk, bi, sp, lru_h0)
        x1, hm, gates = _outproj_call(l, x, o_ctx, o_lat, yf, yb, yfix, hl, g_lru, mod,
                                      w_glu_b, s5_b_glu, w_out_b, ln1_g, ln1_b, rw_split, rw_hi, rb)
        pos, counts = _sort_call(gates)
        items = _moe_schedule(counts[:, :N_GROUPS, 0].astype(I32))
        x = _moe_call(l, pos.reshape(N_TOK), items, hm, gates, moe_w_gate, moe_w_up,
                      moe_w_down, x1, mod, ln2_g, ln2_b, split=(l == DEPTH - 1))
        s5_states.append(hend[:, :N_CTX].reshape(2, CTX_B, 2, S5_G, S5_P).transpose(1, 0, 2, 3, 4))
        lru_states.append(lru_end[:, :N_CTX].transpose(1, 0, 2))

    y_p = x[0].reshape(CTX_B, CTX_T, D)
    y_s = x[1].reshape(LAT_B, LAT_T, D)
    return (y_p, y_s, kc, vc, jnp.stack(s5_states, axis=1), jnp.stack(lru_states, axis=1))
```

```python
import functools
import math

import jax
import jax.numpy as jnp
from jax import lax
from jax.experimental import pallas as pl
from jax.experimental.pallas import tpu as pltpu

F32 = jnp.float32
BF16 = jnp.bfloat16
I32 = jnp.int32

D = 1024
DEPTH = 4
CTX_B = 16
CTX_T = 256
LAT_B = 2
LAT_T = 2048
PAST = 256
GRID_W = 64
HEADS = 4
QK = 64
DV = 128
S5_W = 256
S5_G = 16
S5_C = 16
S5_P = 64
S5_N = S5_G * S5_P
S5_C_SHIFT = S5_C.bit_length() - 1
S5_P_SHIFT = S5_P.bit_length() - 1
LRU_W = 256
LRU_BLOCKS = 4
LRU_C = 8.0
N_GROUPS = 4
EPG = 4
N_EXPERTS = 16
D_EXPERT = 512
N_MOD = 6
IN_W = 2304
ROPE_BASE = 10000.0
ALPHA = (2 * DEPTH) ** 0.25
EPS = 1e-5

TILE = 256
N_TOK = CTX_B * CTX_T + LAT_B * LAT_T
N_SEQ = N_TOK // TILE
N_CTX = CTX_B
CHUNKS = LAT_T // TILE
N_LAT = LAT_B * CHUNKS
MOD_ROWS = 8
GATE_LANE0 = 4
VMEM_LIMIT = 56 * 1024 * 1024
MOE_VMEM_LIMIT = 60 * 1024 * 1024

GROUP_SEQS = 8
GROUP_ROWS = GROUP_SEQS * TILE
N_GROUP_TILES = N_TOK // GROUP_ROWS
CTX_GROUP_TILES = N_CTX // GROUP_SEQS
SUB_ROWS = 512
SUBS = GROUP_ROWS // SUB_ROWS
SUB_SEQS = SUB_ROWS // TILE


def _cparams(sem):
    return pltpu.CompilerParams(dimension_semantics=sem, vmem_limit_bytes=VMEM_LIMIT)


def _group_mod_row(i):
    return jnp.where(i < CTX_GROUP_TILES, 0, i - CTX_GROUP_TILES + 1)


def _ada_kernel(c_ref, w_ref, b_ref, o_ref):
    c = c_ref[...]
    s = (c * jax.nn.sigmoid(c)).astype(BF16)
    o_ref[0] = jnp.dot(s, w_ref[0].astype(BF16), preferred_element_type=F32) + b_ref[0]


def _ada_call(cond, w_ada, b_ada):
    tn = 1536
    return pl.pallas_call(
        _ada_kernel,
        grid=(DEPTH, N_MOD * D // tn),
        in_specs=[
            pl.BlockSpec((MOD_ROWS, D), lambda l, j: (0, 0)),
            pl.BlockSpec((1, D, tn), lambda l, j: (l, 0, j)),
            pl.BlockSpec((1, 1, tn), lambda l, j: (l, 0, j)),
        ],
        out_specs=pl.BlockSpec((1, MOD_ROWS, tn), lambda l, j: (l, 0, j)),
        out_shape=jax.ShapeDtypeStruct((DEPTH, MOD_ROWS, N_MOD * D), F32),
        compiler_params=_cparams(("arbitrary", "arbitrary")),
        name="adaln",
    )(cond, w_ada, b_ada.reshape(DEPTH, 1, N_MOD * D))


def _inproj_kernel(x_ref, mod_ref, w_ref, cos_ref, sin_ref, cw_ref, cb_ref, kc_in, vc_in,
                   q_ref, k_ref, v_ref, kc_ref, vc_ref, u_ref, xc_ref, g_ref, u_s, xl_s, w_s):
    del kc_in, vc_in
    i = pl.program_id(0)
    s = pl.program_id(1)

    @pl.when(jnp.logical_and(i == 0, s == 0))
    def _():
        for t in range(2 * HEADS):
            kind, head = divmod(t, HEADS)
            halves = []
            for m in range(2):
                col = (2 * kind + m) * HEADS * QK + head * QK
                blk = w_ref[0, :, (col // 128) * 128:(col // 128 + 1) * 128]
                halves.append(blk[:, col % 128:col % 128 + QK])
            w_s[:, t * 128:(t + 1) * 128] = jnp.concatenate(halves, axis=1).astype(BF16)
        w_s[:, 1024:IN_W] = w_ref[0, :, 1024:IN_W].astype(BF16)

    r = _group_mod_row(i)
    sh = mod_ref[0, pl.ds(r, 1), 0:D]
    sc = mod_ref[0, pl.ds(r, 1), D:2 * D]
    xm = (x_ref[...] * (1.0 + sc) + sh).astype(BF16)
    proj = jnp.dot(xm, w_s[...], preferred_element_type=F32)

    qk = proj[:, 0:1024]
    cos = jnp.concatenate([cos_ref[...]] * 8, axis=1)
    sin = jnp.concatenate([sin_ref[...]] * 8, axis=1)
    lane = lax.broadcasted_iota(I32, qk.shape, 1)
    swapped = jnp.where((lane & 31) < 16, pltpu.roll(qk, 1024 - 16, 1), pltpu.roll(qk, 16, 1))
    qk = qk * cos + swapped * sin
    q_ref[...] = qk[:, 0:512].astype(BF16)
    k_ref[...] = qk[:, 512:1024].astype(BF16)
    v = proj[:, 1024:1536]
    v_ref[...] = v.astype(BF16)

    @pl.when(i < CTX_GROUP_TILES)
    def _():
        for jj in range(SUB_SEQS):
            rows = slice(jj * TILE, (jj + 1) * TILE)
            for h in range(HEADS):
                kc_ref[jj, 0, h] = qk[rows, 512 + h * 128:512 + (h + 1) * 128]
                vc_ref[jj, 0, h] = v[rows, h * 128:(h + 1) * 128]

    g_ref[...] = proj[:, 2048:2304]
    sub_rows = pl.ds(pl.multiple_of(s * SUB_ROWS, SUB_ROWS), SUB_ROWS)
    u_s[sub_rows, :] = proj[:, 1536:1792]
    xl_s[sub_rows, :] = proj[:, 1792:2048]

    @pl.when(s == SUBS - 1)
    def _():
        xl = xl_s[...]
        row = lax.broadcasted_iota(I32, xl.shape, 0)
        is_ctx = i < CTX_GROUP_TILES
        pos = jnp.where(is_ctx, row & (TILE - 1), row)
        last = jnp.where(is_ctx, TILE - 1, GROUP_ROWS - 1)
        x_m1 = jnp.where(pos == 0, 0.0, pltpu.roll(xl, 1, 0))
        x_p1 = jnp.where(pos == last, 0.0, pltpu.roll(xl, GROUP_ROWS - 1, 0))
        x_p2 = jnp.where(pos >= last - 1, 0.0, pltpu.roll(xl, GROUP_ROWS - 2, 0))
        cw = cw_ref[0]
        xc = cb_ref[0] + x_m1 * cw[0:1] + xl * cw[1:2] + x_p1 * cw[2:3] + x_p2 * cw[3:4]
        for j in range(GROUP_SEQS):
            xc_ref[:, j, :] = xc[j * TILE:(j + 1) * TILE]
            u_ref[:, j, :] = u_s[j * TILE:(j + 1) * TILE, :]


def _inproj_call(layer, x, mod, w_in, rope_cos, rope_sin, conv_w, conv_b, kc, vc):
    n_ctx_blocks = N_CTX // SUB_SEQS

    def rope_idx(i, s):
        return (jnp.where(i < CTX_GROUP_TILES, SUBS, s), 0)

    def cache_idx(i, s):
        return (jnp.minimum(i * SUBS + s, n_ctx_blocks - 1), layer, 0, 0, 0)

    row_spec = lambda w: pl.BlockSpec((SUB_ROWS, w), lambda i, s: (i * SUBS + s, 0))
    tm_spec = pl.BlockSpec((TILE, GROUP_SEQS, 256), lambda i, s: (0, i, 0))
    cache_spec = pl.BlockSpec((SUB_SEQS, 1, HEADS, TILE, 128), cache_idx)
    return pl.pallas_call(
        _inproj_kernel,
        grid=(N_GROUP_TILES, SUBS),
        in_specs=[
            row_spec(D),
            pl.BlockSpec((1, MOD_ROWS, N_MOD * D), lambda i, s: (layer, 0, 0)),
            pl.BlockSpec((1, D, IN_W), lambda i, s: (layer, 0, 0)),
            pl.BlockSpec((SUB_ROWS, 128), rope_idx),
            pl.BlockSpec((SUB_ROWS, 128), rope_idx),
            pl.BlockSpec((1, 4, LRU_W), lambda i, s: (layer, 0, 0)),
            pl.BlockSpec((1, 1, LRU_W), lambda i, s: (layer, 0, 0)),
            pl.BlockSpec(memory_space=pl.ANY),
            pl.BlockSpec(memory_space=pl.ANY),
        ],
        out_specs=[
            row_spec(512), row_spec(512), row_spec(512),
            cache_spec, cache_spec,
            tm_spec, tm_spec,
            row_spec(LRU_W),
        ],
        out_shape=[
            jax.ShapeDtypeStruct((N_TOK, 512), BF16),
            jax.ShapeDtypeStruct((N_TOK, 512), BF16),
            jax.ShapeDtypeStruct((N_TOK, 512), BF16),
            jax.ShapeDtypeStruct(kc.shape, F32),
            jax.ShapeDtypeStruct(vc.shape, F32),
            jax.ShapeDtypeStruct((TILE, N_SEQ, S5_W), F32),
            jax.ShapeDtypeStruct((TILE, N_SEQ, LRU_W), F32),
            jax.ShapeDtypeStruct((N_TOK, LRU_W), F32),
        ],
        scratch_shapes=[pltpu.VMEM((GROUP_ROWS, S5_W), F32), pltpu.VMEM((GROUP_ROWS, LRU_W), F32),
                        pltpu.VMEM((D, IN_W), BF16)],
        input_output_aliases={7: 3, 8: 4},
        compiler_params=_cparams(("arbitrary", "arbitrary")),
        name="inproj",
    )(x, mod, w_in, rope_cos, rope_sin, conv_w, conv_b.reshape(DEPTH, 1, LRU_W), kc, vc)


_NT = (((1,), (1,)), ((), ()))


def _split_maps(q):
    lane = lax.broadcasted_iota(I32, q.shape, 1)
    zero = jnp.zeros_like(q)
    qs = q * (QK ** -0.5)
    return jnp.where(lane < QK, qs, zero), jnp.where(lane >= QK, qs, zero)


def _softmax_values(qm, keys, values):
    scores = [lax.dot_general(qm, k, _NT, preferred_element_type=F32) for k in keys]
    m = scores[0].max(axis=-1, keepdims=True)
    for sc in scores[1:]:
        m = jnp.maximum(m, sc.max(axis=-1, keepdims=True))
    z = None
    o = None
    for sc, v in zip(scores, values):
        e = jnp.exp(sc - m)
        ez = e.sum(axis=-1, keepdims=True)
        eo = jnp.dot(e.astype(BF16), v, preferred_element_type=F32)
        z = ez if z is None else z + ez
        o = eo if o is None else o + eo
    return o * (1.0 / z)


def _diff_attention(q, keys, values, lam, g, post_scale):
    q1, q2 = _split_maps(q)
    o = _softmax_values(q1, keys, values) - lam * _softmax_values(q2, keys, values)
    ms = jnp.mean(o * o, axis=-1, keepdims=True)
    return (o * lax.rsqrt(ms + EPS) * g) * post_scale


def _att_ctx_kernel(lam_ref, q_ref, k_ref, v_ref, g_ref, o_ref, *, post_scale):
    lam = lam_ref[0]
    for s in range(ATT_CTX_SEQS):
        rows = slice(s * TILE, (s + 1) * TILE)
        for h in range(HEADS):
            cols = slice(h * 128, (h + 1) * 128)
            o = _diff_attention(q_ref[rows, cols], [k_ref[rows, cols]], [v_ref[rows, cols]], lam, g_ref[0],
                                post_scale)
            o_ref[rows, cols] = o.astype(BF16)


ATT_CTX_SEQS = 2


def _att_ctx_call(layer, lam, q, k, v, subln_g, post_scale):
    blk = pl.BlockSpec((ATT_CTX_SEQS * TILE, HEADS * 128), lambda b: (b, 0))
    return pl.pallas_call(
        functools.partial(_att_ctx_kernel, post_scale=post_scale),
        grid=(CTX_B // ATT_CTX_SEQS,),
        in_specs=[
            pl.BlockSpec(memory_space=pltpu.SMEM),
            blk, blk, blk,
            pl.BlockSpec((1, 1, DV), lambda b: (layer, 0, 0)),
        ],
        out_specs=blk,
        out_shape=jax.ShapeDtypeStruct((CTX_B * CTX_T, HEADS * DV), BF16),
        compiler_params=_cparams(("arbitrary",)),
        name="att_ctx",
    )(lam, q, k, v, subln_g.reshape(DEPTH, 1, DV))


def _att_lat_kernel(lam_ref, q_ref, k_ref, v_ref, ck_ref, cv_ref, g_ref, o_ref, *, post_scale):
    o = _diff_attention(q_ref[...], [k_ref[...], ck_ref[...].astype(BF16)],
                        [v_ref[...], cv_ref[...].astype(BF16)], lam_ref[0], g_ref[0], post_scale)
    o_ref[...] = o.astype(BF16)


def _att_lat_call(layer, lam, q, k, v, cache_k, cache_v, subln_g, post_scale):
    lat0 = N_CTX
    cache_spec = pl.BlockSpec((None, None, None, PAST, 128), lambda b, h, t: (b, layer, h, 0, 0))
    return pl.pallas_call(
        functools.partial(_att_lat_kernel, post_scale=post_scale),
        grid=(LAT_B, HEADS, CHUNKS),
        in_specs=[
            pl.BlockSpec(memory_space=pltpu.SMEM),
            pl.BlockSpec((TILE, 128), lambda b, h, t: (lat0 + b * CHUNKS + t, h)),
            pl.BlockSpec((LAT_T, 128), lambda b, h, t: (lat0 // CHUNKS + b, h)),
            pl.BlockSpec((LAT_T, 128), lambda b, h, t: (lat0 // CHUNKS + b, h)),
            cache_spec, cache_spec,
            pl.BlockSpec((1, 1, DV), lambda b, h, t: (layer, 0, 0)),
        ],
        out_specs=pl.BlockSpec((TILE, 128), lambda b, h, t: (b * CHUNKS + t, h)),
        out_shape=jax.ShapeDtypeStruct((LAT_B * LAT_T, HEADS * DV), BF16),
        compiler_params=_cparams(("arbitrary", "arbitrary", "arbitrary")),
        name="att_lat",
    )(lam, q, k, v, cache_k, cache_v, subln_g.reshape(DEPTH, 1, DV))


S5_TB = 32
S5_LANES = 128


def _cmul(ar, ai, br, bi):
    return ar * br - ai * bi, ar * bi + ai * br


def _expand_b(b):
    full = jnp.concatenate([b[:, 0:128]] * (S5_N // 128) + [b[:, 128:256]] * (S5_N // 128), axis=1)
    row = lax.broadcasted_iota(I32, full.shape, 0)
    col = lax.broadcasted_iota(I32, full.shape, 1)
    same_group = (row >> S5_C_SHIFT) == ((col & (S5_N - 1)) >> S5_P_SHIFT)
    return jnp.where(same_group, full, 0.0).astype(BF16)


def _expand_c(c):
    full = jnp.concatenate([c[0:S5_P]] * S5_G + [c[S5_P:2 * S5_P]] * S5_G, axis=0)
    row = lax.broadcasted_iota(I32, full.shape, 0)
    col = lax.broadcasted_iota(I32, full.shape, 1)
    same_group = ((row & (S5_N - 1)) >> S5_P_SHIFT) == (col >> S5_C_SHIFT)
    return jnp.where(same_group, full, 0.0).astype(BF16)


def _s5_kernel(uf_ref, ub_ref, b_ref, c_ref, a_ref, d_ref, yf_ref, yb_ref, hend_ref,
               hf_s, hb_s, buf_f, buf_b, bblk_s, cblk_s):
    i = pl.program_id(0)

    @pl.when(i == 0)
    def _():
        hf_s[...] = jnp.zeros_like(hf_s)
        hb_s[...] = jnp.zeros_like(hb_s)
        for d in range(2):
            bblk_s[d] = _expand_b(b_ref[0, d])
            cblk_s[d] = _expand_c(c_ref[0, d])

    buf_f[...] = jnp.dot(uf_ref[...].astype(BF16), bblk_s[0], preferred_element_type=F32)
    buf_b[...] = jnp.dot(ub_ref[...].astype(BF16), bblk_s[1], preferred_element_type=F32)

    def scan(buf, h_s, d):
        for c in range(S5_N // S5_LANES):
            re = slice(c * S5_LANES, (c + 1) * S5_LANES)
            im = slice(S5_N + c * S5_LANES, S5_N + (c + 1) * S5_LANES)
            ar = a_ref[0, d:d + 1, re]
            ai = a_ref[0, d:d + 1, im]
            hr, hi = h_s[:, re], h_s[:, im]
            for t in range(S5_TB):
                tt = t if d == 0 else S5_TB - 1 - t
                rows = slice(tt * N_SEQ, (tt + 1) * N_SEQ)
                pr, pi = _cmul(ar, ai, hr, hi)
                hr = pr + buf[rows, re]
                hi = pi + buf[rows, im]
                buf[rows, re] = hr
                buf[rows, im] = hi
            h_s[:, re] = hr
            h_s[:, im] = hi

    scan(buf_f, hf_s, 0)
    yf_ref[...] = (jnp.dot(buf_f[...].astype(BF16), cblk_s[0], preferred_element_type=F32)
                   + d_ref[0] * uf_ref[...])
    scan(buf_b, hb_s, 1)
    yb_ref[...] = jnp.dot(buf_b[...].astype(BF16), cblk_s[1], preferred_element_type=F32)

    @pl.when(i == pl.num_programs(0) - 1)
    def _():
        hend_ref[0] = hf_s[...]
        hend_ref[1] = hb_s[...]


def _s5_call(layer, u2, b_cmp, c_cmp, a_bar, s5_d):
    nb = CTX_T // S5_TB
    rows = S5_TB * N_SEQ
    return pl.pallas_call(
        _s5_kernel,
        grid=(nb,),
        in_specs=[
            pl.BlockSpec((rows, S5_W), lambda i: (i, 0)),
            pl.BlockSpec((rows, S5_W), lambda i: (nb - 1 - i, 0)),
            pl.BlockSpec((1, 2, S5_W, 256), lambda i: (layer, 0, 0, 0)),
            pl.BlockSpec((1, 2, 2 * S5_P, S5_W), lambda i: (layer, 0, 0, 0)),
            pl.BlockSpec((1, 2, 2 * S5_N), lambda i: (layer, 0, 0)),
            pl.BlockSpec((1, 1, S5_W), lambda i: (layer, 0, 0)),
        ],
        out_specs=[
            pl.BlockSpec((rows, S5_W), lambda i: (i, 0)),
            pl.BlockSpec((rows, S5_W), lambda i: (nb - 1 - i, 0)),
            pl.BlockSpec((2, N_SEQ, 2 * S5_N), lambda i: (0, 0, 0)),
        ],
        out_shape=[
            jax.ShapeDtypeStruct((N_TOK, S5_W), F32),
            jax.ShapeDtypeStruct((N_TOK, S5_W), F32),
            jax.ShapeDtypeStruct((2, N_SEQ, 2 * S5_N), F32),
        ],
        scratch_shapes=[
            pltpu.VMEM((N_SEQ, 2 * S5_N), F32),
            pltpu.VMEM((N_SEQ, 2 * S5_N), F32),
            pltpu.VMEM((rows, 2 * S5_N), F32),
            pltpu.VMEM((rows, 2 * S5_N), F32),
            pltpu.VMEM((2, S5_W, 2 * S5_N), BF16),
            pltpu.VMEM((2, 2 * S5_N, S5_W), BF16),
        ],
        compiler_params=_cparams(("arbitrary",)),
        name="s5_scan",
    )(u2, u2, b_cmp, c_cmp, a_bar, s5_d.reshape(DEPTH, 1, S5_W))


FIX_SEQS = 4


def _s5_fix_kernel(hend_ref, h0_ref, a_ref, c_ref, o_ref, pf_s, pb_s, cf_s, cb_s, cblk_s):
    s = pl.program_id(0)

    @pl.when(s == 0)
    def _():
        for d in range(2):
            cblk_s[d] = _expand_c(c_ref[0, d])
        row8 = lax.broadcasted_iota(I32, (8, S5_N), 0)
        for d, tab in ((0, pf_s), (1, pb_s)):
            ar = a_ref[0, d:d + 1, 0:S5_N]
            ai = a_ref[0, d:d + 1, S5_N:2 * S5_N]
            pr, pi = ar, ai
            r8 = jnp.zeros((8, S5_N), F32)
            i8 = jnp.zeros((8, S5_N), F32)
            for r in range(8):
                if r:
                    pr, pi = _cmul(pr, pi, ar, ai)
                at = r if d == 0 else 7 - r
                r8 = jnp.where(row8 == at, pr, r8)
                i8 = jnp.where(row8 == at, pi, i8)
            base = 0 if d == 0 else TILE - 8
            tab[base:base + 8, 0:S5_N] = r8
            tab[base:base + 8, S5_N:2 * S5_N] = i8
            m = 8
            while m < TILE:
                if d == 0:
                    src, dst, top = slice(0, m), slice(m, 2 * m), slice(m - 1, m)
                else:
                    src, dst, top = slice(TILE - m, TILE), slice(TILE - 2 * m, TILE - m), slice(TILE - m, TILE - m + 1)
                mr, mi = tab[top, 0:S5_N], tab[top, S5_N:2 * S5_N]
                nr, ni = _cmul(tab[src, 0:S5_N], tab[src, S5_N:2 * S5_N], mr, mi)
                tab[dst, 0:S5_N] = nr
                tab[dst, S5_N:2 * S5_N] = ni
                m *= 2

        for b in range(LAT_B):
            ar, ai = pf_s[TILE - 1:TILE, 0:S5_N], pf_s[TILE - 1:TILE, S5_N:2 * S5_N]
            cr, ci = h0_ref[b, 0:1, 0:S5_N], h0_ref[b, 0:1, S5_N:2 * S5_N]
            for j in range(CHUNKS):
                row = b * CHUNKS + j
                cf_s[row:row + 1, 0:S5_N] = cr
                cf_s[row:row + 1, S5_N:2 * S5_N] = ci
                pr, pi = _cmul(ar, ai, cr, ci)
                cr = pr + hend_ref[0, N_CTX + row:N_CTX + row + 1, 0:S5_N]
                ci = pi + hend_ref[0, N_CTX + row:N_CTX + row + 1, S5_N:2 * S5_N]
            ar, ai = pb_s[0:1, 0:S5_N], pb_s[0:1, S5_N:2 * S5_N]
            cr, ci = h0_ref[b, 1:2, 0:S5_N], h0_ref[b, 1:2, S5_N:2 * S5_N]
            for j in reversed(range(CHUNKS)):
                row = b * CHUNKS + j
                cb_s[row:row + 1, 0:S5_N] = cr
                cb_s[row:row + 1, S5_N:2 * S5_N] = ci
                pr, pi = _cmul(ar, ai, cr, ci)
                cr = pr + hend_ref[1, N_CTX + row:N_CTX + row + 1, 0:S5_N]
                ci = pi + hend_ref[1, N_CTX + row:N_CTX + row + 1, S5_N:2 * S5_N]

    acc = None
    for d, tab, car in ((0, pf_s, cf_s), (1, pb_s, cb_s)):
        hs = []
        for k in range(FIX_SEQS):
            cr = car[pl.ds(s * FIX_SEQS + k, 1), 0:S5_N]
            ci = car[pl.ds(s * FIX_SEQS + k, 1), S5_N:2 * S5_N]
            hr, hi = _cmul(tab[:, 0:S5_N], tab[:, S5_N:2 * S5_N], cr, ci)
            hs.append(jnp.concatenate([hr, hi], axis=1).astype(BF16))
        y = jnp.dot(jnp.concatenate(hs, axis=0), cblk_s[d], preferred_element_type=F32)
        acc = y if acc is None else acc + y
    o_ref[...] = acc


def _s5_fix_call(layer, hend, h0, a_bar, c_cmp):
    return pl.pallas_call(
        _s5_fix_kernel,
        grid=(N_LAT // FIX_SEQS,),
        in_specs=[
            pl.BlockSpec((2, N_SEQ, 2 * S5_N), lambda s: (0, 0, 0)),
            pl.BlockSpec((LAT_B, 2, 2 * S5_N), lambda s: (0, 0, 0)),
            pl.BlockSpec((1, 2, 2 * S5_N), lambda s: (layer, 0, 0)),
            pl.BlockSpec((1, 2, 2 * S5_P, S5_W), lambda s: (layer, 0, 0, 0)),
        ],
        out_specs=pl.BlockSpec((FIX_SEQS * TILE, S5_W), lambda s: (s, 0)),
        out_shape=jax.ShapeDtypeStruct((N_LAT * TILE, S5_W), F32),
        scratch_shapes=[
            pltpu.VMEM((TILE, 2 * S5_N), F32),
            pltpu.VMEM((TILE, 2 * S5_N), F32),
            pltpu.VMEM((N_LAT, 2 * S5_N), F32),
            pltpu.VMEM((N_LAT, 2 * S5_N), F32),
            pltpu.VMEM((2, 2 * S5_N, S5_W), BF16),
        ],
        compiler_params=_cparams(("arbitrary",)),
        name="s5_fix",
    )(hend, h0, a_bar, c_cmp)


LRU_ROWS = 1024


def _sigmoid(x):
    return 0.5 * jnp.tanh(0.5 * x) + 0.5


def _lru_kernel(xc_ref, wa_ref, ba_ref, wi_ref, bi_ref, sp_ref, h0_ref, out_ref, hend_ref,
                a_s, h_s, p_s):
    row = lax.broadcasted_iota(I32, (N_SEQ, LRU_W), 0)
    for d in range(2):
        def gates(cix, _, d=d):
            r0 = pl.multiple_of(cix * LRU_ROWS, LRU_ROWS)
            xc = xc_ref[pl.ds(r0, LRU_ROWS), :]
            xb = xc.astype(BF16)
            r = _sigmoid(jnp.dot(xb, wa_ref[0, d], preferred_element_type=F32) + ba_ref[0, d])
            g = _sigmoid(jnp.dot(xb, wi_ref[0, d], preferred_element_type=F32) + bi_ref[0, d])
            log_a = (-LRU_C) * r * sp_ref[0, d]
            a_s[pl.ds(r0, LRU_ROWS), :] = jnp.exp(log_a)
            th = jnp.tanh(log_a)
            h_s[pl.ds(r0, LRU_ROWS), :] = jnp.sqrt(-2.0 * th / (1.0 - th)) * g * xc
            return 0

        lax.fori_loop(0, N_TOK // LRU_ROWS, gates, 0)

        def scan(t, carry, d=d):
            h, p = carry
            tt = t if d == 0 else TILE - 1 - t
            r0 = pl.multiple_of(tt * N_SEQ, N_SEQ)
            a = a_s[pl.ds(r0, N_SEQ), :]
            h = a * h + h_s[pl.ds(r0, N_SEQ), :]
            p = a * p
            h_s[pl.ds(r0, N_SEQ), :] = h
            p_s[pl.ds(r0, N_SEQ), :] = p
            return h, p

        h_end, p_end = lax.fori_loop(0, TILE, scan,
                                     (jnp.zeros((N_SEQ, LRU_W), F32), jnp.ones((N_SEQ, LRU_W), F32)),
                                     unroll=8)
        hend_ref[d] = h_end

        carry_slab = jnp.zeros((N_SEQ, LRU_W), F32)
        for b in range(LAT_B):
            c = h0_ref[0, b, d:d + 1, :]
            order = range(CHUNKS) if d == 0 else reversed(range(CHUNKS))
            for j in order:
                s = N_CTX + b * CHUNKS + j
                carry_slab = jnp.where(row == s, c, carry_slab)
                c = h_end[s:s + 1, :] + p_end[s:s + 1, :] * c

        def fix(t, _, d=d, carry_slab=carry_slab):
            r0 = pl.multiple_of(t * N_SEQ, N_SEQ)
            v = h_s[pl.ds(r0, N_SEQ), :] + p_s[pl.ds(r0, N_SEQ), :] * carry_slab
            if d == 0:
                out_ref[pl.ds(r0, N_SEQ), :] = v
            else:
                out_ref[pl.ds(r0, N_SEQ), :] += v
            return 0

        lax.fori_loop(0, TILE, fix, 0, unroll=8)


def _lru_call(layer, xc2, wa, ba, wi, bi, sp, h0):
    full = lambda shape: pl.BlockSpec(shape, lambda i: (0,) * len(shape))
    per_layer = lambda shape: pl.BlockSpec((1,) + shape, lambda i: (layer,) + (0,) * len(shape))
    return pl.pallas_call(
        _lru_kernel,
        grid=(1,),
        in_specs=[
            full((N_TOK, LRU_W)),
            per_layer((2, LRU_W, LRU_W)),
            per_layer((2, 1, LRU_W)),
            per_layer((2, LRU_W, LRU_W)),
            per_layer((2, 1, LRU_W)),
            per_layer((2, 1, LRU_W)),
            per_layer((LAT_B, 2, LRU_W)),
        ],
        out_specs=[full((N_TOK, LRU_W)), full((2, N_SEQ, LRU_W))],
        out_shape=[
            jax.ShapeDtypeStruct((N_TOK, LRU_W), F32),
            jax.ShapeDtypeStruct((2, N_SEQ, LRU_W), F32),
        ],
        scratch_shapes=[pltpu.VMEM((N_TOK, LRU_W), F32)] * 3,
        compiler_params=_cparams(("arbitrary",)),
        name="rglru",
    )(xc2, wa, ba, wi, bi, sp, h0)


def _layer_norm(z, g, b):
    mu = jnp.mean(z, axis=-1, keepdims=True)
    zc = z - mu
    var = jnp.mean(zc * zc, axis=-1, keepdims=True)
    return zc * lax.rsqrt(var + EPS) * g + b


def _route(logits):
    lane_i = lax.broadcasted_iota(I32, logits.shape, 1)
    lane = lane_i.astype(F32)
    big = jnp.float32(1024.0)
    neg = jnp.float32(-jnp.inf)
    is_g = lane_i < N_GROUPS
    gmax = jnp.max(jnp.where(is_g, logits, neg), axis=-1, keepdims=True)
    g_sel = jnp.min(jnp.where(jnp.logical_and(is_g, logits == gmax), lane, big), axis=-1, keepdims=True)
    p_group = 1.0 / jnp.sum(jnp.where(is_g, jnp.exp(logits - gmax), 0.0), axis=-1, keepdims=True)
    e_idx = lane_i - GATE_LANE0
    e_group = (e_idx >> 2).astype(F32)
    in_g = jnp.logical_and(jnp.logical_and(e_idx >= 0, e_idx < N_EXPERTS), e_group == g_sel)
    v1 = jnp.max(jnp.where(in_g, logits, neg), axis=-1, keepdims=True)
    i1 = jnp.min(jnp.where(jnp.logical_and(in_g, logits == v1), lane, big), axis=-1, keepdims=True)
    rest = jnp.logical_and(in_g, lane != i1)
    v2 = jnp.max(jnp.where(rest, logits, neg), axis=-1, keepdims=True)
    i2 = jnp.min(jnp.where(jnp.logical_and(rest, logits == v2), lane, big), axis=-1, keepdims=True)
    e2 = jnp.exp(v2 - v1)
    inv = 1.0 / (1.0 + e2)
    w1 = inv * p_group
    w2 = e2 * inv * p_group
    return jnp.where(lane == i1, w1, jnp.where(lane == i2, w2, jnp.where(lane_i == 0, g_sel, 0.0)))


def _outproj_kernel(x_ref, oc_ref, ol_ref, yf_ref, yb_ref, yfix_ref, hl_ref, g_ref, mod_ref,
                    wglu_ref, bglu_ref, wout_ref, lng_ref, lnb_ref, rw_ref, rwhi_ref, rb_ref,
                    x1_ref, hm_ref, gates_ref, y_s, hl_s):
    i = pl.program_id(0)
    s = pl.program_id(1)

    @pl.when(s == 0)
    def _():
        for j in range(GROUP_SEQS):
            rows = slice(j * TILE, (j + 1) * TILE)
            y_s[rows, :] = yf_ref[:, j, :] + yb_ref[:, j, :]
            hl_s[rows, :] = hl_ref[:, j, :]

    lat = i >= CTX_GROUP_TILES
    sub_rows = pl.ds(pl.multiple_of(s * SUB_ROWS, SUB_ROWS), SUB_ROWS)
    o_att = jnp.where(lat, ol_ref[...], oc_ref[...])
    y = y_s[sub_rows, :] + jnp.where(lat, yfix_ref[...], 0.0)
    g = jax.nn.gelu(y, approximate=True)
    glu = jnp.dot(g.astype(BF16), wglu_ref[0], preferred_element_type=F32) + bglu_ref[0]
    o_s5 = g * jax.nn.sigmoid(glu)
    o_lru = hl_s[sub_rows, :] * jax.nn.gelu(g_ref[...], approximate=True)
    mix = jnp.concatenate([o_att, o_s5.astype(BF16), o_lru.astype(BF16)], axis=-1)
    out = jnp.dot(mix, wout_ref[0], preferred_element_type=F32)
    r = _group_mod_row(i)
    g1 = mod_ref[0, pl.ds(r, 1), 2 * D:3 * D]
    sh2 = mod_ref[0, pl.ds(r, 1), 3 * D:4 * D]
    sc2 = mod_ref[0, pl.ds(r, 1), 4 * D:5 * D]
    x1 = _layer_norm(ALPHA * x_ref[...] + g1 * out, lng_ref[0], lnb_ref[0])
    x1_ref[...] = x1
    hm = x1 * (1.0 + sc2) + sh2
    hm_ref[...] = hm.astype(BF16)
    hm_hi = hm.astype(BF16)
    hm_lo = (hm - hm_hi.astype(F32)).astype(BF16)
    p_hi = jnp.dot(hm_hi, rw_ref[0], preferred_element_type=F32)
    p_lo = jnp.dot(hm_lo, rwhi_ref[0], preferred_element_type=F32)
    logits = p_hi[:, 0:128] + p_hi[:, 128:256] + p_lo + rb_ref[0]
    gates_ref[...] = _route(logits)


def _outproj_call(layer, x, o_ctx, o_lat, yf, yb, yfix, hl, g_lru, mod, w_glu, b_glu,
                  w_out, ln_g, ln_b, rw_split, rw_hi, rb):
    n_ctx_blocks = N_CTX * TILE // SUB_ROWS
    row_spec = lambda w: pl.BlockSpec((SUB_ROWS, w), lambda i, s: (i * SUBS + s, 0))
    lat_spec = lambda w: pl.BlockSpec((SUB_ROWS, w), lambda i, s: (jnp.maximum(i * SUBS + s - n_ctx_blocks, 0), 0))
    tm_spec = pl.BlockSpec((TILE, GROUP_SEQS, 256), lambda i, s: (0, i, 0))
    vec = lambda n: pl.BlockSpec((1, 1, n), lambda i, s: (layer, 0, 0))
    mat = lambda a, b: pl.BlockSpec((1, a, b), lambda i, s: (layer, 0, 0))
    return pl.pallas_call(
        _outproj_kernel,
        grid=(N_GROUP_TILES, SUBS),
        in_specs=[
            row_spec(D),
            pl.BlockSpec((SUB_ROWS, 512), lambda i, s: (jnp.minimum(i * SUBS + s, n_ctx_blocks - 1), 0)),
            lat_spec(512),
            tm_spec, tm_spec,
            lat_spec(S5_W),
            tm_spec,
            row_spec(LRU_W),
            mat(MOD_ROWS, N_MOD * D),
            mat(S5_W, S5_W), vec(S5_W),
            mat(D, D), vec(D), vec(D),
            mat(D, 256), mat(D, 128), vec(128),
        ],
        out_specs=[row_spec(D), row_spec(D), row_spec(128)],
        out_shape=[
            jax.ShapeDtypeStruct((N_TOK, D), F32),
            jax.ShapeDtypeStruct((N_TOK, D), BF16),
            jax.ShapeDtypeStruct((N_TOK, 128), F32),
        ],
        scratch_shapes=[pltpu.VMEM((GROUP_ROWS, 256), F32)] * 2,
        compiler_params=_cparams(("arbitrary", "arbitrary")),
        name="outproj",
    )(x, o_ctx, o_lat, yf.reshape(TILE, N_SEQ, S5_W), yb.reshape(TILE, N_SEQ, S5_W), yfix,
      hl.reshape(TILE, N_SEQ, LRU_W), g_lru, mod, w_glu,
      b_glu.reshape(DEPTH, 1, S5_W), w_out, ln_g.reshape(DEPTH, 1, D), ln_b.reshape(DEPTH, 1, D),
      rw_split, rw_hi, rb)


HALF = N_TOK // 2
HALF_TILES = HALF // TILE


def _sort_kernel(gates_ref, pos_ref, cnt_ref, rank_s):
    sub = lax.broadcasted_iota(I32, (8, TILE), 0)
    sel_r = lax.broadcasted_iota(I32, (8, 128), 0)
    sel_c = lax.broadcasted_iota(I32, (8, 128), 1)
    pick_lane0 = jnp.where(jnp.logical_and(sel_r == 0, sel_c == 0), 1.0, 0.0).astype(BF16)
    ri = lax.broadcasted_iota(I32, (TILE, TILE), 0)
    ci = lax.broadcasted_iota(I32, (TILE, TILE), 1)
    upper = jnp.where(ri <= ci, 1.0, 0.0).astype(BF16)

    def count(b, carry):
        rows = slice(b * TILE, (b + 1) * TILE)
        g = gates_ref[rows, :].astype(BF16)
        g_t = lax.dot_general(pick_lane0, g, _NT, preferred_element_type=F32)
        g_sel = jnp.sum(g_t, axis=0, keepdims=True)
        onehot = jnp.where(jnp.logical_and(sub < N_GROUPS, sub.astype(F32) == g_sel), 1.0, 0.0)
        cum = jnp.dot(onehot.astype(BF16), upper, preferred_element_type=F32) + carry
        rank_s[b] = jnp.where(onehot > 0.0, cum, 0.0)
        return carry + jnp.sum(onehot, axis=1, keepdims=True)

    counts = jnp.zeros((8, 1), F32)
    for b in range(HALF_TILES):
        counts = count(b, counts)
    sub1 = lax.broadcasted_iota(I32, (8, 1), 0)
    c = [jnp.sum(jnp.where(sub1 == g, counts, 0.0), axis=0, keepdims=True) for g in range(N_GROUPS - 1)]
    start = jnp.where(sub1 == 1, c[0], jnp.where(sub1 == 2, c[0] + c[1],
                      jnp.where(sub1 == 3, c[0] + c[1] + c[2], 0.0)))

    for b in range(HALF_TILES):
        rk = rank_s[b]
        p = jnp.sum(jnp.where(rk > 0.0, rk + start - 1.0, 0.0), axis=0, keepdims=True)
        pos_ref[b:b + 1, :] = p.astype(I32)
    cnt_ref[0] = jnp.broadcast_to(counts, (8, 128))


def _sort_call(gates):
    return pl.pallas_call(
        _sort_kernel,
        grid=(2,),
        in_specs=[pl.BlockSpec((HALF, 128), lambda h: (h, 0))],
        out_specs=[pl.BlockSpec((HALF_TILES, TILE), lambda h: (h, 0)),
                   pl.BlockSpec((1, 8, 128), lambda h: (h, 0, 0))],
        out_shape=[jax.ShapeDtypeStruct((N_SEQ, TILE), I32), jax.ShapeDtypeStruct((2, 8, 128), F32)],
        scratch_shapes=[pltpu.VMEM((HALF_TILES, 8, TILE), F32)],
        compiler_params=_cparams(("arbitrary",)),
        name="group_sort",
    )(gates)


ITEM_TILES = 2
N_ITEMS = N_EXPERTS * (HALF_TILES + 2 * N_GROUPS) // (EPG * ITEM_TILES)
SCATTER_ROWS = 2 * TILE
SCATTER_STEPS = HALF // SCATTER_ROWS
MOE_STEPS = SCATTER_STEPS + N_ITEMS + HALF_TILES


def _moe_schedule(counts):
    start = jnp.cumsum(counts, axis=1) - counts
    lo = start // TILE
    hi = (start + counts + TILE - 1) // TILE
    tiles_g = jnp.where(counts > 0, hi - lo, 0)
    n_g = (tiles_g + ITEM_TILES - 1) // ITEM_TILES
    n_e = jnp.repeat(n_g, EPG, axis=1)
    lo_e = jnp.repeat(lo, EPG, axis=1)
    tiles_e = jnp.repeat(tiles_g, EPG, axis=1)
    off_end = jnp.cumsum(n_e, axis=1)
    off = off_end - n_e
    total = off_end[:, -1:]
    w = jnp.arange(N_ITEMS, dtype=I32)[None, :]
    w_eff = jnp.minimum(w, total - 1)
    e_w = jnp.sum((w_eff[:, :, None] >= off_end[:, None, :]).astype(I32), axis=-1)
    first = ITEM_TILES * (w_eff - jnp.take_along_axis(off, e_w, axis=1))
    tile = jnp.take_along_axis(lo_e, e_w, axis=1) + first
    size = jnp.minimum(jnp.take_along_axis(tiles_e, e_w, axis=1) - first, ITEM_TILES)
    size = jnp.where(w < total, size, 0)
    e_f, n_f = e_w.reshape(-1), size.reshape(-1)
    n_all = 2 * N_ITEMS
    idx = jnp.arange(n_all, dtype=I32)
    key = (idx // N_ITEMS) * N_EXPERTS + e_f
    run_start = jnp.logical_and(n_f > 0, key != jnp.concatenate([jnp.full((1,), -1, I32), key[:-1]]))
    slot = (jnp.cumsum(run_start.astype(I32)) - 1) % 2
    at_or_after = lax.cummin(jnp.where(run_start, idx, n_all), reverse=True)
    after = jnp.concatenate([at_or_after[1:], jnp.full((1,), n_all, I32)])
    next_e = jnp.where(after < n_all, e_f[jnp.minimum(after, n_all - 1)], -1)
    return e_f, tile.reshape(-1), n_f, run_start.astype(I32), slot, next_e


def _moe_kernel(pos_ref, ite_ref, itt_ref, itv_ref, itf_ref, its_ref, itx_ref,
                hm_ref, gates_ref, wg_hbm, wu_hbm, wd_hbm, x1_ref, mod_ref, lng_ref, lnb_ref,
                *rest, split, layer):
    out_refs, (xs_s, gs_s, acc_s, hm_s, wg_b, wu_b, wd_b, w_sem) = rest[:-8], rest[-8:]
    h = pl.program_id(0)
    w = pl.program_id(1)
    base = h * HALF

    def weight_copies(e, slot):
        pairs = ((wg_hbm, wg_b), (wu_hbm, wu_b), (wd_hbm, wd_b))
        return [pltpu.make_async_copy(src.at[layer, e], dst.at[slot], w_sem.at[slot, k])
                for k, (src, dst) in enumerate(pairs)]

    @pl.when(w < SCATTER_STEPS)
    def _():
        @pl.when(w == 0)
        def _():
            acc_s[...] = jnp.zeros_like(acc_s)

        @pl.when(jnp.logical_and(h == 0, w == 0))
        def _():
            for c in weight_copies(ite_ref[0], 0):
                c.start()

        hm_s[...] = hm_ref[...].astype(F32)

        def body(r, c):
            p = pos_ref[base + w * SCATTER_ROWS + r]
            xs_s[pl.ds(p, 1), :] = hm_s[pl.ds(r, 1), :]
            gs_s[pl.ds(p, 1), :] = gates_ref[pl.ds(r, 1), :]
            return c

        lax.fori_loop(0, SCATTER_ROWS, body, 0, unroll=8)

    @pl.when(jnp.logical_and(w >= SCATTER_STEPS, w < SCATTER_STEPS + N_ITEMS))
    def _():
        idx = h * N_ITEMS + (w - SCATTER_STEPS)

        slot = its_ref[idx]

        @pl.when(itf_ref[idx] > 0)
        def _():
            @pl.when(itx_ref[idx] >= 0)
            def _():
                for c in weight_copies(itx_ref[idx], 1 - slot):
                    c.start()

            for c in weight_copies(ite_ref[idx], slot):
                c.wait()

        def item(n_tiles):
            e = ite_ref[idx]
            rows = pl.ds(pl.multiple_of(itt_ref[idx] * TILE, TILE), n_tiles * TILE)
            x = xs_s[rows, :].astype(BF16)
            a = jnp.dot(x, wg_b[slot].astype(BF16), preferred_element_type=F32)
            u = jnp.dot(x, wu_b[slot].astype(BF16), preferred_element_type=F32)
            g = gs_s[rows, :]
            lane = lax.broadcasted_iota(I32, g.shape, 1)
            ge = jnp.sum(jnp.where(lane == e + GATE_LANE0, g, 0.0), axis=-1, keepdims=True)
            act = ((a * jax.nn.sigmoid(a)) * u * ge).astype(BF16)
            acc_s[rows, :] += jnp.dot(act, wd_b[slot].astype(BF16), preferred_element_type=F32)

        for n_tiles in range(1, ITEM_TILES + 1):
            pl.when(itv_ref[idx] == n_tiles)(functools.partial(item, n_tiles))

    @pl.when(w >= SCATTER_STEPS + N_ITEMS)
    def _():
        j = w - SCATTER_STEPS - N_ITEMS

        def finish(o_ref):
            def body(r, c):
                p = pos_ref[base + j * TILE + r]
                o_ref[pl.ds(r, 1), :] = acc_s[pl.ds(p, 1), :]
                return c

            lax.fori_loop(0, TILE, body, 0, unroll=8)
            r = jnp.where(h == 0, 0, 1 + j // CHUNKS)
            g2 = mod_ref[0, pl.ds(r, 1), 5 * D:6 * D]
            o_ref[...] = _layer_norm(ALPHA * x1_ref[...] + g2 * o_ref[...], lng_ref[0], lnb_ref[0])

        if split:
            for half, o_ref in enumerate(out_refs):
                pl.when(h == half)(functools.partial(finish, o_ref))
        else:
            finish(out_refs[0])


def _moe_call(layer, pos, items, hm, gates, w_gate, w_up, w_down, x1, mod, ln_g, ln_b, split):
    def in_tile(h, w, *_):
        return (h * SCATTER_STEPS + jnp.minimum(w, SCATTER_STEPS - 1), 0)

    def out_step(w):
        return jnp.clip(w - SCATTER_STEPS - N_ITEMS, 0, HALF_TILES - 1)

    def out_tile(h, w, *_):
        return (h * HALF_TILES + out_step(w), 0)

    if split:
        out_specs = [pl.BlockSpec((TILE, D), lambda h, w, *_: (jnp.where(h == 0, out_step(w), HALF_TILES - 1), 0)),
                     pl.BlockSpec((TILE, D), lambda h, w, *_: (jnp.where(h == 1, out_step(w), 0), 0))]
        out_shape = [jax.ShapeDtypeStruct((HALF, D), F32)] * 2
    else:
        out_specs = pl.BlockSpec((TILE, D), out_tile)
        out_shape = jax.ShapeDtypeStruct((N_TOK, D), F32)

    vec = lambda n: pl.BlockSpec((1, 1, n), lambda h, w, *_: (layer, 0, 0))
    grid_spec = pltpu.PrefetchScalarGridSpec(
        num_scalar_prefetch=1 + len(items),
        grid=(2, MOE_STEPS),
        in_specs=[
            pl.BlockSpec((SCATTER_ROWS, D), in_tile),
            pl.BlockSpec((SCATTER_ROWS, 128), in_tile),
            pl.BlockSpec(memory_space=pl.ANY),
            pl.BlockSpec(memory_space=pl.ANY),
            pl.BlockSpec(memory_space=pl.ANY),
            pl.BlockSpec((TILE, D), out_tile),
            pl.BlockSpec((1, MOD_ROWS, N_MOD * D), lambda h, w, *_: (layer, 0, 0)),
            vec(D), vec(D),
        ],
        out_specs=out_specs,
        scratch_shapes=[
            pltpu.VMEM((HALF, D), F32),
            pltpu.VMEM((HALF, 128), F32),
            pltpu.VMEM((HALF, D), F32),
            pltpu.VMEM((SCATTER_ROWS, D), F32),
            pltpu.VMEM((2, D, D_EXPERT), F32),
            pltpu.VMEM((2, D, D_EXPERT), F32),
            pltpu.VMEM((2, D_EXPERT, D), F32),
            pltpu.SemaphoreType.DMA((2, 3)),
        ],
    )
    return pl.pallas_call(
        functools.partial(_moe_kernel, split=split, layer=layer),
        grid_spec=grid_spec,
        out_shape=out_shape,
        compiler_params=pltpu.CompilerParams(dimension_semantics=("arbitrary", "arbitrary"),
                                             vmem_limit_bytes=MOE_VMEM_LIMIT),
        name="moe",
    )(pos, *items, hm, gates, w_gate, w_up, w_down, x1, mod,
      ln_g.reshape(DEPTH, 1, D), ln_b.reshape(DEPTH, 1, D))


def _rope_tables():
    rows = LAT_T // GRID_W
    row = jnp.repeat(jnp.arange(rows, dtype=F32), GRID_W)
    col = jnp.tile(jnp.arange(GRID_W, dtype=F32), rows)
    n_freq = QK // 4
    inv = ROPE_BASE ** (-jnp.arange(n_freq, dtype=F32) / n_freq)
    ang_r = row[:, None] * inv
    ang_c = col[:, None] * inv
    cos64 = jnp.concatenate([jnp.cos(ang_r), jnp.cos(ang_r), jnp.cos(ang_c), jnp.cos(ang_c)], axis=1)
    sin64 = jnp.concatenate([-jnp.sin(ang_r), jnp.sin(ang_r), -jnp.sin(ang_c), jnp.sin(ang_c)], axis=1)
    cos = jnp.concatenate([jnp.tile(cos64, (1, 2)), jnp.ones((SUB_ROWS, 128), F32)], axis=0)
    sin = jnp.concatenate([jnp.tile(sin64, (1, 2)), jnp.zeros((SUB_ROWS, 128), F32)], axis=0)
    return cos, sin


def _s5_params(a_re, a_im, b_re, b_im, c_re, c_im, log_dt):
    dt = jnp.exp(log_dt)[..., None]
    mag = jnp.exp(a_re * dt)
    abar_r = mag * jnp.cos(a_im * dt)
    abar_i = mag * jnp.sin(a_im * dt)
    den = a_re * a_re + a_im * a_im
    nr = abar_r - 1.0
    coef_r = (nr * a_re + abar_i * a_im) / den
    coef_i = (abar_i * a_re - nr * a_im) / den
    bbar_r = coef_r[..., None] * b_re - coef_i[..., None] * b_im
    bbar_i = coef_r[..., None] * b_im + coef_i[..., None] * b_re
    def b_rows(bb):
        return bb.transpose(0, 1, 2, 4, 3).reshape(DEPTH, 2, S5_W, S5_P)
    br, bi = b_rows(bbar_r), b_rows(bbar_i)
    b_cmp = jnp.concatenate([br, br, bi, bi], axis=-1)
    def c_rows(cc):
        return cc.transpose(0, 1, 4, 2, 3).reshape(DEPTH, 2, S5_P, S5_W)
    c_cmp = jnp.concatenate([c_rows(c_re), c_rows(-c_im)], axis=-2)
    a_bar = jnp.concatenate([abar_r.reshape(DEPTH, 2, S5_N), abar_i.reshape(DEPTH, 2, S5_N)], axis=-1)
    return b_cmp, c_cmp, a_bar


def _block_diag(w):
    eye = jnp.eye(LRU_BLOCKS, dtype=F32)
    m = jnp.einsum('ldkij,kh->ldkihj', w, eye)
    return m.reshape(DEPTH, 2, LRU_W, LRU_W).astype(BF16)


def kernel(x_prompt, x_sample, cache_k, cache_v, state_s5, state_lru, c, c_ctx, w_ada, b_ada, w_in, w_out, lam_q1, lam_k1, lam_q2, lam_k2, subln_g, s5_a_re, s5_a_im, s5_b_re, s5_b_im, s5_c_re, s5_c_im, s5_log_dt, s5_d, s5_w_glu, s5_b_glu, lru_conv_w, lru_conv_b, lru_w_a, lru_b_a, lru_w_i, lru_b_i, lru_lambda, ln1_g, ln1_b, ln2_g, ln2_b, router_group_w, router_group_b, router_expert_w, router_expert_b, moe_w_gate, moe_w_up, moe_w_down):
    x = jnp.concatenate([x_prompt.reshape(CTX_B * CTX_T, D), x_sample.reshape(LAT_B * LAT_T, D)], axis=0)
    cond = jnp.concatenate([c_ctx[None, :], c, jnp.zeros((MOD_ROWS - 1 - LAT_B, D), F32)], axis=0)
    mod = _ada_call(cond, w_ada, b_ada)

    w_out_b = w_out.astype(BF16)
    w_glu_b = s5_w_glu.astype(BF16)
    rope_cos, rope_sin = _rope_tables()
    b_cmp, c_cmp, a_bar = _s5_params(s5_a_re, s5_a_im, s5_b_re, s5_b_im, s5_c_re, s5_c_im, s5_log_dt)
    wa_blk = _block_diag(lru_w_a)
    wi_blk = _block_diag(lru_w_i)
    ba = lru_b_a.reshape(DEPTH, 2, 1, LRU_W)
    bi = lru_b_i.reshape(DEPTH, 2, 1, LRU_W)
    sp = jax.nn.softplus(-lru_lambda).reshape(DEPTH, 2, 1, LRU_W)
    lam = (jnp.exp(jnp.sum(lam_q1 * lam_k1, axis=-1)) - jnp.exp(jnp.sum(lam_q2 * lam_k2, axis=-1)))
    rw = jnp.concatenate([router_group_w, router_expert_w,
                          jnp.zeros((DEPTH, D, 128 - N_GROUPS - N_EXPERTS), F32)], axis=-1)
    rw_hi = rw.astype(BF16)
    rw_split = jnp.concatenate([rw_hi, (rw - rw_hi.astype(F32)).astype(BF16)], axis=-1)
    rb = jnp.concatenate([router_group_b, router_expert_b,
                          jnp.zeros((DEPTH, 128 - N_GROUPS - N_EXPERTS), F32)], axis=-1).reshape(DEPTH, 1, 128)
    s5_h0 = state_s5.reshape(LAT_B, DEPTH, 2, 2 * S5_N)
    lru_h0 = state_lru.transpose(1, 0, 2, 3)

    kc = jnp.zeros((CTX_B, DEPTH, HEADS, CTX_T, 2 * QK), F32)
    vc = jnp.zeros((CTX_B, DEPTH, HEADS, CTX_T, DV), F32)
    s5_states = []
    lru_states = []
    for l in range(DEPTH):
        lambda_init = 0.8 - 0.6 * math.exp(-0.3 * l)
        lam_l = (lam[l] + lambda_init).reshape(1)
        q, k, v, kc, vc, u_tm, xc_tm, g_lru = _inproj_call(
            l, x, mod, w_in, rope_cos, rope_sin, lru_conv_w, lru_conv_b, kc, vc)
        o_ctx = _att_ctx_call(l, lam_l, q, k, v, subln_g, 1.0 - lambda_init)
        o_lat = _att_lat_call(l, lam_l, q, k, v, cache_k, cache_v, subln_g, 1.0 - lambda_init)
        yf, yb, hend = _s5_call(l, u_tm.reshape(N_TOK, S5_W), b_cmp, c_cmp, a_bar, s5_d)
        yfix = _s5_fix_call(l, hend, s5_h0[:, l], a_bar, c_cmp)
        hl, lru_end = _lru_call(l, xc_tm.reshape(N_TOK, LRU_W), wa_blk, ba, wi_blk, bi, sp, lru_h0)
        x1, hm, gates = _outproj_call(l, x, o_ctx, o_lat, yf, yb, yfix, hl, g_lru, mod,
                                      w_glu_b, s5_b_glu, w_out_b, ln1_g, ln1_b, rw_split, rw_hi, rb)
        pos, counts = _sort_call(gates)
        items = _moe_schedule(counts[:, :N_GROUPS, 0].astype(I32))
        x = _moe_call(l, pos.reshape(N_TOK), items, hm, gates, moe_w_gate, moe_w_up,
                      moe_w_down, x1, mod, ln2_g, ln2_b, split=(l == DEPTH - 1))
        s5_states.append(hend[:, :N_CTX].reshape(2, CTX_B, 2, S5_G, S5_P).transpose(1, 0, 2, 3, 4))
        lru_states.append(lru_end[:, :N_CTX].transpose(1, 0, 2))

    y_p = x[0].reshape(CTX_B, CTX_T, D)
    y_s = x[1].reshape(LAT_B, LAT_T, D)
    return (y_p, y_s, kc, vc, jnp.stack(s5_states, axis=1), jnp.stack(lru_states, axis=1))
```

```python
import functools
import math

import jax
import jax.numpy as jnp
from jax import lax
from jax.experimental import pallas as pl
from jax.experimental.pallas import tpu as pltpu

F32 = jnp.float32
BF16 = jnp.bfloat16
I32 = jnp.int32

D = 1024
DEPTH = 4
CTX_B = 16
CTX_T = 256
LAT_B = 2
LAT_T = 2048
PAST = 256
GRID_W = 64
HEADS = 4
QK = 64
DV = 128
S5_W = 256
S5_G = 16
S5_C = 16
S5_P = 64
S5_N = S5_G * S5_P
S5_C_SHIFT = S5_C.bit_length() - 1
S5_P_SHIFT = S5_P.bit_length() - 1
LRU_W = 256
LRU_BLOCKS = 4
LRU_C = 8.0
N_GROUPS = 4
EPG = 4
N_EXPERTS = 16
D_EXPERT = 512
N_MOD = 6
IN_W = 2304
ROPE_BASE = 10000.0
ALPHA = (2 * DEPTH) ** 0.25
EPS = 1e-5

TILE = 256
N_TOK = CTX_B * CTX_T + LAT_B * LAT_T
N_SEQ = N_TOK // TILE
N_CTX = CTX_B
CHUNKS = LAT_T // TILE
N_LAT = LAT_B * CHUNKS
MOD_ROWS = 8
GATE_LANE0 = 4
VMEM_LIMIT = 56 * 1024 * 1024
MOE_VMEM_LIMIT = 60 * 1024 * 1024

GROUP_SEQS = 8
GROUP_ROWS = GROUP_SEQS * TILE
N_GROUP_TILES = N_TOK // GROUP_ROWS
CTX_GROUP_TILES = N_CTX // GROUP_SEQS
SUB_ROWS = 512
SUBS = GROUP_ROWS // SUB_ROWS
SUB_SEQS = SUB_ROWS // TILE


def _cparams(sem):
    return pltpu.CompilerParams(dimension_semantics=sem, vmem_limit_bytes=VMEM_LIMIT)


def _group_mod_row(i):
    return jnp.where(i < CTX_GROUP_TILES, 0, i - CTX_GROUP_TILES + 1)


def _ada_kernel(c_ref, w_ref, b_ref, o_ref):
    c = c_ref[...]
    s = (c * jax.nn.sigmoid(c)).astype(BF16)
    o_ref[0] = jnp.dot(s, w_ref[0].astype(BF16), preferred_element_type=F32) + b_ref[0]


def _ada_call(cond, w_ada, b_ada):
    tn = 3072
    return pl.pallas_call(
        _ada_kernel,
        grid=(DEPTH, N_MOD * D // tn),
        in_specs=[
            pl.BlockSpec((MOD_ROWS, D), lambda l, j: (0, 0)),
            pl.BlockSpec((1, D, tn), lambda l, j: (l, 0, j)),
            pl.BlockSpec((1, 1, tn), lambda l, j: (l, 0, j)),
        ],
        out_specs=pl.BlockSpec((1, MOD_ROWS, tn), lambda l, j: (l, 0, j)),
        out_shape=jax.ShapeDtypeStruct((DEPTH, MOD_ROWS, N_MOD * D), F32),
        compiler_params=_cparams(("arbitrary", "arbitrary")),
        name="adaln",
    )(cond, w_ada, b_ada.reshape(DEPTH, 1, N_MOD * D))


def _inproj_kernel(x_ref, mod_ref, w_ref, cos_ref, sin_ref, cw_ref, cb_ref, kc_in, vc_in,
                   q_ref, k_ref, v_ref, kc_ref, vc_ref, u_ref, xc_ref, g_ref, u_s, xl_s, w_s):
    del kc_in, vc_in
    i = pl.program_id(0)
    s = pl.program_id(1)

    @pl.when(jnp.logical_and(i == 0, s == 0))
    def _():
        for t in range(2 * HEADS):
            kind, head = divmod(t, HEADS)
            halves = []
            for m in range(2):
                col = (2 * kind + m) * HEADS * QK + head * QK
                blk = w_ref[0, :, (col // 128) * 128:(col // 128 + 1) * 128]
                halves.append(blk[:, col % 128:col % 128 + QK])
            w_s[:, t * 128:(t + 1) * 128] = jnp.concatenate(halves, axis=1).astype(BF16)
        w_s[:, 1024:IN_W] = w_ref[0, :, 1024:IN_W].astype(BF16)

    r = _group_mod_row(i)
    sh = mod_ref[0, pl.ds(r, 1), 0:D]
    sc = mod_ref[0, pl.ds(r, 1), D:2 * D]
    xm = (x_ref[...] * (1.0 + sc) + sh).astype(BF16)
    proj = jnp.dot(xm, w_s[...], preferred_element_type=F32)

    qk = proj[:, 0:1024]
    cos = jnp.concatenate([cos_ref[...]] * 8, axis=1)
    sin = jnp.concatenate([sin_ref[...]] * 8, axis=1)
    lane = lax.broadcasted_iota(I32, qk.shape, 1)
    swapped = jnp.where((lane & 31) < 16, pltpu.roll(qk, 1024 - 16, 1), pltpu.roll(qk, 16, 1))
    qk = qk * cos + swapped * sin
    q_ref[...] = qk[:, 0:512].astype(BF16)
    k_ref[...] = qk[:, 512:1024].astype(BF16)
    v = proj[:, 1024:1536]
    v_ref[...] = v.astype(BF16)

    @pl.when(i < CTX_GROUP_TILES)
    def _():
        for jj in range(SUB_SEQS):
            rows = slice(jj * TILE, (jj + 1) * TILE)
            for h in range(HEADS):
                kc_ref[jj, 0, h] = qk[rows, 512 + h * 128:512 + (h + 1) * 128]
                vc_ref[jj, 0, h] = v[rows, h * 128:(h + 1) * 128]

    g_ref[...] = proj[:, 2048:2304]
    sub_rows = pl.ds(pl.multiple_of(s * SUB_ROWS, SUB_ROWS), SUB_ROWS)
    u_s[sub_rows, :] = proj[:, 1536:1792]
    xl_s[sub_rows, :] = proj[:, 1792:2048]

    @pl.when(s == SUBS - 1)
    def _():
        xl = xl_s[...]
        row = lax.broadcasted_iota(I32, xl.shape, 0)
        is_ctx = i < CTX_GROUP_TILES
        pos = jnp.where(is_ctx, row & (TILE - 1), row)
        last = jnp.where(is_ctx, TILE - 1, GROUP_ROWS - 1)
        x_m1 = jnp.where(pos == 0, 0.0, pltpu.roll(xl, 1, 0))
        x_p1 = jnp.where(pos == last, 0.0, pltpu.roll(xl, GROUP_ROWS - 1, 0))
        x_p2 = jnp.where(pos >= last - 1, 0.0, pltpu.roll(xl, GROUP_ROWS - 2, 0))
        cw = cw_ref[0]
        xc = cb_ref[0] + x_m1 * cw[0:1] + xl * cw[1:2] + x_p1 * cw[2:3] + x_p2 * cw[3:4]
        for j in range(GROUP_SEQS):
            xc_ref[:, j, :] = xc[j * TILE:(j + 1) * TILE]
            u_ref[:, j, :] = u_s[j * TILE:(j + 1) * TILE, :]


def _inproj_call(layer, x, mod, w_in, rope_cos, rope_sin, conv_w, conv_b, kc, vc):
    n_ctx_blocks = N_CTX // SUB_SEQS

    def rope_idx(i, s):
        return (jnp.where(i < CTX_GROUP_TILES, SUBS, s), 0)

    def cache_idx(i, s):
        return (jnp.minimum(i * SUBS + s, n_ctx_blocks - 1), layer, 0, 0, 0)

    row_spec = lambda w: pl.BlockSpec((SUB_ROWS, w), lambda i, s: (i * SUBS + s, 0))
    tm_spec = pl.BlockSpec((TILE, GROUP_SEQS, 256), lambda i, s: (0, i, 0))
    cache_spec = pl.BlockSpec((SUB_SEQS, 1, HEADS, TILE, 128), cache_idx)
    return pl.pallas_call(
        _inproj_kernel,
        grid=(N_GROUP_TILES, SUBS),
        in_specs=[
            row_spec(D),
            pl.BlockSpec((1, MOD_ROWS, N_MOD * D), lambda i, s: (layer, 0, 0)),
            pl.BlockSpec((1, D, IN_W), lambda i, s: (layer, 0, 0)),
            pl.BlockSpec((SUB_ROWS, 128), rope_idx),
            pl.BlockSpec((SUB_ROWS, 128), rope_idx),
            pl.BlockSpec((1, 4, LRU_W), lambda i, s: (layer, 0, 0)),
            pl.BlockSpec((1, 1, LRU_W), lambda i, s: (layer, 0, 0)),
            pl.BlockSpec(memory_space=pl.ANY),
            pl.BlockSpec(memory_space=pl.ANY),
        ],
        out_specs=[
            row_spec(512), row_spec(512), row_spec(512),
            cache_spec, cache_spec,
            tm_spec, tm_spec,
            row_spec(LRU_W),
        ],
        out_shape=[
            jax.ShapeDtypeStruct((N_TOK, 512), BF16),
            jax.ShapeDtypeStruct((N_TOK, 512), BF16),
            jax.ShapeDtypeStruct((N_TOK, 512), BF16),
            jax.ShapeDtypeStruct(kc.shape, F32),
            jax.ShapeDtypeStruct(vc.shape, F32),
            jax.ShapeDtypeStruct((TILE, N_SEQ, S5_W), F32),
            jax.ShapeDtypeStruct((TILE, N_SEQ, LRU_W), F32),
            jax.ShapeDtypeStruct((N_TOK, LRU_W), F32),
        ],
        scratch_shapes=[pltpu.VMEM((GROUP_ROWS, S5_W), F32), pltpu.VMEM((GROUP_ROWS, LRU_W), F32),
                        pltpu.VMEM((D, IN_W), BF16)],
        input_output_aliases={7: 3, 8: 4},
        compiler_params=_cparams(("arbitrary", "arbitrary")),
        name="inproj",
    )(x, mod, w_in, rope_cos, rope_sin, conv_w, conv_b.reshape(DEPTH, 1, LRU_W), kc, vc)


_NT = (((1,), (1,)), ((), ()))


def _split_maps(q):
    lane = lax.broadcasted_iota(I32, q.shape, 1)
    zero = jnp.zeros_like(q)
    qs = q * (QK ** -0.5)
    return jnp.where(lane < QK, qs, zero), jnp.where(lane >= QK, qs, zero)


def _softmax_values(qm, keys, values):
    scores = [lax.dot_general(qm, k, _NT, preferred_element_type=F32) for k in keys]
    m = scores[0].max(axis=-1, keepdims=True)
    for sc in scores[1:]:
        m = jnp.maximum(m, sc.max(axis=-1, keepdims=True))
    z = None
    o = None
    for sc, v in zip(scores, values):
        e = jnp.exp(sc - m)
        ez = e.sum(axis=-1, keepdims=True)
        eo = jnp.dot(e.astype(BF16), v, preferred_element_type=F32)
        z = ez if z is None else z + ez
        o = eo if o is None else o + eo
    return o * (1.0 / z)


def _diff_attention(q, keys, values, lam, g, post_scale):
    q1, q2 = _split_maps(q)
    o = _softmax_values(q1, keys, values) - lam * _softmax_values(q2, keys, values)
    ms = jnp.mean(o * o, axis=-1, keepdims=True)
    return (o * lax.rsqrt(ms + EPS) * g) * post_scale


def _att_ctx_kernel(lam_ref, q_ref, k_ref, v_ref, g_ref, o_ref, *, post_scale):
    lam = lam_ref[0]
    for s in range(ATT_CTX_SEQS):
        rows = slice(s * TILE, (s + 1) * TILE)
        for h in range(HEADS):
            cols = slice(h * 128, (h + 1) * 128)
            o = _diff_attention(q_ref[rows, cols], [k_ref[rows, cols]], [v_ref[rows, cols]], lam, g_ref[0],
                                post_scale)
            o_ref[rows, cols] = o.astype(BF16)


ATT_CTX_SEQS = 4


def _att_ctx_call(layer, lam, q, k, v, subln_g, post_scale):
    blk = pl.BlockSpec((ATT_CTX_SEQS * TILE, HEADS * 128), lambda b: (b, 0))
    return pl.pallas_call(
        functools.partial(_att_ctx_kernel, post_scale=post_scale),
        grid=(CTX_B // ATT_CTX_SEQS,),
        in_specs=[
            pl.BlockSpec(memory_space=pltpu.SMEM),
            blk, blk, blk,
            pl.BlockSpec((1, 1, DV), lambda b: (layer, 0, 0)),
        ],
        out_specs=blk,
        out_shape=jax.ShapeDtypeStruct((CTX_B * CTX_T, HEADS * DV), BF16),
        compiler_params=_cparams(("arbitrary",)),
        name="att_ctx",
    )(lam, q, k, v, subln_g.reshape(DEPTH, 1, DV))


def _att_lat_kernel(lam_ref, q_ref, k_ref, v_ref, ck_ref, cv_ref, g_ref, o_ref, *, post_scale):
    o = _diff_attention(q_ref[...], [k_ref[...], ck_ref[...].astype(BF16)],
                        [v_ref[...], cv_ref[...].astype(BF16)], lam_ref[0], g_ref[0], post_scale)
    o_ref[...] = o.astype(BF16)


def _att_lat_call(layer, lam, q, k, v, cache_k, cache_v, subln_g, post_scale):
    lat0 = N_CTX
    cache_spec = pl.BlockSpec((None, None, None, PAST, 128), lambda b, h, t: (b, layer, h, 0, 0))
    return pl.pallas_call(
        functools.partial(_att_lat_kernel, post_scale=post_scale),
        grid=(LAT_B, HEADS, CHUNKS),
        in_specs=[
            pl.BlockSpec(memory_space=pltpu.SMEM),
            pl.BlockSpec((TILE, 128), lambda b, h, t: (lat0 + b * CHUNKS + t, h)),
            pl.BlockSpec((LAT_T, 128), lambda b, h, t: (lat0 // CHUNKS + b, h)),
            pl.BlockSpec((LAT_T, 128), lambda b, h, t: (lat0 // CHUNKS + b, h)),
            cache_spec, cache_spec,
            pl.BlockSpec((1, 1, DV), lambda b, h, t: (layer, 0, 0)),
        ],
        out_specs=pl.BlockSpec((TILE, 128), lambda b, h, t: (b * CHUNKS + t, h)),
        out_shape=jax.ShapeDtypeStruct((LAT_B * LAT_T, HEADS * DV), BF16),
        compiler_params=_cparams(("arbitrary", "arbitrary", "arbitrary")),
        name="att_lat",
    )(lam, q, k, v, cache_k, cache_v, subln_g.reshape(DEPTH, 1, DV))


S5_TB = 32
S5_LANES = 128


def _cmul(ar, ai, br, bi):
    return ar * br - ai * bi, ar * bi + ai * br


def _expand_b(b):
    full = jnp.concatenate([b[:, 0:128]] * (S5_N // 128) + [b[:, 128:256]] * (S5_N // 128), axis=1)
    row = lax.broadcasted_iota(I32, full.shape, 0)
    col = lax.broadcasted_iota(I32, full.shape, 1)
    same_group = (row >> S5_C_SHIFT) == ((col & (S5_N - 1)) >> S5_P_SHIFT)
    return jnp.where(same_group, full, 0.0).astype(BF16)


def _expand_c(c):
    full = jnp.concatenate([c[0:S5_P]] * S5_G + [c[S5_P:2 * S5_P]] * S5_G, axis=0)
    row = lax.broadcasted_iota(I32, full.shape, 0)
    col = lax.broadcasted_iota(I32, full.shape, 1)
    same_group = ((row & (S5_N - 1)) >> S5_P_SHIFT) == (col >> S5_C_SHIFT)
    return jnp.where(same_group, full, 0.0).astype(BF16)


def _s5_kernel(uf_ref, ub_ref, b_ref, c_ref, a_ref, d_ref, yf_ref, yb_ref, hend_ref,
               hf_s, hb_s, buf_f, buf_b, bblk_s, cblk_s):
    i = pl.program_id(0)

    @pl.when(i == 0)
    def _():
        hf_s[...] = jnp.zeros_like(hf_s)
        hb_s[...] = jnp.zeros_like(hb_s)
        for d in range(2):
            bblk_s[d] = _expand_b(b_ref[0, d])
            cblk_s[d] = _expand_c(c_ref[0, d])

    buf_f[...] = jnp.dot(uf_ref[...].astype(BF16), bblk_s[0], preferred_element_type=F32)
    buf_b[...] = jnp.dot(ub_ref[...].astype(BF16), bblk_s[1], preferred_element_type=F32)

    def scan(buf, h_s, d):
        for c in range(S5_N // S5_LANES):
            re = slice(c * S5_LANES, (c + 1) * S5_LANES)
            im = slice(S5_N + c * S5_LANES, S5_N + (c + 1) * S5_LANES)
            ar = a_ref[0, d:d + 1, re]
            ai = a_ref[0, d:d + 1, im]
            hr, hi = h_s[:, re], h_s[:, im]
            for t in range(S5_TB):
                tt = t if d == 0 else S5_TB - 1 - t
                rows = slice(tt * N_SEQ, (tt + 1) * N_SEQ)
                pr, pi = _cmul(ar, ai, hr, hi)
                hr = pr + buf[rows, re]
                hi = pi + buf[rows, im]
                buf[rows, re] = hr
                buf[rows, im] = hi
            h_s[:, re] = hr
            h_s[:, im] = hi

    scan(buf_f, hf_s, 0)
    yf_ref[...] = (jnp.dot(buf_f[...].astype(BF16), cblk_s[0], preferred_element_type=F32)
                   + d_ref[0] * uf_ref[...])
    scan(buf_b, hb_s, 1)
    yb_ref[...] = jnp.dot(buf_b[...].astype(BF16), cblk_s[1], preferred_element_type=F32)

    @pl.when(i == pl.num_programs(0) - 1)
    def _():
        hend_ref[0] = hf_s[...]
        hend_ref[1] = hb_s[...]


def _s5_call(layer, u2, b_cmp, c_cmp, a_bar, s5_d):
    nb = CTX_T // S5_TB
    rows = S5_TB * N_SEQ
    return pl.pallas_call(
        _s5_kernel,
        grid=(nb,),
        in_specs=[
            pl.BlockSpec((rows, S5_W), lambda i: (i, 0)),
            pl.BlockSpec((rows, S5_W), lambda i: (nb - 1 - i, 0)),
            pl.BlockSpec((1, 2, S5_W, 256), lambda i: (layer, 0, 0, 0)),
            pl.BlockSpec((1, 2, 2 * S5_P, S5_W), lambda i: (layer, 0, 0, 0)),
            pl.BlockSpec((1, 2, 2 * S5_N), lambda i: (layer, 0, 0)),
            pl.BlockSpec((1, 1, S5_W), lambda i: (layer, 0, 0)),
        ],
        out_specs=[
            pl.BlockSpec((rows, S5_W), lambda i: (i, 0)),
            pl.BlockSpec((rows, S5_W), lambda i: (nb - 1 - i, 0)),
            pl.BlockSpec((2, N_SEQ, 2 * S5_N), lambda i: (0, 0, 0)),
        ],
        out_shape=[
            jax.ShapeDtypeStruct((N_TOK, S5_W), F32),
            jax.ShapeDtypeStruct((N_TOK, S5_W), F32),
            jax.ShapeDtypeStruct((2, N_SEQ, 2 * S5_N), F32),
        ],
        scratch_shapes=[
            pltpu.VMEM((N_SEQ, 2 * S5_N), F32),
            pltpu.VMEM((N_SEQ, 2 * S5_N), F32),
            pltpu.VMEM((rows, 2 * S5_N), F32),
            pltpu.VMEM((rows, 2 * S5_N), F32),
            pltpu.VMEM((2, S5_W, 2 * S5_N), BF16),
            pltpu.VMEM((2, 2 * S5_N, S5_W), BF16),
        ],
        compiler_params=_cparams(("arbitrary",)),
        name="s5_scan",
    )(u2, u2, b_cmp, c_cmp, a_bar, s5_d.reshape(DEPTH, 1, S5_W))


FIX_SEQS = 8


def _s5_fix_kernel(hend_ref, h0_ref, a_ref, c_ref, o_ref, pf_s, pb_s, cf_s, cb_s, cblk_s):
    s = pl.program_id(0)

    @pl.when(s == 0)
    def _():
        for d in range(2):
            cblk_s[d] = _expand_c(c_ref[0, d])
        row8 = lax.broadcasted_iota(I32, (8, S5_N), 0)
        for d, tab in ((0, pf_s), (1, pb_s)):
            ar = a_ref[0, d:d + 1, 0:S5_N]
            ai = a_ref[0, d:d + 1, S5_N:2 * S5_N]
            pr, pi = ar, ai
            r8 = jnp.zeros((8, S5_N), F32)
            i8 = jnp.zeros((8, S5_N), F32)
            for r in range(8):
                if r:
                    pr, pi = _cmul(pr, pi, ar, ai)
                at = r if d == 0 else 7 - r
                r8 = jnp.where(row8 == at, pr, r8)
                i8 = jnp.where(row8 == at, pi, i8)
            base = 0 if d == 0 else TILE - 8
            tab[base:base + 8, 0:S5_N] = r8
            tab[base:base + 8, S5_N:2 * S5_N] = i8
            m = 8
            while m < TILE:
                if d == 0:
                    src, dst, top = slice(0, m), slice(m, 2 * m), slice(m - 1, m)
                else:
                    src, dst, top = slice(TILE - m, TILE), slice(TILE - 2 * m, TILE - m), slice(TILE - m, TILE - m + 1)
                mr, mi = tab[top, 0:S5_N], tab[top, S5_N:2 * S5_N]
                nr, ni = _cmul(tab[src, 0:S5_N], tab[src, S5_N:2 * S5_N], mr, mi)
                tab[dst, 0:S5_N] = nr
                tab[dst, S5_N:2 * S5_N] = ni
                m *= 2

        for b in range(LAT_B):
            ar, ai = pf_s[TILE - 1:TILE, 0:S5_N], pf_s[TILE - 1:TILE, S5_N:2 * S5_N]
            cr, ci = h0_ref[b, 0:1, 0:S5_N], h0_ref[b, 0:1, S5_N:2 * S5_N]
            for j in range(CHUNKS):
                row = b * CHUNKS + j
                cf_s[row:row + 1, 0:S5_N] = cr
                cf_s[row:row + 1, S5_N:2 * S5_N] = ci
                pr, pi = _cmul(ar, ai, cr, ci)
                cr = pr + hend_ref[0, N_CTX + row:N_CTX + row + 1, 0:S5_N]
                ci = pi + hend_ref[0, N_CTX + row:N_CTX + row + 1, S5_N:2 * S5_N]
            ar, ai = pb_s[0:1, 0:S5_N], pb_s[0:1, S5_N:2 * S5_N]
            cr, ci = h0_ref[b, 1:2, 0:S5_N], h0_ref[b, 1:2, S5_N:2 * S5_N]
            for j in reversed(range(CHUNKS)):
                row = b * CHUNKS + j
                cb_s[row:row + 1, 0:S5_N] = cr
                cb_s[row:row + 1, S5_N:2 * S5_N] = ci
                pr, pi = _cmul(ar, ai, cr, ci)
                cr = pr + hend_ref[1, N_CTX + row:N_CTX + row + 1, 0:S5_N]
                ci = pi + hend_ref[1, N_CTX + row:N_CTX + row + 1, S5_N:2 * S5_N]

    acc = None
    for d, tab, car in ((0, pf_s, cf_s), (1, pb_s, cb_s)):
        hs = []
        for k in range(FIX_SEQS):
            cr = car[pl.ds(s * FIX_SEQS + k, 1), 0:S5_N]
            ci = car[pl.ds(s * FIX_SEQS + k, 1), S5_N:2 * S5_N]
            hr, hi = _cmul(tab[:, 0:S5_N], tab[:, S5_N:2 * S5_N], cr, ci)
            hs.append(jnp.concatenate([hr, hi], axis=1).astype(BF16))
        y = jnp.dot(jnp.concatenate(hs, axis=0), cblk_s[d], preferred_element_type=F32)
        acc = y if acc is None else acc + y
    o_ref[...] = acc


def _s5_fix_call(layer, hend, h0, a_bar, c_cmp):
    return pl.pallas_call(
        _s5_fix_kernel,
        grid=(N_LAT // FIX_SEQS,),
        in_specs=[
            pl.BlockSpec((2, N_SEQ, 2 * S5_N), lambda s: (0, 0, 0)),
            pl.BlockSpec((LAT_B, 2, 2 * S5_N), lambda s: (0, 0, 0)),
            pl.BlockSpec((1, 2, 2 * S5_N), lambda s: (layer, 0, 0)),
            pl.BlockSpec((1, 2, 2 * S5_P, S5_W), lambda s: (layer, 0, 0, 0)),
        ],
        out_specs=pl.BlockSpec((FIX_SEQS * TILE, S5_W), lambda s: (s, 0)),
        out_shape=jax.ShapeDtypeStruct((N_LAT * TILE, S5_W), F32),
        scratch_shapes=[
            pltpu.VMEM((TILE, 2 * S5_N), F32),
            pltpu.VMEM((TILE, 2 * S5_N), F32),
            pltpu.VMEM((N_LAT, 2 * S5_N), F32),
            pltpu.VMEM((N_LAT, 2 * S5_N), F32),
            pltpu.VMEM((2, 2 * S5_N, S5_W), BF16),
        ],
        compiler_params=_cparams(("arbitrary",)),
        name="s5_fix",
    )(hend, h0, a_bar, c_cmp)


LRU_ROWS = 1024


def _sigmoid(x):
    return 0.5 * jnp.tanh(0.5 * x) + 0.5


def _lru_kernel(xc_ref, wa_ref, ba_ref, wi_ref, bi_ref, sp_ref, h0_ref, out_ref, hend_ref,
                a_s, h_s, p_s):
    row = lax.broadcasted_iota(I32, (N_SEQ, LRU_W), 0)
    for d in range(2):
        def gates(cix, _, d=d):
            r0 = pl.multiple_of(cix * LRU_ROWS, LRU_ROWS)
            xc = xc_ref[pl.ds(r0, LRU_ROWS), :]
            xb = xc.astype(BF16)
            r = _sigmoid(jnp.dot(xb, wa_ref[0, d], preferred_element_type=F32) + ba_ref[0, d])
            g = _sigmoid(jnp.dot(xb, wi_ref[0, d], preferred_element_type=F32) + bi_ref[0, d])
            log_a = (-LRU_C) * r * sp_ref[0, d]
            a_s[pl.ds(r0, LRU_ROWS), :] = jnp.exp(log_a)
            th = jnp.tanh(log_a)
            h_s[pl.ds(r0, LRU_ROWS), :] = jnp.sqrt(-2.0 * th / (1.0 - th)) * g * xc
            return 0

        lax.fori_loop(0, N_TOK // LRU_ROWS, gates, 0)

        def scan(t, carry, d=d):
            h, p = carry
            tt = t if d == 0 else TILE - 1 - t
            r0 = pl.multiple_of(tt * N_SEQ, N_SEQ)
            a = a_s[pl.ds(r0, N_SEQ), :]
            h = a * h + h_s[pl.ds(r0, N_SEQ), :]
            p = a * p
            h_s[pl.ds(r0, N_SEQ), :] = h
            p_s[pl.ds(r0, N_SEQ), :] = p
            return h, p

        h_end, p_end = lax.fori_loop(0, TILE, scan,
                                     (jnp.zeros((N_SEQ, LRU_W), F32), jnp.ones((N_SEQ, LRU_W), F32)),
                                     unroll=8)
        hend_ref[d] = h_end

        carry_slab = jnp.zeros((N_SEQ, LRU_W), F32)
        for b in range(LAT_B):
            c = h0_ref[0, b, d:d + 1, :]
            order = range(CHUNKS) if d == 0 else reversed(range(CHUNKS))
            for j in order:
                s = N_CTX + b * CHUNKS + j
                carry_slab = jnp.where(row == s, c, carry_slab)
                c = h_end[s:s + 1, :] + p_end[s:s + 1, :] * c

        def fix(t, _, d=d, carry_slab=carry_slab):
            r0 = pl.multiple_of(t * N_SEQ, N_SEQ)
            v = h_s[pl.ds(r0, N_SEQ), :] + p_s[pl.ds(r0, N_SEQ), :] * carry_slab
            if d == 0:
                out_ref[pl.ds(r0, N_SEQ), :] = v
            else:
                out_ref[pl.ds(r0, N_SEQ), :] += v
            return 0

        lax.fori_loop(0, TILE, fix, 0, unroll=8)


def _lru_call(layer, xc2, wa, ba, wi, bi, sp, h0):
    full = lambda shape: pl.BlockSpec(shape, lambda i: (0,) * len(shape))
    per_layer = lambda shape: pl.BlockSpec((1,) + shape, lambda i: (layer,) + (0,) * len(shape))
    return pl.pallas_call(
        _lru_kernel,
        grid=(1,),
        in_specs=[
            full((N_TOK, LRU_W)),
            per_layer((2, LRU_W, LRU_W)),
            per_layer((2, 1, LRU_W)),
            per_layer((2, LRU_W, LRU_W)),
            per_layer((2, 1, LRU_W)),
            per_layer((2, 1, LRU_W)),
            per_layer((LAT_B, 2, LRU_W)),
        ],
        out_specs=[full((N_TOK, LRU_W)), full((2, N_SEQ, LRU_W))],
        out_shape=[
            jax.ShapeDtypeStruct((N_TOK, LRU_W), F32),
            jax.ShapeDtypeStruct((2, N_SEQ, LRU_W), F32),
        ],
        scratch_shapes=[pltpu.VMEM((N_TOK, LRU_W), F32)] * 3,
        compiler_params=_cparams(("arbitrary",)),
        name="rglru",
    )(xc2, wa, ba, wi, bi, sp, h0)


def _layer_norm(z, g, b):
    mu = jnp.mean(z, axis=-1, keepdims=True)
    zc = z - mu
    var = jnp.mean(zc * zc, axis=-1, keepdims=True)
    return zc * lax.rsqrt(var + EPS) * g + b


def _route(logits):
    lane_i = lax.broadcasted_iota(I32, logits.shape, 1)
    lane = lane_i.astype(F32)
    big = jnp.float32(1024.0)
    neg = jnp.float32(-jnp.inf)
    is_g = lane_i < N_GROUPS
    gmax = jnp.max(jnp.where(is_g, logits, neg), axis=-1, keepdims=True)
    g_sel = jnp.min(jnp.where(jnp.logical_and(is_g, logits == gmax), lane, big), axis=-1, keepdims=True)
    p_group = 1.0 / jnp.sum(jnp.where(is_g, jnp.exp(logits - gmax), 0.0), axis=-1, keepdims=True)
    e_idx = lane_i - GATE_LANE0
    e_group = (e_idx >> 2).astype(F32)
    in_g = jnp.logical_and(jnp.logical_and(e_idx >= 0, e_idx < N_EXPERTS), e_group == g_sel)
    v1 = jnp.max(jnp.where(in_g, logits, neg), axis=-1, keepdims=True)
    i1 = jnp.min(jnp.where(jnp.logical_and(in_g, logits == v1), lane, big), axis=-1, keepdims=True)
    rest = jnp.logical_and(in_g, lane != i1)
    v2 = jnp.max(jnp.where(rest, logits, neg), axis=-1, keepdims=True)
    i2 = jnp.min(jnp.where(jnp.logical_and(rest, logits == v2), lane, big), axis=-1, keepdims=True)
    e2 = jnp.exp(v2 - v1)
    inv = 1.0 / (1.0 + e2)
    w1 = inv * p_group
    w2 = e2 * inv * p_group
    return jnp.where(lane == i1, w1, jnp.where(lane == i2, w2, jnp.where(lane_i == 0, g_sel, 0.0)))


def _outproj_kernel(x_ref, oc_ref, ol_ref, yf_ref, yb_ref, yfix_ref, hl_ref, g_ref, mod_ref,
                    wglu_ref, bglu_ref, wout_ref, lng_ref, lnb_ref, rw_ref, rwhi_ref, rb_ref,
                    x1_ref, hm_ref, gates_ref, y_s, hl_s):
    i = pl.program_id(0)
    s = pl.program_id(1)

    @pl.when(s == 0)
    def _():
        for j in range(GROUP_SEQS):
            rows = slice(j * TILE, (j + 1) * TILE)
            y_s[rows, :] = yf_ref[:, j, :] + yb_ref[:, j, :]
            hl_s[rows, :] = hl_ref[:, j, :]

    lat = i >= CTX_GROUP_TILES
    sub_rows = pl.ds(pl.multiple_of(s * SUB_ROWS, SUB_ROWS), SUB_ROWS)
    o_att = jnp.where(lat, ol_ref[...], oc_ref[...])
    y = y_s[sub_rows, :] + jnp.where(lat, yfix_ref[...], 0.0)
    g = jax.nn.gelu(y, approximate=True)
    glu = jnp.dot(g.astype(BF16), wglu_ref[0], preferred_element_type=F32) + bglu_ref[0]
    o_s5 = g * jax.nn.sigmoid(glu)
    o_lru = hl_s[sub_rows, :] * jax.nn.gelu(g_ref[...], approximate=True)
    mix = jnp.concatenate([o_att, o_s5.astype(BF16), o_lru.astype(BF16)], axis=-1)
    out = jnp.dot(mix, wout_ref[0], preferred_element_type=F32)
    r = _group_mod_row(i)
    g1 = mod_ref[0, pl.ds(r, 1), 2 * D:3 * D]
    sh2 = mod_ref[0, pl.ds(r, 1), 3 * D:4 * D]
    sc2 = mod_ref[0, pl.ds(r, 1), 4 * D:5 * D]
    x1 = _layer_norm(ALPHA * x_ref[...] + g1 * out, lng_ref[0], lnb_ref[0])
    x1_ref[...] = x1
    hm = x1 * (1.0 + sc2) + sh2
    hm_ref[...] = hm.astype(BF16)
    hm_hi = hm.astype(BF16)
    hm_lo = (hm - hm_hi.astype(F32)).astype(BF16)
    p_hi = jnp.dot(hm_hi, rw_ref[0], preferred_element_type=F32)
    p_lo = jnp.dot(hm_lo, rwhi_ref[0], preferred_element_type=F32)
    logits = p_hi[:, 0:128] + p_hi[:, 128:256] + p_lo + rb_ref[0]
    gates_ref[...] = _route(logits)


def _outproj_call(layer, x, o_ctx, o_lat, yf, yb, yfix, hl, g_lru, mod, w_glu, b_glu,
                  w_out, ln_g, ln_b, rw_split, rw_hi, rb):
    n_ctx_blocks = N_CTX * TILE // SUB_ROWS
    row_spec = lambda w: pl.BlockSpec((SUB_ROWS, w), lambda i, s: (i * SUBS + s, 0))
    lat_spec = lambda w: pl.BlockSpec((SUB_ROWS, w), lambda i, s: (jnp.maximum(i * SUBS + s - n_ctx_blocks, 0), 0))
    tm_spec = pl.BlockSpec((TILE, GROUP_SEQS, 256), lambda i, s: (0, i, 0))
    vec = lambda n: pl.BlockSpec((1, 1, n), lambda i, s: (layer, 0, 0))
    mat = lambda a, b: pl.BlockSpec((1, a, b), lambda i, s: (layer, 0, 0))
    return pl.pallas_call(
        _outproj_kernel,
        grid=(N_GROUP_TILES, SUBS),
        in_specs=[
            row_spec(D),
            pl.BlockSpec((SUB_ROWS, 512), lambda i, s: (jnp.minimum(i * SUBS + s, n_ctx_blocks - 1), 0)),
            lat_spec(512),
            tm_spec, tm_spec,
            lat_spec(S5_W),
            tm_spec,
            row_spec(LRU_W),
            mat(MOD_ROWS, N_MOD * D),
            mat(S5_W, S5_W), vec(S5_W),
            mat(D, D), vec(D), vec(D),
            mat(D, 256), mat(D, 128), vec(128),
        ],
        out_specs=[row_spec(D), row_spec(D), row_spec(128)],
        out_shape=[
            jax.ShapeDtypeStruct((N_TOK, D), F32),
            jax.ShapeDtypeStruct((N_TOK, D), BF16),
            jax.ShapeDtypeStruct((N_TOK, 128), F32),
        ],
        scratch_shapes=[pltpu.VMEM((GROUP_ROWS, 256), F32)] * 2,
        compiler_params=_cparams(("arbitrary", "arbitrary")),
        name="outproj",
    )(x, o_ctx, o_lat, yf.reshape(TILE, N_SEQ, S5_W), yb.reshape(TILE, N_SEQ, S5_W), yfix,
      hl.reshape(TILE, N_SEQ, LRU_W), g_lru, mod, w_glu,
      b_glu.reshape(DEPTH, 1, S5_W), w_out, ln_g.reshape(DEPTH, 1, D), ln_b.reshape(DEPTH, 1, D),
      rw_split, rw_hi, rb)


HALF = N_TOK // 2
HALF_TILES = HALF // TILE


def _sort_kernel(gates_ref, pos_ref, cnt_ref, rank_s):
    sub = lax.broadcasted_iota(I32, (8, TILE), 0)
    sel_r = lax.broadcasted_iota(I32, (8, 128), 0)
    sel_c = lax.broadcasted_iota(I32, (8, 128), 1)
    pick_lane0 = jnp.where(jnp.logical_and(sel_r == 0, sel_c == 0), 1.0, 0.0).astype(BF16)
    ri = lax.broadcasted_iota(I32, (TILE, TILE), 0)
    ci = lax.broadcasted_iota(I32, (TILE, TILE), 1)
    upper = jnp.where(ri <= ci, 1.0, 0.0).astype(BF16)

    def count(b, carry):
        rows = slice(b * TILE, (b + 1) * TILE)
        g = gates_ref[rows, :].astype(BF16)
        g_t = lax.dot_general(pick_lane0, g, _NT, preferred_element_type=F32)
        g_sel = jnp.sum(g_t, axis=0, keepdims=True)
        onehot = jnp.where(jnp.logical_and(sub < N_GROUPS, sub.astype(F32) == g_sel), 1.0, 0.0)
        cum = jnp.dot(onehot.astype(BF16), upper, preferred_element_type=F32) + carry
        rank_s[b] = jnp.where(onehot > 0.0, cum, 0.0)
        return carry + jnp.sum(onehot, axis=1, keepdims=True)

    counts = jnp.zeros((8, 1), F32)
    for b in range(HALF_TILES):
        counts = count(b, counts)
    sub1 = lax.broadcasted_iota(I32, (8, 1), 0)
    c = [jnp.sum(jnp.where(sub1 == g, counts, 0.0), axis=0, keepdims=True) for g in range(N_GROUPS - 1)]
    start = jnp.where(sub1 == 1, c[0], jnp.where(sub1 == 2, c[0] + c[1],
                      jnp.where(sub1 == 3, c[0] + c[1] + c[2], 0.0)))

    for b in range(HALF_TILES):
        rk = rank_s[b]
        p = jnp.sum(jnp.where(rk > 0.0, rk + start - 1.0, 0.0), axis=0, keepdims=True)
        pos_ref[b:b + 1, :] = p.astype(I32)
    cnt_ref[0] = jnp.broadcast_to(counts, (8, 128))


def _sort_call(gates):
    return pl.pallas_call(
        _sort_kernel,
        grid=(2,),
        in_specs=[pl.BlockSpec((HALF, 128), lambda h: (h, 0))],
        out_specs=[pl.BlockSpec((HALF_TILES, TILE), lambda h: (h, 0)),
                   pl.BlockSpec((1, 8, 128), lambda h: (h, 0, 0))],
        out_shape=[jax.ShapeDtypeStruct((N_SEQ, TILE), I32), jax.ShapeDtypeStruct((2, 8, 128), F32)],
        scratch_shapes=[pltpu.VMEM((HALF_TILES, 8, TILE), F32)],
        compiler_params=_cparams(("arbitrary",)),
        name="group_sort",
    )(gates)


ITEM_TILES = 2
N_ITEMS = N_EXPERTS * (HALF_TILES + 2 * N_GROUPS) // (EPG * ITEM_TILES)
SCATTER_ROWS = 2 * TILE
SCATTER_STEPS = HALF // SCATTER_ROWS
MOE_STEPS = SCATTER_STEPS + N_ITEMS + HALF_TILES


def _moe_schedule(counts):
    start = jnp.cumsum(counts, axis=1) - counts
    lo = start // TILE
    hi = (start + counts + TILE - 1) // TILE
    tiles_g = jnp.where(counts > 0, hi - lo, 0)
    n_g = (tiles_g + ITEM_TILES - 1) // ITEM_TILES
    n_e = jnp.repeat(n_g, EPG, axis=1)
    lo_e = jnp.repeat(lo, EPG, axis=1)
    tiles_e = jnp.repeat(tiles_g, EPG, axis=1)
    off_end = jnp.cumsum(n_e, axis=1)
    off = off_end - n_e
    total = off_end[:, -1:]
    w = jnp.arange(N_ITEMS, dtype=I32)[None, :]
    w_eff = jnp.minimum(w, total - 1)
    e_w = jnp.sum((w_eff[:, :, None] >= off_end[:, None, :]).astype(I32), axis=-1)
    first = ITEM_TILES * (w_eff - jnp.take_along_axis(off, e_w, axis=1))
    tile = jnp.take_along_axis(lo_e, e_w, axis=1) + first
    size = jnp.minimum(jnp.take_along_axis(tiles_e, e_w, axis=1) - first, ITEM_TILES)
    size = jnp.where(w < total, size, 0)
    e_f, n_f = e_w.reshape(-1), size.reshape(-1)
    n_all = 2 * N_ITEMS
    idx = jnp.arange(n_all, dtype=I32)
    key = (idx // N_ITEMS) * N_EXPERTS + e_f
    run_start = jnp.logical_and(n_f > 0, key != jnp.concatenate([jnp.full((1,), -1, I32), key[:-1]]))
    slot = (jnp.cumsum(run_start.astype(I32)) - 1) % 2
    at_or_after = lax.cummin(jnp.where(run_start, idx, n_all), reverse=True)
    after = jnp.concatenate([at_or_after[1:], jnp.full((1,), n_all, I32)])
    next_e = jnp.where(after < n_all, e_f[jnp.minimum(after, n_all - 1)], -1)
    return e_f, tile.reshape(-1), n_f, run_start.astype(I32), slot, next_e


def _moe_kernel(pos_ref, ite_ref, itt_ref, itv_ref, itf_ref, its_ref, itx_ref,
                hm_ref, gates_ref, wg_hbm, wu_hbm, wd_hbm, x1_ref, mod_ref, lng_ref, lnb_ref,
                *rest, split, layer):
    out_refs, (xs_s, gs_s, acc_s, hm_s, wg_b, wu_b, wd_b, w_sem) = rest[:-8], rest[-8:]
    h = pl.program_id(0)
    w = pl.program_id(1)
    base = h * HALF

    def weight_copies(e, slot):
        pairs = ((wg_hbm, wg_b), (wu_hbm, wu_b), (wd_hbm, wd_b))
        return [pltpu.make_async_copy(src.at[layer, e], dst.at[slot], w_sem.at[slot, k])
                for k, (src, dst) in enumerate(pairs)]

    @pl.when(w < SCATTER_STEPS)
    def _():
        @pl.when(w == 0)
        def _():
            acc_s[...] = jnp.zeros_like(acc_s)

        @pl.when(jnp.logical_and(h == 0, w == 0))
        def _():
            for c in weight_copies(ite_ref[0], 0):
                c.start()

        hm_s[...] = hm_ref[...].astype(F32)

        def body(r, c):
            p = pos_ref[base + w * SCATTER_ROWS + r]
            xs_s[pl.ds(p, 1), :] = hm_s[pl.ds(r, 1), :]
            gs_s[pl.ds(p, 1), :] = gates_ref[pl.ds(r, 1), :]
            return c

        lax.fori_loop(0, SCATTER_ROWS, body, 0, unroll=32)

    @pl.when(jnp.logical_and(w >= SCATTER_STEPS, w < SCATTER_STEPS + N_ITEMS))
    def _():
        idx = h * N_ITEMS + (w - SCATTER_STEPS)

        slot = its_ref[idx]

        @pl.when(itf_ref[idx] > 0)
        def _():
            @pl.when(itx_ref[idx] >= 0)
            def _():
                for c in weight_copies(itx_ref[idx], 1 - slot):
                    c.start()

            for c in weight_copies(ite_ref[idx], slot):
                c.wait()

        def item(n_tiles):
            e = ite_ref[idx]
            rows = pl.ds(pl.multiple_of(itt_ref[idx] * TILE, TILE), n_tiles * TILE)
            x = xs_s[rows, :].astype(BF16)
            a = jnp.dot(x, wg_b[slot].astype(BF16), preferred_element_type=F32)
            u = jnp.dot(x, wu_b[slot].astype(BF16), preferred_element_type=F32)
            g = gs_s[rows, :]
            lane = lax.broadcasted_iota(I32, g.shape, 1)
            ge = jnp.sum(jnp.where(lane == e + GATE_LANE0, g, 0.0), axis=-1, keepdims=True)
            act = ((a * jax.nn.sigmoid(a)) * u * ge).astype(BF16)
            acc_s[rows, :] += jnp.dot(act, wd_b[slot].astype(BF16), preferred_element_type=F32)

        for n_tiles in range(1, ITEM_TILES + 1):
            pl.when(itv_ref[idx] == n_tiles)(functools.partial(item, n_tiles))

    @pl.when(w >= SCATTER_STEPS + N_ITEMS)
    def _():
        j = w - SCATTER_STEPS - N_ITEMS

        def finish(o_ref):
            def body(r, c):
                p = pos_ref[base + j * TILE + r]
                o_ref[pl.ds(r, 1), :] = acc_s[pl.ds(p, 1), :]
                return c

            lax.fori_loop(0, TILE, body, 0, unroll=32)
            r = jnp.where(h == 0, 0, 1 + j // CHUNKS)
            g2 = mod_ref[0, pl.ds(r, 1), 5 * D:6 * D]
            o_ref[...] = _layer_norm(ALPHA * x1_ref[...] + g2 * o_ref[...], lng_ref[0], lnb_ref[0])

        if split:
            for half, o_ref in enumerate(out_refs):
                pl.when(h == half)(functools.partial(finish, o_ref))
        else:
            finish(out_refs[0])


def _moe_call(layer, pos, items, hm, gates, w_gate, w_up, w_down, x1, mod, ln_g, ln_b, split):
    def in_tile(h, w, *_):
        return (h * SCATTER_STEPS + jnp.minimum(w, SCATTER_STEPS - 1), 0)

    def out_step(w):
        return jnp.clip(w - SCATTER_STEPS - N_ITEMS, 0, HALF_TILES - 1)

    def out_tile(h, w, *_):
        return (h * HALF_TILES + out_step(w), 0)

    if split:
        out_specs = [pl.BlockSpec((TILE, D), lambda h, w, *_: (jnp.where(h == 0, out_step(w), HALF_TILES - 1), 0)),
                     pl.BlockSpec((TILE, D), lambda h, w, *_: (jnp.where(h == 1, out_step(w), 0), 0))]
        out_shape = [jax.ShapeDtypeStruct((HALF, D), F32)] * 2
    else:
        out_specs = pl.BlockSpec((TILE, D), out_tile)
        out_shape = jax.ShapeDtypeStruct((N_TOK, D), F32)

    vec = lambda n: pl.BlockSpec((1, 1, n), lambda h, w, *_: (layer, 0, 0))
    grid_spec = pltpu.PrefetchScalarGridSpec(
        num_scalar_prefetch=1 + len(items),
        grid=(2, MOE_STEPS),
        in_specs=[
            pl.BlockSpec((SCATTER_ROWS, D), in_tile),
            pl.BlockSpec((SCATTER_ROWS, 128), in_tile),
            pl.BlockSpec(memory_space=pl.ANY),
            pl.BlockSpec(memory_space=pl.ANY),
            pl.BlockSpec(memory_space=pl.ANY),
            pl.BlockSpec((TILE, D), out_tile),
            pl.BlockSpec((1, MOD_ROWS, N_MOD * D), lambda h, w, *_: (layer, 0, 0)),
            vec(D), vec(D),
        ],
        out_specs=out_specs,
        scratch_shapes=[
            pltpu.VMEM((HALF, D), F32),
            pltpu.VMEM((HALF, 128), F32),
            pltpu.VMEM((HALF, D), F32),
            pltpu.VMEM((SCATTER_ROWS, D), F32),
            pltpu.VMEM((2, D, D_EXPERT), F32),
            pltpu.VMEM((2, D, D_EXPERT), F32),
            pltpu.VMEM((2, D_EXPERT, D), F32),
            pltpu.SemaphoreType.DMA((2, 3)),
        ],
    )
    return pl.pallas_call(
        functools.partial(_moe_kernel, split=split, layer=layer),
        grid_spec=grid_spec,
        out_shape=out_shape,
        compiler_params=pltpu.CompilerParams(dimension_semantics=("arbitrary", "arbitrary"),
                                             vmem_limit_bytes=MOE_VMEM_LIMIT),
        name="moe",
    )(pos, *items, hm, gates, w_gate, w_up, w_down, x1, mod,
      ln_g.reshape(DEPTH, 1, D), ln_b.reshape(DEPTH, 1, D))


def _rope_tables():
    rows = LAT_T // GRID_W
    row = jnp.repeat(jnp.arange(rows, dtype=F32), GRID_W)
    col = jnp.tile(jnp.arange(GRID_W, dtype=F32), rows)
    n_freq = QK // 4
    inv = ROPE_BASE ** (-jnp.arange(n_freq, dtype=F32) / n_freq)
    ang_r = row[:, None] * inv
    ang_c = col[:, None] * inv
    cos64 = jnp.concatenate([jnp.cos(ang_r), jnp.cos(ang_r), jnp.cos(ang_c), jnp.cos(ang_c)], axis=1)
    sin64 = jnp.concatenate([-jnp.sin(ang_r), jnp.sin(ang_r), -jnp.sin(ang_c), jnp.sin(ang_c)], axis=1)
    cos = jnp.concatenate([jnp.tile(cos64, (1, 2)), jnp.ones((SUB_ROWS, 128), F32)], axis=0)
    sin = jnp.concatenate([jnp.tile(sin64, (1, 2)), jnp.zeros((SUB_ROWS, 128), F32)], axis=0)
    return cos, sin


def _s5_params(a_re, a_im, b_re, b_im, c_re, c_im, log_dt):
    dt = jnp.exp(log_dt)[..., None]
    mag = jnp.exp(a_re * dt)
    abar_r = mag * jnp.cos(a_im * dt)
    abar_i = mag * jnp.sin(a_im * dt)
    den = a_re * a_re + a_im * a_im
    nr = abar_r - 1.0
    coef_r = (nr * a_re + abar_i * a_im) / den
    coef_i = (abar_i * a_re - nr * a_im) / den
    bbar_r = coef_r[..., None] * b_re - coef_i[..., None] * b_im
    bbar_i = coef_r[..., None] * b_im + coef_i[..., None] * b_re
    def b_rows(bb):
        return bb.transpose(0, 1, 2, 4, 3).reshape(DEPTH, 2, S5_W, S5_P)
    br, bi = b_rows(bbar_r), b_rows(bbar_i)
    b_cmp = jnp.concatenate([br, br, bi, bi], axis=-1)
    def c_rows(cc):
        return cc.transpose(0, 1, 4, 2, 3).reshape(DEPTH, 2, S5_P, S5_W)
    c_cmp = jnp.concatenate([c_rows(c_re), c_rows(-c_im)], axis=-2)
    a_bar = jnp.concatenate([abar_r.reshape(DEPTH, 2, S5_N), abar_i.reshape(DEPTH, 2, S5_N)], axis=-1)
    return b_cmp, c_cmp, a_bar


def _block_diag(w):
    eye = jnp.eye(LRU_BLOCKS, dtype=F32)
    m = jnp.einsum('ldkij,kh->ldkihj', w, eye)
    return m.reshape(DEPTH, 2, LRU_W, LRU_W).astype(BF16)


def kernel(x_prompt, x_sample, cache_k, cache_v, state_s5, state_lru, c, c_ctx, w_ada, b_ada, w_in, w_out, lam_q1, lam_k1, lam_q2, lam_k2, subln_g, s5_a_re, s5_a_im, s5_b_re, s5_b_im, s5_c_re, s5_c_im, s5_log_dt, s5_d, s5_w_glu, s5_b_glu, lru_conv_w, lru_conv_b, lru_w_a, lru_b_a, lru_w_i, lru_b_i, lru_lambda, ln1_g, ln1_b, ln2_g, ln2_b, router_group_w, router_group_b, router_expert_w, router_expert_b, moe_w_gate, moe_w_up, moe_w_down):
    x = jnp.concatenate([x_prompt.reshape(CTX_B * CTX_T, D), x_sample.reshape(LAT_B * LAT_T, D)], axis=0)
    cond = jnp.concatenate([c_ctx[None, :], c, jnp.zeros((MOD_ROWS - 1 - LAT_B, D), F32)], axis=0)
    mod = _ada_call(cond, w_ada, b_ada)

    w_out_b = w_out.astype(BF16)
    w_glu_b = s5_w_glu.astype(BF16)
    rope_cos, rope_sin = _rope_tables()
    b_cmp, c_cmp, a_bar = _s5_params(s5_a_re, s5_a_im, s5_b_re, s5_b_im, s5_c_re, s5_c_im, s5_log_dt)
    wa_blk = _block_diag(lru_w_a)
    wi_blk = _block_diag(lru_w_i)
    ba = lru_b_a.reshape(DEPTH, 2, 1, LRU_W)
    bi = lru_b_i.reshape(DEPTH, 2, 1, LRU_W)
    sp = jax.nn.softplus(-lru_lambda).reshape(DEPTH, 2, 1, LRU_W)
    lam = (jnp.exp(jnp.sum(lam_q1 * lam_k1, axis=-1)) - jnp.exp(jnp.sum(lam_q2 * lam_k2, axis=-1)))
    rw = jnp.concatenate([router_group_w, router_expert_w,
                          jnp.zeros((DEPTH, D, 128 - N_GROUPS - N_EXPERTS), F32)], axis=-1)
    rw_hi = rw.astype(BF16)
    rw_split = jnp.concatenate([rw_hi, (rw - rw_hi.astype(F32)).astype(BF16)], axis=-1)
    rb = jnp.concatenate([router_group_b, router_expert_b,
                          jnp.zeros((DEPTH, 128 - N_GROUPS - N_EXPERTS), F32)], axis=-1).reshape(DEPTH, 1, 128)
    s5_h0 = state_s5.reshape(LAT_B, DEPTH, 2, 2 * S5_N)
    lru_h0 = state_lru.transpose(1, 0, 2, 3)

    kc = jnp.zeros((CTX_B, DEPTH, HEADS, CTX_T, 2 * QK), F32)
    vc = jnp.zeros((CTX_B, DEPTH, HEADS, CTX_T, DV), F32)
    s5_states = []
    lru_states = []
    for l in range(DEPTH):
        lambda_init = 0.8 - 0.6 * math.exp(-0.3 * l)
        lam_l = (lam[l] + lambda_init).reshape(1)
        q, k, v, kc, vc, u_tm, xc_tm, g_lru = _inproj_call(
            l, x, mod, w_in, rope_cos, rope_sin, lru_conv_w, lru_conv_b, kc, vc)
        o_ctx = _att_ctx_call(l, lam_l, q, k, v, subln_g, 1.0 - lambda_init)
        o_lat = _att_lat_call(l, lam_l, q, k, v, cache_k, cache_v, subln_g, 1.0 - lambda_init)
        yf, yb, hend = _s5_call(l, u_tm.reshape(N_TOK, S5_W), b_cmp, c_cmp, a_bar, s5_d)
        yfix = _s5_fix_call(l, hend, s5_h0[:, l], a_bar, c_cmp)
        hl, lru_end = _lru_call(l, xc_tm.reshape(N_TOK, LRU_W), wa_blk, ba, wi_blk, bi, sp, lru_h0)
        x1, hm, gates = _outproj_call(l, x, o_ctx, o_lat, yf, yb, yfix, hl, g_lru, mod,
                                      w_glu_b, s5_b_glu, w_out_b, ln1_g, ln1_b, rw_split, rw_hi, rb)
        pos, counts = _sort_call(gates)
        items = _moe_schedule(counts[:, :N_GROUPS, 0].astype(I32))
        x = _moe_call(l, pos.reshape(N_TOK), items, hm, gates, moe_w_gate, moe_w_up,
                      moe_w_down, x1, mod, ln2_g, ln2_b, split=(l == DEPTH - 1))
        s5_states.append(hend[:, :N_CTX].reshape(2, CTX_B, 2, S5_G, S5_P).transpose(1, 0, 2, 3, 4))
        lru_states.append(lru_end[:, :N_CTX].transpose(1, 0, 2))

    y_p = x[0].reshape(CTX_B, CTX_T, D)
    y_s = x[1].reshape(LAT_B, LAT_T, D)
    return (y_p, y_s, kc, vc, jnp.stack(s5_states, axis=1), jnp.stack(lru_states, axis=1))
```

```python
import functools
import math

import jax
import jax.numpy as jnp
from jax import lax
from jax.experimental import pallas as pl
from jax.experimental.pallas import tpu as pltpu

F32 = jnp.float32
BF16 = jnp.bfloat16
I32 = jnp.int32

D = 1024
DEPTH = 4
CTX_B = 16
CTX_T = 256
LAT_B = 2
LAT_T = 2048
PAST = 256
GRID_W = 64
HEADS = 4
QK = 64
DV = 128
S5_W = 256
S5_G = 16
S5_C = 16
S5_P = 64
S5_N = S5_G * S5_P
S5_C_SHIFT = S5_C.bit_length() - 1
S5_P_SHIFT = S5_P.bit_length() - 1
LRU_W = 256
LRU_BLOCKS = 4
LRU_C = 8.0
N_GROUPS = 4
EPG = 4
N_EXPERTS = 16
D_EXPERT = 512
N_MOD = 6
IN_W = 2304
ROPE_BASE = 10000.0
ALPHA = (2 * DEPTH) ** 0.25
EPS = 1e-5

TILE = 256
N_TOK = CTX_B * CTX_T + LAT_B * LAT_T
N_SEQ = N_TOK // TILE
N_CTX = CTX_B
CHUNKS = LAT_T // TILE
N_LAT = LAT_B * CHUNKS
MOD_ROWS = 8
GATE_LANE0 = 4
VMEM_LIMIT = 56 * 1024 * 1024
MOE_VMEM_LIMIT = 60 * 1024 * 1024

GROUP_SEQS = 8
GROUP_ROWS = GROUP_SEQS * TILE
N_GROUP_TILES = N_TOK // GROUP_ROWS
CTX_GROUP_TILES = N_CTX // GROUP_SEQS
SUB_ROWS = 512
SUBS = GROUP_ROWS // SUB_ROWS
SUB_SEQS = SUB_ROWS // TILE


def _cparams(sem):
    return pltpu.CompilerParams(dimension_semantics=sem, vmem_limit_bytes=VMEM_LIMIT)


def _group_mod_row(i):
    return jnp.where(i < CTX_GROUP_TILES, 0, i - CTX_GROUP_TILES + 1)


def _ada_kernel(c_ref, w_ref, b_ref, o_ref):
    c = c_ref[...]
    s = (c * jax.nn.sigmoid(c)).astype(BF16)
    o_ref[0] = jnp.dot(s, w_ref[0].astype(BF16), preferred_element_type=F32) + b_ref[0]


def _ada_call(cond, w_ada, b_ada):
    tn = 1536
    return pl.pallas_call(
        _ada_kernel,
        grid=(DEPTH, N_MOD * D // tn),
        in_specs=[
            pl.BlockSpec((MOD_ROWS, D), lambda l, j: (0, 0)),
            pl.BlockSpec((1, D, tn), lambda l, j: (l, 0, j)),
            pl.BlockSpec((1, 1, tn), lambda l, j: (l, 0, j)),
        ],
        out_specs=pl.BlockSpec((1, MOD_ROWS, tn), lambda l, j: (l, 0, j)),
        out_shape=jax.ShapeDtypeStruct((DEPTH, MOD_ROWS, N_MOD * D), F32),
        compiler_params=_cparams(("arbitrary", "arbitrary")),
        name="adaln",
    )(cond, w_ada, b_ada.reshape(DEPTH, 1, N_MOD * D))


def _inproj_kernel(x_ref, mod_ref, w_ref, cos_ref, sin_ref, cw_ref, cb_ref, kc_in, vc_in,
                   q_ref, k_ref, v_ref, kc_ref, vc_ref, u_ref, xc_ref, g_ref, u_s, xl_s, w_s):
    del kc_in, vc_in
    i = pl.program_id(0)
    s = pl.program_id(1)

    @pl.when(jnp.logical_and(i == 0, s == 0))
    def _():
        for t in range(2 * HEADS):
            kind, head = divmod(t, HEADS)
            halves = []
            for m in range(2):
                col = (2 * kind + m) * HEADS * QK + head * QK
                blk = w_ref[0, :, (col // 128) * 128:(col // 128 + 1) * 128]
                halves.append(blk[:, col % 128:col % 128 + QK])
            w_s[:, t * 128:(t + 1) * 128] = jnp.concatenate(halves, axis=1).astype(BF16)
        w_s[:, 1024:IN_W] = w_ref[0, :, 1024:IN_W].astype(BF16)

    r = _group_mod_row(i)
    sh = mod_ref[0, pl.ds(r, 1), 0:D]
    sc = mod_ref[0, pl.ds(r, 1), D:2 * D]
    xm = (x_ref[...] * (1.0 + sc) + sh).astype(BF16)
    proj = jnp.dot(xm, w_s[...], preferred_element_type=F32)

    qk = proj[:, 0:1024]
    cos = jnp.concatenate([cos_ref[...]] * 8, axis=1)
    sin = jnp.concatenate([sin_ref[...]] * 8, axis=1)
    lane = lax.broadcasted_iota(I32, qk.shape, 1)
    swapped = jnp.where((lane & 31) < 16, pltpu.roll(qk, 1024 - 16, 1), pltpu.roll(qk, 16, 1))
    qk = qk * cos + swapped * sin
    q_ref[...] = qk[:, 0:512].astype(BF16)
    k_ref[...] = qk[:, 512:1024].astype(BF16)
    v = proj[:, 1024:1536]
    v_ref[...] = v.astype(BF16)

    @pl.when(i < CTX_GROUP_TILES)
    def _():
        for jj in range(SUB_SEQS):
            rows = slice(jj * TILE, (jj + 1) * TILE)
            for h in range(HEADS):
                kc_ref[jj, 0, h] = qk[rows, 512 + h * 128:512 + (h + 1) * 128]
                vc_ref[jj, 0, h] = v[rows, h * 128:(h + 1) * 128]

    g_ref[...] = proj[:, 2048:2304]
    sub_rows = pl.ds(pl.multiple_of(s * SUB_ROWS, SUB_ROWS), SUB_ROWS)
    u_s[sub_rows, :] = proj[:, 1536:1792]
    xl_s[sub_rows, :] = proj[:, 1792:2048]

    @pl.when(s == SUBS - 1)
    def _():
        xl = xl_s[...]
        row = lax.broadcasted_iota(I32, xl.shape, 0)
        is_ctx = i < CTX_GROUP_TILES
        pos = jnp.where(is_ctx, row & (TILE - 1), row)
        last = jnp.where(is_ctx, TILE - 1, GROUP_ROWS - 1)
        x_m1 = jnp.where(pos == 0, 0.0, pltpu.roll(xl, 1, 0))
        x_p1 = jnp.where(pos == last, 0.0, pltpu.roll(xl, GROUP_ROWS - 1, 0))
        x_p2 = jnp.where(pos >= last - 1, 0.0, pltpu.roll(xl, GROUP_ROWS - 2, 0))
        cw = cw_ref[0]
        xc = cb_ref[0] + x_m1 * cw[0:1] + xl * cw[1:2] + x_p1 * cw[2:3] + x_p2 * cw[3:4]
        for j in range(GROUP_SEQS):
            xc_ref[:, j, :] = xc[j * TILE:(j + 1) * TILE]
            u_ref[:, j, :] = u_s[j * TILE:(j + 1) * TILE, :]


def _inproj_call(layer, x, mod, w_in, rope_cos, rope_sin, conv_w, conv_b, kc, vc):
    n_ctx_blocks = N_CTX // SUB_SEQS

    def rope_idx(i, s):
        return (jnp.where(i < CTX_GROUP_TILES, SUBS, s), 0)

    def cache_idx(i, s):
        return (jnp.minimum(i * SUBS + s, n_ctx_blocks - 1), layer, 0, 0, 0)

    row_spec = lambda w: pl.BlockSpec((SUB_ROWS, w), lambda i, s: (i * SUBS + s, 0))
    tm_spec = pl.BlockSpec((TILE, GROUP_SEQS, 256), lambda i, s: (0, i, 0))
    cache_spec = pl.BlockSpec((SUB_SEQS, 1, HEADS, TILE, 128), cache_idx)
    return pl.pallas_call(
        _inproj_kernel,
        grid=(N_GROUP_TILES, SUBS),
        in_specs=[
            row_spec(D),
            pl.BlockSpec((1, MOD_ROWS, N_MOD * D), lambda i, s: (layer, 0, 0)),
            pl.BlockSpec((1, D, IN_W), lambda i, s: (layer, 0, 0)),
            pl.BlockSpec((SUB_ROWS, 128), rope_idx),
            pl.BlockSpec((SUB_ROWS, 128), rope_idx),
            pl.BlockSpec((1, 4, LRU_W), lambda i, s: (layer, 0, 0)),
            pl.BlockSpec((1, 1, LRU_W), lambda i, s: (layer, 0, 0)),
            pl.BlockSpec(memory_space=pl.ANY),
            pl.BlockSpec(memory_space=pl.ANY),
        ],
        out_specs=[
            row_spec(512), row_spec(512), row_spec(512),
            cache_spec, cache_spec,
            tm_spec, tm_spec,
            row_spec(LRU_W),
        ],
        out_shape=[
            jax.ShapeDtypeStruct((N_TOK, 512), BF16),
            jax.ShapeDtypeStruct((N_TOK, 512), BF16),
            jax.ShapeDtypeStruct((N_TOK, 512), BF16),
            jax.ShapeDtypeStruct(kc.shape, F32),
            jax.ShapeDtypeStruct(vc.shape, F32),
            jax.ShapeDtypeStruct((TILE, N_SEQ, S5_W), F32),
            jax.ShapeDtypeStruct((TILE, N_SEQ, LRU_W), F32),
            jax.ShapeDtypeStruct((N_TOK, LRU_W), F32),
        ],
        scratch_shapes=[pltpu.VMEM((GROUP_ROWS, S5_W), F32), pltpu.VMEM((GROUP_ROWS, LRU_W), F32),
                        pltpu.VMEM((D, IN_W), BF16)],
        input_output_aliases={7: 3, 8: 4},
        compiler_params=_cparams(("arbitrary", "arbitrary")),
        name="inproj",
    )(x, mod, w_in, rope_cos, rope_sin, conv_w, conv_b.reshape(DEPTH, 1, LRU_W), kc, vc)


_NT = (((1,), (1,)), ((), ()))


def _split_maps(q):
    lane = lax.broadcasted_iota(I32, q.shape, 1)
    zero = jnp.zeros_like(q)
    qs = q * (QK ** -0.5)
    return jnp.where(lane < QK, qs, zero), jnp.where(lane >= QK, qs, zero)


def _softmax_values(qm, keys, values):
    scores = [lax.dot_general(qm, k, _NT, preferred_element_type=F32) for k in keys]
    m = scores[0].max(axis=-1, keepdims=True)
    for sc in scores[1:]:
        m = jnp.maximum(m, sc.max(axis=-1, keepdims=True))
    z = None
    o = None
    for sc, v in zip(scores, values):
        e = jnp.exp(sc - m)
        ez = e.sum(axis=-1, keepdims=True)
        eo = jnp.dot(e.astype(BF16), v, preferred_element_type=F32)
        z = ez if z is None else z + ez
        o = eo if o is None else o + eo
    return o * (1.0 / z)


def _diff_attention(q, keys, values, lam, g, post_scale):
    q1, q2 = _split_maps(q)
    o = _softmax_values(q1, keys, values) - lam * _softmax_values(q2, keys, values)
    ms = jnp.mean(o * o, axis=-1, keepdims=True)
    return (o * lax.rsqrt(ms + EPS) * g) * post_scale


def _att_ctx_kernel(lam_ref, q_ref, k_ref, v_ref, g_ref, o_ref, *, post_scale):
    lam = lam_ref[0]
    for s in range(ATT_CTX_SEQS):
        rows = slice(s * TILE, (s + 1) * TILE)
        for h in range(HEADS):
            cols = slice(h * 128, (h + 1) * 128)
            o = _diff_attention(q_ref[rows, cols], [k_ref[rows, cols]], [v_ref[rows, cols]], lam, g_ref[0],
                                post_scale)
            o_ref[rows, cols] = o.astype(BF16)


ATT_CTX_SEQS = 4


def _att_ctx_call(layer, lam, q, k, v, subln_g, post_scale):
    blk = pl.BlockSpec((ATT_CTX_SEQS * TILE, HEADS * 128), lambda b: (b, 0))
    return pl.pallas_call(
        functools.partial(_att_ctx_kernel, post_scale=post_scale),
        grid=(CTX_B // ATT_CTX_SEQS,),
        in_specs=[
            pl.BlockSpec(memory_space=pltpu.SMEM),
            blk, blk, blk,
            pl.BlockSpec((1, 1, DV), lambda b: (layer, 0, 0)),
        ],
        out_specs=blk,
        out_shape=jax.ShapeDtypeStruct((CTX_B * CTX_T, HEADS * DV), BF16),
        compiler_params=_cparams(("arbitrary",)),
        name="att_ctx",
    )(lam, q, k, v, subln_g.reshape(DEPTH, 1, DV))


def _att_lat_kernel(lam_ref, q_ref, k_ref, v_ref, ck_ref, cv_ref, g_ref, o_ref, *, post_scale):
    o = _diff_attention(q_ref[...], [k_ref[...], ck_ref[...].astype(BF16)],
                        [v_ref[...], cv_ref[...].astype(BF16)], lam_ref[0], g_ref[0], post_scale)
    o_ref[...] = o.astype(BF16)


def _att_lat_call(layer, lam, q, k, v, cache_k, cache_v, subln_g, post_scale):
    lat0 = N_CTX
    cache_spec = pl.BlockSpec((None, None, None, PAST, 128), lambda b, h, t: (b, layer, h, 0, 0))
    return pl.pallas_call(
        functools.partial(_att_lat_kernel, post_scale=post_scale),
        grid=(LAT_B, HEADS, CHUNKS),
        in_specs=[
            pl.BlockSpec(memory_space=pltpu.SMEM),
            pl.BlockSpec((TILE, 128), lambda b, h, t: (lat0 + b * CHUNKS + t, h)),
            pl.BlockSpec((LAT_T, 128), lambda b, h, t: (lat0 // CHUNKS + b, h)),
            pl.BlockSpec((LAT_T, 128), lambda b, h, t: (lat0 // CHUNKS + b, h)),
            cache_spec, cache_spec,
            pl.BlockSpec((1, 1, DV), lambda b, h, t: (layer, 0, 0)),
        ],
        out_specs=pl.BlockSpec((TILE, 128), lambda b, h, t: (b * CHUNKS + t, h)),
        out_shape=jax.ShapeDtypeStruct((LAT_B * LAT_T, HEADS * DV), BF16),
        compiler_params=_cparams(("arbitrary", "arbitrary", "arbitrary")),
        name="att_lat",
    )(lam, q, k, v, cache_k, cache_v, subln_g.reshape(DEPTH, 1, DV))


S5_TB = 32
S5_LANES = 128


def _cmul(ar, ai, br, bi):
    return ar * br - ai * bi, ar * bi + ai * br


def _expand_b(b):
    full = jnp.concatenate([b[:, 0:128]] * (S5_N // 128) + [b[:, 128:256]] * (S5_N // 128), axis=1)
    row = lax.broadcasted_iota(I32, full.shape, 0)
    col = lax.broadcasted_iota(I32, full.shape, 1)
    same_group = (row >> S5_C_SHIFT) == ((col & (S5_N - 1)) >> S5_P_SHIFT)
    return jnp.where(same_group, full, 0.0).astype(BF16)


def _expand_c(c):
    full = jnp.concatenate([c[0:S5_P]] * S5_G + [c[S5_P:2 * S5_P]] * S5_G, axis=0)
    row = lax.broadcasted_iota(I32, full.shape, 0)
    col = lax.broadcasted_iota(I32, full.shape, 1)
    same_group = ((row & (S5_N - 1)) >> S5_P_SHIFT) == (col >> S5_C_SHIFT)
    return jnp.where(same_group, full, 0.0).astype(BF16)


def _s5_kernel(uf_ref, ub_ref, b_ref, c_ref, a_ref, d_ref, yf_ref, yb_ref, hend_ref,
               hf_s, hb_s, buf_f, buf_b, bblk_s, cblk_s):
    i = pl.program_id(0)

    @pl.when(i == 0)
    def _():
        hf_s[...] = jnp.zeros_like(hf_s)
        hb_s[...] = jnp.zeros_like(hb_s)
        for d in range(2):
            bblk_s[d] = _expand_b(b_ref[0, d])
            cblk_s[d] = _expand_c(c_ref[0, d])

    buf_f[...] = jnp.dot(uf_ref[...].astype(BF16), bblk_s[0], preferred_element_type=F32)
    buf_b[...] = jnp.dot(ub_ref[...].astype(BF16), bblk_s[1], preferred_element_type=F32)

    def scan(buf, h_s, d):
        for c in range(S5_N // S5_LANES):
            re = slice(c * S5_LANES, (c + 1) * S5_LANES)
            im = slice(S5_N + c * S5_LANES, S5_N + (c + 1) * S5_LANES)
            ar = a_ref[0, d:d + 1, re]
            ai = a_ref[0, d:d + 1, im]
            hr, hi = h_s[:, re], h_s[:, im]
            for t in range(S5_TB):
                tt = t if d == 0 else S5_TB - 1 - t
                rows = slice(tt * N_SEQ, (tt + 1) * N_SEQ)
                pr, pi = _cmul(ar, ai, hr, hi)
                hr = pr + buf[rows, re]
                hi = pi + buf[rows, im]
                buf[rows, re] = hr
                buf[rows, im] = hi
            h_s[:, re] = hr
            h_s[:, im] = hi

    scan(buf_f, hf_s, 0)
    yf_ref[...] = (jnp.dot(buf_f[...].astype(BF16), cblk_s[0], preferred_element_type=F32)
                   + d_ref[0] * uf_ref[...])
    scan(buf_b, hb_s, 1)
    yb_ref[...] = jnp.dot(buf_b[...].astype(BF16), cblk_s[1], preferred_element_type=F32)

    @pl.when(i == pl.num_programs(0) - 1)
    def _():
        hend_ref[0] = hf_s[...]
        hend_ref[1] = hb_s[...]


def _s5_call(layer, u2, b_cmp, c_cmp, a_bar, s5_d):
    nb = CTX_T // S5_TB
    rows = S5_TB * N_SEQ
    return pl.pallas_call(
        _s5_kernel,
        grid=(nb,),
        in_specs=[
            pl.BlockSpec((rows, S5_W), lambda i: (i, 0)),
            pl.BlockSpec((rows, S5_W), lambda i: (nb - 1 - i, 0)),
            pl.BlockSpec((1, 2, S5_W, 256), lambda i: (layer, 0, 0, 0)),
            pl.BlockSpec((1, 2, 2 * S5_P, S5_W), lambda i: (layer, 0, 0, 0)),
            pl.BlockSpec((1, 2, 2 * S5_N), lambda i: (layer, 0, 0)),
            pl.BlockSpec((1, 1, S5_W), lambda i: (layer, 0, 0)),
        ],
        out_specs=[
            pl.BlockSpec((rows, S5_W), lambda i: (i, 0)),
            pl.BlockSpec((rows, S5_W), lambda i: (nb - 1 - i, 0)),
            pl.BlockSpec((2, N_SEQ, 2 * S5_N), lambda i: (0, 0, 0)),
        ],
        out_shape=[
            jax.ShapeDtypeStruct((N_TOK, S5_W), F32),
            jax.ShapeDtypeStruct((N_TOK, S5_W), F32),
            jax.ShapeDtypeStruct((2, N_SEQ, 2 * S5_N), F32),
        ],
        scratch_shapes=[
            pltpu.VMEM((N_SEQ, 2 * S5_N), F32),
            pltpu.VMEM((N_SEQ, 2 * S5_N), F32),
            pltpu.VMEM((rows, 2 * S5_N), F32),
            pltpu.VMEM((rows, 2 * S5_N), F32),
            pltpu.VMEM((2, S5_W, 2 * S5_N), BF16),
            pltpu.VMEM((2, 2 * S5_N, S5_W), BF16),
        ],
        compiler_params=_cparams(("arbitrary",)),
        name="s5_scan",
    )(u2, u2, b_cmp, c_cmp, a_bar, s5_d.reshape(DEPTH, 1, S5_W))


FIX_SEQS = 4


def _s5_fix_kernel(hend_ref, h0_ref, a_ref, c_ref, o_ref, pf_s, pb_s, cf_s, cb_s, cblk_s):
    s = pl.program_id(0)

    @pl.when(s == 0)
    def _():
        for d in range(2):
            cblk_s[d] = _expand_c(c_ref[0, d])
        row8 = lax.broadcasted_iota(I32, (8, S5_N), 0)
        for d, tab in ((0, pf_s), (1, pb_s)):
            ar = a_ref[0, d:d + 1, 0:S5_N]
            ai = a_ref[0, d:d + 1, S5_N:2 * S5_N]
            pr, pi = ar, ai
            r8 = jnp.zeros((8, S5_N), F32)
            i8 = jnp.zeros((8, S5_N), F32)
            for r in range(8):
                if r:
                    pr, pi = _cmul(pr, pi, ar, ai)
                at = r if d == 0 else 7 - r
                r8 = jnp.where(row8 == at, pr, r8)
                i8 = jnp.where(row8 == at, pi, i8)
            base = 0 if d == 0 else TILE - 8
            tab[base:base + 8, 0:S5_N] = r8
            tab[base:base + 8, S5_N:2 * S5_N] = i8
            m = 8
            while m < TILE:
                if d == 0:
                    src, dst, top = slice(0, m), slice(m, 2 * m), slice(m - 1, m)
                else:
                    src, dst, top = slice(TILE - m, TILE), slice(TILE - 2 * m, TILE - m), slice(TILE - m, TILE - m + 1)
                mr, mi = tab[top, 0:S5_N], tab[top, S5_N:2 * S5_N]
                nr, ni = _cmul(tab[src, 0:S5_N], tab[src, S5_N:2 * S5_N], mr, mi)
                tab[dst, 0:S5_N] = nr
                tab[dst, S5_N:2 * S5_N] = ni
                m *= 2

        for b in range(LAT_B):
            ar, ai = pf_s[TILE - 1:TILE, 0:S5_N], pf_s[TILE - 1:TILE, S5_N:2 * S5_N]
            cr, ci = h0_ref[b, 0:1, 0:S5_N], h0_ref[b, 0:1, S5_N:2 * S5_N]
            for j in range(CHUNKS):
                row = b * CHUNKS + j
                cf_s[row:row + 1, 0:S5_N] = cr
                cf_s[row:row + 1, S5_N:2 * S5_N] = ci
                pr, pi = _cmul(ar, ai, cr, ci)
                cr = pr + hend_ref[0, N_CTX + row:N_CTX + row + 1, 0:S5_N]
                ci = pi + hend_ref[0, N_CTX + row:N_CTX + row + 1, S5_N:2 * S5_N]
            ar, ai = pb_s[0:1, 0:S5_N], pb_s[0:1, S5_N:2 * S5_N]
            cr, ci = h0_ref[b, 1:2, 0:S5_N], h0_ref[b, 1:2, S5_N:2 * S5_N]
            for j in reversed(range(CHUNKS)):
                row = b * CHUNKS + j
                cb_s[row:row + 1, 0:S5_N] = cr
                cb_s[row:row + 1, S5_N:2 * S5_N] = ci
                pr, pi = _cmul(ar, ai, cr, ci)
                cr = pr + hend_ref[1, N_CTX + row:N_CTX + row + 1, 0:S5_N]
                ci = pi + hend_ref[1, N_CTX + row:N_CTX + row + 1, S5_N:2 * S5_N]

    acc = None
    for d, tab, car in ((0, pf_s, cf_s), (1, pb_s, cb_s)):
        hs = []
        for k in range(FIX_SEQS):
            cr = car[pl.ds(s * FIX_SEQS + k, 1), 0:S5_N]
            ci = car[pl.ds(s * FIX_SEQS + k, 1), S5_N:2 * S5_N]
            hr, hi = _cmul(tab[:, 0:S5_N], tab[:, S5_N:2 * S5_N], cr, ci)
            hs.append(jnp.concatenate([hr, hi], axis=1).astype(BF16))
        y = jnp.dot(jnp.concatenate(hs, axis=0), cblk_s[d], preferred_element_type=F32)
        acc = y if acc is None else acc + y
    o_ref[...] = acc


def _s5_fix_call(layer, hend, h0, a_bar, c_cmp):
    return pl.pallas_call(
        _s5_fix_kernel,
        grid=(N_LAT // FIX_SEQS,),
        in_specs=[
            pl.BlockSpec((2, N_SEQ, 2 * S5_N), lambda s: (0, 0, 0)),
            pl.BlockSpec((LAT_B, 2, 2 * S5_N), lambda s: (0, 0, 0)),
            pl.BlockSpec((1, 2, 2 * S5_N), lambda s: (layer, 0, 0)),
            pl.BlockSpec((1, 2, 2 * S5_P, S5_W), lambda s: (layer, 0, 0, 0)),
        ],
        out_specs=pl.BlockSpec((FIX_SEQS * TILE, S5_W), lambda s: (s, 0)),
        out_shape=jax.ShapeDtypeStruct((N_LAT * TILE, S5_W), F32),
        scratch_shapes=[
            pltpu.VMEM((TILE, 2 * S5_N), F32),
            pltpu.VMEM((TILE, 2 * S5_N), F32),
            pltpu.VMEM((N_LAT, 2 * S5_N), F32),
            pltpu.VMEM((N_LAT, 2 * S5_N), F32),
            pltpu.VMEM((2, 2 * S5_N, S5_W), BF16),
        ],
        compiler_params=_cparams(("arbitrary",)),
        name="s5_fix",
    )(hend, h0, a_bar, c_cmp)


LRU_ROWS = 1024


def _sigmoid(x):
    return 0.5 * jnp.tanh(0.5 * x) + 0.5


def _lru_kernel(xc_ref, wa_ref, ba_ref, wi_ref, bi_ref, sp_ref, h0_ref, out_ref, hend_ref,
                a_s, h_s, p_s):
    row = lax.broadcasted_iota(I32, (N_SEQ, LRU_W), 0)
    for d in range(2):
        def gates(cix, _, d=d):
            r0 = pl.multiple_of(cix * LRU_ROWS, LRU_ROWS)
            xc = xc_ref[pl.ds(r0, LRU_ROWS), :]
            xb = xc.astype(BF16)
            r = _sigmoid(jnp.dot(xb, wa_ref[0, d], preferred_element_type=F32) + ba_ref[0, d])
            g = _sigmoid(jnp.dot(xb, wi_ref[0, d], preferred_element_type=F32) + bi_ref[0, d])
            log_a = (-LRU_C) * r * sp_ref[0, d]
            a_s[pl.ds(r0, LRU_ROWS), :] = jnp.exp(log_a)
            th = jnp.tanh(log_a)
            h_s[pl.ds(r0, LRU_ROWS), :] = jnp.sqrt(-2.0 * th / (1.0 - th)) * g * xc
            return 0

        lax.fori_loop(0, N_TOK // LRU_ROWS, gates, 0)

        def scan(t, carry, d=d):
            h, p = carry
            tt = t if d == 0 else TILE - 1 - t
            r0 = pl.multiple_of(tt * N_SEQ, N_SEQ)
            a = a_s[pl.ds(r0, N_SEQ), :]
            h = a * h + h_s[pl.ds(r0, N_SEQ), :]
            p = a * p
            h_s[pl.ds(r0, N_SEQ), :] = h
            p_s[pl.ds(r0, N_SEQ), :] = p
            return h, p

        h_end, p_end = lax.fori_loop(0, TILE, scan,
                                     (jnp.zeros((N_SEQ, LRU_W), F32), jnp.ones((N_SEQ, LRU_W), F32)),
                                     unroll=8)
        hend_ref[d] = h_end

        carry_slab = jnp.zeros((N_SEQ, LRU_W), F32)
        for b in range(LAT_B):
            c = h0_ref[0, b, d:d + 1, :]
            order = range(CHUNKS) if d == 0 else reversed(range(CHUNKS))
            for j in order:
                s = N_CTX + b * CHUNKS + j
                carry_slab = jnp.where(row == s, c, carry_slab)
                c = h_end[s:s + 1, :] + p_end[s:s + 1, :] * c

        def fix(t, _, d=d, carry_slab=carry_slab):
            r0 = pl.multiple_of(t * N_SEQ, N_SEQ)
            v = h_s[pl.ds(r0, N_SEQ), :] + p_s[pl.ds(r0, N_SEQ), :] * carry_slab
            if d == 0:
                out_ref[pl.ds(r0, N_SEQ), :] = v
            else:
                out_ref[pl.ds(r0, N_SEQ), :] += v
            return 0

        lax.fori_loop(0, TILE, fix, 0, unroll=16)


def _lru_call(layer, xc2, wa, ba, wi, bi, sp, h0):
    full = lambda shape: pl.BlockSpec(shape, lambda i: (0,) * len(shape))
    per_layer = lambda shape: pl.BlockSpec((1,) + shape, lambda i: (layer,) + (0,) * len(shape))
    return pl.pallas_call(
        _lru_kernel,
        grid=(1,),
        in_specs=[
            full((N_TOK, LRU_W)),
            per_layer((2, LRU_W, LRU_W)),
            per_layer((2, 1, LRU_W)),
            per_layer((2, LRU_W, LRU_W)),
            per_layer((2, 1, LRU_W)),
            per_layer((2, 1, LRU_W)),
            per_layer((LAT_B, 2, LRU_W)),
        ],
        out_specs=[full((N_TOK, LRU_W)), full((2, N_SEQ, LRU_W))],
        out_shape=[
            jax.ShapeDtypeStruct((N_TOK, LRU_W), F32),
            jax.ShapeDtypeStruct((2, N_SEQ, LRU_W), F32),
        ],
        scratch_shapes=[pltpu.VMEM((N_TOK, LRU_W), F32)] * 3,
        compiler_params=_cparams(("arbitrary",)),
        name="rglru",
    )(xc2, wa, ba, wi, bi, sp, h0)


def _layer_norm(z, g, b):
    mu = jnp.mean(z, axis=-1, keepdims=True)
    zc = z - mu
    var = jnp.mean(zc * zc, axis=-1, keepdims=True)
    return zc * lax.rsqrt(var + EPS) * g + b


def _route(logits):
    lane_i = lax.broadcasted_iota(I32, logits.shape, 1)
    lane = lane_i.astype(F32)
    big = jnp.float32(1024.0)
    neg = jnp.float32(-jnp.inf)
    is_g = lane_i < N_GROUPS
    gmax = jnp.max(jnp.where(is_g, logits, neg), axis=-1, keepdims=True)
    g_sel = jnp.min(jnp.where(jnp.logical_and(is_g, logits == gmax), lane, big), axis=-1, keepdims=True)
    p_group = 1.0 / jnp.sum(jnp.where(is_g, jnp.exp(logits - gmax), 0.0), axis=-1, keepdims=True)
    e_idx = lane_i - GATE_LANE0
    e_group = (e_idx >> 2).astype(F32)
    in_g = jnp.logical_and(jnp.logical_and(e_idx >= 0, e_idx < N_EXPERTS), e_group == g_sel)
    v1 = jnp.max(jnp.where(in_g, logits, neg), axis=-1, keepdims=True)
    i1 = jnp.min(jnp.where(jnp.logical_and(in_g, logits == v1), lane, big), axis=-1, keepdims=True)
    rest = jnp.logical_and(in_g, lane != i1)
    v2 = jnp.max(jnp.where(rest, logits, neg), axis=-1, keepdims=True)
    i2 = jnp.min(jnp.where(jnp.logical_and(rest, logits == v2), lane, big), axis=-1, keepdims=True)
    e2 = jnp.exp(v2 - v1)
    inv = 1.0 / (1.0 + e2)
    w1 = inv * p_group
    w2 = e2 * inv * p_group
    return jnp.where(lane == i1, w1, jnp.where(lane == i2, w2, jnp.where(lane_i == 0, g_sel, 0.0)))


def _outproj_kernel(x_ref, oc_ref, ol_ref, yf_ref, yb_ref, yfix_ref, hl_ref, g_ref, mod_ref,
                    wglu_ref, bglu_ref, wout_ref, lng_ref, lnb_ref, rw_ref, rwhi_ref, rb_ref,
                    x1_ref, hm_ref, gates_ref, y_s, hl_s):
    i = pl.program_id(0)
    s = pl.program_id(1)

    @pl.when(s == 0)
    def _():
        for j in range(GROUP_SEQS):
            rows = slice(j * TILE, (j + 1) * TILE)
            y_s[rows, :] = yf_ref[:, j, :] + yb_ref[:, j, :]
            hl_s[rows, :] = hl_ref[:, j, :]

    lat = i >= CTX_GROUP_TILES
    sub_rows = pl.ds(pl.multiple_of(s * SUB_ROWS, SUB_ROWS), SUB_ROWS)
    o_att = jnp.where(lat, ol_ref[...], oc_ref[...])
    y = y_s[sub_rows, :] + jnp.where(lat, yfix_ref[...], 0.0)
    g = jax.nn.gelu(y, approximate=True)
    glu = jnp.dot(g.astype(BF16), wglu_ref[0], preferred_element_type=F32) + bglu_ref[0]
    o_s5 = g * jax.nn.sigmoid(glu)
    o_lru = hl_s[sub_rows, :] * jax.nn.gelu(g_ref[...], approximate=True)
    mix = jnp.concatenate([o_att, o_s5.astype(BF16), o_lru.astype(BF16)], axis=-1)
    out = jnp.dot(mix, wout_ref[0], preferred_element_type=F32)
    r = _group_mod_row(i)
    g1 = mod_ref[0, pl.ds(r, 1), 2 * D:3 * D]
    sh2 = mod_ref[0, pl.ds(r, 1), 3 * D:4 * D]
    sc2 = mod_ref[0, pl.ds(r, 1), 4 * D:5 * D]
    x1 = _layer_norm(ALPHA * x_ref[...] + g1 * out, lng_ref[0], lnb_ref[0])
    x1_ref[...] = x1
    hm = x1 * (1.0 + sc2) + sh2
    hm_ref[...] = hm.astype(BF16)
    hm_hi = hm.astype(BF16)
    hm_lo = (hm - hm_hi.astype(F32)).astype(BF16)
    p_hi = jnp.dot(hm_hi, rw_ref[0], preferred_element_type=F32)
    p_lo = jnp.dot(hm_lo, rwhi_ref[0], preferred_element_type=F32)
    logits = p_hi[:, 0:128] + p_hi[:, 128:256] + p_lo + rb_ref[0]
    gates_ref[...] = _route(logits)


def _outproj_call(layer, x, o_ctx, o_lat, yf, yb, yfix, hl, g_lru, mod, w_glu, b_glu,
                  w_out, ln_g, ln_b, rw_split, rw_hi, rb):
    n_ctx_blocks = N_CTX * TILE // SUB_ROWS
    row_spec = lambda w: pl.BlockSpec((SUB_ROWS, w), lambda i, s: (i * SUBS + s, 0))
    lat_spec = lambda w: pl.BlockSpec((SUB_ROWS, w), lambda i, s: (jnp.maximum(i * SUBS + s - n_ctx_blocks, 0), 0))
    tm_spec = pl.BlockSpec((TILE, GROUP_SEQS, 256), lambda i, s: (0, i, 0))
    vec = lambda n: pl.BlockSpec((1, 1, n), lambda i, s: (layer, 0, 0))
    mat = lambda a, b: pl.BlockSpec((1, a, b), lambda i, s: (layer, 0, 0))
    return pl.pallas_call(
        _outproj_kernel,
        grid=(N_GROUP_TILES, SUBS),
        in_specs=[
            row_spec(D),
            pl.BlockSpec((SUB_ROWS, 512), lambda i, s: (jnp.minimum(i * SUBS + s, n_ctx_blocks - 1), 0)),
            lat_spec(512),
            tm_spec, tm_spec,
            lat_spec(S5_W),
            tm_spec,
            row_spec(LRU_W),
            mat(MOD_ROWS, N_MOD * D),
            mat(S5_W, S5_W), vec(S5_W),
            mat(D, D), vec(D), vec(D),
            mat(D, 256), mat(D, 128), vec(128),
        ],
        out_specs=[row_spec(D), row_spec(D), row_spec(128)],
        out_shape=[
            jax.ShapeDtypeStruct((N_TOK, D), F32),
            jax.ShapeDtypeStruct((N_TOK, D), BF16),
            jax.ShapeDtypeStruct((N_TOK, 128), F32),
        ],
        scratch_shapes=[pltpu.VMEM((GROUP_ROWS, 256), F32)] * 2,
        compiler_params=_cparams(("arbitrary", "arbitrary")),
        name="outproj",
    )(x, o_ctx, o_lat, yf.reshape(TILE, N_SEQ, S5_W), yb.reshape(TILE, N_SEQ, S5_W), yfix,
      hl.reshape(TILE, N_SEQ, LRU_W), g_lru, mod, w_glu,
      b_glu.reshape(DEPTH, 1, S5_W), w_out, ln_g.reshape(DEPTH, 1, D), ln_b.reshape(DEPTH, 1, D),
      rw_split, rw_hi, rb)


HALF = N_TOK // 2
HALF_TILES = HALF // TILE


def _sort_kernel(gates_ref, pos_ref, cnt_ref, rank_s):
    sub = lax.broadcasted_iota(I32, (8, TILE), 0)
    sel_r = lax.broadcasted_iota(I32, (8, 128), 0)
    sel_c = lax.broadcasted_iota(I32, (8, 128), 1)
    pick_lane0 = jnp.where(jnp.logical_and(sel_r == 0, sel_c == 0), 1.0, 0.0).astype(BF16)
    ri = lax.broadcasted_iota(I32, (TILE, TILE), 0)
    ci = lax.broadcasted_iota(I32, (TILE, TILE), 1)
    upper = jnp.where(ri <= ci, 1.0, 0.0).astype(BF16)

    def count(b, carry):
        rows = slice(b * TILE, (b + 1) * TILE)
        g = gates_ref[rows, :].astype(BF16)
        g_t = lax.dot_general(pick_lane0, g, _NT, preferred_element_type=F32)
        g_sel = jnp.sum(g_t, axis=0, keepdims=True)
        onehot = jnp.where(jnp.logical_and(sub < N_GROUPS, sub.astype(F32) == g_sel), 1.0, 0.0)
        cum = jnp.dot(onehot.astype(BF16), upper, preferred_element_type=F32) + carry
        rank_s[b] = jnp.where(onehot > 0.0, cum, 0.0)
        return carry + jnp.sum(onehot, axis=1, keepdims=True)

    counts = jnp.zeros((8, 1), F32)
    for b in range(HALF_TILES):
        counts = count(b, counts)
    sub1 = lax.broadcasted_iota(I32, (8, 1), 0)
    c = [jnp.sum(jnp.where(sub1 == g, counts, 0.0), axis=0, keepdims=True) for g in range(N_GROUPS - 1)]
    start = jnp.where(sub1 == 1, c[0], jnp.where(sub1 == 2, c[0] + c[1],
                      jnp.where(sub1 == 3, c[0] + c[1] + c[2], 0.0)))

    for b in range(HALF_TILES):
        rk = rank_s[b]
        p = jnp.sum(jnp.where(rk > 0.0, rk + start - 1.0, 0.0), axis=0, keepdims=True)
        pos_ref[b:b + 1, :] = p.astype(I32)
    cnt_ref[0] = jnp.broadcast_to(counts, (8, 128))


def _sort_call(gates):
    return pl.pallas_call(
        _sort_kernel,
        grid=(2,),
        in_specs=[pl.BlockSpec((HALF, 128), lambda h: (h, 0))],
        out_specs=[pl.BlockSpec((HALF_TILES, TILE), lambda h: (h, 0)),
                   pl.BlockSpec((1, 8, 128), lambda h: (h, 0, 0))],
        out_shape=[jax.ShapeDtypeStruct((N_SEQ, TILE), I32), jax.ShapeDtypeStruct((2, 8, 128), F32)],
        scratch_shapes=[pltpu.VMEM((HALF_TILES, 8, TILE), F32)],
        compiler_params=_cparams(("arbitrary",)),
        name="group_sort",
    )(gates)


ITEM_TILES = 2
N_ITEMS = N_EXPERTS * (HALF_TILES + 2 * N_GROUPS) // (EPG * ITEM_TILES)
SCATTER_ROWS = 2 * TILE
SCATTER_STEPS = HALF // SCATTER_ROWS
MOE_STEPS = SCATTER_STEPS + N_ITEMS + HALF_TILES


def _moe_schedule(counts):
    start = jnp.cumsum(counts, axis=1) - counts
    lo = start // TILE
    hi = (start + counts + TILE - 1) // TILE
    tiles_g = jnp.where(counts > 0, hi - lo, 0)
    n_g = (tiles_g + ITEM_TILES - 1) // ITEM_TILES
    n_e = jnp.repeat(n_g, EPG, axis=1)
    lo_e = jnp.repeat(lo, EPG, axis=1)
    tiles_e = jnp.repeat(tiles_g, EPG, axis=1)
    off_end = jnp.cumsum(n_e, axis=1)
    off = off_end - n_e
    total = off_end[:, -1:]
    w = jnp.arange(N_ITEMS, dtype=I32)[None, :]
    w_eff = jnp.minimum(w, total - 1)
    e_w = jnp.sum((w_eff[:, :, None] >= off_end[:, None, :]).astype(I32), axis=-1)
    first = ITEM_TILES * (w_eff - jnp.take_along_axis(off, e_w, axis=1))
    tile = jnp.take_along_axis(lo_e, e_w, axis=1) + first
    size = jnp.minimum(jnp.take_along_axis(tiles_e, e_w, axis=1) - first, ITEM_TILES)
    size = jnp.where(w < total, size, 0)
    e_f, n_f = e_w.reshape(-1), size.reshape(-1)
    n_all = 2 * N_ITEMS
    idx = jnp.arange(n_all, dtype=I32)
    key = (idx // N_ITEMS) * N_EXPERTS + e_f
    run_start = jnp.logical_and(n_f > 0, key != jnp.concatenate([jnp.full((1,), -1, I32), key[:-1]]))
    slot = (jnp.cumsum(run_start.astype(I32)) - 1) % 2
    at_or_after = lax.cummin(jnp.where(run_start, idx, n_all), reverse=True)
    after = jnp.concatenate([at_or_after[1:], jnp.full((1,), n_all, I32)])
    next_e = jnp.where(after < n_all, e_f[jnp.minimum(after, n_all - 1)], -1)
    return e_f, tile.reshape(-1), n_f, run_start.astype(I32), slot, next_e


def _moe_kernel(pos_ref, ite_ref, itt_ref, itv_ref, itf_ref, its_ref, itx_ref,
                hm_ref, gates_ref, wg_hbm, wu_hbm, wd_hbm, x1_ref, mod_ref, lng_ref, lnb_ref,
                *rest, split, layer):
    out_refs, (xs_s, gs_s, acc_s, hm_s, wg_b, wu_b, wd_b, w_sem) = rest[:-8], rest[-8:]
    h = pl.program_id(0)
    w = pl.program_id(1)
    base = h * HALF

    def weight_copies(e, slot):
        pairs = ((wg_hbm, wg_b), (wu_hbm, wu_b), (wd_hbm, wd_b))
        return [pltpu.make_async_copy(src.at[layer, e], dst.at[slot], w_sem.at[slot, k])
                for k, (src, dst) in enumerate(pairs)]

    @pl.when(w < SCATTER_STEPS)
    def _():
        @pl.when(w == 0)
        def _():
            acc_s[...] = jnp.zeros_like(acc_s)

        @pl.when(jnp.logical_and(h == 0, w == 0))
        def _():
            for c in weight_copies(ite_ref[0], 0):
                c.start()

        hm_s[...] = hm_ref[...].astype(F32)

        def body(r, c):
            p = pos_ref[base + w * SCATTER_ROWS + r]
            xs_s[pl.ds(p, 1), :] = hm_s[pl.ds(r, 1), :]
            gs_s[pl.ds(p, 1), :] = gates_ref[pl.ds(r, 1), :]
            return c

        lax.fori_loop(0, SCATTER_ROWS, body, 0, unroll=64)

    @pl.when(jnp.logical_and(w >= SCATTER_STEPS, w < SCATTER_STEPS + N_ITEMS))
    def _():
        idx = h * N_ITEMS + (w - SCATTER_STEPS)

        slot = its_ref[idx]

        @pl.when(itf_ref[idx] > 0)
        def _():
            @pl.when(itx_ref[idx] >= 0)
            def _():
                for c in weight_copies(itx_ref[idx], 1 - slot):
                    c.start()

            for c in weight_copies(ite_ref[idx], slot):
                c.wait()

        def item(n_tiles):
            e = ite_ref[idx]
            rows = pl.ds(pl.multiple_of(itt_ref[idx] * TILE, TILE), n_tiles * TILE)
            x = xs_s[rows, :].astype(BF16)
            a = jnp.dot(x, wg_b[slot].astype(BF16), preferred_element_type=F32)
            u = jnp.dot(x, wu_b[slot].astype(BF16), preferred_element_type=F32)
            g = gs_s[rows, :]
            lane = lax.broadcasted_iota(I32, g.shape, 1)
            ge = jnp.sum(jnp.where(lane == e + GATE_LANE0, g, 0.0), axis=-1, keepdims=True)
            act = ((a * jax.nn.sigmoid(a)) * u * ge).astype(BF16)
            acc_s[rows, :] += jnp.dot(act, wd_b[slot].astype(BF16), preferred_element_type=F32)

        for n_tiles in range(1, ITEM_TILES + 1):
            pl.when(itv_ref[idx] == n_tiles)(functools.partial(item, n_tiles))

    @pl.when(w >= SCATTER_STEPS + N_ITEMS)
    def _():
        j = w - SCATTER_STEPS - N_ITEMS

        def finish(o_ref):
            def body(r, c):
                p = pos_ref[base + j * TILE + r]
                o_ref[pl.ds(r, 1), :] = acc_s[pl.ds(p, 1), :]
                return c

            lax.fori_loop(0, TILE, body, 0, unroll=64)
            r = jnp.where(h == 0, 0, 1 + j // CHUNKS)
            g2 = mod_ref[0, pl.ds(r, 1), 5 * D:6 * D]
            o_ref[...] = _layer_norm(ALPHA * x1_ref[...] + g2 * o_ref[...], lng_ref[0], lnb_ref[0])

        if split:
            for half, o_ref in enumerate(out_refs):
                pl.when(h == half)(functools.partial(finish, o_ref))
        else:
            finish(out_refs[0])


def _moe_call(layer, pos, items, hm, gates, w_gate, w_up, w_down, x1, mod, ln_g, ln_b, split):
    def in_tile(h, w, *_):
        return (h * SCATTER_STEPS + jnp.minimum(w, SCATTER_STEPS - 1), 0)

    def out_step(w):
        return jnp.clip(w - SCATTER_STEPS - N_ITEMS, 0, HALF_TILES - 1)

    def out_tile(h, w, *_):
        return (h * HALF_TILES + out_step(w), 0)

    if split:
        out_specs = [pl.BlockSpec((TILE, D), lambda h, w, *_: (jnp.where(h == 0, out_step(w), HALF_TILES - 1), 0)),
                     pl.BlockSpec((TILE, D), lambda h, w, *_: (jnp.where(h == 1, out_step(w), 0), 0))]
        out_shape = [jax.ShapeDtypeStruct((HALF, D), F32)] * 2
    else:
        out_specs = pl.BlockSpec((TILE, D), out_tile)
        out_shape = jax.ShapeDtypeStruct((N_TOK, D), F32)

    vec = lambda n: pl.BlockSpec((1, 1, n), lambda h, w, *_: (layer, 0, 0))
    grid_spec = pltpu.PrefetchScalarGridSpec(
        num_scalar_prefetch=1 + len(items),
        grid=(2, MOE_STEPS),
        in_specs=[
            pl.BlockSpec((SCATTER_ROWS, D), in_tile),
            pl.BlockSpec((SCATTER_ROWS, 128), in_tile),
            pl.BlockSpec(memory_space=pl.ANY),
            pl.BlockSpec(memory_space=pl.ANY),
            pl.BlockSpec(memory_space=pl.ANY),
            pl.BlockSpec((TILE, D), out_tile),
            pl.BlockSpec((1, MOD_ROWS, N_MOD * D), lambda h, w, *_: (layer, 0, 0)),
            vec(D), vec(D),
        ],
        out_specs=out_specs,
        scratch_shapes=[
            pltpu.VMEM((HALF, D), F32),
            pltpu.VMEM((HALF, 128), F32),
            pltpu.VMEM((HALF, D), F32),
            pltpu.VMEM((SCATTER_ROWS, D), F32),
            pltpu.VMEM((2, D, D_EXPERT), F32),
            pltpu.VMEM((2, D, D_EXPERT), F32),
            pltpu.VMEM((2, D_EXPERT, D), F32),
            pltpu.SemaphoreType.DMA((2, 3)),
        ],
    )
    return pl.pallas_call(
        functools.partial(_moe_kernel, split=split, layer=layer),
        grid_spec=grid_spec,
        out_shape=out_shape,
        compiler_params=pltpu.CompilerParams(dimension_semantics=("arbitrary", "arbitrary"),
                                             vmem_limit_bytes=MOE_VMEM_LIMIT),
        name="moe",
    )(pos, *items, hm, gates, w_gate, w_up, w_down, x1, mod,
      ln_g.reshape(DEPTH, 1, D), ln_b.reshape(DEPTH, 1, D))


def _rope_tables():
    rows = LAT_T // GRID_W
    row = jnp.repeat(jnp.arange(rows, dtype=F32), GRID_W)
    col = jnp.tile(jnp.arange(GRID_W, dtype=F32), rows)
    n_freq = QK // 4
    inv = ROPE_BASE ** (-jnp.arange(n_freq, dtype=F32) / n_freq)
    ang_r = row[:, None] * inv
    ang_c = col[:, None] * inv
    cos64 = jnp.concatenate([jnp.cos(ang_r), jnp.cos(ang_r), jnp.cos(ang_c), jnp.cos(ang_c)], axis=1)
    sin64 = jnp.concatenate([-jnp.sin(ang_r), jnp.sin(ang_r), -jnp.sin(ang_c), jnp.sin(ang_c)], axis=1)
    cos = jnp.concatenate([jnp.tile(cos64, (1, 2)), jnp.ones((SUB_ROWS, 128), F32)], axis=0)
    sin = jnp.concatenate([jnp.tile(sin64, (1, 2)), jnp.zeros((SUB_ROWS, 128), F32)], axis=0)
    return cos, sin


def _s5_params(a_re, a_im, b_re, b_im, c_re, c_im, log_dt):
    dt = jnp.exp(log_dt)[..., None]
    mag = jnp.exp(a_re * dt)
    abar_r = mag * jnp.cos(a_im * dt)
    abar_i = mag * jnp.sin(a_im * dt)
    den = a_re * a_re + a_im * a_im
    nr = abar_r - 1.0
    coef_r = (nr * a_re + abar_i * a_im) / den
    coef_i = (abar_i * a_re - nr * a_im) / den
    bbar_r = coef_r[..., None] * b_re - coef_i[..., None] * b_im
    bbar_i = coef_r[..., None] * b_im + coef_i[..., None] * b_re
    def b_rows(bb):
        return bb.transpose(0, 1, 2, 4, 3).reshape(DEPTH, 2, S5_W, S5_P)
    br, bi = b_rows(bbar_r), b_rows(bbar_i)
    b_cmp = jnp.concatenate([br, br, bi, bi], axis=-1)
    def c_rows(cc):
        return cc.transpose(0, 1, 4, 2, 3).reshape(DEPTH, 2, S5_P, S5_W)
    c_cmp = jnp.concatenate([c_rows(c_re), c_rows(-c_im)], axis=-2)
    a_bar = jnp.concatenate([abar_r.reshape(DEPTH, 2, S5_N), abar_i.reshape(DEPTH, 2, S5_N)], axis=-1)
    return b_cmp, c_cmp, a_bar


def _block_diag(w):
    eye = jnp.eye(LRU_BLOCKS, dtype=F32)
    m = jnp.einsum('ldkij,kh->ldkihj', w, eye)
    return m.reshape(DEPTH, 2, LRU_W, LRU_W).astype(BF16)


def kernel(x_prompt, x_sample, cache_k, cache_v, state_s5, state_lru, c, c_ctx, w_ada, b_ada, w_in, w_out, lam_q1, lam_k1, lam_q2, lam_k2, subln_g, s5_a_re, s5_a_im, s5_b_re, s5_b_im, s5_c_re, s5_c_im, s5_log_dt, s5_d, s5_w_glu, s5_b_glu, lru_conv_w, lru_conv_b, lru_w_a, lru_b_a, lru_w_i, lru_b_i, lru_lambda, ln1_g, ln1_b, ln2_g, ln2_b, router_group_w, router_group_b, router_expert_w, router_expert_b, moe_w_gate, moe_w_up, moe_w_down):
    x = jnp.concatenate([x_prompt.reshape(CTX_B * CTX_T, D), x_sample.reshape(LAT_B * LAT_T, D)], axis=0)
    cond = jnp.concatenate([c_ctx[None, :], c, jnp.zeros((MOD_ROWS - 1 - LAT_B, D), F32)], axis=0)
    mod = _ada_call(cond, w_ada, b_ada)

    w_out_b = w_out.astype(BF16)
    w_glu_b = s5_w_glu.astype(BF16)
    rope_cos, rope_sin = _rope_tables()
    b_cmp, c_cmp, a_bar = _s5_params(s5_a_re, s5_a_im, s5_b_re, s5_b_im, s5_c_re, s5_c_im, s5_log_dt)
    wa_blk = _block_diag(lru_w_a)
    wi_blk = _block_diag(lru_w_i)
    ba = lru_b_a.reshape(DEPTH, 2, 1, LRU_W)
    bi = lru_b_i.reshape(DEPTH, 2, 1, LRU_W)
    sp = jax.nn.softplus(-lru_lambda).reshape(DEPTH, 2, 1, LRU_W)
    lam = (jnp.exp(jnp.sum(lam_q1 * lam_k1, axis=-1)) - jnp.exp(jnp.sum(lam_q2 * lam_k2, axis=-1)))
    rw = jnp.concatenate([router_group_w, router_expert_w,
                          jnp.zeros((DEPTH, D, 128 - N_GROUPS - N_EXPERTS), F32)], axis=-1)
    rw_hi = rw.astype(BF16)
    rw_split = jnp.concatenate([rw_hi, (rw - rw_hi.astype(F32)).astype(BF16)], axis=-1)
    rb = jnp.concatenate([router_group_b, router_expert_b,
                          jnp.zeros((DEPTH, 128 - N_GROUPS - N_EXPERTS), F32)], axis=-1).reshape(DEPTH, 1, 128)
    s5_h0 = state_s5.reshape(LAT_B, DEPTH, 2, 2 * S5_N)
    lru_h0 = state_lru.transpose(1, 0, 2, 3)

    kc = jnp.zeros((CTX_B, DEPTH, HEADS, CTX_T, 2 * QK), F32)
    vc = jnp.zeros((CTX_B, DEPTH, HEADS, CTX_T, DV), F32)
    s5_states = []
    lru_states = []
    for l in range(DEPTH):
        lambda_init = 0.8 - 0.6 * math.exp(-0.3 * l)
        lam_l = (lam[l] + lambda_init).reshape(1)
        q, k, v, kc, vc, u_tm, xc_tm, g_lru = _inproj_call(
            l, x, mod, w_in, rope_cos, rope_sin, lru_conv_w, lru_conv_b, kc, vc)
        o_ctx = _att_ctx_call(l, lam_l, q, k, v, subln_g, 1.0 - lambda_init)
        o_lat = _att_lat_call(l, lam_l, q, k, v, cache_k, cache_v, subln_g, 1.0 - lambda_init)
        yf, yb, hend = _s5_call(l, u_tm.reshape(N_TOK, S5_W), b_cmp, c_cmp, a_bar, s5_d)
        yfix = _s5_fix_call(l, hend, s5_h0[:, l], a_bar, c_cmp)
        hl, lru_end = _lru_call(l, xc_tm.reshape(N_TOK, LRU_W), wa_blk, ba, wi_blk, bi, sp, lru_h0)
        x1, hm, gates = _outproj_call(l, x, o_ctx, o_lat, yf, yb, yfix, hl, g_lru, mod,
                                      w_glu_b, s5_b_glu, w_out_b, ln1_g, ln1_b, rw_split, rw_hi, rb)
        pos, counts = _sort_call(gates)
        items = _moe_schedule(counts[:, :N_GROUPS, 0].astype(I32))
        x = _moe_call(l, pos.reshape(N_TOK), items, hm, gates, moe_w_gate, moe_w_up,
                      moe_w_down, x1, mod, ln2_g, ln2_b, split=(l == DEPTH - 1))
        s5_states.append(hend[:, :N_CTX].reshape(2, CTX_B, 2, S5_G, S5_P).transpose(1, 0, 2, 3, 4))
        lru_states.append(lru_end[:, :N_CTX].transpose(1, 0, 2))

    y_p = x[0].reshape(CTX_B, CTX_T, D)
    y_s = x[1].reshape(LAT_B, LAT_T, D)
    return (y_p, y_s, kc, vc, jnp.stack(s5_states, axis=1), jnp.stack(lru_states, axis=1))
```

```python
import functools
import math

import jax
import jax.numpy as jnp
from jax import lax
from jax.experimental import pallas as pl
from jax.experimental.pallas import tpu as pltpu

F32 = jnp.float32
BF16 = jnp.bfloat16
I32 = jnp.int32

D = 1024
DEPTH = 4
CTX_B = 16
CTX_T = 256
LAT_B = 2
LAT_T = 2048
PAST = 256
GRID_W = 64
HEADS = 4
QK = 64
DV = 128
S5_W = 256
S5_G = 16
S5_C = 16
S5_P = 64
S5_N = S5_G * S5_P
S5_C_SHIFT = S5_C.bit_length() - 1
S5_P_SHIFT = S5_P.bit_length() - 1
LRU_W = 256
LRU_BLOCKS = 4
LRU_C = 8.0
N_GROUPS = 4
EPG = 4
N_EXPERTS = 16
D_EXPERT = 512
N_MOD = 6
IN_W = 2304
ROPE_BASE = 10000.0
ALPHA = (2 * DEPTH) ** 0.25
EPS = 1e-5

TILE = 256
N_TOK = CTX_B * CTX_T + LAT_B * LAT_T
N_SEQ = N_TOK // TILE
N_CTX = CTX_B
CHUNKS = LAT_T // TILE
N_LAT = LAT_B * CHUNKS
MOD_ROWS = 8
GATE_LANE0 = 4
VMEM_LIMIT = 56 * 1024 * 1024
MOE_VMEM_LIMIT = 60 * 1024 * 1024

GROUP_SEQS = 8
GROUP_ROWS = GROUP_SEQS * TILE
N_GROUP_TILES = N_TOK // GROUP_ROWS
CTX_GROUP_TILES = N_CTX // GROUP_SEQS
SUB_ROWS = 512
SUBS = GROUP_ROWS // SUB_ROWS
SUB_SEQS = SUB_ROWS // TILE


def _cparams(sem):
    return pltpu.CompilerParams(dimension_semantics=sem, vmem_limit_bytes=VMEM_LIMIT)


def _group_mod_row(i):
    return jnp.where(i < CTX_GROUP_TILES, 0, i - CTX_GROUP_TILES + 1)


def _ada_kernel(c_ref, w_ref, b_ref, o_ref):
    c = c_ref[...]
    s = (c * jax.nn.sigmoid(c)).astype(BF16)
    o_ref[0] = jnp.dot(s, w_ref[0].astype(BF16), preferred_element_type=F32) + b_ref[0]


def _ada_call(cond, w_ada, b_ada):
    tn = 1536
    return pl.pallas_call(
        _ada_kernel,
        grid=(DEPTH, N_MOD * D // tn),
        in_specs=[
            pl.BlockSpec((MOD_ROWS, D), lambda l, j: (0, 0)),
            pl.BlockSpec((1, D, tn), lambda l, j: (l, 0, j)),
            pl.BlockSpec((1, 1, tn), lambda l, j: (l, 0, j)),
        ],
        out_specs=pl.BlockSpec((1, MOD_ROWS, tn), lambda l, j: (l, 0, j)),
        out_shape=jax.ShapeDtypeStruct((DEPTH, MOD_ROWS, N_MOD * D), F32),
        compiler_params=_cparams(("arbitrary", "arbitrary")),
        name="adaln",
    )(cond, w_ada, b_ada.reshape(DEPTH, 1, N_MOD * D))


def _inproj_kernel(x_ref, mod_ref, w_ref, cos_ref, sin_ref, cw_ref, cb_ref, kc_in, vc_in,
                   q_ref, k_ref, v_ref, kc_ref, vc_ref, u_ref, xc_ref, g_ref, u_s, xl_s, w_s):
    del kc_in, vc_in
    i = pl.program_id(0)
    s = pl.program_id(1)

    @pl.when(jnp.logical_and(i == 0, s == 0))
    def _():
        for t in range(2 * HEADS):
            kind, head = divmod(t, HEADS)
            halves = []
            for m in range(2):
                col = (2 * kind + m) * HEADS * QK + head * QK
                blk = w_ref[0, :, (col // 128) * 128:(col // 128 + 1) * 128]
                halves.append(blk[:, col % 128:col % 128 + QK])
            w_s[:, t * 128:(t + 1) * 128] = jnp.concatenate(halves, axis=1).astype(BF16)
        w_s[:, 1024:IN_W] = w_ref[0, :, 1024:IN_W].astype(BF16)

    r = _group_mod_row(i)
    sh = mod_ref[0, pl.ds(r, 1), 0:D]
    sc = mod_ref[0, pl.ds(r, 1), D:2 * D]
    xm = (x_ref[...] * (1.0 + sc) + sh).astype(BF16)
    proj = jnp.dot(xm, w_s[...], preferred_element_type=F32)

    qk = proj[:, 0:1024]
    cos = jnp.concatenate([cos_ref[...]] * 8, axis=1)
    sin = jnp.concatenate([sin_ref[...]] * 8, axis=1)
    lane = lax.broadcasted_iota(I32, qk.shape, 1)
    swapped = jnp.where((lane & 31) < 16, pltpu.roll(qk, 1024 - 16, 1), pltpu.roll(qk, 16, 1))
    qk = qk * cos + swapped * sin
    q_ref[...] = qk[:, 0:512].astype(BF16)
    k_ref[...] = qk[:, 512:1024].astype(BF16)
    v = proj[:, 1024:1536]
    v_ref[...] = v.astype(BF16)

    @pl.when(i < CTX_GROUP_TILES)
    def _():
        for jj in range(SUB_SEQS):
            rows = slice(jj * TILE, (jj + 1) * TILE)
            for h in range(HEADS):
                kc_ref[jj, 0, h] = qk[rows, 512 + h * 128:512 + (h + 1) * 128]
                vc_ref[jj, 0, h] = v[rows, h * 128:(h + 1) * 128]

    g_ref[...] = proj[:, 2048:2304]
    sub_rows = pl.ds(pl.multiple_of(s * SUB_ROWS, SUB_ROWS), SUB_ROWS)
    u_s[sub_rows, :] = proj[:, 1536:1792]
    xl_s[sub_rows, :] = proj[:, 1792:2048]

    @pl.when(s == SUBS - 1)
    def _():
        xl = xl_s[...]
        row = lax.broadcasted_iota(I32, xl.shape, 0)
        is_ctx = i < CTX_GROUP_TILES
        pos = jnp.where(is_ctx, row & (TILE - 1), row)
        last = jnp.where(is_ctx, TILE - 1, GROUP_ROWS - 1)
        x_m1 = jnp.where(pos == 0, 0.0, pltpu.roll(xl, 1, 0))
        x_p1 = jnp.where(pos == last, 0.0, pltpu.roll(xl, GROUP_ROWS - 1, 0))
        x_p2 = jnp.where(pos >= last - 1, 0.0, pltpu.roll(xl, GROUP_ROWS - 2, 0))
        cw = cw_ref[0]
        xc = cb_ref[0] + x_m1 * cw[0:1] + xl * cw[1:2] + x_p1 * cw[2:3] + x_p2 * cw[3:4]
        for j in range(GROUP_SEQS):
            xc_ref[:, j, :] = xc[j * TILE:(j + 1) * TILE]
            u_ref[:, j, :] = u_s[j * TILE:(j + 1) * TILE, :]


def _inproj_call(layer, x, mod, w_in, rope_cos, rope_sin, conv_w, conv_b, kc, vc):
    n_ctx_blocks = N_CTX // SUB_SEQS

    def rope_idx(i, s):
        return (jnp.where(i < CTX_GROUP_TILES, SUBS, s), 0)

    def cache_idx(i, s):
        return (jnp.minimum(i * SUBS + s, n_ctx_blocks - 1), layer, 0, 0, 0)

    row_spec = lambda w: pl.BlockSpec((SUB_ROWS, w), lambda i, s: (i * SUBS + s, 0))
    tm_spec = pl.BlockSpec((TILE, GROUP_SEQS, 256), lambda i, s: (0, i, 0))
    cache_spec = pl.BlockSpec((SUB_SEQS, 1, HEADS, TILE, 128), cache_idx)
    return pl.pallas_call(
        _inproj_kernel,
        grid=(N_GROUP_TILES, SUBS),
        in_specs=[
            row_spec(D),
            pl.BlockSpec((1, MOD_ROWS, N_MOD * D), lambda i, s: (layer, 0, 0)),
            pl.BlockSpec((1, D, IN_W), lambda i, s: (layer, 0, 0)),
            pl.BlockSpec((SUB_ROWS, 128), rope_idx),
            pl.BlockSpec((SUB_ROWS, 128), rope_idx),
            pl.BlockSpec((1, 4, LRU_W), lambda i, s: (layer, 0, 0)),
            pl.BlockSpec((1, 1, LRU_W), lambda i, s: (layer, 0, 0)),
            pl.BlockSpec(memory_space=pl.ANY),
            pl.BlockSpec(memory_space=pl.ANY),
        ],
        out_specs=[
            row_spec(512), row_spec(512), row_spec(512),
            cache_spec, cache_spec,
            tm_spec, tm_spec,
            row_spec(LRU_W),
        ],
        out_shape=[
            jax.ShapeDtypeStruct((N_TOK, 512), BF16),
            jax.ShapeDtypeStruct((N_TOK, 512), BF16),
            jax.ShapeDtypeStruct((N_TOK, 512), BF16),
            jax.ShapeDtypeStruct(kc.shape, F32),
            jax.ShapeDtypeStruct(vc.shape, F32),
            jax.ShapeDtypeStruct((TILE, N_SEQ, S5_W), F32),
            jax.ShapeDtypeStruct((TILE, N_SEQ, LRU_W), F32),
            jax.ShapeDtypeStruct((N_TOK, LRU_W), F32),
        ],
        scratch_shapes=[pltpu.VMEM((GROUP_ROWS, S5_W), F32), pltpu.VMEM((GROUP_ROWS, LRU_W), F32),
                        pltpu.VMEM((D, IN_W), BF16)],
        input_output_aliases={7: 3, 8: 4},
        compiler_params=_cparams(("arbitrary", "arbitrary")),
        name="inproj",
    )(x, mod, w_in, rope_cos, rope_sin, conv_w, conv_b.reshape(DEPTH, 1, LRU_W), kc, vc)


_NT = (((1,), (1,)), ((), ()))


def _split_maps(q):
    lane = lax.broadcasted_iota(I32, q.shape, 1)
    zero = jnp.zeros_like(q)
    qs = q * (QK ** -0.5)
    return jnp.where(lane < QK, qs, zero), jnp.where(lane >= QK, qs, zero)


def _softmax_values(qm, keys, values):
    scores = [lax.dot_general(qm, k, _NT, preferred_element_type=F32) for k in keys]
    m = scores[0].max(axis=-1, keepdims=True)
    for sc in scores[1:]:
        m = jnp.maximum(m, sc.max(axis=-1, keepdims=True))
    z = None
    o = None
    for sc, v in zip(scores, values):
        e = jnp.exp(sc - m)
        ez = e.sum(axis=-1, keepdims=True)
        eo = jnp.dot(e.astype(BF16), v, preferred_element_type=F32)
        z = ez if z is None else z + ez
        o = eo if o is None else o + eo
    return o * (1.0 / z)


def _diff_attention(q, keys, values, lam, g, post_scale):
    q1, q2 = _split_maps(q)
    o = _softmax_values(q1, keys, values) - lam * _softmax_values(q2, keys, values)
    ms = jnp.mean(o * o, axis=-1, keepdims=True)
    return (o * lax.rsqrt(ms + EPS) * g) * post_scale


def _att_ctx_kernel(lam_ref, q_ref, k_ref, v_ref, g_ref, o_ref, *, post_scale):
    lam = lam_ref[0]
    for s in range(ATT_CTX_SEQS):
        rows = slice(s * TILE, (s + 1) * TILE)
        for h in range(HEADS):
            cols = slice(h * 128, (h + 1) * 128)
            o = _diff_attention(q_ref[rows, cols], [k_ref[rows, cols]], [v_ref[rows, cols]], lam, g_ref[0],
                                post_scale)
            o_ref[rows, cols] = o.astype(BF16)


ATT_CTX_SEQS = 4


def _att_ctx_call(layer, lam, q, k, v, subln_g, post_scale):
    blk = pl.BlockSpec((ATT_CTX_SEQS * TILE, HEADS * 128), lambda b: (b, 0))
    return pl.pallas_call(
        functools.partial(_att_ctx_kernel, post_scale=post_scale),
        grid=(CTX_B // ATT_CTX_SEQS,),
        in_specs=[
            pl.BlockSpec(memory_space=pltpu.SMEM),
            blk, blk, blk,
            pl.BlockSpec((1, 1, DV), lambda b: (layer, 0, 0)),
        ],
        out_specs=blk,
        out_shape=jax.ShapeDtypeStruct((CTX_B * CTX_T, HEADS * DV), BF16),
        compiler_params=_cparams(("arbitrary",)),
        name="att_ctx",
    )(lam, q, k, v, subln_g.reshape(DEPTH, 1, DV))


def _att_lat_kernel(lam_ref, q_ref, k_ref, v_ref, ck_ref, cv_ref, g_ref, o_ref, *, post_scale):
    o = _diff_attention(q_ref[...], [k_ref[...], ck_ref[...].astype(BF16)],
                        [v_ref[...], cv_ref[...].astype(BF16)], lam_ref[0], g_ref[0], post_scale)
    o_ref[...] = o.astype(BF16)


def _att_lat_call(layer, lam, q, k, v, cache_k, cache_v, subln_g, post_scale):
    lat0 = N_CTX
    cache_spec = pl.BlockSpec((None, None, None, PAST, 128), lambda b, h, t: (b, layer, h, 0, 0))
    return pl.pallas_call(
        functools.partial(_att_lat_kernel, post_scale=post_scale),
        grid=(LAT_B, HEADS, CHUNKS),
        in_specs=[
            pl.BlockSpec(memory_space=pltpu.SMEM),
            pl.BlockSpec((TILE, 128), lambda b, h, t: (lat0 + b * CHUNKS + t, h)),
            pl.BlockSpec((LAT_T, 128), lambda b, h, t: (lat0 // CHUNKS + b, h)),
            pl.BlockSpec((LAT_T, 128), lambda b, h, t: (lat0 // CHUNKS + b, h)),
            cache_spec, cache_spec,
            pl.BlockSpec((1, 1, DV), lambda b, h, t: (layer, 0, 0)),
        ],
        out_specs=pl.BlockSpec((TILE, 128), lambda b, h, t: (b * CHUNKS + t, h)),
        out_shape=jax.ShapeDtypeStruct((LAT_B * LAT_T, HEADS * DV), BF16),
        compiler_params=_cparams(("arbitrary", "arbitrary", "arbitrary")),
        name="att_lat",
    )(lam, q, k, v, cache_k, cache_v, subln_g.reshape(DEPTH, 1, DV))


S5_TB = 32
S5_LANES = 128


def _cmul(ar, ai, br, bi):
    return ar * br - ai * bi, ar * bi + ai * br


def _expand_b(b):
    full = jnp.concatenate([b[:, 0:128]] * (S5_N // 128) + [b[:, 128:256]] * (S5_N // 128), axis=1)
    row = lax.broadcasted_iota(I32, full.shape, 0)
    col = lax.broadcasted_iota(I32, full.shape, 1)
    same_group = (row >> S5_C_SHIFT) == ((col & (S5_N - 1)) >> S5_P_SHIFT)
    return jnp.where(same_group, full, 0.0).astype(BF16)


def _expand_c(c):
    full = jnp.concatenate([c[0:S5_P]] * S5_G + [c[S5_P:2 * S5_P]] * S5_G, axis=0)
    row = lax.broadcasted_iota(I32, full.shape, 0)
    col = lax.broadcasted_iota(I32, full.shape, 1)
    same_group = ((row & (S5_N - 1)) >> S5_P_SHIFT) == (col >> S5_C_SHIFT)
    return jnp.where(same_group, full, 0.0).astype(BF16)


def _s5_kernel(uf_ref, ub_ref, b_ref, c_ref, a_ref, d_ref, yf_ref, yb_ref, hend_ref,
               hf_s, hb_s, buf_f, buf_b, bblk_s, cblk_s):
    i = pl.program_id(0)

    @pl.when(i == 0)
    def _():
        hf_s[...] = jnp.zeros_like(hf_s)
        hb_s[...] = jnp.zeros_like(hb_s)
        for d in range(2):
            bblk_s[d] = _expand_b(b_ref[0, d])
            cblk_s[d] = _expand_c(c_ref[0, d])

    buf_f[...] = jnp.dot(uf_ref[...].astype(BF16), bblk_s[0], preferred_element_type=F32)
    buf_b[...] = jnp.dot(ub_ref[...].astype(BF16), bblk_s[1], preferred_element_type=F32)

    def scan(buf, h_s, d):
        for c in range(S5_N // S5_LANES):
            re = slice(c * S5_LANES, (c + 1) * S5_LANES)
            im = slice(S5_N + c * S5_LANES, S5_N + (c + 1) * S5_LANES)
            ar = a_ref[0, d:d + 1, re]
            ai = a_ref[0, d:d + 1, im]
            hr, hi = h_s[:, re], h_s[:, im]
            for t in range(S5_TB):
                tt = t if d == 0 else S5_TB - 1 - t
                rows = slice(tt * N_SEQ, (tt + 1) * N_SEQ)
                pr, pi = _cmul(ar, ai, hr, hi)
                hr = pr + buf[rows, re]
                hi = pi + buf[rows, im]
                buf[rows, re] = hr
                buf[rows, im] = hi
            h_s[:, re] = hr
            h_s[:, im] = hi

    scan(buf_f, hf_s, 0)
    yf_ref[...] = (jnp.dot(buf_f[...].astype(BF16), cblk_s[0], preferred_element_type=F32)
                   + d_ref[0] * uf_ref[...])
    scan(buf_b, hb_s, 1)
    yb_ref[...] = jnp.dot(buf_b[...].astype(BF16), cblk_s[1], preferred_element_type=F32)

    @pl.when(i == pl.num_programs(0) - 1)
    def _():
        hend_ref[0] = hf_s[...]
        hend_ref[1] = hb_s[...]


def _s5_call(layer, u2, b_cmp, c_cmp, a_bar, s5_d):
    nb = CTX_T // S5_TB
    rows = S5_TB * N_SEQ
    return pl.pallas_call(
        _s5_kernel,
        grid=(nb,),
        in_specs=[
            pl.BlockSpec((rows, S5_W), lambda i: (i, 0)),
            pl.BlockSpec((rows, S5_W), lambda i: (nb - 1 - i, 0)),
            pl.BlockSpec((1, 2, S5_W, 256), lambda i: (layer, 0, 0, 0)),
            pl.BlockSpec((1, 2, 2 * S5_P, S5_W), lambda i: (layer, 0, 0, 0)),
            pl.BlockSpec((1, 2, 2 * S5_N), lambda i: (layer, 0, 0)),
            pl.BlockSpec((1, 1, S5_W), lambda i: (layer, 0, 0)),
        ],
        out_specs=[
            pl.BlockSpec((rows, S5_W), lambda i: (i, 0)),
            pl.BlockSpec((rows, S5_W), lambda i: (nb - 1 - i, 0)),
            pl.BlockSpec((2, N_SEQ, 2 * S5_N), lambda i: (0, 0, 0)),
        ],
        out_shape=[
            jax.ShapeDtypeStruct((N_TOK, S5_W), F32),
            jax.ShapeDtypeStruct((N_TOK, S5_W), F32),
            jax.ShapeDtypeStruct((2, N_SEQ, 2 * S5_N), F32),
        ],
        scratch_shapes=[
            pltpu.VMEM((N_SEQ, 2 * S5_N), F32),
            pltpu.VMEM((N_SEQ, 2 * S5_N), F32),
            pltpu.VMEM((rows, 2 * S5_N), F32),
            pltpu.VMEM((rows, 2 * S5_N), F32),
            pltpu.VMEM((2, S5_W, 2 * S5_N), BF16),
            pltpu.VMEM((2, 2 * S5_N, S5_W), BF16),
        ],
        compiler_params=_cparams(("arbitrary",)),
        name="s5_scan",
    )(u2, u2, b_cmp, c_cmp, a_bar, s5_d.reshape(DEPTH, 1, S5_W))


FIX_SEQS = 4


def _s5_fix_kernel(hend_ref, h0_ref, a_ref, c_ref, o_ref, pf_s, pb_s, cf_s, cb_s, cblk_s):
    s = pl.program_id(0)

    @pl.when(s == 0)
    def _():
        for d in range(2):
            cblk_s[d] = _expand_c(c_ref[0, d])
        row8 = lax.broadcasted_iota(I32, (8, S5_N), 0)
        for d, tab in ((0, pf_s), (1, pb_s)):
            ar = a_ref[0, d:d + 1, 0:S5_N]
            ai = a_ref[0, d:d + 1, S5_N:2 * S5_N]
            pr, pi = ar, ai
            r8 = jnp.zeros((8, S5_N), F32)
            i8 = jnp.zeros((8, S5_N), F32)
            for r in range(8):
                if r:
                    pr, pi = _cmul(pr, pi, ar, ai)
                at = r if d == 0 else 7 - r
                r8 = jnp.where(row8 == at, pr, r8)
                i8 = jnp.where(row8 == at, pi, i8)
            base = 0 if d == 0 else TILE - 8
            tab[base:base + 8, 0:S5_N] = r8
            tab[base:base + 8, S5_N:2 * S5_N] = i8
            m = 8
            while m < TILE:
                if d == 0:
                    src, dst, top = slice(0, m), slice(m, 2 * m), slice(m - 1, m)
                else:
                    src, dst, top = slice(TILE - m, TILE), slice(TILE - 2 * m, TILE - m), slice(TILE - m, TILE - m + 1)
                mr, mi = tab[top, 0:S5_N], tab[top, S5_N:2 * S5_N]
                nr, ni = _cmul(tab[src, 0:S5_N], tab[src, S5_N:2 * S5_N], mr, mi)
                tab[dst, 0:S5_N] = nr
                tab[dst, S5_N:2 * S5_N] = ni
                m *= 2

        for b in range(LAT_B):
            ar, ai = pf_s[TILE - 1:TILE, 0:S5_N], pf_s[TILE - 1:TILE, S5_N:2 * S5_N]
            cr, ci = h0_ref[b, 0:1, 0:S5_N], h0_ref[b, 0:1, S5_N:2 * S5_N]
            for j in range(CHUNKS):
                row = b * CHUNKS + j
                cf_s[row:row + 1, 0:S5_N] = cr
                cf_s[row:row + 1, S5_N:2 * S5_N] = ci
                pr, pi = _cmul(ar, ai, cr, ci)
                cr = pr + hend_ref[0, N_CTX + row:N_CTX + row + 1, 0:S5_N]
                ci = pi + hend_ref[0, N_CTX + row:N_CTX + row + 1, S5_N:2 * S5_N]
            ar, ai = pb_s[0:1, 0:S5_N], pb_s[0:1, S5_N:2 * S5_N]
            cr, ci = h0_ref[b, 1:2, 0:S5_N], h0_ref[b, 1:2, S5_N:2 * S5_N]
            for j in reversed(range(CHUNKS)):
                row = b * CHUNKS + j
                cb_s[row:row + 1, 0:S5_N] = cr
                cb_s[row:row + 1, S5_N:2 * S5_N] = ci
                pr, pi = _cmul(ar, ai, cr, ci)
                cr = pr + hend_ref[1, N_CTX + row:N_CTX + row + 1, 0:S5_N]
                ci = pi + hend_ref[1, N_CTX + row:N_CTX + row + 1, S5_N:2 * S5_N]

    acc = None
    for d, tab, car in ((0, pf_s, cf_s), (1, pb_s, cb_s)):
        hs = []
        for k in range(FIX_SEQS):
            cr = car[pl.ds(s * FIX_SEQS + k, 1), 0:S5_N]
            ci = car[pl.ds(s * FIX_SEQS + k, 1), S5_N:2 * S5_N]
            hr, hi = _cmul(tab[:, 0:S5_N], tab[:, S5_N:2 * S5_N], cr, ci)
            hs.append(jnp.concatenate([hr, hi], axis=1).astype(BF16))
        y = jnp.dot(jnp.concatenate(hs, axis=0), cblk_s[d], preferred_element_type=F32)
        acc = y if acc is None else acc + y
    o_ref[...] = acc


def _s5_fix_call(layer, hend, h0, a_bar, c_cmp):
    return pl.pallas_call(
        _s5_fix_kernel,
        grid=(N_LAT // FIX_SEQS,),
        in_specs=[
            pl.BlockSpec((2, N_SEQ, 2 * S5_N), lambda s: (0, 0, 0)),
            pl.BlockSpec((LAT_B, 2, 2 * S5_N), lambda s: (0, 0, 0)),
            pl.BlockSpec((1, 2, 2 * S5_N), lambda s: (layer, 0, 0)),
            pl.BlockSpec((1, 2, 2 * S5_P, S5_W), lambda s: (layer, 0, 0, 0)),
        ],
        out_specs=pl.BlockSpec((FIX_SEQS * TILE, S5_W), lambda s: (s, 0)),
        out_shape=jax.ShapeDtypeStruct((N_LAT * TILE, S5_W), F32),
        scratch_shapes=[
            pltpu.VMEM((TILE, 2 * S5_N), F32),
            pltpu.VMEM((TILE, 2 * S5_N), F32),
            pltpu.VMEM((N_LAT, 2 * S5_N), F32),
            pltpu.VMEM((N_LAT, 2 * S5_N), F32),
            pltpu.VMEM((2, 2 * S5_N, S5_W), BF16),
        ],
        compiler_params=_cparams(("arbitrary",)),
        name="s5_fix",
    )(hend, h0, a_bar, c_cmp)


LRU_ROWS = 1024


def _sigmoid(x):
    return 0.5 * jnp.tanh(0.5 * x) + 0.5


def _lru_kernel(xc_ref, wa_ref, ba_ref, wi_ref, bi_ref, sp_ref, h0_ref, out_ref, hend_ref,
                a_s, h_s, p_s):
    row = lax.broadcasted_iota(I32, (N_SEQ, LRU_W), 0)
    for d in range(2):
        def gates(cix, _, d=d):
            r0 = pl.multiple_of(cix * LRU_ROWS, LRU_ROWS)
            xc = xc_ref[pl.ds(r0, LRU_ROWS), :]
            xb = xc.astype(BF16)
            r = _sigmoid(jnp.dot(xb, wa_ref[0, d], preferred_element_type=F32) + ba_ref[0, d])
            g = _sigmoid(jnp.dot(xb, wi_ref[0, d], preferred_element_type=F32) + bi_ref[0, d])
            log_a = (-LRU_C) * r * sp_ref[0, d]
            a_s[pl.ds(r0, LRU_ROWS), :] = jnp.exp(log_a)
            th = jnp.tanh(log_a)
            h_s[pl.ds(r0, LRU_ROWS), :] = jnp.sqrt(-2.0 * th / (1.0 - th)) * g * xc
            return 0

        lax.fori_loop(0, N_TOK // LRU_ROWS, gates, 0)

        def scan(t, carry, d=d):
            h, p = carry
            tt = t if d == 0 else TILE - 1 - t
            r0 = pl.multiple_of(tt * N_SEQ, N_SEQ)
            a = a_s[pl.ds(r0, N_SEQ), :]
            h = a * h + h_s[pl.ds(r0, N_SEQ), :]
            p = a * p
            h_s[pl.ds(r0, N_SEQ), :] = h
            p_s[pl.ds(r0, N_SEQ), :] = p
            return h, p

        h_end, p_end = lax.fori_loop(0, TILE, scan,
                                     (jnp.zeros((N_SEQ, LRU_W), F32), jnp.ones((N_SEQ, LRU_W), F32)),
                                     unroll=8)
        hend_ref[d] = h_end

        carry_slab = jnp.zeros((N_SEQ, LRU_W), F32)
        for b in range(LAT_B):
            c = h0_ref[0, b, d:d + 1, :]
            order = range(CHUNKS) if d == 0 else reversed(range(CHUNKS))
            for j in order:
                s = N_CTX + b * CHUNKS + j
                carry_slab = jnp.where(row == s, c, carry_slab)
                c = h_end[s:s + 1, :] + p_end[s:s + 1, :] * c

        def fix(t, _, d=d, carry_slab=carry_slab):
            r0 = pl.multiple_of(t * N_SEQ, N_SEQ)
            v = h_s[pl.ds(r0, N_SEQ), :] + p_s[pl.ds(r0, N_SEQ), :] * carry_slab
            if d == 0:
                out_ref[pl.ds(r0, N_SEQ), :] = v
            else:
                out_ref[pl.ds(r0, N_SEQ), :] += v
            return 0

        lax.fori_loop(0, TILE, fix, 0, unroll=16)


def _lru_call(layer, xc2, wa, ba, wi, bi, sp, h0):
    full = lambda shape: pl.BlockSpec(shape, lambda i: (0,) * len(shape))
    per_layer = lambda shape: pl.BlockSpec((1,) + shape, lambda i: (layer,) + (0,) * len(shape))
    return pl.pallas_call(
        _lru_kernel,
        grid=(1,),
        in_specs=[
            full((N_TOK, LRU_W)),
            per_layer((2, LRU_W, LRU_W)),
            per_layer((2, 1, LRU_W)),
            per_layer((2, LRU_W, LRU_W)),
            per_layer((2, 1, LRU_W)),
            per_layer((2, 1, LRU_W)),
            per_layer((LAT_B, 2, LRU_W)),
        ],
        out_specs=[full((N_TOK, LRU_W)), full((2, N_SEQ, LRU_W))],
        out_shape=[
            jax.ShapeDtypeStruct((N_TOK, LRU_W), F32),
            jax.ShapeDtypeStruct((2, N_SEQ, LRU_W), F32),
        ],
        scratch_shapes=[pltpu.VMEM((N_TOK, LRU_W), F32)] * 3,
        compiler_params=_cparams(("arbitrary",)),
        name="rglru",
    )(xc2, wa, ba, wi, bi, sp, h0)


def _layer_norm(z, g, b):
    mu = jnp.mean(z, axis=-1, keepdims=True)
    zc = z - mu
    var = jnp.mean(zc * zc, axis=-1, keepdims=True)
    return zc * lax.rsqrt(var + EPS) * g + b


def _route(logits):
    lane_i = lax.broadcasted_iota(I32, logits.shape, 1)
    lane = lane_i.astype(F32)
    big = jnp.float32(1024.0)
    neg = jnp.float32(-jnp.inf)
    is_g = lane_i < N_GROUPS
    gmax = jnp.max(jnp.where(is_g, logits, neg), axis=-1, keepdims=True)
    g_sel = jnp.min(jnp.where(jnp.logical_and(is_g, logits == gmax), lane, big), axis=-1, keepdims=True)
    p_group = 1.0 / jnp.sum(jnp.where(is_g, jnp.exp(logits - gmax), 0.0), axis=-1, keepdims=True)
    e_idx = lane_i - GATE_LANE0
    e_group = (e_idx >> 2).astype(F32)
    in_g = jnp.logical_and(jnp.logical_and(e_idx >= 0, e_idx < N_EXPERTS), e_group == g_sel)
    v1 = jnp.max(jnp.where(in_g, logits, neg), axis=-1, keepdims=True)
    i1 = jnp.min(jnp.where(jnp.logical_and(in_g, logits == v1), lane, big), axis=-1, keepdims=True)
    rest = jnp.logical_and(in_g, lane != i1)
    v2 = jnp.max(jnp.where(rest, logits, neg), axis=-1, keepdims=True)
    i2 = jnp.min(jnp.where(jnp.logical_and(rest, logits == v2), lane, big), axis=-1, keepdims=True)
    e2 = jnp.exp(v2 - v1)
    inv = 1.0 / (1.0 + e2)
    w1 = inv * p_group
    w2 = e2 * inv * p_group
    return jnp.where(lane == i1, w1, jnp.where(lane == i2, w2, jnp.where(lane_i == 0, g_sel, 0.0)))


def _outproj_kernel(x_ref, oc_ref, ol_ref, yf_ref, yb_ref, yfix_ref, hl_ref, g_ref, mod_ref,
                    wglu_ref, bglu_ref, wout_ref, lng_ref, lnb_ref, rw_ref, rwhi_ref, rb_ref,
                    x1_ref, hm_ref, gates_ref, y_s, hl_s):
    i = pl.program_id(0)
    s = pl.program_id(1)

    @pl.when(s == 0)
    def _():
        for j in range(GROUP_SEQS):
            rows = slice(j * TILE, (j + 1) * TILE)
            y_s[rows, :] = yf_ref[:, j, :] + yb_ref[:, j, :]
            hl_s[rows, :] = hl_ref[:, j, :]

    lat = i >= CTX_GROUP_TILES
    sub_rows = pl.ds(pl.multiple_of(s * SUB_ROWS, SUB_ROWS), SUB_ROWS)
    o_att = jnp.where(lat, ol_ref[...], oc_ref[...])
    y = y_s[sub_rows, :] + jnp.where(lat, yfix_ref[...], 0.0)
    g = jax.nn.gelu(y, approximate=True)
    glu = jnp.dot(g.astype(BF16), wglu_ref[0], preferred_element_type=F32) + bglu_ref[0]
    o_s5 = g * jax.nn.sigmoid(glu)
    o_lru = hl_s[sub_rows, :] * jax.nn.gelu(g_ref[...], approximate=True)
    mix = jnp.concatenate([o_att, o_s5.astype(BF16), o_lru.astype(BF16)], axis=-1)
    out = jnp.dot(mix, wout_ref[0], preferred_element_type=F32)
    r = _group_mod_row(i)
    g1 = mod_ref[0, pl.ds(r, 1), 2 * D:3 * D]
    sh2 = mod_ref[0, pl.ds(r, 1), 3 * D:4 * D]
    sc2 = mod_ref[0, pl.ds(r, 1), 4 * D:5 * D]
    x1 = _layer_norm(ALPHA * x_ref[...] + g1 * out, lng_ref[0], lnb_ref[0])
    x1_ref[...] = x1
    hm = x1 * (1.0 + sc2) + sh2
    hm_ref[...] = hm.astype(BF16)
    hm_hi = hm.astype(BF16)
    hm_lo = (hm - hm_hi.astype(F32)).astype(BF16)
    p_hi = jnp.dot(hm_hi, rw_ref[0], preferred_element_type=F32)
    p_lo = jnp.dot(hm_lo, rwhi_ref[0], preferred_element_type=F32)
    logits = p_hi[:, 0:128] + p_hi[:, 128:256] + p_lo + rb_ref[0]
    gates_ref[...] = _route(logits)


def _outproj_call(layer, x, o_ctx, o_lat, yf, yb, yfix, hl, g_lru, mod, w_glu, b_glu,
                  w_out, ln_g, ln_b, rw_split, rw_hi, rb):
    n_ctx_blocks = N_CTX * TILE // SUB_ROWS
    row_spec = lambda w: pl.BlockSpec((SUB_ROWS, w), lambda i, s: (i * SUBS + s, 0))
    lat_spec = lambda w: pl.BlockSpec((SUB_ROWS, w), lambda i, s: (jnp.maximum(i * SUBS + s - n_ctx_blocks, 0), 0))
    tm_spec = pl.BlockSpec((TILE, GROUP_SEQS, 256), lambda i, s: (0, i, 0))
    vec = lambda n: pl.BlockSpec((1, 1, n), lambda i, s: (layer, 0, 0))
    mat = lambda a, b: pl.BlockSpec((1, a, b), lambda i, s: (layer, 0, 0))
    return pl.pallas_call(
        _outproj_kernel,
        grid=(N_GROUP_TILES, SUBS),
        in_specs=[
            row_spec(D),
            pl.BlockSpec((SUB_ROWS, 512), lambda i, s: (jnp.minimum(i * SUBS + s, n_ctx_blocks - 1), 0)),
            lat_spec(512),
            tm_spec, tm_spec,
            lat_spec(S5_W),
            tm_spec,
            row_spec(LRU_W),
            mat(MOD_ROWS, N_MOD * D),
            mat(S5_W, S5_W), vec(S5_W),
            mat(D, D), vec(D), vec(D),
            mat(D, 256), mat(D, 128), vec(128),
        ],
        out_specs=[row_spec(D), row_spec(D), row_spec(128)],
        out_shape=[
            jax.ShapeDtypeStruct((N_TOK, D), F32),
            jax.ShapeDtypeStruct((N_TOK, D), BF16),
            jax.ShapeDtypeStruct((N_TOK, 128), F32),
        ],
        scratch_shapes=[pltpu.VMEM((GROUP_ROWS, 256), F32)] * 2,
        compiler_params=_cparams(("arbitrary", "arbitrary")),
        name="outproj",
    )(x, o_ctx, o_lat, yf.reshape(TILE, N_SEQ, S5_W), yb.reshape(TILE, N_SEQ, S5_W), yfix,
      hl.reshape(TILE, N_SEQ, LRU_W), g_lru, mod, w_glu,
      b_glu.reshape(DEPTH, 1, S5_W), w_out, ln_g.reshape(DEPTH, 1, D), ln_b.reshape(DEPTH, 1, D),
      rw_split, rw_hi, rb)


HALF = N_TOK // 2
HALF_TILES = HALF // TILE


def _sort_kernel(gates_ref, pos_ref, cnt_ref, rank_s):
    sub = lax.broadcasted_iota(I32, (8, TILE), 0)
    sel_r = lax.broadcasted_iota(I32, (8, 128), 0)
    sel_c = lax.broadcasted_iota(I32, (8, 128), 1)
    pick_lane0 = jnp.where(jnp.logical_and(sel_r == 0, sel_c == 0), 1.0, 0.0).astype(BF16)
    ri = lax.broadcasted_iota(I32, (TILE, TILE), 0)
    ci = lax.broadcasted_iota(I32, (TILE, TILE), 1)
    upper = jnp.where(ri <= ci, 1.0, 0.0).astype(BF16)

    def count(b, carry):
        rows = slice(b * TILE, (b + 1) * TILE)
        g = gates_ref[rows, :].astype(BF16)
        g_t = lax.dot_general(pick_lane0, g, _NT, preferred_element_type=F32)
        g_sel = jnp.sum(g_t, axis=0, keepdims=True)
        onehot = jnp.where(jnp.logical_and(sub < N_GROUPS, sub.astype(F32) == g_sel), 1.0, 0.0)
        cum = jnp.dot(onehot.astype(BF16), upper, preferred_element_type=F32) + carry
        rank_s[b] = jnp.where(onehot > 0.0, cum, 0.0)
        return carry + jnp.sum(onehot, axis=1, keepdims=True)

    counts = jnp.zeros((8, 1), F32)
    for b in range(HALF_TILES):
        counts = count(b, counts)
    sub1 = lax.broadcasted_iota(I32, (8, 1), 0)
    c = [jnp.sum(jnp.where(sub1 == g, counts, 0.0), axis=0, keepdims=True) for g in range(N_GROUPS - 1)]
    start = jnp.where(sub1 == 1, c[0], jnp.where(sub1 == 2, c[0] + c[1],
                      jnp.where(sub1 == 3, c[0] + c[1] + c[2], 0.0)))

    for b in range(HALF_TILES):
        rk = rank_s[b]
        p = jnp.sum(jnp.where(rk > 0.0, rk + start - 1.0, 0.0), axis=0, keepdims=True)
        pos_ref[b:b + 1, :] = p.astype(I32)
    cnt_ref[0] = jnp.broadcast_to(counts, (8, 128))


def _sort_call(gates):
    return pl.pallas_call(
        _sort_kernel,
        grid=(2,),
        in_specs=[pl.BlockSpec((HALF, 128), lambda h: (h, 0))],
        out_specs=[pl.BlockSpec((HALF_TILES, TILE), lambda h: (h, 0)),
                   pl.BlockSpec((1, 8, 128), lambda h: (h, 0, 0))],
        out_shape=[jax.ShapeDtypeStruct((N_SEQ, TILE), I32), jax.ShapeDtypeStruct((2, 8, 128), F32)],
        scratch_shapes=[pltpu.VMEM((HALF_TILES, 8, TILE), F32)],
        compiler_params=_cparams(("arbitrary",)),
        name="group_sort",
    )(gates)


ITEM_TILES = 3
N_ITEMS = EPG * ((HALF_TILES + N_GROUPS - 1 + (ITEM_TILES - 1) * N_GROUPS) // ITEM_TILES)
SCATTER_ROWS = 2 * TILE
SCATTER_STEPS = HALF // SCATTER_ROWS
MOE_STEPS = SCATTER_STEPS + N_ITEMS + HALF_TILES


def _moe_schedule(counts):
    start = jnp.cumsum(counts, axis=1) - counts
    lo = start // TILE
    hi = (start + counts + TILE - 1) // TILE
    tiles_g = jnp.where(counts > 0, hi - lo, 0)
    n_g = (tiles_g + ITEM_TILES - 1) // ITEM_TILES
    n_e = jnp.repeat(n_g, EPG, axis=1)
    lo_e = jnp.repeat(lo, EPG, axis=1)
    tiles_e = jnp.repeat(tiles_g, EPG, axis=1)
    off_end = jnp.cumsum(n_e, axis=1)
    off = off_end - n_e
    total = off_end[:, -1:]
    w = jnp.arange(N_ITEMS, dtype=I32)[None, :]
    w_eff = jnp.minimum(w, total - 1)
    e_w = jnp.sum((w_eff[:, :, None] >= off_end[:, None, :]).astype(I32), axis=-1)
    first = ITEM_TILES * (w_eff - jnp.take_along_axis(off, e_w, axis=1))
    tile = jnp.take_along_axis(lo_e, e_w, axis=1) + first
    size = jnp.minimum(jnp.take_along_axis(tiles_e, e_w, axis=1) - first, ITEM_TILES)
    size = jnp.where(w < total, size, 0)
    e_f, n_f = e_w.reshape(-1), size.reshape(-1)
    n_all = 2 * N_ITEMS
    idx = jnp.arange(n_all, dtype=I32)
    key = (idx // N_ITEMS) * N_EXPERTS + e_f
    run_start = jnp.logical_and(n_f > 0, key != jnp.concatenate([jnp.full((1,), -1, I32), key[:-1]]))
    slot = (jnp.cumsum(run_start.astype(I32)) - 1) % 2
    at_or_after = lax.cummin(jnp.where(run_start, idx, n_all), reverse=True)
    after = jnp.concatenate([at_or_after[1:], jnp.full((1,), n_all, I32)])
    next_e = jnp.where(after < n_all, e_f[jnp.minimum(after, n_all - 1)], -1)
    return e_f, tile.reshape(-1), n_f, run_start.astype(I32), slot, next_e


def _moe_kernel(pos_ref, ite_ref, itt_ref, itv_ref, itf_ref, its_ref, itx_ref,
                hm_ref, gates_ref, wg_hbm, wu_hbm, wd_hbm, x1_ref, mod_ref, lng_ref, lnb_ref,
                *rest, split, layer):
    out_refs, (xs_s, gs_s, acc_s, hm_s, wg_b, wu_b, wd_b, w_sem) = rest[:-8], rest[-8:]
    h = pl.program_id(0)
    w = pl.program_id(1)
    base = h * HALF

    def weight_copies(e, slot):
        pairs = ((wg_hbm, wg_b), (wu_hbm, wu_b), (wd_hbm, wd_b))
        return [pltpu.make_async_copy(src.at[layer, e], dst.at[slot], w_sem.at[slot, k])
                for k, (src, dst) in enumerate(pairs)]

    @pl.when(w < SCATTER_STEPS)
    def _():
        @pl.when(w == 0)
        def _():
            acc_s[...] = jnp.zeros_like(acc_s)

        @pl.when(jnp.logical_and(h == 0, w == 0))
        def _():
            for c in weight_copies(ite_ref[0], 0):
                c.start()

        hm_s[...] = hm_ref[...].astype(F32)

        def body(r, c):
            p = pos_ref[base + w * SCATTER_ROWS + r]
            xs_s[pl.ds(p, 1), :] = hm_s[pl.ds(r, 1), :]
            gs_s[pl.ds(p, 1), :] = gates_ref[pl.ds(r, 1), :]
            return c

        lax.fori_loop(0, SCATTER_ROWS, body, 0, unroll=64)

    @pl.when(jnp.logical_and(w >= SCATTER_STEPS, w < SCATTER_STEPS + N_ITEMS))
    def _():
        idx = h * N_ITEMS + (w - SCATTER_STEPS)

        slot = its_ref[idx]

        @pl.when(itf_ref[idx] > 0)
        def _():
            @pl.when(itx_ref[idx] >= 0)
            def _():
                for c in weight_copies(itx_ref[idx], 1 - slot):
                    c.start()

            for c in weight_copies(ite_ref[idx], slot):
                c.wait()

        def item(n_tiles):
            e = ite_ref[idx]
            rows = pl.ds(pl.multiple_of(itt_ref[idx] * TILE, TILE), n_tiles * TILE)
            x = xs_s[rows, :].astype(BF16)
            a = jnp.dot(x, wg_b[slot].astype(BF16), preferred_element_type=F32)
            u = jnp.dot(x, wu_b[slot].astype(BF16), preferred_element_type=F32)
            g = gs_s[rows, :]
            lane = lax.broadcasted_iota(I32, g.shape, 1)
            ge = jnp.sum(jnp.where(lane == e + GATE_LANE0, g, 0.0), axis=-1, keepdims=True)
            act = ((a * jax.nn.sigmoid(a)) * u * ge).astype(BF16)
            acc_s[rows, :] += jnp.dot(act, wd_b[slot].astype(BF16), preferred_element_type=F32)

        for n_tiles in range(1, ITEM_TILES + 1):
            pl.when(itv_ref[idx] == n_tiles)(functools.partial(item, n_tiles))

    @pl.when(w >= SCATTER_STEPS + N_ITEMS)
    def _():
        j = w - SCATTER_STEPS - N_ITEMS

        def finish(o_ref):
            def body(r, c):
                p = pos_ref[base + j * TILE + r]
                o_ref[pl.ds(r, 1), :] = acc_s[pl.ds(p, 1), :]
                return c

            lax.fori_loop(0, TILE, body, 0, unroll=64)
            r = jnp.where(h == 0, 0, 1 + j // CHUNKS)
            g2 = mod_ref[0, pl.ds(r, 1), 5 * D:6 * D]
            o_ref[...] = _layer_norm(ALPHA * x1_ref[...] + g2 * o_ref[...], lng_ref[0], lnb_ref[0])

        if split:
            for half, o_ref in enumerate(out_refs):
                pl.when(h == half)(functools.partial(finish, o_ref))
        else:
            finish(out_refs[0])


def _moe_call(layer, pos, items, hm, gates, w_gate, w_up, w_down, x1, mod, ln_g, ln_b, split):
    def in_tile(h, w, *_):
        return (h * SCATTER_STEPS + jnp.minimum(w, SCATTER_STEPS - 1), 0)

    def out_step(w):
        return jnp.clip(w - SCATTER_STEPS - N_ITEMS, 0, HALF_TILES - 1)

    def out_tile(h, w, *_):
        return (h * HALF_TILES + out_step(w), 0)

    if split:
        out_specs = [pl.BlockSpec((TILE, D), lambda h, w, *_: (jnp.where(h == 0, out_step(w), HALF_TILES - 1), 0)),
                     pl.BlockSpec((TILE, D), lambda h, w, *_: (jnp.where(h == 1, out_step(w), 0), 0))]
        out_shape = [jax.ShapeDtypeStruct((HALF, D), F32)] * 2
    else:
        out_specs = pl.BlockSpec((TILE, D), out_tile)
        out_shape = jax.ShapeDtypeStruct((N_TOK, D), F32)

    vec = lambda n: pl.BlockSpec((1, 1, n), lambda h, w, *_: (layer, 0, 0))
    grid_spec = pltpu.PrefetchScalarGridSpec(
        num_scalar_prefetch=1 + len(items),
        grid=(2, MOE_STEPS),
        in_specs=[
            pl.BlockSpec((SCATTER_ROWS, D), in_tile),
            pl.BlockSpec((SCATTER_ROWS, 128), in_tile),
            pl.BlockSpec(memory_space=pl.ANY),
            pl.BlockSpec(memory_space=pl.ANY),
            pl.BlockSpec(memory_space=pl.ANY),
            pl.BlockSpec((TILE, D), out_tile),
            pl.BlockSpec((1, MOD_ROWS, N_MOD * D), lambda h, w, *_: (layer, 0, 0)),
            vec(D), vec(D),
        ],
        out_specs=out_specs,
        scratch_shapes=[
            pltpu.VMEM((HALF, D), F32),
            pltpu.VMEM((HALF, 128), F32),
            pltpu.VMEM((HALF, D), F32),
            pltpu.VMEM((SCATTER_ROWS, D), F32),
            pltpu.VMEM((2, D, D_EXPERT), F32),
            pltpu.VMEM((2, D, D_EXPERT), F32),
            pltpu.VMEM((2, D_EXPERT, D), F32),
            pltpu.SemaphoreType.DMA((2, 3)),
        ],
    )
    return pl.pallas_call(
        functools.partial(_moe_kernel, split=split, layer=layer),
        grid_spec=grid_spec,
        out_shape=out_shape,
        compiler_params=pltpu.CompilerParams(dimension_semantics=("arbitrary", "arbitrary"),
                                             vmem_limit_bytes=MOE_VMEM_LIMIT),
        name="moe",
    )(pos, *items, hm, gates, w_gate, w_up, w_down, x1, mod,
      ln_g.reshape(DEPTH, 1, D), ln_b.reshape(DEPTH, 1, D))


def _rope_tables():
    rows = LAT_T // GRID_W
    row = jnp.repeat(jnp.arange(rows, dtype=F32), GRID_W)
    col = jnp.tile(jnp.arange(GRID_W, dtype=F32), rows)
    n_freq = QK // 4
    inv = ROPE_BASE ** (-jnp.arange(n_freq, dtype=F32) / n_freq)
    ang_r = row[:, None] * inv
    ang_c = col[:, None] * inv
    cos64 = jnp.concatenate([jnp.cos(ang_r), jnp.cos(ang_r), jnp.cos(ang_c), jnp.cos(ang_c)], axis=1)
    sin64 = jnp.concatenate([-jnp.sin(ang_r), jnp.sin(ang_r), -jnp.sin(ang_c), jnp.sin(ang_c)], axis=1)
    cos = jnp.concatenate([jnp.tile(cos64, (1, 2)), jnp.ones((SUB_ROWS, 128), F32)], axis=0)
    sin = jnp.concatenate([jnp.tile(sin64, (1, 2)), jnp.zeros((SUB_ROWS, 128), F32)], axis=0)
    return cos, sin


def _s5_params(a_re, a_im, b_re, b_im, c_re, c_im, log_dt):
    dt = jnp.exp(log_dt)[..., None]
    mag = jnp.exp(a_re * dt)
    abar_r = mag * jnp.cos(a_im * dt)
    abar_i = mag * jnp.sin(a_im * dt)
    den = a_re * a_re + a_im * a_im
    nr = abar_r - 1.0
    coef_r = (nr * a_re + abar_i * a_im) / den
    coef_i = (abar_i * a_re - nr * a_im) / den
    bbar_r = coef_r[..., None] * b_re - coef_i[..., None] * b_im
    bbar_i = coef_r[..., None] * b_im + coef_i[..., None] * b_re
    def b_rows(bb):
        return bb.transpose(0, 1, 2, 4, 3).reshape(DEPTH, 2, S5_W, S5_P)
    br, bi = b_rows(bbar_r), b_rows(bbar_i)
    b_cmp = jnp.concatenate([br, br, bi, bi], axis=-1)
    def c_rows(cc):
        return cc.transpose(0, 1, 4, 2, 3).reshape(DEPTH, 2, S5_P, S5_W)
    c_cmp = jnp.concatenate([c_rows(c_re), c_rows(-c_im)], axis=-2)
    a_bar = jnp.concatenate([abar_r.reshape(DEPTH, 2, S5_N), abar_i.reshape(DEPTH, 2, S5_N)], axis=-1)
    return b_cmp, c_cmp, a_bar


def _block_diag(w):
    eye = jnp.eye(LRU_BLOCKS, dtype=F32)
    m = jnp.einsum('ldkij,kh->ldkihj', w, eye)
    return m.reshape(DEPTH, 2, LRU_W, LRU_W).astype(BF16)


def kernel(x_prompt, x_sample, cache_k, cache_v, state_s5, state_lru, c, c_ctx, w_ada, b_ada, w_in, w_out, lam_q1, lam_k1, lam_q2, lam_k2, subln_g, s5_a_re, s5_a_im, s5_b_re, s5_b_im, s5_c_re, s5_c_im, s5_log_dt, s5_d, s5_w_glu, s5_b_glu, lru_conv_w, lru_conv_b, lru_w_a, lru_b_a, lru_w_i, lru_b_i, lru_lambda, ln1_g, ln1_b, ln2_g, ln2_b, router_group_w, router_group_b, router_expert_w, router_expert_b, moe_w_gate, moe_w_up, moe_w_down):
    x = jnp.concatenate([x_prompt.reshape(CTX_B * CTX_T, D), x_sample.reshape(LAT_B * LAT_T, D)], axis=0)
    cond = jnp.concatenate([c_ctx[None, :], c, jnp.zeros((MOD_ROWS - 1 - LAT_B, D), F32)], axis=0)
    mod = _ada_call(cond, w_ada, b_ada)

    w_out_b = w_out.astype(BF16)
    w_glu_b = s5_w_glu.astype(BF16)
    rope_cos, rope_sin = _rope_tables()
    b_cmp, c_cmp, a_bar = _s5_params(s5_a_re, s5_a_im, s5_b_re, s5_b_im, s5_c_re, s5_c_im, s5_log_dt)
    wa_blk = _block_diag(lru_w_a)
    wi_blk = _block_diag(lru_w_i)
    ba = lru_b_a.reshape(DEPTH, 2, 1, LRU_W)
    bi = lru_b_i.reshape(DEPTH, 2, 1, LRU_W)
    sp = jax.nn.softplus(-lru_lambda).reshape(DEPTH, 2, 1, LRU_W)
    lam = (jnp.exp(jnp.sum(lam_q1 * lam_k1, axis=-1)) - jnp.exp(jnp.sum(lam_q2 * lam_k2, axis=-1)))
    rw = jnp.concatenate([router_group_w, router_expert_w,
                          jnp.zeros((DEPTH, D, 128 - N_GROUPS - N_EXPERTS), F32)], axis=-1)
    rw_hi = rw.astype(BF16)
    rw_split = jnp.concatenate([rw_hi, (rw - rw_hi.astype(F32)).astype(BF16)], axis=-1)
    rb = jnp.concatenate([router_group_b, router_expert_b,
                          jnp.zeros((DEPTH, 128 - N_GROUPS - N_EXPERTS), F32)], axis=-1).reshape(DEPTH, 1, 128)
    s5_h0 = state_s5.reshape(LAT_B, DEPTH, 2, 2 * S5_N)
    lru_h0 = state_lru.transpose(1, 0, 2, 3)

    kc = jnp.zeros((CTX_B, DEPTH, HEADS, CTX_T, 2 * QK), F32)
    vc = jnp.zeros((CTX_B, DEPTH, HEADS, CTX_T, DV), F32)
    s5_states = []
    lru_states = []
    for l in range(DEPTH):
        lambda_init = 0.8 - 0.6 * math.exp(-0.3 * l)
        lam_l = (lam[l] + lambda_init).reshape(1)
        q, k, v, kc, vc, u_tm, xc_tm, g_lru = _inproj_call(
            l, x, mod, w_in, rope_cos, rope_sin, lru_conv_w, lru_conv_b, kc, vc)
        o_ctx = _att_ctx_call(l, lam_l, q, k, v, subln_g, 1.0 - lambda_init)
        o_lat = _att_lat_call(l, lam_l, q, k, v, cache_k, cache_v, subln_g, 1.0 - lambda_init)
        yf, yb, hend = _s5_call(l, u_tm.reshape(N_TOK, S5_W), b_cmp, c_cmp, a_bar, s5_d)
        yfix = _s5_fix_call(l, hend, s5_h0[:, l], a_bar, c_cmp)
        hl, lru_end = _lru_call(l, xc_tm.reshape(N_TOK, LRU_W), wa_blk, ba, wi_blk, bi, sp, lru_h0)
        x1, hm, gates = _outproj_call(l, x, o_ctx, o_lat, yf, yb, yfix, hl, g_lru, mod,
                                      w_glu_b, s5_b_glu, w_out_b, ln1_g, ln1_b, rw_split, rw_hi, rb)
        pos, counts = _sort_call(gates)
        items = _moe_schedule(counts[:, :N_GROUPS, 0].astype(I32))
        x = _moe_call(l, pos.reshape(N_TOK), items, hm, gates, moe_w_gate, moe_w_up,
                      moe_w_down, x1, mod, ln2_g, ln2_b, split=(l == DEPTH - 1))
        s5_states.append(hend[:, :N_CTX].reshape(2, CTX_B, 2, S5_G, S5_P).transpose(1, 0, 2, 3, 4))
        lru_states.append(lru_end[:, :N_CTX].transpose(1, 0, 2))

    y_p = x[0].reshape(CTX_B, CTX_T, D)
    y_s = x[1].reshape(LAT_B, LAT_T, D)
    return (y_p, y_s, kc, vc, jnp.stack(s5_states, axis=1), jnp.stack(lru_states, axis=1))
```

```python
import functools
import math

import jax
import jax.numpy as jnp
from jax import lax
from jax.experimental import pallas as pl
from jax.experimental.pallas import tpu as pltpu

F32 = jnp.float32
BF16 = jnp.bfloat16
I32 = jnp.int32

D = 1024
DEPTH = 4
CTX_B = 16
CTX_T = 256
LAT_B = 2
LAT_T = 2048
PAST = 256
GRID_W = 64
HEADS = 4
QK = 64
DV = 128
S5_W = 256
S5_G = 16
S5_C = 16
S5_P = 64
S5_N = S5_G * S5_P
S5_C_SHIFT = S5_C.bit_length() - 1
S5_P_SHIFT = S5_P.bit_length() - 1
LRU_W = 256
LRU_BLOCKS = 4
LRU_C = 8.0
N_GROUPS = 4
EPG = 4
N_EXPERTS = 16
D_EXPERT = 512
N_MOD = 6
IN_W = 2304
ROPE_BASE = 10000.0
ALPHA = (2 * DEPTH) ** 0.25
EPS = 1e-5

TILE = 256
N_TOK = CTX_B * CTX_T + LAT_B * LAT_T
N_SEQ = N_TOK // TILE
N_CTX = CTX_B
CHUNKS = LAT_T // TILE
N_LAT = LAT_B * CHUNKS
MOD_ROWS = 8
GATE_LANE0 = 4
VMEM_LIMIT = 56 * 1024 * 1024
MOE_VMEM_LIMIT = 60 * 1024 * 1024

GROUP_SEQS = 8
GROUP_ROWS = GROUP_SEQS * TILE
N_GROUP_TILES = N_TOK // GROUP_ROWS
CTX_GROUP_TILES = N_CTX // GROUP_SEQS
SUB_ROWS = 512
SUBS = GROUP_ROWS // SUB_ROWS
SUB_SEQS = SUB_ROWS // TILE


def _cparams(sem):
    return pltpu.CompilerParams(dimension_semantics=sem, vmem_limit_bytes=VMEM_LIMIT)


def _group_mod_row(i):
    return jnp.where(i < CTX_GROUP_TILES, 0, i - CTX_GROUP_TILES + 1)


def _ada_kernel(c_ref, w_ref, b_ref, o_ref):
    c = c_ref[...]
    s = (c * jax.nn.sigmoid(c)).astype(BF16)
    o_ref[0] = jnp.dot(s, w_ref[0].astype(BF16), preferred_element_type=F32) + b_ref[0]


def _ada_call(cond, w_ada, b_ada):
    tn = 1536
    return pl.pallas_call(
        _ada_kernel,
        grid=(DEPTH, N_MOD * D // tn),
        in_specs=[
            pl.BlockSpec((MOD_ROWS, D), lambda l, j: (0, 0)),
            pl.BlockSpec((1, D, tn), lambda l, j: (l, 0, j)),
            pl.BlockSpec((1, 1, tn), lambda l, j: (l, 0, j)),
        ],
        out_specs=pl.BlockSpec((1, MOD_ROWS, tn), lambda l, j: (l, 0, j)),
        out_shape=jax.ShapeDtypeStruct((DEPTH, MOD_ROWS, N_MOD * D), F32),
        compiler_params=_cparams(("arbitrary", "arbitrary")),
        name="adaln",
    )(cond, w_ada, b_ada.reshape(DEPTH, 1, N_MOD * D))


def _inproj_kernel(x_ref, mod_ref, w_ref, cos_ref, sin_ref, cw_ref, cb_ref, kc_in, vc_in,
                   q_ref, k_ref, v_ref, kc_ref, vc_ref, u_ref, xc_ref, g_ref, u_s, xl_s, w_s):
    del kc_in, vc_in
    i = pl.program_id(0)
    s = pl.program_id(1)

    @pl.when(jnp.logical_and(i == 0, s == 0))
    def _():
        for t in range(2 * HEADS):
            kind, head = divmod(t, HEADS)
            halves = []
            for m in range(2):
                col = (2 * kind + m) * HEADS * QK + head * QK
                blk = w_ref[0, :, (col // 128) * 128:(col // 128 + 1) * 128]
                halves.append(blk[:, col % 128:col % 128 + QK])
            w_s[:, t * 128:(t + 1) * 128] = jnp.concatenate(halves, axis=1).astype(BF16)
        w_s[:, 1024:IN_W] = w_ref[0, :, 1024:IN_W].astype(BF16)

    r = _group_mod_row(i)
    sh = mod_ref[0, pl.ds(r, 1), 0:D]
    sc = mod_ref[0, pl.ds(r, 1), D:2 * D]
    xm = (x_ref[...] * (1.0 + sc) + sh).astype(BF16)
    proj = jnp.dot(xm, w_s[...], preferred_element_type=F32)

    qk = proj[:, 0:1024]
    cos = jnp.concatenate([cos_ref[...]] * 8, axis=1)
    sin = jnp.concatenate([sin_ref[...]] * 8, axis=1)
    lane = lax.broadcasted_iota(I32, qk.shape, 1)
    swapped = jnp.where((lane & 31) < 16, pltpu.roll(qk, 1024 - 16, 1), pltpu.roll(qk, 16, 1))
    qk = qk * cos + swapped * sin
    q_ref[...] = qk[:, 0:512].astype(BF16)
    k_ref[...] = qk[:, 512:1024].astype(BF16)
    v = proj[:, 1024:1536]
    v_ref[...] = v.astype(BF16)

    @pl.when(i < CTX_GROUP_TILES)
    def _():
        for jj in range(SUB_SEQS):
            rows = slice(jj * TILE, (jj + 1) * TILE)
            for h in range(HEADS):
                kc_ref[jj, 0, h] = qk[rows, 512 + h * 128:512 + (h + 1) * 128]
                vc_ref[jj, 0, h] = v[rows, h * 128:(h + 1) * 128]

    g_ref[...] = proj[:, 2048:2304]
    sub_rows = pl.ds(pl.multiple_of(s * SUB_ROWS, SUB_ROWS), SUB_ROWS)
    u_s[sub_rows, :] = proj[:, 1536:1792]
    xl_s[sub_rows, :] = proj[:, 1792:2048]

    @pl.when(s == SUBS - 1)
    def _():
        xl = xl_s[...]
        row = lax.broadcasted_iota(I32, xl.shape, 0)
        is_ctx = i < CTX_GROUP_TILES
        pos = jnp.where(is_ctx, row & (TILE - 1), row)
        last = jnp.where(is_ctx, TILE - 1, GROUP_ROWS - 1)
        x_m1 = jnp.where(pos == 0, 0.0, pltpu.roll(xl, 1, 0))
        x_p1 = jnp.where(pos == last, 0.0, pltpu.roll(xl, GROUP_ROWS - 1, 0))
        x_p2 = jnp.where(pos >= last - 1, 0.0, pltpu.roll(xl, GROUP_ROWS - 2, 0))
        cw = cw_ref[0]
        xc = cb_ref[0] + x_m1 * cw[0:1] + xl * cw[1:2] + x_p1 * cw[2:3] + x_p2 * cw[3:4]
        for j in range(GROUP_SEQS):
            xc_ref[:, j, :] = xc[j * TILE:(j + 1) * TILE]
            u_ref[:, j, :] = u_s[j * TILE:(j + 1) * TILE, :]


def _inproj_call(layer, x, mod, w_in, rope_cos, rope_sin, conv_w, conv_b, kc, vc):
    n_ctx_blocks = N_CTX // SUB_SEQS

    def rope_idx(i, s):
        return (jnp.where(i < CTX_GROUP_TILES, SUBS, s), 0)

    def cache_idx(i, s):
        return (jnp.minimum(i * SUBS + s, n_ctx_blocks - 1), layer, 0, 0, 0)

    row_spec = lambda w: pl.BlockSpec((SUB_ROWS, w), lambda i, s: (i * SUBS + s, 0))
    tm_spec = pl.BlockSpec((TILE, GROUP_SEQS, 256), lambda i, s: (0, i, 0))
    cache_spec = pl.BlockSpec((SUB_SEQS, 1, HEADS, TILE, 128), cache_idx)
    return pl.pallas_call(
        _inproj_kernel,
        grid=(N_GROUP_TILES, SUBS),
        in_specs=[
            row_spec(D),
            pl.BlockSpec((1, MOD_ROWS, N_MOD * D), lambda i, s: (layer, 0, 0)),
            pl.BlockSpec((1, D, IN_W), lambda i, s: (layer, 0, 0)),
            pl.BlockSpec((SUB_ROWS, 128), rope_idx),
            pl.BlockSpec((SUB_ROWS, 128), rope_idx),
            pl.BlockSpec((1, 4, LRU_W), lambda i, s: (layer, 0, 0)),
            pl.BlockSpec((1, 1, LRU_W), lambda i, s: (layer, 0, 0)),
            pl.BlockSpec(memory_space=pl.ANY),
            pl.BlockSpec(memory_space=pl.ANY),
        ],
        out_specs=[
            row_spec(512), row_spec(512), row_spec(512),
            cache_spec, cache_spec,
            tm_spec, tm_spec,
            row_spec(LRU_W),
        ],
        out_shape=[
            jax.ShapeDtypeStruct((N_TOK, 512), BF16),
            jax.ShapeDtypeStruct((N_TOK, 512), BF16),
            jax.ShapeDtypeStruct((N_TOK, 512), BF16),
            jax.ShapeDtypeStruct(kc.shape, F32),
            jax.ShapeDtypeStruct(vc.shape, F32),
            jax.ShapeDtypeStruct((TILE, N_SEQ, S5_W), F32),
            jax.ShapeDtypeStruct((TILE, N_SEQ, LRU_W), F32),
            jax.ShapeDtypeStruct((N_TOK, LRU_W), F32),
        ],
        scratch_shapes=[pltpu.VMEM((GROUP_ROWS, S5_W), F32), pltpu.VMEM((GROUP_ROWS, LRU_W), F32),
                        pltpu.VMEM((D, IN_W), BF16)],
        input_output_aliases={7: 3, 8: 4},
        compiler_params=_cparams(("arbitrary", "arbitrary")),
        name="inproj",
    )(x, mod, w_in, rope_cos, rope_sin, conv_w, conv_b.reshape(DEPTH, 1, LRU_W), kc, vc)


_NT = (((1,), (1,)), ((), ()))


def _split_maps(q):
    lane = lax.broadcasted_iota(I32, q.shape, 1)
    zero = jnp.zeros_like(q)
    qs = q * (QK ** -0.5)
    return jnp.where(lane < QK, qs, zero), jnp.where(lane >= QK, qs, zero)


def _softmax_values(qm, keys, values):
    scores = [lax.dot_general(qm, k, _NT, preferred_element_type=F32) for k in keys]
    m = scores[0].max(axis=-1, keepdims=True)
    for sc in scores[1:]:
        m = jnp.maximum(m, sc.max(axis=-1, keepdims=True))
    z = None
    o = None
    for sc, v in zip(scores, values):
        e = jnp.exp(sc - m)
        ez = e.sum(axis=-1, keepdims=True)
        eo = jnp.dot(e.astype(BF16), v, preferred_element_type=F32)
        z = ez if z is None else z + ez
        o = eo if o is None else o + eo
    return o * (1.0 / z)


def _diff_attention(q, keys, values, lam, g, post_scale):
    q1, q2 = _split_maps(q)
    o = _softmax_values(q1, keys, values) - lam * _softmax_values(q2, keys, values)
    ms = jnp.mean(o * o, axis=-1, keepdims=True)
    return (o * lax.rsqrt(ms + EPS) * g) * post_scale


def _att_ctx_kernel(lam_ref, q_ref, k_ref, v_ref, g_ref, o_ref, *, post_scale):
    lam = lam_ref[0]
    for s in range(ATT_CTX_SEQS):
        rows = slice(s * TILE, (s + 1) * TILE)
        for h in range(HEADS):
            cols = slice(h * 128, (h + 1) * 128)
            o = _diff_attention(q_ref[rows, cols], [k_ref[rows, cols]], [v_ref[rows, cols]], lam, g_ref[0],
                                post_scale)
            o_ref[rows, cols] = o.astype(BF16)


ATT_CTX_SEQS = 4


def _att_ctx_call(layer, lam, q, k, v, subln_g, post_scale):
    blk = pl.BlockSpec((ATT_CTX_SEQS * TILE, HEADS * 128), lambda b: (b, 0))
    return pl.pallas_call(
        functools.partial(_att_ctx_kernel, post_scale=post_scale),
        grid=(CTX_B // ATT_CTX_SEQS,),
        in_specs=[
            pl.BlockSpec(memory_space=pltpu.SMEM),
            blk, blk, blk,
            pl.BlockSpec((1, 1, DV), lambda b: (layer, 0, 0)),
        ],
        out_specs=blk,
        out_shape=jax.ShapeDtypeStruct((CTX_B * CTX_T, HEADS * DV), BF16),
        compiler_params=_cparams(("arbitrary",)),
        name="att_ctx",
    )(lam, q, k, v, subln_g.reshape(DEPTH, 1, DV))


def _att_lat_kernel(lam_ref, q_ref, k_ref, v_ref, ck_ref, cv_ref, g_ref, o_ref, *, post_scale):
    half = LAT_T // 2
    o = _diff_attention(q_ref[...], [k_ref[0:half, :], k_ref[half:LAT_T, :], ck_ref[...].astype(BF16)],
                        [v_ref[0:half, :], v_ref[half:LAT_T, :], cv_ref[...].astype(BF16)],
                        lam_ref[0], g_ref[0], post_scale)
    o_ref[...] = o.astype(BF16)


def _att_lat_call(layer, lam, q, k, v, cache_k, cache_v, subln_g, post_scale):
    lat0 = N_CTX
    cache_spec = pl.BlockSpec((None, None, None, PAST, 128), lambda b, h, t: (b, layer, h, 0, 0))
    return pl.pallas_call(
        functools.partial(_att_lat_kernel, post_scale=post_scale),
        grid=(LAT_B, HEADS, CHUNKS),
        in_specs=[
            pl.BlockSpec(memory_space=pltpu.SMEM),
            pl.BlockSpec((TILE, 128), lambda b, h, t: (lat0 + b * CHUNKS + t, h)),
            pl.BlockSpec((LAT_T, 128), lambda b, h, t: (lat0 // CHUNKS + b, h)),
            pl.BlockSpec((LAT_T, 128), lambda b, h, t: (lat0 // CHUNKS + b, h)),
            cache_spec, cache_spec,
            pl.BlockSpec((1, 1, DV), lambda b, h, t: (layer, 0, 0)),
        ],
        out_specs=pl.BlockSpec((TILE, 128), lambda b, h, t: (b * CHUNKS + t, h)),
        out_shape=jax.ShapeDtypeStruct((LAT_B * LAT_T, HEADS * DV), BF16),
        compiler_params=_cparams(("arbitrary", "arbitrary", "arbitrary")),
        name="att_lat",
    )(lam, q, k, v, cache_k, cache_v, subln_g.reshape(DEPTH, 1, DV))


S5_TB = 32
S5_LANES = 128


def _cmul(ar, ai, br, bi):
    return ar * br - ai * bi, ar * bi + ai * br


def _expand_b(b):
    full = jnp.concatenate([b[:, 0:128]] * (S5_N // 128) + [b[:, 128:256]] * (S5_N // 128), axis=1)
    row = lax.broadcasted_iota(I32, full.shape, 0)
    col = lax.broadcasted_iota(I32, full.shape, 1)
    same_group = (row >> S5_C_SHIFT) == ((col & (S5_N - 1)) >> S5_P_SHIFT)
    return jnp.where(same_group, full, 0.0).astype(BF16)


def _expand_c(c):
    full = jnp.concatenate([c[0:S5_P]] * S5_G + [c[S5_P:2 * S5_P]] * S5_G, axis=0)
    row = lax.broadcasted_iota(I32, full.shape, 0)
    col = lax.broadcasted_iota(I32, full.shape, 1)
    same_group = ((row & (S5_N - 1)) >> S5_P_SHIFT) == (col >> S5_C_SHIFT)
    return jnp.where(same_group, full, 0.0).astype(BF16)


def _s5_kernel(uf_ref, ub_ref, b_ref, c_ref, a_ref, d_ref, yf_ref, yb_ref, hend_ref,
               hf_s, hb_s, buf_f, buf_b, bblk_s, cblk_s):
    i = pl.program_id(0)

    @pl.when(i == 0)
    def _():
        hf_s[...] = jnp.zeros_like(hf_s)
        hb_s[...] = jnp.zeros_like(hb_s)
        for d in range(2):
            bblk_s[d] = _expand_b(b_ref[0, d])
            cblk_s[d] = _expand_c(c_ref[0, d])

    buf_f[...] = jnp.dot(uf_ref[...].astype(BF16), bblk_s[0], preferred_element_type=F32)
    buf_b[...] = jnp.dot(ub_ref[...].astype(BF16), bblk_s[1], preferred_element_type=F32)

    def scan(buf, h_s, d):
        for c in range(S5_N // S5_LANES):
            re = slice(c * S5_LANES, (c + 1) * S5_LANES)
            im = slice(S5_N + c * S5_LANES, S5_N + (c + 1) * S5_LANES)
            ar = a_ref[0, d:d + 1, re]
            ai = a_ref[0, d:d + 1, im]
            hr, hi = h_s[:, re], h_s[:, im]
            for t in range(S5_TB):
                tt = t if d == 0 else S5_TB - 1 - t
                rows = slice(tt * N_SEQ, (tt + 1) * N_SEQ)
                pr, pi = _cmul(ar, ai, hr, hi)
                hr = pr + buf[rows, re]
                hi = pi + buf[rows, im]
                buf[rows, re] = hr
                buf[rows, im] = hi
            h_s[:, re] = hr
            h_s[:, im] = hi

    scan(buf_f, hf_s, 0)
    yf_ref[...] = (jnp.dot(buf_f[...].astype(BF16), cblk_s[0], preferred_element_type=F32)
                   + d_ref[0] * uf_ref[...])
    scan(buf_b, hb_s, 1)
    yb_ref[...] = jnp.dot(buf_b[...].astype(BF16), cblk_s[1], preferred_element_type=F32)

    @pl.when(i == pl.num_programs(0) - 1)
    def _():
        hend_ref[0] = hf_s[...]
        hend_ref[1] = hb_s[...]


def _s5_call(layer, u2, b_cmp, c_cmp, a_bar, s5_d):
    nb = CTX_T // S5_TB
    rows = S5_TB * N_SEQ
    return pl.pallas_call(
        _s5_kernel,
        grid=(nb,),
        in_specs=[
            pl.BlockSpec((rows, S5_W), lambda i: (i, 0)),
            pl.BlockSpec((rows, S5_W), lambda i: (nb - 1 - i, 0)),
            pl.BlockSpec((1, 2, S5_W, 256), lambda i: (layer, 0, 0, 0)),
            pl.BlockSpec((1, 2, 2 * S5_P, S5_W), lambda i: (layer, 0, 0, 0)),
            pl.BlockSpec((1, 2, 2 * S5_N), lambda i: (layer, 0, 0)),
            pl.BlockSpec((1, 1, S5_W), lambda i: (layer, 0, 0)),
        ],
        out_specs=[
            pl.BlockSpec((rows, S5_W), lambda i: (i, 0)),
            pl.BlockSpec((rows, S5_W), lambda i: (nb - 1 - i, 0)),
            pl.BlockSpec((2, N_SEQ, 2 * S5_N), lambda i: (0, 0, 0)),
        ],
        out_shape=[
            jax.ShapeDtypeStruct((N_TOK, S5_W), F32),
            jax.ShapeDtypeStruct((N_TOK, S5_W), F32),
            jax.ShapeDtypeStruct((2, N_SEQ, 2 * S5_N), F32),
        ],
        scratch_shapes=[
            pltpu.VMEM((N_SEQ, 2 * S5_N), F32),
            pltpu.VMEM((N_SEQ, 2 * S5_N), F32),
            pltpu.VMEM((rows, 2 * S5_N), F32),
            pltpu.VMEM((rows, 2 * S5_N), F32),
            pltpu.VMEM((2, S5_W, 2 * S5_N), BF16),
            pltpu.VMEM((2, 2 * S5_N, S5_W), BF16),
        ],
        compiler_params=_cparams(("arbitrary",)),
        name="s5_scan",
    )(u2, u2, b_cmp, c_cmp, a_bar, s5_d.reshape(DEPTH, 1, S5_W))


FIX_SEQS = 4


def _s5_fix_kernel(hend_ref, h0_ref, a_ref, c_ref, o_ref, pf_s, pb_s, cf_s, cb_s, cblk_s):
    s = pl.program_id(0)

    @pl.when(s == 0)
    def _():
        for d in range(2):
            cblk_s[d] = _expand_c(c_ref[0, d])
        row8 = lax.broadcasted_iota(I32, (8, S5_N), 0)
        for d, tab in ((0, pf_s), (1, pb_s)):
            ar = a_ref[0, d:d + 1, 0:S5_N]
            ai = a_ref[0, d:d + 1, S5_N:2 * S5_N]
            pr, pi = ar, ai
            r8 = jnp.zeros((8, S5_N), F32)
            i8 = jnp.zeros((8, S5_N), F32)
            for r in range(8):
                if r:
                    pr, pi = _cmul(pr, pi, ar, ai)
                at = r if d == 0 else 7 - r
                r8 = jnp.where(row8 == at, pr, r8)
                i8 = jnp.where(row8 == at, pi, i8)
            base = 0 if d == 0 else TILE - 8
            tab[base:base + 8, 0:S5_N] = r8
            tab[base:base + 8, S5_N:2 * S5_N] = i8
            m = 8
            while m < TILE:
                if d == 0:
                    src, dst, top = slice(0, m), slice(m, 2 * m), slice(m - 1, m)
                else:
                    src, dst, top = slice(TILE - m, TILE), slice(TILE - 2 * m, TILE - m), slice(TILE - m, TILE - m + 1)
                mr, mi = tab[top, 0:S5_N], tab[top, S5_N:2 * S5_N]
                nr, ni = _cmul(tab[src, 0:S5_N], tab[src, S5_N:2 * S5_N], mr, mi)
                tab[dst, 0:S5_N] = nr
                tab[dst, S5_N:2 * S5_N] = ni
                m *= 2

        for b in range(LAT_B):
            ar, ai = pf_s[TILE - 1:TILE, 0:S5_N], pf_s[TILE - 1:TILE, S5_N:2 * S5_N]
            cr, ci = h0_ref[b, 0:1, 0:S5_N], h0_ref[b, 0:1, S5_N:2 * S5_N]
            for j in range(CHUNKS):
                row = b * CHUNKS + j
                cf_s[row:row + 1, 0:S5_N] = cr
                cf_s[row:row + 1, S5_N:2 * S5_N] = ci
                pr, pi = _cmul(ar, ai, cr, ci)
                cr = pr + hend_ref[0, N_CTX + row:N_CTX + row + 1, 0:S5_N]
                ci = pi + hend_ref[0, N_CTX + row:N_CTX + row + 1, S5_N:2 * S5_N]
            ar, ai = pb_s[0:1, 0:S5_N], pb_s[0:1, S5_N:2 * S5_N]
            cr, ci = h0_ref[b, 1:2, 0:S5_N], h0_ref[b, 1:2, S5_N:2 * S5_N]
            for j in reversed(range(CHUNKS)):
                row = b * CHUNKS + j
                cb_s[row:row + 1, 0:S5_N] = cr
                cb_s[row:row + 1, S5_N:2 * S5_N] = ci
                pr, pi = _cmul(ar, ai, cr, ci)
                cr = pr + hend_ref[1, N_CTX + row:N_CTX + row + 1, 0:S5_N]
                ci = pi + hend_ref[1, N_CTX + row:N_CTX + row + 1, S5_N:2 * S5_N]

    acc = None
    for d, tab, car in ((0, pf_s, cf_s), (1, pb_s, cb_s)):
        hs = []
        for k in range(FIX_SEQS):
            cr = car[pl.ds(s * FIX_SEQS + k, 1), 0:S5_N]
            ci = car[pl.ds(s * FIX_SEQS + k, 1), S5_N:2 * S5_N]
            hr, hi = _cmul(tab[:, 0:S5_N], tab[:, S5_N:2 * S5_N], cr, ci)
            hs.append(jnp.concatenate([hr, hi], axis=1).astype(BF16))
        y = jnp.dot(jnp.concatenate(hs, axis=0), cblk_s[d], preferred_element_type=F32)
        acc = y if acc is None else acc + y
    o_ref[...] = acc


def _s5_fix_call(layer, hend, h0, a_bar, c_cmp):
    return pl.pallas_call(
        _s5_fix_kernel,
        grid=(N_LAT // FIX_SEQS,),
        in_specs=[
            pl.BlockSpec((2, N_SEQ, 2 * S5_N), lambda s: (0, 0, 0)),
            pl.BlockSpec((LAT_B, 2, 2 * S5_N), lambda s: (0, 0, 0)),
            pl.BlockSpec((1, 2, 2 * S5_N), lambda s: (layer, 0, 0)),
            pl.BlockSpec((1, 2, 2 * S5_P, S5_W), lambda s: (layer, 0, 0, 0)),
        ],
        out_specs=pl.BlockSpec((FIX_SEQS * TILE, S5_W), lambda s: (s, 0)),
        out_shape=jax.ShapeDtypeStruct((N_LAT * TILE, S5_W), F32),
        scratch_shapes=[
            pltpu.VMEM((TILE, 2 * S5_N), F32),
            pltpu.VMEM((TILE, 2 * S5_N), F32),
            pltpu.VMEM((N_LAT, 2 * S5_N), F32),
            pltpu.VMEM((N_LAT, 2 * S5_N), F32),
            pltpu.VMEM((2, 2 * S5_N, S5_W), BF16),
        ],
        compiler_params=_cparams(("arbitrary",)),
        name="s5_fix",
    )(hend, h0, a_bar, c_cmp)


LRU_ROWS = 1024


def _sigmoid(x):
    return 0.5 * jnp.tanh(0.5 * x) + 0.5


def _lru_kernel(xc_ref, wa_ref, ba_ref, wi_ref, bi_ref, sp_ref, h0_ref, out_ref, hend_ref,
                a_s, h_s, p_s):
    row = lax.broadcasted_iota(I32, (N_SEQ, LRU_W), 0)
    for d in range(2):
        def gates(cix, _, d=d):
            r0 = pl.multiple_of(cix * LRU_ROWS, LRU_ROWS)
            xc = xc_ref[pl.ds(r0, LRU_ROWS), :]
            xb = xc.astype(BF16)
            r = _sigmoid(jnp.dot(xb, wa_ref[0, d], preferred_element_type=F32) + ba_ref[0, d])
            g = _sigmoid(jnp.dot(xb, wi_ref[0, d], preferred_element_type=F32) + bi_ref[0, d])
            log_a = (-LRU_C) * r * sp_ref[0, d]
            a_s[pl.ds(r0, LRU_ROWS), :] = jnp.exp(log_a)
            th = jnp.tanh(log_a)
            h_s[pl.ds(r0, LRU_ROWS), :] = jnp.sqrt(-2.0 * th / (1.0 - th)) * g * xc
            return 0

        lax.fori_loop(0, N_TOK // LRU_ROWS, gates, 0)

        def scan(t, carry, d=d):
            h, p = carry
            tt = t if d == 0 else TILE - 1 - t
            r0 = pl.multiple_of(tt * N_SEQ, N_SEQ)
            a = a_s[pl.ds(r0, N_SEQ), :]
            h = a * h + h_s[pl.ds(r0, N_SEQ), :]
            p = a * p
            h_s[pl.ds(r0, N_SEQ), :] = h
            p_s[pl.ds(r0, N_SEQ), :] = p
            return h, p

        h_end, p_end = lax.fori_loop(0, TILE, scan,
                                     (jnp.zeros((N_SEQ, LRU_W), F32), jnp.ones((N_SEQ, LRU_W), F32)),
                                     unroll=8)
        hend_ref[d] = h_end

        carry_slab = jnp.zeros((N_SEQ, LRU_W), F32)
        for b in range(LAT_B):
            c = h0_ref[0, b, d:d + 1, :]
            order = range(CHUNKS) if d == 0 else reversed(range(CHUNKS))
            for j in order:
                s = N_CTX + b * CHUNKS + j
                carry_slab = jnp.where(row == s, c, carry_slab)
                c = h_end[s:s + 1, :] + p_end[s:s + 1, :] * c

        def fix(t, _, d=d, carry_slab=carry_slab):
            r0 = pl.multiple_of(t * N_SEQ, N_SEQ)
            v = h_s[pl.ds(r0, N_SEQ), :] + p_s[pl.ds(r0, N_SEQ), :] * carry_slab
            if d == 0:
                out_ref[pl.ds(r0, N_SEQ), :] = v
            else:
                out_ref[pl.ds(r0, N_SEQ), :] += v
            return 0

        lax.fori_loop(0, TILE, fix, 0, unroll=16)


def _lru_call(layer, xc2, wa, ba, wi, bi, sp, h0):
    full = lambda shape: pl.BlockSpec(shape, lambda i: (0,) * len(shape))
    per_layer = lambda shape: pl.BlockSpec((1,) + shape, lambda i: (layer,) + (0,) * len(shape))
    return pl.pallas_call(
        _lru_kernel,
        grid=(1,),
        in_specs=[
            full((N_TOK, LRU_W)),
            per_layer((2, LRU_W, LRU_W)),
            per_layer((2, 1, LRU_W)),
            per_layer((2, LRU_W, LRU_W)),
            per_layer((2, 1, LRU_W)),
            per_layer((2, 1, LRU_W)),
            per_layer((LAT_B, 2, LRU_W)),
        ],
        out_specs=[full((N_TOK, LRU_W)), full((2, N_SEQ, LRU_W))],
        out_shape=[
            jax.ShapeDtypeStruct((N_TOK, LRU_W), F32),
            jax.ShapeDtypeStruct((2, N_SEQ, LRU_W), F32),
        ],
        scratch_shapes=[pltpu.VMEM((N_TOK, LRU_W), F32)] * 3,
        compiler_params=_cparams(("arbitrary",)),
        name="rglru",
    )(xc2, wa, ba, wi, bi, sp, h0)


def _layer_norm(z, g, b):
    mu = jnp.mean(z, axis=-1, keepdims=True)
    zc = z - mu
    var = jnp.mean(zc * zc, axis=-1, keepdims=True)
    return zc * lax.rsqrt(var + EPS) * g + b


def _route(logits):
    lane_i = lax.broadcasted_iota(I32, logits.shape, 1)
    lane = lane_i.astype(F32)
    big = jnp.float32(1024.0)
    neg = jnp.float32(-jnp.inf)
    is_g = lane_i < N_GROUPS
    gmax = jnp.max(jnp.where(is_g, logits, neg), axis=-1, keepdims=True)
    g_sel = jnp.min(jnp.where(jnp.logical_and(is_g, logits == gmax), lane, big), axis=-1, keepdims=True)
    p_group = 1.0 / jnp.sum(jnp.where(is_g, jnp.exp(logits - gmax), 0.0), axis=-1, keepdims=True)
    e_idx = lane_i - GATE_LANE0
    e_group = (e_idx >> 2).astype(F32)
    in_g = jnp.logical_and(jnp.logical_and(e_idx >= 0, e_idx < N_EXPERTS), e_group == g_sel)
    v1 = jnp.max(jnp.where(in_g, logits, neg), axis=-1, keepdims=True)
    i1 = jnp.min(jnp.where(jnp.logical_and(in_g, logits == v1), lane, big), axis=-1, keepdims=True)
    rest = jnp.logical_and(in_g, lane != i1)
    v2 = jnp.max(jnp.where(rest, logits, neg), axis=-1, keepdims=True)
    i2 = jnp.min(jnp.where(jnp.logical_and(rest, logits == v2), lane, big), axis=-1, keepdims=True)
    e2 = jnp.exp(v2 - v1)
    inv = 1.0 / (1.0 + e2)
    w1 = inv * p_group
    w2 = e2 * inv * p_group
    return jnp.where(lane == i1, w1, jnp.where(lane == i2, w2, jnp.where(lane_i == 0, g_sel, 0.0)))


def _outproj_kernel(x_ref, oc_ref, ol_ref, yf_ref, yb_ref, yfix_ref, hl_ref, g_ref, mod_ref,
                    wglu_ref, bglu_ref, wout_ref, lng_ref, lnb_ref, rw_ref, rwhi_ref, rb_ref,
                    x1_ref, hm_ref, gates_ref, y_s, hl_s):
    i = pl.program_id(0)
    s = pl.program_id(1)

    @pl.when(s == 0)
    def _():
        for j in range(GROUP_SEQS):
            rows = slice(j * TILE, (j + 1) * TILE)
            y_s[rows, :] = yf_ref[:, j, :] + yb_ref[:, j, :]
            hl_s[rows, :] = hl_ref[:, j, :]

    lat = i >= CTX_GROUP_TILES
    sub_rows = pl.ds(pl.multiple_of(s * SUB_ROWS, SUB_ROWS), SUB_ROWS)
    o_att = jnp.where(lat, ol_ref[...], oc_ref[...])
    y = y_s[sub_rows, :] + jnp.where(lat, yfix_ref[...], 0.0)
    g = jax.nn.gelu(y, approximate=True)
    glu = jnp.dot(g.astype(BF16), wglu_ref[0], preferred_element_type=F32) + bglu_ref[0]
    o_s5 = g * jax.nn.sigmoid(glu)
    o_lru = hl_s[sub_rows, :] * jax.nn.gelu(g_ref[...], approximate=True)
    mix = jnp.concatenate([o_att, o_s5.astype(BF16), o_lru.astype(BF16)], axis=-1)
    out = jnp.dot(mix, wout_ref[0], preferred_element_type=F32)
    r = _group_mod_row(i)
    g1 = mod_ref[0, pl.ds(r, 1), 2 * D:3 * D]
    sh2 = mod_ref[0, pl.ds(r, 1), 3 * D:4 * D]
    sc2 = mod_ref[0, pl.ds(r, 1), 4 * D:5 * D]
    x1 = _layer_norm(ALPHA * x_ref[...] + g1 * out, lng_ref[0], lnb_ref[0])
    x1_ref[...] = x1
    hm = x1 * (1.0 + sc2) + sh2
    hm_ref[...] = hm.astype(BF16)
    hm_hi = hm.astype(BF16)
    hm_lo = (hm - hm_hi.astype(F32)).astype(BF16)
    p_hi = jnp.dot(hm_hi, rw_ref[0], preferred_element_type=F32)
    p_lo = jnp.dot(hm_lo, rwhi_ref[0], preferred_element_type=F32)
    logits = p_hi[:, 0:128] + p_hi[:, 128:256] + p_lo + rb_ref[0]
    gates_ref[...] = _route(logits)


def _outproj_call(layer, x, o_ctx, o_lat, yf, yb, yfix, hl, g_lru, mod, w_glu, b_glu,
                  w_out, ln_g, ln_b, rw_split, rw_hi, rb):
    n_ctx_blocks = N_CTX * TILE // SUB_ROWS
    row_spec = lambda w: pl.BlockSpec((SUB_ROWS, w), lambda i, s: (i * SUBS + s, 0))
    lat_spec = lambda w: pl.BlockSpec((SUB_ROWS, w), lambda i, s: (jnp.maximum(i * SUBS + s - n_ctx_blocks, 0), 0))
    tm_spec = pl.BlockSpec((TILE, GROUP_SEQS, 256), lambda i, s: (0, i, 0))
    vec = lambda n: pl.BlockSpec((1, 1, n), lambda i, s: (layer, 0, 0))
    mat = lambda a, b: pl.BlockSpec((1, a, b), lambda i, s: (layer, 0, 0))
    return pl.pallas_call(
        _outproj_kernel,
        grid=(N_GROUP_TILES, SUBS),
        in_specs=[
            row_spec(D),
            pl.BlockSpec((SUB_ROWS, 512), lambda i, s: (jnp.minimum(i * SUBS + s, n_ctx_blocks - 1), 0)),
            lat_spec(512),
            tm_spec, tm_spec,
            lat_spec(S5_W),
            tm_spec,
            row_spec(LRU_W),
            mat(MOD_ROWS, N_MOD * D),
            mat(S5_W, S5_W), vec(S5_W),
            mat(D, D), vec(D), vec(D),
            mat(D, 256), mat(D, 128), vec(128),
        ],
        out_specs=[row_spec(D), row_spec(D), row_spec(128)],
        out_shape=[
            jax.ShapeDtypeStruct((N_TOK, D), F32),
            jax.ShapeDtypeStruct((N_TOK, D), BF16),
            jax.ShapeDtypeStruct((N_TOK, 128), F32),
        ],
        scratch_shapes=[pltpu.VMEM((GROUP_ROWS, 256), F32)] * 2,
        compiler_params=_cparams(("arbitrary", "arbitrary")),
        name="outproj",
    )(x, o_ctx, o_lat, yf.reshape(TILE, N_SEQ, S5_W), yb.reshape(TILE, N_SEQ, S5_W), yfix,
      hl.reshape(TILE, N_SEQ, LRU_W), g_lru, mod, w_glu,
      b_glu.reshape(DEPTH, 1, S5_W), w_out, ln_g.reshape(DEPTH, 1, D), ln_b.reshape(DEPTH, 1, D),
      rw_split, rw_hi, rb)


HALF = N_TOK // 2
HALF_TILES = HALF // TILE


def _sort_kernel(gates_ref, pos_ref, cnt_ref, rank_s):
    sub = lax.broadcasted_iota(I32, (8, TILE), 0)
    sel_r = lax.broadcasted_iota(I32, (8, 128), 0)
    sel_c = lax.broadcasted_iota(I32, (8, 128), 1)
    pick_lane0 = jnp.where(jnp.logical_and(sel_r == 0, sel_c == 0), 1.0, 0.0).astype(BF16)
    ri = lax.broadcasted_iota(I32, (TILE, TILE), 0)
    ci = lax.broadcasted_iota(I32, (TILE, TILE), 1)
    upper = jnp.where(ri <= ci, 1.0, 0.0).astype(BF16)

    def count(b, carry):
        rows = slice(b * TILE, (b + 1) * TILE)
        g = gates_ref[rows, :].astype(BF16)
        g_t = lax.dot_general(pick_lane0, g, _NT, preferred_element_type=F32)
        g_sel = jnp.sum(g_t, axis=0, keepdims=True)
        onehot = jnp.where(jnp.logical_and(sub < N_GROUPS, sub.astype(F32) == g_sel), 1.0, 0.0)
        cum = jnp.dot(onehot.astype(BF16), upper, preferred_element_type=F32) + carry
        rank_s[b] = jnp.where(onehot > 0.0, cum, 0.0)
        return carry + jnp.sum(onehot, axis=1, keepdims=True)

    counts = jnp.zeros((8, 1), F32)
    for b in range(HALF_TILES):
        counts = count(b, counts)
    sub1 = lax.broadcasted_iota(I32, (8, 1), 0)
    c = [jnp.sum(jnp.where(sub1 == g, counts, 0.0), axis=0, keepdims=True) for g in range(N_GROUPS - 1)]
    start = jnp.where(sub1 == 1, c[0], jnp.where(sub1 == 2, c[0] + c[1],
                      jnp.where(sub1 == 3, c[0] + c[1] + c[2], 0.0)))

    for b in range(HALF_TILES):
        rk = rank_s[b]
        p = jnp.sum(jnp.where(rk > 0.0, rk + start - 1.0, 0.0), axis=0, keepdims=True)
        pos_ref[b:b + 1, :] = p.astype(I32)
    cnt_ref[0] = jnp.broadcast_to(counts, (8, 128))


def _sort_call(gates):
    return pl.pallas_call(
        _sort_kernel,
        grid=(2,),
        in_specs=[pl.BlockSpec((HALF, 128), lambda h: (h, 0))],
        out_specs=[pl.BlockSpec((HALF_TILES, TILE), lambda h: (h, 0)),
                   pl.BlockSpec((1, 8, 128), lambda h: (h, 0, 0))],
        out_shape=[jax.ShapeDtypeStruct((N_SEQ, TILE), I32), jax.ShapeDtypeStruct((2, 8, 128), F32)],
        scratch_shapes=[pltpu.VMEM((HALF_TILES, 8, TILE), F32)],
        compiler_params=_cparams(("arbitrary",)),
        name="group_sort",
    )(gates)


ITEM_TILES = 3
N_ITEMS = EPG * ((HALF_TILES + N_GROUPS - 1 + (ITEM_TILES - 1) * N_GROUPS) // ITEM_TILES)
SCATTER_ROWS = 2 * TILE
SCATTER_STEPS = HALF // SCATTER_ROWS
MOE_STEPS = SCATTER_STEPS + N_ITEMS + HALF_TILES


def _moe_schedule(counts):
    start = jnp.cumsum(counts, axis=1) - counts
    lo = start // TILE
    hi = (start + counts + TILE - 1) // TILE
    tiles_g = jnp.where(counts > 0, hi - lo, 0)
    n_g = (tiles_g + ITEM_TILES - 1) // ITEM_TILES
    n_e = jnp.repeat(n_g, EPG, axis=1)
    lo_e = jnp.repeat(lo, EPG, axis=1)
    tiles_e = jnp.repeat(tiles_g, EPG, axis=1)
    off_end = jnp.cumsum(n_e, axis=1)
    off = off_end - n_e
    total = off_end[:, -1:]
    w = jnp.arange(N_ITEMS, dtype=I32)[None, :]
    w_eff = jnp.minimum(w, total - 1)
    e_w = jnp.sum((w_eff[:, :, None] >= off_end[:, None, :]).astype(I32), axis=-1)
    first = ITEM_TILES * (w_eff - jnp.take_along_axis(off, e_w, axis=1))
    tile = jnp.take_along_axis(lo_e, e_w, axis=1) + first
    size = jnp.minimum(jnp.take_along_axis(tiles_e, e_w, axis=1) - first, ITEM_TILES)
    size = jnp.where(w < total, size, 0)
    e_f, n_f = e_w.reshape(-1), size.reshape(-1)
    n_all = 2 * N_ITEMS
    idx = jnp.arange(n_all, dtype=I32)
    key = (idx // N_ITEMS) * N_EXPERTS + e_f
    run_start = jnp.logical_and(n_f > 0, key != jnp.concatenate([jnp.full((1,), -1, I32), key[:-1]]))
    slot = (jnp.cumsum(run_start.astype(I32)) - 1) % 2
    at_or_after = lax.cummin(jnp.where(run_start, idx, n_all), reverse=True)
    after = jnp.concatenate([at_or_after[1:], jnp.full((1,), n_all, I32)])
    next_e = jnp.where(after < n_all, e_f[jnp.minimum(after, n_all - 1)], -1)
    return e_f, tile.reshape(-1), n_f, run_start.astype(I32), slot, next_e


def _moe_kernel(pos_ref, ite_ref, itt_ref, itv_ref, itf_ref, its_ref, itx_ref,
                hm_ref, gates_ref, wg_hbm, wu_hbm, wd_hbm, x1_ref, mod_ref, lng_ref, lnb_ref,
                *rest, split, layer):
    out_refs, (xs_s, gs_s, acc_s, hm_s, wg_b, wu_b, wd_b, w_sem) = rest[:-8], rest[-8:]
    h = pl.program_id(0)
    w = pl.program_id(1)
    base = h * HALF

    def weight_copies(e, slot):
        pairs = ((wg_hbm, wg_b), (wu_hbm, wu_b), (wd_hbm, wd_b))
        return [pltpu.make_async_copy(src.at[layer, e], dst.at[slot], w_sem.at[slot, k])
                for k, (src, dst) in enumerate(pairs)]

    @pl.when(w < SCATTER_STEPS)
    def _():
        @pl.when(w == 0)
        def _():
            acc_s[...] = jnp.zeros_like(acc_s)

        @pl.when(jnp.logical_and(h == 0, w == 0))
        def _():
            for c in weight_copies(ite_ref[0], 0):
                c.start()

        hm_s[...] = hm_ref[...].astype(F32)

        def body(r, c):
            p = pos_ref[base + w * SCATTER_ROWS + r]
            xs_s[pl.ds(p, 1), :] = hm_s[pl.ds(r, 1), :]
            gs_s[pl.ds(p, 1), :] = gates_ref[pl.ds(r, 1), :]
            return c

        lax.fori_loop(0, SCATTER_ROWS, body, 0, unroll=64)

    @pl.when(jnp.logical_and(w >= SCATTER_STEPS, w < SCATTER_STEPS + N_ITEMS))
    def _():
        idx = h * N_ITEMS + (w - SCATTER_STEPS)

        slot = its_ref[idx]

        @pl.when(itf_ref[idx] > 0)
        def _():
            @pl.when(itx_ref[idx] >= 0)
            def _():
                for c in weight_copies(itx_ref[idx], 1 - slot):
                    c.start()

            for c in weight_copies(ite_ref[idx], slot):
                c.wait()

        def item(n_tiles):
            e = ite_ref[idx]
            rows = pl.ds(pl.multiple_of(itt_ref[idx] * TILE, TILE), n_tiles * TILE)
            x = xs_s[rows, :].astype(BF16)
            a = jnp.dot(x, wg_b[slot].astype(BF16), preferred_element_type=F32)
            u = jnp.dot(x, wu_b[slot].astype(BF16), preferred_element_type=F32)
            g = gs_s[rows, :]
            lane = lax.broadcasted_iota(I32, g.shape, 1)
            ge = jnp.sum(jnp.where(lane == e + GATE_LANE0, g, 0.0), axis=-1, keepdims=True)
            act = ((a * jax.nn.sigmoid(a)) * u * ge).astype(BF16)
            acc_s[rows, :] += jnp.dot(act, wd_b[slot].astype(BF16), preferred_element_type=F32)

        for n_tiles in range(1, ITEM_TILES + 1):
            pl.when(itv_ref[idx] == n_tiles)(functools.partial(item, n_tiles))

    @pl.when(w >= SCATTER_STEPS + N_ITEMS)
    def _():
        j = w - SCATTER_STEPS - N_ITEMS

        def finish(o_ref):
            def body(r, c):
                p = pos_ref[base + j * TILE + r]
                o_ref[pl.ds(r, 1), :] = acc_s[pl.ds(p, 1), :]
                return c

            lax.fori_loop(0, TILE, body, 0, unroll=64)
            r = jnp.where(h == 0, 0, 1 + j // CHUNKS)
            g2 = mod_ref[0, pl.ds(r, 1), 5 * D:6 * D]
            o_ref[...] = _layer_norm(ALPHA * x1_ref[...] + g2 * o_ref[...], lng_ref[0], lnb_ref[0])

        if split:
            for half, o_ref in enumerate(out_refs):
                pl.when(h == half)(functools.partial(finish, o_ref))
        else:
            finish(out_refs[0])


def _moe_call(layer, pos, items, hm, gates, w_gate, w_up, w_down, x1, mod, ln_g, ln_b, split):
    def in_tile(h, w, *_):
        return (h * SCATTER_STEPS + jnp.minimum(w, SCATTER_STEPS - 1), 0)

    def out_step(w):
        return jnp.clip(w - SCATTER_STEPS - N_ITEMS, 0, HALF_TILES - 1)

    def out_tile(h, w, *_):
        return (h * HALF_TILES + out_step(w), 0)

    if split:
        out_specs = [pl.BlockSpec((TILE, D), lambda h, w, *_: (jnp.where(h == 0, out_step(w), HALF_TILES - 1), 0)),
                     pl.BlockSpec((TILE, D), lambda h, w, *_: (jnp.where(h == 1, out_step(w), 0), 0))]
        out_shape = [jax.ShapeDtypeStruct((HALF, D), F32)] * 2
    else:
        out_specs = pl.BlockSpec((TILE, D), out_tile)
        out_shape = jax.ShapeDtypeStruct((N_TOK, D), F32)

    vec = lambda n: pl.BlockSpec((1, 1, n), lambda h, w, *_: (layer, 0, 0))
    grid_spec = pltpu.PrefetchScalarGridSpec(
        num_scalar_prefetch=1 + len(items),
        grid=(2, MOE_STEPS),
        in_specs=[
            pl.BlockSpec((SCATTER_ROWS, D), in_tile),
            pl.BlockSpec((SCATTER_ROWS, 128), in_tile),
            pl.BlockSpec(memory_space=pl.ANY),
            pl.BlockSpec(memory_space=pl.ANY),
            pl.BlockSpec(memory_space=pl.ANY),
            pl.BlockSpec((TILE, D), out_tile),
            pl.BlockSpec((1, MOD_ROWS, N_MOD * D), lambda h, w, *_: (layer, 0, 0)),
            vec(D), vec(D),
        ],
        out_specs=out_specs,
        scratch_shapes=[
            pltpu.VMEM((HALF, D), F32),
            pltpu.VMEM((HALF, 128), F32),
            pltpu.VMEM((HALF, D), F32),
            pltpu.VMEM((SCATTER_ROWS, D), F32),
            pltpu.VMEM((2, D, D_EXPERT), F32),
            pltpu.VMEM((2, D, D_EXPERT), F32),
            pltpu.VMEM((2, D_EXPERT, D), F32),
            pltpu.SemaphoreType.DMA((2, 3)),
        ],
    )
    return pl.pallas_call(
        functools.partial(_moe_kernel, split=split, layer=layer),
        grid_spec=grid_spec,
        out_shape=out_shape,
        compiler_params=pltpu.CompilerParams(dimension_semantics=("arbitrary", "arbitrary"),
                                             vmem_limit_bytes=MOE_VMEM_LIMIT),
        name="moe",
    )(pos, *items, hm, gates, w_gate, w_up, w_down, x1, mod,
      ln_g.reshape(DEPTH, 1, D), ln_b.reshape(DEPTH, 1, D))


def _rope_tables():
    rows = LAT_T // GRID_W
    row = jnp.repeat(jnp.arange(rows, dtype=F32), GRID_W)
    col = jnp.tile(jnp.arange(GRID_W, dtype=F32), rows)
    n_freq = QK // 4
    inv = ROPE_BASE ** (-jnp.arange(n_freq, dtype=F32) / n_freq)
    ang_r = row[:, None] * inv
    ang_c = col[:, None] * inv
    cos64 = jnp.concatenate([jnp.cos(ang_r), jnp.cos(ang_r), jnp.cos(ang_c), jnp.cos(ang_c)], axis=1)
    sin64 = jnp.concatenate([-jnp.sin(ang_r), jnp.sin(ang_r), -jnp.sin(ang_c), jnp.sin(ang_c)], axis=1)
    cos = jnp.concatenate([jnp.tile(cos64, (1, 2)), jnp.ones((SUB_ROWS, 128), F32)], axis=0)
    sin = jnp.concatenate([jnp.tile(sin64, (1, 2)), jnp.zeros((SUB_ROWS, 128), F32)], axis=0)
    return cos, sin


def _s5_params(a_re, a_im, b_re, b_im, c_re, c_im, log_dt):
    dt = jnp.exp(log_dt)[..., None]
    mag = jnp.exp(a_re * dt)
    abar_r = mag * jnp.cos(a_im * dt)
    abar_i = mag * jnp.sin(a_im * dt)
    den = a_re * a_re + a_im * a_im
    nr = abar_r - 1.0
    coef_r = (nr * a_re + abar_i * a_im) / den
    coef_i = (abar_i * a_re - nr * a_im) / den
    bbar_r = coef_r[..., None] * b_re - coef_i[..., None] * b_im
    bbar_i = coef_r[..., None] * b_im + coef_i[..., None] * b_re
    def b_rows(bb):
        return bb.transpose(0, 1, 2, 4, 3).reshape(DEPTH, 2, S5_W, S5_P)
    br, bi = b_rows(bbar_r), b_rows(bbar_i)
    b_cmp = jnp.concatenate([br, br, bi, bi], axis=-1)
    def c_rows(cc):
        return cc.transpose(0, 1, 4, 2, 3).reshape(DEPTH, 2, S5_P, S5_W)
    c_cmp = jnp.concatenate([c_rows(c_re), c_rows(-c_im)], axis=-2)
    a_bar = jnp.concatenate([abar_r.reshape(DEPTH, 2, S5_N), abar_i.reshape(DEPTH, 2, S5_N)], axis=-1)
    return b_cmp, c_cmp, a_bar


def _block_diag(w):
    eye = jnp.eye(LRU_BLOCKS, dtype=F32)
    m = jnp.einsum('ldkij,kh->ldkihj', w, eye)
    return m.reshape(DEPTH, 2, LRU_W, LRU_W).astype(BF16)


def kernel(x_prompt, x_sample, cache_k, cache_v, state_s5, state_lru, c, c_ctx, w_ada, b_ada, w_in, w_out, lam_q1, lam_k1, lam_q2, lam_k2, subln_g, s5_a_re, s5_a_im, s5_b_re, s5_b_im, s5_c_re, s5_c_im, s5_log_dt, s5_d, s5_w_glu, s5_b_glu, lru_conv_w, lru_conv_b, lru_w_a, lru_b_a, lru_w_i, lru_b_i, lru_lambda, ln1_g, ln1_b, ln2_g, ln2_b, router_group_w, router_group_b, router_expert_w, router_expert_b, moe_w_gate, moe_w_up, moe_w_down):
    x = jnp.concatenate([x_prompt.reshape(CTX_B * CTX_T, D), x_sample.reshape(LAT_B * LAT_T, D)], axis=0)
    cond = jnp.concatenate([c_ctx[None, :], c, jnp.zeros((MOD_ROWS - 1 - LAT_B, D), F32)], axis=0)
    mod = _ada_call(cond, w_ada, b_ada)

    w_out_b = w_out.astype(BF16)
    w_glu_b = s5_w_glu.astype(BF16)
    rope_cos, rope_sin = _rope_tables()
    b_cmp, c_cmp, a_bar = _s5_params(s5_a_re, s5_a_im, s5_b_re, s5_b_im, s5_c_re, s5_c_im, s5_log_dt)
    wa_blk = _block_diag(lru_w_a)
    wi_blk = _block_diag(lru_w_i)
    ba = lru_b_a.reshape(DEPTH, 2, 1, LRU_W)
    bi = lru_b_i.reshape(DEPTH, 2, 1, LRU_W)
    sp = jax.nn.softplus(-lru_lambda).reshape(DEPTH, 2, 1, LRU_W)
    lam = (jnp.exp(jnp.sum(lam_q1 * lam_k1, axis=-1)) - jnp.exp(jnp.sum(lam_q2 * lam_k2, axis=-1)))
    rw = jnp.concatenate([router_group_w, router_expert_w,
                          jnp.zeros((DEPTH, D, 128 - N_GROUPS - N_EXPERTS), F32)], axis=-1)
    rw_hi = rw.astype(BF16)
    rw_split = jnp.concatenate([rw_hi, (rw - rw_hi.astype(F32)).astype(BF16)], axis=-1)
    rb = jnp.concatenate([router_group_b, router_expert_b,
                          jnp.zeros((DEPTH, 128 - N_GROUPS - N_EXPERTS), F32)], axis=-1).reshape(DEPTH, 1, 128)
    s5_h0 = state_s5.reshape(LAT_B, DEPTH, 2, 2 * S5_N)
    lru_h0 = state_lru.transpose(1, 0, 2, 3)

    kc = jnp.zeros((CTX_B, DEPTH, HEADS, CTX_T, 2 * QK), F32)
    vc = jnp.zeros((CTX_B, DEPTH, HEADS, CTX_T, DV), F32)
    s5_states = []
    lru_states = []
    for l in range(DEPTH):
        lambda_init = 0.8 - 0.6 * math.exp(-0.3 * l)
        lam_l = (lam[l] + lambda_init).reshape(1)
        q, k, v, kc, vc, u_tm, xc_tm, g_lru = _inproj_call(
            l, x, mod, w_in, rope_cos, rope_sin, lru_conv_w, lru_conv_b, kc, vc)
        o_ctx = _att_ctx_call(l, lam_l, q, k, v, subln_g, 1.0 - lambda_init)
        o_lat = _att_lat_call(l, lam_l, q, k, v, cache_k, cache_v, subln_g, 1.0 - lambda_init)
        yf, yb, hend = _s5_call(l, u_tm.reshape(N_TOK, S5_W), b_cmp, c_cmp, a_bar, s5_d)
        yfix = _s5_fix_call(l, hend, s5_h0[:, l], a_bar, c_cmp)
        hl, lru_end = _lru_call(l, xc_tm.reshape(N_TOK, LRU_W), wa_blk, ba, wi_blk, bi, sp, lru_h0)
        x1, hm, gates = _outproj_call(l, x, o_ctx, o_lat, yf, yb, yfix, hl, g_lru, mod,
                                      w_glu_b, s5_b_glu, w_out_b, ln1_g, ln1_b, rw_split, rw_hi, rb)
        pos, counts = _sort_call(gates)
        items = _moe_schedule(counts[:, :N_GROUPS, 0].astype(I32))
        x = _moe_call(l, pos.reshape(N_TOK), items, hm, gates, moe_w_gate, moe_w_up,
                      moe_w_down, x1, mod, ln2_g, ln2_b, split=(l == DEPTH - 1))
        s5_states.append(hend[:, :N_CTX].reshape(2, CTX_B, 2, S5_G, S5_P).transpose(1, 0, 2, 3, 4))
        lru_states.append(lru_end[:, :N_CTX].transpose(1, 0, 2))

    y_p = x[0].reshape(CTX_B, CTX_T, D)
    y_s = x[1].reshape(LAT_B, LAT_T, D)
    return (y_p, y_s, kc, vc, jnp.stack(s5_states, axis=1), jnp.stack(lru_states, axis=1))
```
